```python
import jax, jax.numpy as jnp
from jax import lax
import numpy as np


D_MODEL = 2048
BATCH = 1
SEQ = 16384
DEPTH = 2

HEAD_DIM = 128
N_MAIN_HEADS = 12
N_KV_HEADS = 4
GROUP = N_MAIN_HEADS // N_KV_HEADS
N_MEM_HEADS = 4
N_MEM_TOKENS = 256
IDX_HEADS = 16
IDX_DIM = 64
IDX_TOPK_MAX = 256
MOBA_BLOCK = 256
MOBA_TOPK_MAX = 3
D_FF = 5632
ROPE_THETA = 10000.0
RMS_EPS = 1e-6
Q_CHUNK_A = 128
Q_CHUNK_B = 64
N_A_LAYERS = DEPTH // 2
N_B_LAYERS = DEPTH - N_A_LAYERS

A_SIZES = (N_MAIN_HEADS * HEAD_DIM, N_KV_HEADS * HEAD_DIM, N_KV_HEADS * HEAD_DIM,
           IDX_HEADS * IDX_DIM, IDX_DIM, IDX_HEADS, N_MEM_HEADS * HEAD_DIM)
B_SIZES = (N_MAIN_HEADS * HEAD_DIM, N_MEM_HEADS * HEAD_DIM)
MIX_WIDTH = (N_MAIN_HEADS + N_MEM_HEADS) * HEAD_DIM

kernel_name = 'yoco_dsa_moba_macaron_memory'


def split_cols(t, sizes):
    idx = [int(c) for c in np.cumsum(sizes)[:-1]]
    return jnp.split(t, idx, axis=-1)


def rms_norm(x, g):
    xf = x.astype(jnp.float32)
    y = xf * lax.rsqrt(jnp.mean(xf * xf, axis=-1, keepdims=True) + RMS_EPS)
    return (y * g.astype(jnp.float32)).astype(x.dtype)


def rope_tables(positions, dim):
    inv = 1.0 / (ROPE_THETA ** (jnp.arange(0, dim, 2, dtype=jnp.float32) / dim))
    ang = positions.astype(jnp.float32)[..., None] * inv
    return jnp.cos(ang), jnp.sin(ang)


def apply_rope(t, cos, sin):
    t1, t2 = jnp.split(t.astype(jnp.float32), 2, axis=-1)
    c = cos[:, :, None, :]
    s = sin[:, :, None, :]
    return jnp.concatenate([t1 * c - t2 * s, t1 * s + t2 * c], axis=-1).astype(t.dtype)


def swiglu(h, w_gate_up, w_down):
    g, u = jnp.split(h @ w_gate_up, 2, axis=-1)
    return (jax.nn.silu(g) * u) @ w_down


def to_chunks(t, chunk):
    b, s = t.shape[:2]
    return jnp.moveaxis(t.reshape((b, s // chunk, chunk) + t.shape[2:]), 1, 0)


def dsa_attention(q, k, v, q_idx, k_idx, w_idx, topk):
    b, s = q.shape[:2]
    n_chunks = s // Q_CHUNK_A
    key_pos = jnp.arange(s)
    bidx = jnp.arange(b)[:, None, None]
    scale = HEAD_DIM ** -0.5

    def one_chunk(args):
        qc, qic, wc, c0 = args
        qpos = c0 + jnp.arange(Q_CHUNK_A)
        dots = jnp.einsum('bqhd,bsd->bqhs', qic, k_idx).astype(jnp.float32)
        iscore = jnp.einsum('bqh,bqhs->bqs', wc.astype(jnp.float32), jax.nn.relu(dots))
        causal = key_pos[None, :] <= qpos[:, None]
        iscore = jnp.where(causal[None], iscore, -jnp.inf)
        _, sel = lax.top_k(iscore, topk)
        valid = sel <= qpos[None, :, None]
        ks = k[bidx, sel]
        vs = v[bidx, sel]
        qg = qc.reshape(b, Q_CHUNK_A, N_KV_HEADS, GROUP, HEAD_DIM)
        logits = jnp.einsum('bqgrd,bqkgd->bqgrk', qg, ks).astype(jnp.float32) * scale
        logits = jnp.where(valid[:, :, None, None, :], logits, -jnp.inf)
        p = jax.nn.softmax(logits, axis=-1).astype(vs.dtype)
        o = jnp.einsum('bqgrk,bqkgd->bqgrd', p, vs)
        return o.reshape(b, Q_CHUNK_A, N_MAIN_HEADS * HEAD_DIM)

    starts = jnp.arange(n_chunks, dtype=jnp.int32) * Q_CHUNK_A
    out = lax.map(one_chunk, (to_chunks(q, Q_CHUNK_A), to_chunks(q_idx, Q_CHUNK_A),
                              to_chunks(w_idx, Q_CHUNK_A), starts))
    return jnp.moveaxis(out, 0, 1).reshape(b, s, N_MAIN_HEADS * HEAD_DIM)


def moba_attention(q, k_blocks, v_blocks, k_means, n_sel):
    b, s = q.shape[:2]
    nb = k_blocks.shape[2]
    n_chunks = s // Q_CHUNK_B
    scale = HEAD_DIM ** -0.5
    bidx = jnp.arange(b)[:, None, None, None]
    hidx = (jnp.arange(N_MAIN_HEADS) // GROUP)[None, None, :, None]
    block_ids = jnp.arange(nb)

    def one_chunk(args):
        qc, c0 = args
        qpos = c0 + jnp.arange(Q_CHUNK_B)
        cur = c0 // MOBA_BLOCK
        qg = qc.reshape(b, Q_CHUNK_B, N_KV_HEADS, GROUP, HEAD_DIM)
        gate = jnp.einsum('bqgrd,bgnd->bqgrn', qg, k_means).astype(jnp.float32)
        gate = gate.reshape(b, Q_CHUNK_B, N_MAIN_HEADS, nb)
        gate = jnp.where(block_ids < cur, gate, -jnp.inf)
        _, sel = lax.top_k(gate, n_sel)
        sel_valid = sel < cur
        ks = k_blocks[bidx, hidx, sel]
        vs = v_blocks[bidx, hidx, sel]
        lp = jnp.einsum('bqhd,bqhnkd->bqhnk', qc, ks).astype(jnp.float32) * scale
        lp = jnp.where(sel_valid[..., None], lp, -jnp.inf)
        lp = lp.reshape(b, Q_CHUNK_B, N_MAIN_HEADS, n_sel * MOBA_BLOCK)
        k_own = lax.dynamic_index_in_dim(k_blocks, cur, axis=2, keepdims=False)
        v_own = lax.dynamic_index_in_dim(v_blocks, cur, axis=2, keepdims=False)
        own_pos = cur * MOBA_BLOCK + jnp.arange(MOBA_BLOCK)
        lo = jnp.einsum('bqgrd,bgkd->bqgrk', qg, k_own).astype(jnp.float32) * scale
        lo = lo.reshape(b, Q_CHUNK_B, N_MAIN_HEADS, MOBA_BLOCK)
        lo = jnp.where((own_pos[None, :] <= qpos[:, None])[None, :, None, :], lo, -jnp.inf)
        p = jax.nn.softmax(jnp.concatenate([lp, lo], axis=-1), axis=-1).astype(vs.dtype)
        p_sel = p[..., :n_sel * MOBA_BLOCK].reshape(b, Q_CHUNK_B, N_MAIN_HEADS, n_sel, MOBA_BLOCK)
        p_own = p[..., n_sel * MOBA_BLOCK:].reshape(b, Q_CHUNK_B, N_KV_HEADS, GROUP, MOBA_BLOCK)
        o = (jnp.einsum('bqhnk,bqhnkd->bqhd', p_sel, vs)
             + jnp.einsum('bqgrk,bgkd->bqgrd', p_own, v_own).reshape(b, Q_CHUNK_B, N_MAIN_HEADS, HEAD_DIM))
        return o.reshape(b, Q_CHUNK_B, N_MAIN_HEADS * HEAD_DIM)

    starts = jnp.arange(n_chunks, dtype=jnp.int32) * Q_CHUNK_B
    out = lax.map(one_chunk, (to_chunks(q, Q_CHUNK_B), starts))
    return jnp.moveaxis(out, 0, 1).reshape(b, s, N_MAIN_HEADS * HEAD_DIM)


def memory_attention(q_mem, mem_kv):
    b, s = q_mem.shape[:2]
    m = mem_kv.shape[1]
    qm = q_mem.reshape(b, s, N_MEM_HEADS, HEAD_DIM)
    km, vm = jnp.split(mem_kv, 2, axis=-1)
    km = km.reshape(b, m, N_MEM_HEADS, HEAD_DIM)
    vm = vm.reshape(b, m, N_MEM_HEADS, HEAD_DIM)
    logits = jnp.einsum('bshd,bmhd->bhsm', qm, km).astype(jnp.float32) * (HEAD_DIM ** -0.5)
    p = jax.nn.softmax(logits, axis=-1).astype(vm.dtype)
    return jnp.einsum('bhsm,bmhd->bshd', p, vm).reshape(b, s, N_MEM_HEADS * HEAD_DIM)


def setup_inputs(seed: int = 0) -> dict:
    key = jax.random.key(seed)
    ks = jax.random.split(key, 24)

    def w(k, shape, fan_in):
        return jax.random.normal(k, shape, jnp.float32) * (fan_in ** -0.5)

    def gain(k, shape):
        return 1.0 + 0.02 * jax.random.normal(k, shape, jnp.float32)

    a_in = sum(A_SIZES)
    b_in = sum(B_SIZES)
    kv_w = 2 * N_KV_HEADS * HEAD_DIM
    mem_w = 2 * N_MEM_HEADS * HEAD_DIM
    return {
        'x': jax.random.normal(ks[0], (BATCH, SEQ, D_MODEL), jnp.float32),
        'mem': jax.random.normal(ks[1], (BATCH, N_MEM_TOKENS, D_MODEL), jnp.float32),
        'positions': jnp.broadcast_to(jnp.arange(SEQ, dtype=jnp.int32), (BATCH, SEQ)),
        'ffn1_norm': gain(ks[2], (DEPTH, D_MODEL)),
        'ffn1_w_gate_up': w(ks[3], (DEPTH, D_MODEL, 2 * D_FF), D_MODEL),
        'ffn1_w_down': w(ks[4], (DEPTH, D_FF, D_MODEL), D_FF),
        'attn_norm': gain(ks[5], (DEPTH, D_MODEL)),
        'mem_norm': gain(ks[6], (DEPTH, D_MODEL)),
        'a_w_in': w(ks[7], (N_A_LAYERS, D_MODEL, a_in), D_MODEL),
        'idx_k_norm': gain(ks[8], (N_A_LAYERS, IDX_DIM)),
        'b_w_in': w(ks[9], (N_B_LAYERS, D_MODEL, b_in), D_MODEL),
        'w_mem_kv': w(ks[10], (DEPTH, D_MODEL, mem_w), D_MODEL),
        'w_out': w(ks[11], (DEPTH, MIX_WIDTH, D_MODEL), MIX_WIDTH),
        'ffn2_norm': gain(ks[12], (DEPTH, D_MODEL)),
        'ffn2_w_gate_up': w(ks[13], (DEPTH, D_MODEL, 2 * D_FF), D_MODEL),
        'ffn2_w_down': w(ks[14], (DEPTH, D_FF, D_MODEL), D_FF),
        'kv_norm': gain(ks[15], (D_MODEL,)),
        'w_kv_shared': w(ks[16], (D_MODEL, kv_w), D_MODEL),
        'final_norm': gain(ks[17], (D_MODEL,)),
    }


def reference(x, mem, positions, ffn1_norm, ffn1_w_gate_up, ffn1_w_down, attn_norm, mem_norm,
              a_w_in, idx_k_norm, b_w_in, w_mem_kv, w_out, ffn2_norm, ffn2_w_gate_up, ffn2_w_down,
              kv_norm, w_kv_shared, final_norm):
    b, s, _ = x.shape
    cos_h, sin_h = rope_tables(positions, HEAD_DIM)
    cos_i, sin_i = rope_tables(positions, IDX_DIM)
    topk = min(IDX_TOPK_MAX, s // 4)
    nb = -(-s // MOBA_BLOCK)
    n_sel = min(MOBA_TOPK_MAX, max(nb - 1, 1))
    idx_w_scale = (IDX_HEADS ** -0.5) * (IDX_DIM ** -0.5)

    k_blocks = v_blocks = k_means = None
    for i in range(DEPTH):
        if i == N_A_LAYERS:
            hk = rms_norm(x, kv_norm)
            k_sh, v_sh = jnp.split(hk @ w_kv_shared, 2, axis=-1)
            k_sh = apply_rope(k_sh.reshape(b, s, N_KV_HEADS, HEAD_DIM), cos_h, sin_h)
            v_sh = v_sh.reshape(b, s, N_KV_HEADS, HEAD_DIM)
            pad = nb * MOBA_BLOCK - s
            k_sh = jnp.pad(k_sh, ((0, 0), (0, pad), (0, 0), (0, 0)))
            v_sh = jnp.pad(v_sh, ((0, 0), (0, pad), (0, 0), (0, 0)))
            k_blocks = jnp.transpose(k_sh.reshape(b, nb, MOBA_BLOCK, N_KV_HEADS, HEAD_DIM), (0, 3, 1, 2, 4))
            v_blocks = jnp.transpose(v_sh.reshape(b, nb, MOBA_BLOCK, N_KV_HEADS, HEAD_DIM), (0, 3, 1, 2, 4))
            k_means = jnp.mean(k_blocks.astype(jnp.float32), axis=3).astype(k_blocks.dtype)

        x = x + 0.5 * swiglu(rms_norm(x, ffn1_norm[i]), ffn1_w_gate_up[i], ffn1_w_down[i])

        h = rms_norm(x, attn_norm[i])
        mem_kv = rms_norm(mem, mem_norm[i]) @ w_mem_kv[i]
        if i < N_A_LAYERS:
            q, k, v, qi, ki, wi, qm = split_cols(h @ a_w_in[i], A_SIZES)
            q = apply_rope(q.reshape(b, s, N_MAIN_HEADS, HEAD_DIM), cos_h, sin_h)
            k = apply_rope(k.reshape(b, s, N_KV_HEADS, HEAD_DIM), cos_h, sin_h)
            v = v.reshape(b, s, N_KV_HEADS, HEAD_DIM)
            qi = apply_rope(qi.reshape(b, s, IDX_HEADS, IDX_DIM), cos_i, sin_i)
            ki = apply_rope(rms_norm(ki, idx_k_norm[i])[:, :, None, :], cos_i, sin_i)[:, :, 0, :]
            wi = wi * idx_w_scale
            o_main = dsa_attention(q, k, v, qi, ki, wi, topk)
        else:
            j = i - N_A_LAYERS
            q, qm = split_cols(h @ b_w_in[j], B_SIZES)
            q = apply_rope(q.reshape(b, s, N_MAIN_HEADS, HEAD_DIM), cos_h, sin_h)
            o_main = moba_attention(q, k_blocks, v_blocks, k_means, n_sel)
        o_mem = memory_attention(qm, mem_kv)
        x = x + jnp.concatenate([o_main, o_mem], axis=-1) @ w_out[i]

        x = x + 0.5 * swiglu(rms_norm(x, ffn2_norm[i]), ffn2_w_gate_up[i], ffn2_w_down[i])

    return rms_norm(x, final_norm)
```

```python
import functools

import jax
import jax.numpy as jnp
import numpy as np
from jax import lax
from jax.experimental import pallas as pl
from jax.experimental.pallas import tpu as pltpu

HEAD_DIM = 128
N_MAIN_HEADS = 12
N_KV_HEADS = 4
GROUP = N_MAIN_HEADS // N_KV_HEADS
N_MEM_HEADS = 4
IDX_HEADS = 16
IDX_DIM = 64
IDX_TOPK_MAX = 256
MOBA_BLOCK = 256
MOBA_TOPK_MAX = 3
ROPE_THETA = 10000.0
RMS_EPS = 1e-6

LANES = 128
SUBLANES = 8
VMEM_LIMIT = 56 * 1024 * 1024
NEG = -1e30
INT_MIN = -2 ** 31
INT_MAX = 2 ** 31 - 1

F32 = jnp.float32
BF16 = jnp.bfloat16


def _params(*sem):
    return pltpu.CompilerParams(dimension_semantics=sem, vmem_limit_bytes=VMEM_LIMIT)


def _rms(x, gain):
    return x * lax.rsqrt(jnp.mean(x * x, axis=-1, keepdims=True) + RMS_EPS) * gain


def _ffn_body(x_ref, g_ref, wg_ref, wu_ref, wd_ref, pg_ref, o_ref, h_ref, *, final_norm):
    j = pl.program_id(1)

    @pl.when(j == 0)
    def _():
        h_ref[...] = _rms(x_ref[...], g_ref[...]).astype(BF16)
        o_ref[...] = jnp.zeros_like(o_ref)

    h = h_ref[...]
    gate = jnp.dot(h, wg_ref[...], preferred_element_type=F32)
    up = jnp.dot(h, wu_ref[...], preferred_element_type=F32)
    act = (gate * (1.0 / (1.0 + jnp.exp(-gate))) * up).astype(BF16)
    o_ref[...] += jnp.dot(act, wd_ref[...], preferred_element_type=F32)

    @pl.when(j == pl.num_programs(1) - 1)
    def _():
        y = x_ref[...] + 0.5 * o_ref[...]
        if final_norm:
            y = _rms(y, pg_ref[...])
        o_ref[...] = y


def _ffn(x, gain, w_gate_up, w_down, post_gain=None, *, tm=512, tf=512):
    s, d = x.shape
    f = w_down.shape[0]
    tm = min(tm, s)
    tf = min(tf, f)
    assert s % tm == 0 and f % tf == 0
    nf = f // tf
    final_norm = post_gain is not None
    pg = post_gain if final_norm else gain
    return pl.pallas_call(
        functools.partial(_ffn_body, final_norm=final_norm),
        grid=(s // tm, nf),
        in_specs=[
            pl.BlockSpec((tm, d), lambda i, j: (i, 0)),
            pl.BlockSpec((1, d), lambda i, j: (0, 0)),
            pl.BlockSpec((d, tf), lambda i, j: (0, j)),
            pl.BlockSpec((d, tf), lambda i, j: (0, j + nf)),
            pl.BlockSpec((tf, d), lambda i, j: (j, 0)),
            pl.BlockSpec((1, d), lambda i, j: (0, 0)),
        ],
        out_specs=pl.BlockSpec((tm, d), lambda i, j: (i, 0)),
        out_shape=jax.ShapeDtypeStruct((s, d), F32),
        scratch_shapes=[pltpu.VMEM((tm, d), BF16)],
        compiler_params=_params("parallel", "arbitrary"),
        name="ffn",
    )(x, gain.reshape(1, d), w_gate_up, w_gate_up, w_down, pg.reshape(1, d))


def _rope_heads(y, cos, sin, o_ref, scale):
    for h in range(y.shape[1] // HEAD_DIM):
        t = y[:, h * HEAD_DIM:(h + 1) * HEAD_DIM]
        r = t * cos + pltpu.roll(t, HEAD_DIM // 2, 1) * sin
        if scale != 1.0:
            r = r * scale
        o_ref[:, h * HEAD_DIM:(h + 1) * HEAD_DIM] = r.astype(o_ref.dtype)


def _rot_idx(t):
    lane = lax.broadcasted_iota(jnp.int32, t.shape, 1)
    first_half = (lane & (IDX_DIM // 2)) == 0
    return jnp.where(first_half, pltpu.roll(t, LANES - IDX_DIM // 2, 1),
                     pltpu.roll(t, IDX_DIM // 2, 1))


def _proj_body(*refs, kinds, q_scale, idx_w_scale):
    x_ref, g_ref, cos_ref, sin_ref, cosi_ref, sini_ref, gk_ref = refs[:7]
    n = len(kinds)
    w_refs = refs[7:7 + n]
    o_refs = refs[7 + n:7 + 2 * n]
    h = _rms(x_ref[...], g_ref[...]).astype(BF16)
    for kind, w_ref, o_ref in zip(kinds, w_refs, o_refs):
        y = jnp.dot(h, w_ref[...], preferred_element_type=F32)
        if kind == "plain":
            o_ref[...] = y.astype(o_ref.dtype)
        elif kind == "scaled":
            o_ref[...] = (y * q_scale).astype(o_ref.dtype)
        elif kind == "rope":
            _rope_heads(y, cos_ref[...], sin_ref[...], o_ref, 1.0)
        elif kind == "rope_scaled":
            _rope_heads(y, cos_ref[...], sin_ref[...], o_ref, q_scale)
        elif kind == "rope_idx":
            cosi, sini = cosi_ref[...], sini_ref[...]
            for c in range(y.shape[1] // LANES):
                t = y[:, c * LANES:(c + 1) * LANES]
                o_ref[:, c * LANES:(c + 1) * LANES] = (t * cosi + _rot_idx(t) * sini).astype(o_ref.dtype)
        elif kind == "kiwi":
            lane = lax.broadcasted_iota(jnp.int32, y.shape, 1)
            is_k = lane < IDX_DIM
            kk = jnp.where(is_k, y, 0.0)
            ms = jnp.sum(kk * kk, axis=-1, keepdims=True) * (1.0 / IDX_DIM)
            kn = kk * lax.rsqrt(ms + RMS_EPS) * gk_ref[...]
            kr = kn * cosi_ref[...] + _rot_idx(kn) * sini_ref[...]
            o_ref[...] = jnp.where(is_k, kr, y * idx_w_scale)
        else:
            raise ValueError(kind)


def _proj(x, gain, tables, gk, weights, kinds, out_dtypes, *, tm=256):
    s, d = x.shape
    tm = min(tm, s)
    assert s % tm == 0
    cos, sin, cosi, sini = tables
    row = lambda i: (i, 0)
    const = lambda i: (0, 0)
    in_specs = [pl.BlockSpec((tm, d), row), pl.BlockSpec((1, d), const)]
    in_specs += [pl.BlockSpec((tm, LANES), row)] * 4
    in_specs += [pl.BlockSpec((1, LANES), const)]
    in_specs += [pl.BlockSpec(w.shape, const, pipeline_mode=pl.Buffered(1)) for w in weights]
    out_specs = [pl.BlockSpec((tm, w.shape[1]), row) for w in weights]
    out_shape = [jax.ShapeDtypeStruct((s, w.shape[1]), dt) for w, dt in zip(weights, out_dtypes)]
    return pl.pallas_call(
        functools.partial(_proj_body, kinds=tuple(kinds), q_scale=HEAD_DIM ** -0.5,
                          idx_w_scale=(IDX_HEADS ** -0.5) * (IDX_DIM ** -0.5)),
        grid=(s // tm,),
        in_specs=in_specs,
        out_specs=out_specs,
        out_shape=out_shape,
        compiler_params=_params("parallel"),
        name="proj",
    )(x, gain.reshape(1, d), cos, sin, cosi, sini, gk, *weights)


def _kv_shared_body(x_ref, g_ref, cos_ref, sin_ref, wk_ref, wv_ref, k_ref, v_ref, km_ref):
    h = _rms(x_ref[...], g_ref[...]).astype(BF16)
    yk = jnp.dot(h, wk_ref[...], preferred_element_type=F32)
    cos, sin = cos_ref[...], sin_ref[...]
    tm = yk.shape[0]
    for hd in range(yk.shape[1] // HEAD_DIM):
        t = yk[:, hd * HEAD_DIM:(hd + 1) * HEAD_DIM]
        r = t * cos + pltpu.roll(t, HEAD_DIM // 2, 1) * sin
        k_ref[:, hd * HEAD_DIM:(hd + 1) * HEAD_DIM] = r.astype(k_ref.dtype)
        for b in range(tm // MOBA_BLOCK):
            blk = r[b * MOBA_BLOCK:(b + 1) * MOBA_BLOCK]
            km_ref[b, :, hd * HEAD_DIM:(hd + 1) * HEAD_DIM] = (
                jnp.sum(blk, axis=0, keepdims=True) * (1.0 / MOBA_BLOCK))
    v_ref[...] = jnp.dot(h, wv_ref[...], preferred_element_type=F32).astype(v_ref.dtype)


def _kv_shared(x, gain, cos, sin, wk, wv, *, tm=256):
    s, d = x.shape
    assert s % MOBA_BLOCK == 0
    tm = min(tm, s)
    assert tm % MOBA_BLOCK == 0 and s % tm == 0
    nkv = wk.shape[1]
    row = lambda i: (i, 0)
    const = lambda i: (0, 0)
    return pl.pallas_call(
        _kv_shared_body,
        grid=(s // tm,),
        in_specs=[pl.BlockSpec((tm, d), row), pl.BlockSpec((1, d), const),
                  pl.BlockSpec((tm, LANES), row), pl.BlockSpec((tm, LANES), row),
                  pl.BlockSpec(wk.shape, const, pipeline_mode=pl.Buffered(1)),
                  pl.BlockSpec(wv.shape, const, pipeline_mode=pl.Buffered(1))],
        out_specs=[pl.BlockSpec((tm, nkv), row), pl.BlockSpec((tm, nkv), row),
                   pl.BlockSpec((tm // MOBA_BLOCK, 1, nkv), lambda i: (i, 0, 0))],
        out_shape=[jax.ShapeDtypeStruct((s, nkv), BF16), jax.ShapeDtypeStruct((s, nkv), BF16),
                   jax.ShapeDtypeStruct((s // MOBA_BLOCK, 1, nkv), F32)],
        compiler_params=_params("parallel"),
        name="kv_shared",
    )(x, gain.reshape(1, d), cos, sin, wk, wv)


def _softmax_step(hh, s, vt_tile, m_ref, l_ref, acc_ref):
    m_old = m_ref[hh]
    m_new = jnp.maximum(m_old, jnp.max(s, axis=0, keepdims=True))
    p = jnp.exp(s - m_new)
    alpha = jnp.exp(m_old - m_new)
    l_ref[hh] = alpha * l_ref[hh] + jnp.sum(p, axis=0, keepdims=True)
    acc_ref[hh] = alpha * acc_ref[hh] + jnp.dot(vt_tile, p.astype(BF16), preferred_element_type=F32)
    m_ref[hh] = m_new


def _softmax_init(m_ref, l_ref, acc_ref):
    m_ref[...] = jnp.full(m_ref.shape, NEG, F32)
    l_ref[...] = jnp.zeros(l_ref.shape, F32)
    acc_ref[...] = jnp.zeros(acc_ref.shape, F32)


def _softmax_finish(o_ref, l_ref, acc_ref):
    for hh in range(GROUP):
        o_ref[hh * HEAD_DIM:(hh + 1) * HEAD_DIM, :] = (acc_ref[hh] / l_ref[hh]).astype(o_ref.dtype)


def _dsa_body(qit_ref, wt_ref, ki_ref, qt_ref, k_ref, vt_ref, o_ref,
              keys_ref, thr_ref, m_ref, l_ref, acc_ref, *, tq, topk):
    i = pl.program_id(0)
    g = pl.program_id(1)
    n_tiles = i + 1
    sub = 128

    def positions(kt, rows):
        kpos = kt * rows + lax.broadcasted_iota(jnp.int32, (rows, tq), 0)
        qpos = i * tq + lax.broadcasted_iota(jnp.int32, (rows, tq), 1)
        return kpos, qpos

    @pl.when(g == 0)
    def _():
        def score_tile(kt, carry):
            r0 = pl.multiple_of(kt * sub, sub)
            kit = ki_ref[pl.ds(r0, sub), :]
            sc = jnp.zeros((sub, tq), F32)
            for h in range(IDX_HEADS):
                d = jnp.dot(kit, qit_ref[h * IDX_DIM:(h + 1) * IDX_DIM, :], preferred_element_type=F32)
                sc = sc + wt_ref[h:h + 1, :] * jnp.maximum(d, 0.0)
            kpos, qpos = positions(kt, sub)
            sc = jnp.where(kpos <= qpos, sc, -jnp.inf)
            bits = pltpu.bitcast(sc, jnp.int32)
            keys_ref[pl.ds(r0, sub), :] = bits ^ ((bits >> 31) & INT_MAX)
            return carry

        lax.fori_loop(0, n_tiles * (tq // sub), score_tile, 0)

        def bisect(_, carry):
            lo, hi = carry
            mid = (lo >> 1) + (hi >> 1) + (lo & hi & 1)

            def count_tile(kt, c):
                blk = keys_ref[pl.ds(pl.multiple_of(kt * tq, tq), tq), :]
                ge = jnp.where(blk >= mid, 1, 0).astype(jnp.int32)
                return c + jnp.sum(ge.reshape(tq // SUBLANES, SUBLANES, tq), axis=0)

            c = lax.fori_loop(0, n_tiles, count_tile, jnp.zeros((SUBLANES, tq), jnp.int32))
            ok = jnp.sum(c, axis=0, keepdims=True) >= topk
            return jnp.where(ok, mid, lo), jnp.where(ok, hi, mid)

        lo, _ = lax.fori_loop(0, 32, bisect, (jnp.full((1, tq), INT_MIN, jnp.int32),
                                              jnp.full((1, tq), INT_MAX, jnp.int32)))
        thr_ref[...] = lo

    _softmax_init(m_ref, l_ref, acc_ref)
    thr = thr_ref[...]

    def attend(kt, carry):
        r0 = pl.multiple_of(kt * tq, tq)
        kpos, qpos = positions(kt, tq)
        mask = (keys_ref[pl.ds(r0, tq), :] >= thr) & (kpos <= qpos)
        k_tile = k_ref[pl.ds(r0, tq), :]
        vt_tile = vt_ref[kt]
        for hh in range(GROUP):
            s = jnp.dot(k_tile, qt_ref[hh * HEAD_DIM:(hh + 1) * HEAD_DIM, :], preferred_element_type=F32)
            _softmax_step(hh, jnp.where(mask, s, NEG), vt_tile, m_ref, l_ref, acc_ref)
        return carry

    lax.fori_loop(0, n_tiles, attend, 0)
    _softmax_finish(o_ref, l_ref, acc_ref)


def _dsa(qit, wt, ki, qt, k, vtt, topk, *, tq=256):
    s = ki.shape[0]
    tq = min(tq, s)
    assert s % tq == 0 and vtt.shape[2] == tq
    gw = GROUP * HEAD_DIM
    return pl.pallas_call(
        functools.partial(_dsa_body, tq=tq, topk=topk),
        grid=(s // tq, N_KV_HEADS),
        in_specs=[
            pl.BlockSpec((IDX_HEADS * IDX_DIM, tq), lambda i, g: (0, i)),
            pl.BlockSpec((IDX_HEADS, tq), lambda i, g: (0, i)),
            pl.BlockSpec((s, IDX_DIM), lambda i, g: (0, 0), pipeline_mode=pl.Buffered(1)),
            pl.BlockSpec((gw, tq), lambda i, g: (g, i)),
            pl.BlockSpec((s, HEAD_DIM), lambda i, g: (0, g)),
            pl.BlockSpec((s // tq, HEAD_DIM, tq), lambda i, g: (0, g, 0)),
        ],
        out_specs=pl.BlockSpec((gw, tq), lambda i, g: (g, i)),
        out_shape=jax.ShapeDtypeStruct((N_MAIN_HEADS * HEAD_DIM, s), BF16),
        scratch_shapes=[
            pltpu.VMEM((s, tq), jnp.int32),
            pltpu.VMEM((1, tq), jnp.int32),
            pltpu.VMEM((GROUP, 1, tq), F32),
            pltpu.VMEM((GROUP, 1, tq), F32),
            pltpu.VMEM((GROUP, HEAD_DIM, tq), F32),
        ],
        compiler_params=_params("arbitrary", "arbitrary"),
        name="dsa",
    )(qit, wt, ki, qt, k, vtt)


def _moba_body(qt_ref, k_ref, vt_ref, km_ref, o_ref, bias_ref, m_ref, l_ref, acc_ref, *, tq, n_sel):
    cur = pl.program_id(0)
    nb = km_ref.shape[0]

    blk_id = lax.broadcasted_iota(jnp.int32, (nb, tq), 0)
    km = km_ref[...]
    for hh in range(GROUP):
        gate = jnp.dot(km, qt_ref[hh * HEAD_DIM:(hh + 1) * HEAD_DIM, :], preferred_element_type=F32)
        gate = jnp.where(blk_id < cur, gate, -jnp.inf)
        chosen = jnp.zeros((nb, tq), jnp.bool_)
        for _ in range(n_sel):
            best = jnp.max(gate, axis=0, keepdims=True)
            first = jnp.min(jnp.where(gate == best, blk_id, nb), axis=0, keepdims=True)
            pick = blk_id == first
            chosen = chosen | pick
            gate = jnp.where(pick, -jnp.inf, gate)
        bias_ref[hh] = jnp.where(chosen & (blk_id < cur), 0.0, NEG)

    _softmax_init(m_ref, l_ref, acc_ref)

    def attend(b, carry):
        r0 = pl.multiple_of(b * tq, tq)
        k_tile = k_ref[pl.ds(r0, tq), :]
        vt_tile = vt_ref[b]
        for hh in range(GROUP):
            s = jnp.dot(k_tile, qt_ref[hh * HEAD_DIM:(hh + 1) * HEAD_DIM, :], preferred_element_type=F32)
            _softmax_step(hh, s + bias_ref[hh, pl.ds(b, 1), :], vt_tile, m_ref, l_ref, acc_ref)
        return carry

    lax.fori_loop(0, cur, attend, 0)

    r0 = pl.multiple_of(cur * tq, tq)
    k_tile = k_ref[pl.ds(r0, tq), :]
    vt_tile = vt_ref[cur]
    causal = (lax.broadcasted_iota(jnp.int32, (tq, tq), 0) <= lax.broadcasted_iota(jnp.int32, (tq, tq), 1))
    for hh in range(GROUP):
        s = jnp.dot(k_tile, qt_ref[hh * HEAD_DIM:(hh + 1) * HEAD_DIM, :], preferred_element_type=F32)
        _softmax_step(hh, jnp.where(causal, s, NEG), vt_tile, m_ref, l_ref, acc_ref)
    _softmax_finish(o_ref, l_ref, acc_ref)


def _moba(qt, k, vtt, kmeans, n_sel):
    s = k.shape[0]
    tq = MOBA_BLOCK
    nb = s // tq
    gw = GROUP * HEAD_DIM
    return pl.pallas_call(
        functools.partial(_moba_body, tq=tq, n_sel=n_sel),
        grid=(nb, N_KV_HEADS),
        in_specs=[
            pl.BlockSpec((gw, tq), lambda i, g: (g, i)),
            pl.BlockSpec((s, HEAD_DIM), lambda i, g: (0, g)),
            pl.BlockSpec((nb, HEAD_DIM, tq), lambda i, g: (0, g, 0)),
            pl.BlockSpec((nb, HEAD_DIM), lambda i, g: (0, g)),
        ],
        out_specs=pl.BlockSpec((gw, tq), lambda i, g: (g, i)),
        out_shape=jax.ShapeDtypeStruct((N_MAIN_HEADS * HEAD_DIM, s), BF16),
        scratch_shapes=[
            pltpu.VMEM((GROUP, nb, tq), F32),
            pltpu.VMEM((GROUP, 1, tq), F32),
            pltpu.VMEM((GROUP, 1, tq), F32),
            pltpu.VMEM((GROUP, HEAD_DIM, tq), F32),
        ],
        compiler_params=_params("parallel", "arbitrary"),
        name="moba",
    )(qt, k, vtt, kmeans)


def _mem_attn_body(q_ref, kt_ref, v_ref, o_ref):
    for h in range(N_MEM_HEADS):
        sl = slice(h * HEAD_DIM, (h + 1) * HEAD_DIM)
        s = jnp.dot(q_ref[:, sl], kt_ref[sl, :], preferred_element_type=F32)
        p = jnp.exp(s - jnp.max(s, axis=-1, keepdims=True))
        o = jnp.dot(p.astype(BF16), v_ref[:, sl], preferred_element_type=F32)
        o_ref[:, sl] = (o / jnp.sum(p, axis=-1, keepdims=True)).astype(o_ref.dtype)


def _mem_attn(qm, kmt, vm, *, tq=512):
    s, w = qm.shape
    m = vm.shape[0]
    tq = min(tq, s)
    assert s % tq == 0
    return pl.pallas_call(
        _mem_attn_body,
        grid=(s // tq,),
        in_specs=[pl.BlockSpec((tq, w), lambda i: (i, 0)),
                  pl.BlockSpec((w, m), lambda i: (0, 0)),
                  pl.BlockSpec((m, w), lambda i: (0, 0))],
        out_specs=pl.BlockSpec((tq, w), lambda i: (i, 0)),
        out_shape=jax.ShapeDtypeStruct((s, w), BF16),
        compiler_params=_params("parallel"),
        name="mem_attn",
    )(qm, kmt, vm)


def _out_proj_body(x_ref, om_ref, oq_ref, w1_ref, w2_ref, o_ref):
    o_ref[...] = (x_ref[...]
                  + jnp.dot(om_ref[...], w1_ref[...], preferred_element_type=F32)
                  + jnp.dot(oq_ref[...], w2_ref[...], preferred_element_type=F32))


def _out_proj(x, o_main, o_mem, w1, w2, *, tm=512):
    s, d = x.shape
    tm = min(tm, s)
    assert s % tm == 0
    row = lambda i: (i, 0)
    const = lambda i: (0, 0)
    return pl.pallas_call(
        _out_proj_body,
        grid=(s // tm,),
        in_specs=[pl.BlockSpec((tm, d), row),
                  pl.BlockSpec((tm, o_main.shape[1]), row),
                  pl.BlockSpec((tm, o_mem.shape[1]), row),
                  pl.BlockSpec(w1.shape, const, pipeline_mode=pl.Buffered(1)),
                  pl.BlockSpec(w2.shape, const, pipeline_mode=pl.Buffered(1))],
        out_specs=pl.BlockSpec((tm, d), row),
        out_shape=jax.ShapeDtypeStruct((s, d), F32),
        compiler_params=_params("parallel"),
        name="out_proj",
    )(x, o_main, o_mem, w1, w2)


def _rope_tables(positions, dim):
    inv = 1.0 / (ROPE_THETA ** (jnp.arange(0, dim, 2, dtype=F32) / dim))
    ang = positions.astype(F32)[:, None] * inv
    c, s = jnp.cos(ang), jnp.sin(ang)
    reps = LANES // dim
    return jnp.tile(jnp.concatenate([c, c], -1), (1, reps)), jnp.tile(jnp.concatenate([-s, s], -1), (1, reps))


def _tile_major_t(v, tile):
    s, w = v.shape
    return jnp.transpose(v.reshape(s // tile, tile, w), (0, 2, 1))


def kernel(x, mem, positions, ffn1_norm, ffn1_w_gate_up, ffn1_w_down, attn_norm, mem_norm, a_w_in, idx_k_norm, b_w_in, w_mem_kv, w_out, ffn2_norm, ffn2_w_gate_up, ffn2_w_down, kv_norm, w_kv_shared, final_norm):
    b, s, d = x.shape
    assert b == 1 and mem.shape[0] == 1
    depth = ffn1_norm.shape[0]
    n_a = a_w_in.shape[0]
    main_w = N_MAIN_HEADS * HEAD_DIM
    kv_w = N_KV_HEADS * HEAD_DIM
    idx_w = IDX_HEADS * IDX_DIM
    mem_w = N_MEM_HEADS * HEAD_DIM
    topk = min(IDX_TOPK_MAX, s // 4)
    nb = s // MOBA_BLOCK
    n_sel = min(MOBA_TOPK_MAX, max(nb - 1, 1))
    tq = MOBA_BLOCK

    cos, sin = _rope_tables(positions[0], HEAD_DIM)
    cosi, sini = _rope_tables(positions[0], IDX_DIM)
    tables = (cos, sin, cosi, sini)
    mem_tables = tuple(t[:mem.shape[1]] for t in tables)
    no_gk = jnp.zeros((1, LANES), F32)

    xs = x[0]
    mem2 = mem[0]
    k_sh = vtt_sh = kmeans = None
    for i in range(depth):
        if i == n_a:
            wk = w_kv_shared[:, :kv_w].astype(BF16)
            wv = w_kv_shared[:, kv_w:].astype(BF16)
            k_sh, v_sh, km = _kv_shared(xs, kv_norm, cos, sin, wk, wv)
            vtt_sh = _tile_major_t(v_sh, tq)
            kmeans = km.reshape(nb, kv_w).astype(BF16)

        last = i == depth - 1
        xs = _ffn(xs, ffn1_norm[i], ffn1_w_gate_up[i].astype(BF16), ffn1_w_down[i].astype(BF16))

        wm = w_mem_kv[i].astype(BF16)
        mk, mv = _proj(mem2, mem_norm[i], mem_tables, no_gk, [wm[:, :mem_w], wm[:, mem_w:]],
                       ["plain", "plain"], [BF16, BF16])
        if i < n_a:
            wa = a_w_in[i]
            o0 = 0
            ws = []
            for width in (main_w, kv_w, kv_w, idx_w, IDX_DIM + IDX_HEADS, mem_w):
                ws.append(wa[:, o0:o0 + width])
                o0 += width
            w_kiwi = jnp.pad(ws[4], ((0, 0), (0, LANES - ws[4].shape[1])))
            ws = [w.astype(BF16) for w in (ws[0], ws[1], ws[2], ws[3], w_kiwi, ws[5])]
            gk = jnp.pad(idx_k_norm[i], (0, LANES - IDX_DIM)).reshape(1, LANES)
            q, k, v, qi, kiwi, qm = _proj(
                xs, attn_norm[i], tables, gk, ws,
                ["rope_scaled", "rope", "plain", "rope_idx", "kiwi", "scaled"],
                [BF16, BF16, BF16, BF16, F32, BF16])
            ki = kiwi[:, :IDX_DIM].astype(BF16)
            wt = kiwi[:, IDX_DIM:IDX_DIM + IDX_HEADS].T
            ot = _dsa(qi.T, wt, ki, q.T, k, _tile_major_t(v, tq), topk, tq=tq)
        else:
            wb = b_w_in[i - n_a]
            ws = [wb[:, :main_w].astype(BF16), wb[:, main_w:].astype(BF16)]
            q, qm = _proj(xs, attn_norm[i], tables, no_gk, ws, ["rope_scaled", "scaled"], [BF16, BF16])
            ot = _moba(q.T, k_sh, vtt_sh, kmeans, n_sel)
        o_mem = _mem_attn(qm, mk.T, mv)
        wo = w_out[i].astype(BF16)
        xs = _out_proj(xs, ot.T, o_mem, wo[:main_w], wo[main_w:])

        xs = _ffn(xs, ffn2_norm[i], ffn2_w_gate_up[i].astype(BF16), ffn2_w_down[i].astype(BF16),
                  final_norm if last else None)
    return xs[None]
```

```python
import functools

import jax
import jax.numpy as jnp
import numpy as np
from jax import lax
from jax.experimental import pallas as pl
from jax.experimental.pallas import tpu as pltpu

HEAD_DIM = 128
N_MAIN_HEADS = 12
N_KV_HEADS = 4
GROUP = N_MAIN_HEADS // N_KV_HEADS
N_MEM_HEADS = 4
IDX_HEADS = 16
IDX_DIM = 64
IDX_TOPK_MAX = 256
MOBA_BLOCK = 256
MOBA_TOPK_MAX = 3
ROPE_THETA = 10000.0
RMS_EPS = 1e-6

LANES = 128
SUBLANES = 8
VMEM_LIMIT = 56 * 1024 * 1024
NEG = -1e30
LOG2_E = 1.4426950408889634
INT_MIN = -2 ** 31
INT_MAX = 2 ** 31 - 1
CHUNK = 256
CHUNKS_PER_STEP = 4

F32 = jnp.float32
BF16 = jnp.bfloat16


def _params(*sem):
    return pltpu.CompilerParams(dimension_semantics=sem, vmem_limit_bytes=VMEM_LIMIT)


def _rms(x, gain):
    return x * lax.rsqrt(jnp.mean(x * x, axis=-1, keepdims=True) + RMS_EPS) * gain


def _ffn_body(x_ref, g_ref, wg_ref, wu_ref, wd_ref, pg_ref, o_ref, h_ref, *, final_norm):
    j = pl.program_id(1)

    @pl.when(j == 0)
    def _():
        h_ref[...] = _rms(x_ref[...], g_ref[...]).astype(BF16)
        o_ref[...] = jnp.zeros_like(o_ref)

    h = h_ref[...]
    gate = jnp.dot(h, wg_ref[...], preferred_element_type=F32)
    up = jnp.dot(h, wu_ref[...], preferred_element_type=F32)
    act = (gate * (1.0 / (1.0 + jnp.exp(-gate))) * up).astype(BF16)
    o_ref[...] += jnp.dot(act, wd_ref[...], preferred_element_type=F32)

    @pl.when(j == pl.num_programs(1) - 1)
    def _():
        y = x_ref[...] + 0.5 * o_ref[...]
        if final_norm:
            y = _rms(y, pg_ref[...])
        o_ref[...] = y


def _ffn(x, gain, w_gate_up, w_down, post_gain=None, *, tm=512, tf=512):
    s, d = x.shape
    f = w_down.shape[0]
    tm = min(tm, s)
    tf = min(tf, f)
    assert s % tm == 0 and f % tf == 0
    nf = f // tf
    final_norm = post_gain is not None
    pg = post_gain if final_norm else gain
    return pl.pallas_call(
        functools.partial(_ffn_body, final_norm=final_norm),
        grid=(s // tm, nf),
        in_specs=[
            pl.BlockSpec((tm, d), lambda i, j: (i, 0)),
            pl.BlockSpec((1, d), lambda i, j: (0, 0)),
            pl.BlockSpec((d, tf), lambda i, j: (0, j)),
            pl.BlockSpec((d, tf), lambda i, j: (0, j + nf)),
            pl.BlockSpec((tf, d), lambda i, j: (j, 0)),
            pl.BlockSpec((1, d), lambda i, j: (0, 0)),
        ],
        out_specs=pl.BlockSpec((tm, d), lambda i, j: (i, 0)),
        out_shape=jax.ShapeDtypeStruct((s, d), F32),
        scratch_shapes=[pltpu.VMEM((tm, d), BF16)],
        compiler_params=_params("parallel", "arbitrary"),
        name="ffn",
    )(x, gain.reshape(1, d), w_gate_up, w_gate_up, w_down, pg.reshape(1, d))


def _rope_heads(y, cos, sin, o_ref, scale):
    for h in range(y.shape[1] // HEAD_DIM):
        t = y[:, h * HEAD_DIM:(h + 1) * HEAD_DIM]
        r = t * cos + pltpu.roll(t, HEAD_DIM // 2, 1) * sin
        if scale != 1.0:
            r = r * scale
        o_ref[:, h * HEAD_DIM:(h + 1) * HEAD_DIM] = r.astype(o_ref.dtype)


def _rot_idx(t):
    lane = lax.broadcasted_iota(jnp.int32, t.shape, 1)
    first_half = (lane & (IDX_DIM // 2)) == 0
    return jnp.where(first_half, pltpu.roll(t, LANES - IDX_DIM // 2, 1),
                     pltpu.roll(t, IDX_DIM // 2, 1))


def _proj_body(*refs, kinds, q_scale, idx_w_scale):
    x_ref, g_ref, cos_ref, sin_ref, cosi_ref, sini_ref, gk_ref = refs[:7]
    n = len(kinds)
    w_refs = refs[7:7 + n]
    o_refs = refs[7 + n:7 + 2 * n]
    h = _rms(x_ref[...], g_ref[...]).astype(BF16)
    for kind, w_ref, o_ref in zip(kinds, w_refs, o_refs):
        y = jnp.dot(h, w_ref[...], preferred_element_type=F32)
        if kind == "plain":
            o_ref[...] = y.astype(o_ref.dtype)
        elif kind == "scaled":
            o_ref[...] = (y * q_scale).astype(o_ref.dtype)
        elif kind == "rope":
            _rope_heads(y, cos_ref[...], sin_ref[...], o_ref, 1.0)
        elif kind == "rope_scaled":
            _rope_heads(y, cos_ref[...], sin_ref[...], o_ref, q_scale * LOG2_E)
        elif kind == "rope_idx":
            cosi, sini = cosi_ref[...], sini_ref[...]
            for c in range(y.shape[1] // LANES):
                t = y[:, c * LANES:(c + 1) * LANES]
                o_ref[:, c * LANES:(c + 1) * LANES] = (t * cosi + _rot_idx(t) * sini).astype(o_ref.dtype)
        elif kind == "kiwi":
            lane = lax.broadcasted_iota(jnp.int32, y.shape, 1)
            is_k = lane < IDX_DIM
            kk = jnp.where(is_k, y, 0.0)
            ms = jnp.sum(kk * kk, axis=-1, keepdims=True) * (1.0 / IDX_DIM)
            kn = kk * lax.rsqrt(ms + RMS_EPS) * gk_ref[...]
            kr = kn * cosi_ref[...] + _rot_idx(kn) * sini_ref[...]
            o_ref[...] = jnp.where(is_k, kr, y * idx_w_scale)
        else:
            raise ValueError(kind)


def _proj(x, gain, tables, gk, weights, kinds, out_dtypes, *, tm=256):
    s, d = x.shape
    tm = min(tm, s)
    assert s % tm == 0
    cos, sin, cosi, sini = tables
    row = lambda i: (i, 0)
    const = lambda i: (0, 0)
    in_specs = [pl.BlockSpec((tm, d), row), pl.BlockSpec((1, d), const)]
    in_specs += [pl.BlockSpec((tm, LANES), row)] * 4
    in_specs += [pl.BlockSpec((1, LANES), const)]
    in_specs += [pl.BlockSpec(w.shape, const, pipeline_mode=pl.Buffered(1)) for w in weights]
    out_specs = [pl.BlockSpec((tm, w.shape[1]), row) for w in weights]
    out_shape = [jax.ShapeDtypeStruct((s, w.shape[1]), dt) for w, dt in zip(weights, out_dtypes)]
    return pl.pallas_call(
        functools.partial(_proj_body, kinds=tuple(kinds), q_scale=HEAD_DIM ** -0.5,
                          idx_w_scale=(IDX_HEADS ** -0.5) * (IDX_DIM ** -0.5)),
        grid=(s // tm,),
        in_specs=in_specs,
        out_specs=out_specs,
        out_shape=out_shape,
        compiler_params=_params("parallel"),
        name="proj",
    )(x, gain.reshape(1, d), cos, sin, cosi, sini, gk, *weights)


def _kv_shared_body(x_ref, g_ref, cos_ref, sin_ref, wk_ref, wv_ref, k_ref, v_ref, km_ref):
    h = _rms(x_ref[...], g_ref[...]).astype(BF16)
    yk = jnp.dot(h, wk_ref[...], preferred_element_type=F32)
    cos, sin = cos_ref[...], sin_ref[...]
    tm = yk.shape[0]
    for hd in range(yk.shape[1] // HEAD_DIM):
        t = yk[:, hd * HEAD_DIM:(hd + 1) * HEAD_DIM]
        r = t * cos + pltpu.roll(t, HEAD_DIM // 2, 1) * sin
        k_ref[:, hd * HEAD_DIM:(hd + 1) * HEAD_DIM] = r.astype(k_ref.dtype)
        for b in range(tm // MOBA_BLOCK):
            blk = r[b * MOBA_BLOCK:(b + 1) * MOBA_BLOCK]
            km_ref[b, :, hd * HEAD_DIM:(hd + 1) * HEAD_DIM] = (
                jnp.sum(blk, axis=0, keepdims=True) * (1.0 / MOBA_BLOCK))
    v_ref[...] = jnp.dot(h, wv_ref[...], preferred_element_type=F32).astype(v_ref.dtype)


def _kv_shared(x, gain, cos, sin, wk, wv, *, tm=256):
    s, d = x.shape
    assert s % MOBA_BLOCK == 0
    tm = min(tm, s)
    assert tm % MOBA_BLOCK == 0 and s % tm == 0
    nkv = wk.shape[1]
    row = lambda i: (i, 0)
    const = lambda i: (0, 0)
    return pl.pallas_call(
        _kv_shared_body,
        grid=(s // tm,),
        in_specs=[pl.BlockSpec((tm, d), row), pl.BlockSpec((1, d), const),
                  pl.BlockSpec((tm, LANES), row), pl.BlockSpec((tm, LANES), row),
                  pl.BlockSpec(wk.shape, const, pipeline_mode=pl.Buffered(1)),
                  pl.BlockSpec(wv.shape, const, pipeline_mode=pl.Buffered(1))],
        out_specs=[pl.BlockSpec((tm, nkv), row), pl.BlockSpec((tm, nkv), row),
                   pl.BlockSpec((tm // MOBA_BLOCK, 1, nkv), lambda i: (i, 0, 0))],
        out_shape=[jax.ShapeDtypeStruct((s, nkv), BF16), jax.ShapeDtypeStruct((s, nkv), BF16),
                   jax.ShapeDtypeStruct((s // MOBA_BLOCK, 1, nkv), F32)],
        compiler_params=_params("parallel"),
        name="kv_shared",
    )(x, gain.reshape(1, d), cos, sin, wk, wv)


def _load_q_group(qt_ref, qall_ref):
    tq = qt_ref.shape[1]
    for hh in range(GROUP):
        qall_ref[:, hh * tq:(hh + 1) * tq] = qt_ref[hh * HEAD_DIM:(hh + 1) * HEAD_DIM, :]


def _softmax_init(m_ref, l_ref, acc_ref):
    m_ref[...] = jnp.full(m_ref.shape, NEG, F32)
    l_ref[...] = jnp.zeros(l_ref.shape, F32)
    acc_ref[...] = jnp.zeros(acc_ref.shape, F32)


def _attend_chunks(chunk_ids, bias_fn, qall_ref, k_ref, vt_ref, s_ref, m_ref, l_ref, acc_ref):
    q_all = qall_ref[...]
    m_old = m_ref[...]
    m_new = m_old
    for j, c in enumerate(chunk_ids):
        r0 = pl.multiple_of(c * CHUNK, CHUNK)
        s = jnp.dot(k_ref[pl.ds(r0, CHUNK), :], q_all, preferred_element_type=F32)
        b = bias_fn(c)
        tq = b.shape[1]
        if tq == s.shape[1]:
            s = s + b
        else:
            s = jnp.concatenate([s[:, hh * tq:(hh + 1) * tq] + b for hh in range(GROUP)], axis=1)
        s_ref[j * CHUNK:(j + 1) * CHUNK, :] = s
        m_new = jnp.maximum(m_new, jnp.max(s, axis=0, keepdims=True))
    alpha = jnp.exp2(m_old - m_new)
    l_add = jnp.zeros_like(m_old)
    pv = None
    for j, c in enumerate(chunk_ids):
        p = jnp.exp2(s_ref[j * CHUNK:(j + 1) * CHUNK, :] - m_new)
        l_add = l_add + jnp.sum(p, axis=0, keepdims=True)
        d = jnp.dot(vt_ref[c], p.astype(BF16), preferred_element_type=F32)
        pv = d if pv is None else pv + d
    m_ref[...] = m_new
    l_ref[...] = alpha * l_ref[...] + l_add
    acc_ref[...] = alpha * acc_ref[...] + pv


def _attend_range(n_chunks, bias_fn, *refs):
    def step(it, carry):
        _attend_chunks([it * CHUNKS_PER_STEP + c for c in range(CHUNKS_PER_STEP)], bias_fn, *refs)
        return carry

    lax.fori_loop(0, pl.cdiv(n_chunks, CHUNKS_PER_STEP), step, 0)


def _softmax_finish(o_ref, l_ref, acc_ref):
    tq = o_ref.shape[1]
    out = acc_ref[...] / l_ref[...]
    for hh in range(GROUP):
        o_ref[hh * HEAD_DIM:(hh + 1) * HEAD_DIM, :] = out[:, hh * tq:(hh + 1) * tq].astype(o_ref.dtype)


def _dsa_body(qit_ref, wt_ref, ki_ref, qt_ref, k_ref, vt_ref, o_ref,
              keys_ref, gmax_ref, thr_ref, qall_ref, s_ref, m_ref, l_ref, acc_ref, *, topk):
    i = pl.program_id(0)
    g = pl.program_id(1)
    tq = CHUNK
    n_chunks = i + 1
    sub = 128

    def causal(c, rows, row_off=0):
        kpos = c * CHUNK + row_off + lax.broadcasted_iota(jnp.int32, (rows, tq), 0)
        qpos = i * tq + lax.broadcasted_iota(jnp.int32, (rows, tq), 1)
        return kpos <= qpos

    @pl.when(g == 0)
    def _():
        gmax_ref[...] = jnp.full(gmax_ref.shape, INT_MIN, jnp.int32)

        def score_chunk(c, carry):
            for part in range(CHUNK // sub):
                r0 = pl.multiple_of(c * CHUNK + part * sub, sub)
                kit = ki_ref[pl.ds(r0, sub), :]
                sc = jnp.zeros((sub, tq), F32)
                for h in range(IDX_HEADS):
                    d = jnp.dot(kit, qit_ref[h * IDX_DIM:(h + 1) * IDX_DIM, :], preferred_element_type=F32)
                    sc = sc + wt_ref[h:h + 1, :] * jnp.maximum(d, 0.0)
                sc = jnp.where(causal(c, sub, part * sub), sc, -jnp.inf)
                bits = pltpu.bitcast(sc, jnp.int32)
                key = bits ^ ((bits >> 31) & INT_MAX)
                keys_ref[pl.ds(r0, sub), :] = key
                gsl = slice(part * sub, (part + 1) * sub)
                gmax_ref[gsl, :] = jnp.maximum(gmax_ref[gsl, :], key)
            return carry

        lax.fori_loop(0, n_chunks, score_chunk, 0)

        def clear_chunk(c, carry):
            keys_ref[pl.ds(pl.multiple_of(c * CHUNK, CHUNK), CHUNK), :] = jnp.full((CHUNK, tq), INT_MIN, jnp.int32)
            return carry

        lax.fori_loop(n_chunks, pl.cdiv(n_chunks, CHUNKS_PER_STEP) * CHUNKS_PER_STEP, clear_chunk, 0)

        def count_ge(t):
            def count_chunk(c, acc):
                blk = keys_ref[pl.ds(pl.multiple_of(c * CHUNK, CHUNK), CHUNK), :]
                ge = jnp.where(blk >= t, 1, 0).astype(jnp.int32)
                return acc + jnp.sum(ge.reshape(CHUNK // SUBLANES, SUBLANES, tq), axis=0)

            part = lax.fori_loop(0, n_chunks, count_chunk, jnp.zeros((SUBLANES, tq), jnp.int32))
            return jnp.sum(part, axis=0, keepdims=True)

        gm = gmax_ref[...]
        lo = jnp.min(gm, axis=0, keepdims=True)
        top = jnp.max(gm, axis=0, keepdims=True)
        hi = jnp.where(top == INT_MAX, top, top + 1)
        done = (count_ge(lo) == topk).astype(jnp.int32)

        def unfinished(carry):
            it, lo, hi, done = carry
            return jnp.logical_and(it < 34, jnp.min(done) == 0)

        def bisect(carry):
            it, lo, hi, done = carry
            mid = (lo >> 1) + (hi >> 1) + (lo & hi & 1)
            cnt = count_ge(mid)
            active = done == 0
            ok = cnt >= topk
            lo = jnp.where(active & ok, mid, lo)
            hi = jnp.where(active & jnp.logical_not(ok), mid, hi)
            closed = hi <= lo + 1
            done = jnp.where(active & ((cnt == topk) | closed), 1, done)
            return it + 1, lo, hi, done

        _, lo, _, _ = lax.while_loop(unfinished, bisect, (jnp.int32(0), lo, hi, done))
        thr_ref[...] = lo

    _load_q_group(qt_ref, qall_ref)
    _softmax_init(m_ref, l_ref, acc_ref)
    thr = thr_ref[...]

    def bias(c):
        r0 = pl.multiple_of(c * CHUNK, CHUNK)
        sel = (keys_ref[pl.ds(r0, CHUNK), :] >= thr) & causal(c, CHUNK)
        return jnp.where(sel, 0.0, NEG)

    _attend_range(n_chunks, bias, qall_ref, k_ref, vt_ref, s_ref, m_ref, l_ref, acc_ref)
    _softmax_finish(o_ref, l_ref, acc_ref)


def _dsa(qit, wt, ki, qt, k, vtt, topk):
    s = ki.shape[0]
    tq = CHUNK
    assert s % (tq * CHUNKS_PER_STEP) == 0 and topk <= CHUNK
    gw = GROUP * HEAD_DIM
    return pl.pallas_call(
        functools.partial(_dsa_body, topk=topk),
        grid=(s // tq, N_KV_HEADS),
        in_specs=[
            pl.BlockSpec((IDX_HEADS * IDX_DIM, tq), lambda i, g: (0, i)),
            pl.BlockSpec((IDX_HEADS, tq), lambda i, g: (0, i)),
            pl.BlockSpec((s, IDX_DIM), lambda i, g: (0, 0), pipeline_mode=pl.Buffered(1)),
            pl.BlockSpec((gw, tq), lambda i, g: (g, i)),
            pl.BlockSpec((s, HEAD_DIM), lambda i, g: (0, g)),
            pl.BlockSpec((s // tq, HEAD_DIM, tq), lambda i, g: (0, g, 0)),
        ],
        out_specs=pl.BlockSpec((gw, tq), lambda i, g: (g, i)),
        out_shape=jax.ShapeDtypeStruct((N_MAIN_HEADS * HEAD_DIM, s), BF16),
        scratch_shapes=[
            pltpu.VMEM((s, tq), jnp.int32),
            pltpu.VMEM((CHUNK, tq), jnp.int32),
            pltpu.VMEM((1, tq), jnp.int32),
            pltpu.VMEM((HEAD_DIM, GROUP * tq), BF16),
            pltpu.VMEM((CHUNKS_PER_STEP * CHUNK, GROUP * tq), F32),
            pltpu.VMEM((1, GROUP * tq), F32),
            pltpu.VMEM((1, GROUP * tq), F32),
            pltpu.VMEM((HEAD_DIM, GROUP * tq), F32),
        ],
        compiler_params=_params("arbitrary", "arbitrary"),
        name="dsa",
    )(qit, wt, ki, qt, k, vtt)


def _moba_body(qt_ref, k_ref, vt_ref, km_ref, o_ref, bias_ref, qall_ref, s_ref, m_ref, l_ref, acc_ref, *, n_sel):
    cur = pl.program_id(0)
    tq = CHUNK
    nb = km_ref.shape[0]
    width = GROUP * tq
    _load_q_group(qt_ref, qall_ref)

    blk_id = lax.broadcasted_iota(jnp.int32, (nb, width), 0)
    past = blk_id < cur
    gate = jnp.dot(km_ref[...], qall_ref[...], preferred_element_type=F32)
    gate = jnp.where(past, gate, -jnp.inf)
    chosen = jnp.zeros((nb, width), jnp.bool_)
    for _ in range(n_sel):
        best = jnp.max(gate, axis=0, keepdims=True)
        first = jnp.min(jnp.where(gate == best, blk_id, nb), axis=0, keepdims=True)
        pick = blk_id == first
        chosen = chosen | pick
        gate = jnp.where(pick, -jnp.inf, gate)
    bias_ref[...] = jnp.where(chosen & past, 0.0, NEG)

    _softmax_init(m_ref, l_ref, acc_ref)
    refs = (qall_ref, k_ref, vt_ref, s_ref, m_ref, l_ref, acc_ref)
    _attend_range(cur, lambda b: bias_ref[pl.ds(b, 1), :], *refs)

    tri = (lax.broadcasted_iota(jnp.int32, (tq, tq), 0) <= lax.broadcasted_iota(jnp.int32, (tq, tq), 1))
    own_bias = jnp.where(tri, 0.0, NEG)
    _attend_chunks([cur], lambda b: own_bias, *refs)
    _softmax_finish(o_ref, l_ref, acc_ref)


def _moba(qt, k, vtt, kmeans, n_sel):
    s = k.shape[0]
    assert MOBA_BLOCK == CHUNK and s % (CHUNK * CHUNKS_PER_STEP) == 0
    tq = CHUNK
    nb = s // tq
    gw = GROUP * HEAD_DIM
    return pl.pallas_call(
        functools.partial(_moba_body, n_sel=n_sel),
        grid=(nb, N_KV_HEADS),
        in_specs=[
            pl.BlockSpec((gw, tq), lambda i, g: (g, i)),
            pl.BlockSpec((s, HEAD_DIM), lambda i, g: (0, g)),
            pl.BlockSpec((nb, HEAD_DIM, tq), lambda i, g: (0, g, 0)),
            pl.BlockSpec((nb, HEAD_DIM), lambda i, g: (0, g)),
        ],
        out_specs=pl.BlockSpec((gw, tq), lambda i, g: (g, i)),
        out_shape=jax.ShapeDtypeStruct((N_MAIN_HEADS * HEAD_DIM, s), BF16),
        scratch_shapes=[
            pltpu.VMEM((nb, GROUP * tq), F32),
            pltpu.VMEM((HEAD_DIM, GROUP * tq), BF16),
            pltpu.VMEM((CHUNKS_PER_STEP * CHUNK, GROUP * tq), F32),
            pltpu.VMEM((1, GROUP * tq), F32),
            pltpu.VMEM((1, GROUP * tq), F32),
            pltpu.VMEM((HEAD_DIM, GROUP * tq), F32),
        ],
        compiler_params=_params("parallel", "arbitrary"),
        name="moba",
    )(qt, k, vtt, kmeans)


def _mem_attn_body(q_ref, kt_ref, v_ref, o_ref):
    for h in range(N_MEM_HEADS):
        sl = slice(h * HEAD_DIM, (h + 1) * HEAD_DIM)
        s = jnp.dot(q_ref[:, sl], kt_ref[sl, :], preferred_element_type=F32)
        p = jnp.exp(s - jnp.max(s, axis=-1, keepdims=True))
        o = jnp.dot(p.astype(BF16), v_ref[:, sl], preferred_element_type=F32)
        o_ref[:, sl] = (o / jnp.sum(p, axis=-1, keepdims=True)).astype(o_ref.dtype)


def _mem_attn(qm, kmt, vm, *, tq=512):
    s, w = qm.shape
    m = vm.shape[0]
    tq = min(tq, s)
    assert s % tq == 0
    return pl.pallas_call(
        _mem_attn_body,
        grid=(s // tq,),
        in_specs=[pl.BlockSpec((tq, w), lambda i: (i, 0)),
                  pl.BlockSpec((w, m), lambda i: (0, 0)),
                  pl.BlockSpec((m, w), lambda i: (0, 0))],
        out_specs=pl.BlockSpec((tq, w), lambda i: (i, 0)),
        out_shape=jax.ShapeDtypeStruct((s, w), BF16),
        compiler_params=_params("parallel"),
        name="mem_attn",
    )(qm, kmt, vm)


def _out_proj_body(x_ref, om_ref, oq_ref, w1_ref, w2_ref, o_ref):
    o_ref[...] = (x_ref[...]
                  + jnp.dot(om_ref[...], w1_ref[...], preferred_element_type=F32)
                  + jnp.dot(oq_ref[...], w2_ref[...], preferred_element_type=F32))


def _out_proj(x, o_main, o_mem, w1, w2, *, tm=512):
    s, d = x.shape
    tm = min(tm, s)
    assert s % tm == 0
    row = lambda i: (i, 0)
    const = lambda i: (0, 0)
    return pl.pallas_call(
        _out_proj_body,
        grid=(s // tm,),
        in_specs=[pl.BlockSpec((tm, d), row),
                  pl.BlockSpec((tm, o_main.shape[1]), row),
                  pl.BlockSpec((tm, o_mem.shape[1]), row),
                  pl.BlockSpec(w1.shape, const, pipeline_mode=pl.Buffered(1)),
                  pl.BlockSpec(w2.shape, const, pipeline_mode=pl.Buffered(1))],
        out_specs=pl.BlockSpec((tm, d), row),
        out_shape=jax.ShapeDtypeStruct((s, d), F32),
        compiler_params=_params("parallel"),
        name="out_proj",
    )(x, o_main, o_mem, w1, w2)


def _rope_tables(positions, dim):
    inv = 1.0 / (ROPE_THETA ** (jnp.arange(0, dim, 2, dtype=F32) / dim))
    ang = positions.astype(F32)[:, None] * inv
    c, s = jnp.cos(ang), jnp.sin(ang)
    reps = LANES // dim
    return jnp.tile(jnp.concatenate([c, c], -1), (1, reps)), jnp.tile(jnp.concatenate([-s, s], -1), (1, reps))


def _tile_major_t(v, tile):
    s, w = v.shape
    return jnp.transpose(v.reshape(s // tile, tile, w), (0, 2, 1))


def kernel(x, mem, positions, ffn1_norm, ffn1_w_gate_up, ffn1_w_down, attn_norm, mem_norm, a_w_in, idx_k_norm, b_w_in, w_mem_kv, w_out, ffn2_norm, ffn2_w_gate_up, ffn2_w_down, kv_norm, w_kv_shared, final_norm):
    b, s, d = x.shape
    assert b == 1 and mem.shape[0] == 1
    depth = ffn1_norm.shape[0]
    n_a = a_w_in.shape[0]
    main_w = N_MAIN_HEADS * HEAD_DIM
    kv_w = N_KV_HEADS * HEAD_DIM
    idx_w = IDX_HEADS * IDX_DIM
    mem_w = N_MEM_HEADS * HEAD_DIM
    topk = min(IDX_TOPK_MAX, s // 4)
    nb = s // MOBA_BLOCK
    n_sel = min(MOBA_TOPK_MAX, max(nb - 1, 1))

    cos, sin = _rope_tables(positions[0], HEAD_DIM)
    cosi, sini = _rope_tables(positions[0], IDX_DIM)
    tables = (cos, sin, cosi, sini)
    mem_tables = tuple(t[:mem.shape[1]] for t in tables)
    no_gk = jnp.zeros((1, LANES), F32)

    xs = x[0]
    mem2 = mem[0]
    k_sh = vtt_sh = kmeans = None
    for i in range(depth):
        if i == n_a:
            wk = w_kv_shared[:, :kv_w].astype(BF16)
            wv = w_kv_shared[:, kv_w:].astype(BF16)
            k_sh, v_sh, km = _kv_shared(xs, kv_norm, cos, sin, wk, wv)
            vtt_sh = _tile_major_t(v_sh, CHUNK)
            kmeans = km.reshape(nb, kv_w).astype(BF16)

        last = i == depth - 1
        xs = _ffn(xs, ffn1_norm[i], ffn1_w_gate_up[i].astype(BF16), ffn1_w_down[i].astype(BF16))

        wm = w_mem_kv[i].astype(BF16)
        mk, mv = _proj(mem2, mem_norm[i], mem_tables, no_gk, [wm[:, :mem_w], wm[:, mem_w:]],
                       ["plain", "plain"], [BF16, BF16])
        if i < n_a:
            wa = a_w_in[i]
            o0 = 0
            ws = []
            for width in (main_w, kv_w, kv_w, idx_w, IDX_DIM + IDX_HEADS, mem_w):
                ws.append(wa[:, o0:o0 + width])
                o0 += width
            w_kiwi = jnp.pad(ws[4], ((0, 0), (0, LANES - ws[4].shape[1])))
            ws = [w.astype(BF16) for w in (ws[0], ws[1], ws[2], ws[3], w_kiwi, ws[5])]
            gk = jnp.pad(idx_k_norm[i], (0, LANES - IDX_DIM)).reshape(1, LANES)
            q, k, v, qi, kiwi, qm = _proj(
                xs, attn_norm[i], tables, gk, ws,
                ["rope_scaled", "rope", "plain", "rope_idx", "kiwi", "scaled"],
                [BF16, BF16, BF16, BF16, F32, BF16])
            ki = kiwi[:, :IDX_DIM].astype(BF16)
            wt = kiwi[:, IDX_DIM:IDX_DIM + IDX_HEADS].T
            ot = _dsa(qi.T, wt, ki, q.T, k, _tile_major_t(v, CHUNK), topk)
        else:
            wb = b_w_in[i - n_a]
            ws = [wb[:, :main_w].astype(BF16), wb[:, main_w:].astype(BF16)]
            q, qm = _proj(xs, attn_norm[i], tables, no_gk, ws, ["rope_scaled", "scaled"], [BF16, BF16])
            ot = _moba(q.T, k_sh, vtt_sh, kmeans, n_sel)
        o_mem = _mem_attn(qm, mk.T, mv)
        wo = w_out[i].astype(BF16)
        xs = _out_proj(xs, ot.T, o_mem, wo[:main_w], wo[main_w:])

        xs = _ffn(xs, ffn2_norm[i], ffn2_w_gate_up[i].astype(BF16), ffn2_w_down[i].astype(BF16),
                  final_norm if last else None)
    return xs[None]
```

```python
import functools

import jax
import jax.numpy as jnp
import numpy as np
from jax import lax
from jax.experimental import pallas as pl
from jax.experimental.pallas import tpu as pltpu

HEAD_DIM = 128
N_MAIN_HEADS = 12
N_KV_HEADS = 4
GROUP = N_MAIN_HEADS // N_KV_HEADS
N_MEM_HEADS = 4
IDX_HEADS = 16
IDX_DIM = 64
IDX_TOPK_MAX = 256
MOBA_BLOCK = 256
MOBA_TOPK_MAX = 3
ROPE_THETA = 10000.0
RMS_EPS = 1e-6

LANES = 128
SUBLANES = 8
VMEM_LIMIT = 56 * 1024 * 1024
NEG = -1e30
LOG2_E = 1.4426950408889634
INT_MIN = -2 ** 31
INT_MAX = 2 ** 31 - 1
CHUNK = 256
CHUNKS_PER_STEP = 4
ONES_ROWS = 16
V_ROWS = HEAD_DIM + ONES_ROWS

F32 = jnp.float32
BF16 = jnp.bfloat16


def _params(*sem):
    return pltpu.CompilerParams(dimension_semantics=sem, vmem_limit_bytes=VMEM_LIMIT)


def _rms(x, gain):
    return x * lax.rsqrt(jnp.mean(x * x, axis=-1, keepdims=True) + RMS_EPS) * gain


def _ffn_body(x_ref, g_ref, wg_ref, wu_ref, wd_ref, pg_ref, o_ref, h_ref, *, final_norm):
    j = pl.program_id(1)

    @pl.when(j == 0)
    def _():
        h_ref[...] = _rms(x_ref[...], g_ref[...]).astype(BF16)
        o_ref[...] = jnp.zeros_like(o_ref)

    h = h_ref[...]
    gate = jnp.dot(h, wg_ref[...], preferred_element_type=F32)
    up = jnp.dot(h, wu_ref[...], preferred_element_type=F32)
    act = (gate * (1.0 / (1.0 + jnp.exp(-gate))) * up).astype(BF16)
    o_ref[...] += jnp.dot(act, wd_ref[...], preferred_element_type=F32)

    @pl.when(j == pl.num_programs(1) - 1)
    def _():
        y = x_ref[...] + 0.5 * o_ref[...]
        if final_norm:
            y = _rms(y, pg_ref[...])
        o_ref[...] = y


def _ffn(x, gain, w_gate_up, w_down, post_gain=None, *, tm=512, tf=512):
    s, d = x.shape
    f = w_down.shape[0]
    tm = min(tm, s)
    tf = min(tf, f)
    assert s % tm == 0 and f % tf == 0
    nf = f // tf
    final_norm = post_gain is not None
    pg = post_gain if final_norm else gain
    return pl.pallas_call(
        functools.partial(_ffn_body, final_norm=final_norm),
        grid=(s // tm, nf),
        in_specs=[
            pl.BlockSpec((tm, d), lambda i, j: (i, 0)),
            pl.BlockSpec((1, d), lambda i, j: (0, 0)),
            pl.BlockSpec((d, tf), lambda i, j: (0, j)),
            pl.BlockSpec((d, tf), lambda i, j: (0, j + nf)),
            pl.BlockSpec((tf, d), lambda i, j: (j, 0)),
            pl.BlockSpec((1, d), lambda i, j: (0, 0)),
        ],
        out_specs=pl.BlockSpec((tm, d), lambda i, j: (i, 0)),
        out_shape=jax.ShapeDtypeStruct((s, d), F32),
        scratch_shapes=[pltpu.VMEM((tm, d), BF16)],
        compiler_params=_params("parallel", "arbitrary"),
        name="ffn",
    )(x, gain.reshape(1, d), w_gate_up, w_gate_up, w_down, pg.reshape(1, d))


def _rope_heads(y, cos, sin, o_ref, scale):
    for h in range(y.shape[1] // HEAD_DIM):
        t = y[:, h * HEAD_DIM:(h + 1) * HEAD_DIM]
        r = t * cos + pltpu.roll(t, HEAD_DIM // 2, 1) * sin
        if scale != 1.0:
            r = r * scale
        o_ref[:, h * HEAD_DIM:(h + 1) * HEAD_DIM] = r.astype(o_ref.dtype)


def _rot_idx(t):
    lane = lax.broadcasted_iota(jnp.int32, t.shape, 1)
    first_half = (lane & (IDX_DIM // 2)) == 0
    return jnp.where(first_half, pltpu.roll(t, LANES - IDX_DIM // 2, 1),
                     pltpu.roll(t, IDX_DIM // 2, 1))


def _proj_body(*refs, kinds, q_scale, idx_w_scale):
    x_ref, g_ref, cos_ref, sin_ref, cosi_ref, sini_ref, gk_ref = refs[:7]
    n = len(kinds)
    w_refs = refs[7:7 + n]
    o_refs = refs[7 + n:7 + 2 * n]
    h = _rms(x_ref[...], g_ref[...]).astype(BF16)
    for kind, w_ref, o_ref in zip(kinds, w_refs, o_refs):
        y = jnp.dot(h, w_ref[...], preferred_element_type=F32)
        if kind == "plain":
            o_ref[...] = y.astype(o_ref.dtype)
        elif kind == "scaled":
            o_ref[...] = (y * q_scale).astype(o_ref.dtype)
        elif kind == "rope":
            _rope_heads(y, cos_ref[...], sin_ref[...], o_ref, 1.0)
        elif kind == "rope_scaled":
            _rope_heads(y, cos_ref[...], sin_ref[...], o_ref, q_scale * LOG2_E)
        elif kind == "rope_idx":
            cosi, sini = cosi_ref[...], sini_ref[...]
            for c in range(y.shape[1] // LANES):
                t = y[:, c * LANES:(c + 1) * LANES]
                o_ref[:, c * LANES:(c + 1) * LANES] = (t * cosi + _rot_idx(t) * sini).astype(o_ref.dtype)
        elif kind == "kiwi":
            lane = lax.broadcasted_iota(jnp.int32, y.shape, 1)
            is_k = lane < IDX_DIM
            kk = jnp.where(is_k, y, 0.0)
            ms = jnp.sum(kk * kk, axis=-1, keepdims=True) * (1.0 / IDX_DIM)
            kn = kk * lax.rsqrt(ms + RMS_EPS) * gk_ref[...]
            kr = kn * cosi_ref[...] + _rot_idx(kn) * sini_ref[...]
            o_ref[...] = jnp.where(is_k, kr, y * idx_w_scale)
        else:
            raise ValueError(kind)


def _proj(x, gain, tables, gk, weights, kinds, out_dtypes, *, tm=256):
    s, d = x.shape
    tm = min(tm, s)
    assert s % tm == 0
    cos, sin, cosi, sini = tables
    row = lambda i: (i, 0)
    const = lambda i: (0, 0)
    in_specs = [pl.BlockSpec((tm, d), row), pl.BlockSpec((1, d), const)]
    in_specs += [pl.BlockSpec((tm, LANES), row)] * 4
    in_specs += [pl.BlockSpec((1, LANES), const)]
    in_specs += [pl.BlockSpec(w.shape, const, pipeline_mode=pl.Buffered(1)) for w in weights]
    out_specs = [pl.BlockSpec((tm, w.shape[1]), row) for w in weights]
    out_shape = [jax.ShapeDtypeStruct((s, w.shape[1]), dt) for w, dt in zip(weights, out_dtypes)]
    return pl.pallas_call(
        functools.partial(_proj_body, kinds=tuple(kinds), q_scale=HEAD_DIM ** -0.5,
                          idx_w_scale=(IDX_HEADS ** -0.5) * (IDX_DIM ** -0.5)),
        grid=(s // tm,),
        in_specs=in_specs,
        out_specs=out_specs,
        out_shape=out_shape,
        compiler_params=_params("parallel"),
        name="proj",
    )(x, gain.reshape(1, d), cos, sin, cosi, sini, gk, *weights)


def _kv_shared_body(x_ref, g_ref, cos_ref, sin_ref, wk_ref, wv_ref, k_ref, v_ref, km_ref):
    h = _rms(x_ref[...], g_ref[...]).astype(BF16)
    yk = jnp.dot(h, wk_ref[...], preferred_element_type=F32)
    cos, sin = cos_ref[...], sin_ref[...]
    tm = yk.shape[0]
    for hd in range(yk.shape[1] // HEAD_DIM):
        t = yk[:, hd * HEAD_DIM:(hd + 1) * HEAD_DIM]
        r = t * cos + pltpu.roll(t, HEAD_DIM // 2, 1) * sin
        k_ref[:, hd * HEAD_DIM:(hd + 1) * HEAD_DIM] = r.astype(k_ref.dtype)
        for b in range(tm // MOBA_BLOCK):
            blk = r[b * MOBA_BLOCK:(b + 1) * MOBA_BLOCK]
            km_ref[b, :, hd * HEAD_DIM:(hd + 1) * HEAD_DIM] = (
                jnp.sum(blk, axis=0, keepdims=True) * (1.0 / MOBA_BLOCK))
    v_ref[...] = jnp.dot(h, wv_ref[...], preferred_element_type=F32).astype(v_ref.dtype)


def _kv_shared(x, gain, cos, sin, wk, wv, *, tm=256):
    s, d = x.shape
    assert s % MOBA_BLOCK == 0
    tm = min(tm, s)
    assert tm % MOBA_BLOCK == 0 and s % tm == 0
    nkv = wk.shape[1]
    row = lambda i: (i, 0)
    const = lambda i: (0, 0)
    return pl.pallas_call(
        _kv_shared_body,
        grid=(s // tm,),
        in_specs=[pl.BlockSpec((tm, d), row), pl.BlockSpec((1, d), const),
                  pl.BlockSpec((tm, LANES), row), pl.BlockSpec((tm, LANES), row),
                  pl.BlockSpec(wk.shape, const, pipeline_mode=pl.Buffered(1)),
                  pl.BlockSpec(wv.shape, const, pipeline_mode=pl.Buffered(1))],
        out_specs=[pl.BlockSpec((tm, nkv), row), pl.BlockSpec((tm, nkv), row),
                   pl.BlockSpec((tm // MOBA_BLOCK, 1, nkv), lambda i: (i, 0, 0))],
        out_shape=[jax.ShapeDtypeStruct((s, nkv), BF16), jax.ShapeDtypeStruct((s, nkv), BF16),
                   jax.ShapeDtypeStruct((s // MOBA_BLOCK, 1, nkv), F32)],
        compiler_params=_params("parallel"),
        name="kv_shared",
    )(x, gain.reshape(1, d), cos, sin, wk, wv)


def _load_q_group(qt_ref, qall_ref):
    tq = qt_ref.shape[1]
    for hh in range(GROUP):
        qall_ref[:, hh * tq:(hh + 1) * tq] = qt_ref[hh * HEAD_DIM:(hh + 1) * HEAD_DIM, :]


def _softmax_init(m_ref, acc_ref):
    m_ref[...] = jnp.full(m_ref.shape, NEG, F32)
    acc_ref[...] = jnp.zeros(acc_ref.shape, F32)


def _add_group_bias(s, b):
    tq = b.shape[1]
    return jnp.concatenate([s[:, hh * tq:(hh + 1) * tq] + b for hh in range(GROUP)], axis=1)


def _logits_pass(chunk_ids, logits_fn, s_ref, cmax_ref):
    cmax = None
    for j, c in enumerate(chunk_ids):
        s = logits_fn(c)
        s_ref[j * CHUNK:(j + 1) * CHUNK, :] = s
        cm = jnp.max(s, axis=0, keepdims=True)
        cmax = cm if cmax is None else jnp.maximum(cmax, cm)
    cmax_ref[...] = cmax


def _softmax_pass(chunk_ids, vt_ref, s_ref, cmax_ref, m_ref, acc_ref):
    m_old = m_ref[...]
    m_new = jnp.maximum(m_old, cmax_ref[...])
    alpha = jnp.exp2(m_old - m_new)
    pv = None
    for j, c in enumerate(chunk_ids):
        p = jnp.exp2(s_ref[j * CHUNK:(j + 1) * CHUNK, :] - m_new)
        d = jnp.dot(vt_ref[c], p.astype(BF16), preferred_element_type=F32)
        pv = d if pv is None else pv + d
    m_ref[...] = m_new
    acc_ref[...] = alpha * acc_ref[...] + pv


def _attend_range(n_chunks, logits_fn, vt_ref, s_refs, cmax_refs, m_ref, acc_ref):
    n_steps = pl.cdiv(n_chunks, CHUNKS_PER_STEP)
    last = jnp.maximum(n_steps - 1, 0)

    def chunks(step):
        return [step * CHUNKS_PER_STEP + j for j in range(CHUNKS_PER_STEP)]

    def logits(step, slot):
        _logits_pass(chunks(jnp.minimum(step, last)), logits_fn, s_refs[slot], cmax_refs[slot])

    def softmax(step, slot):
        _softmax_pass(chunks(step), vt_ref, s_refs[slot], cmax_refs[slot], m_ref, acc_ref)

    logits(0, 0)

    def pair(u, carry):
        t = 2 * u
        logits(t + 1, 1)
        softmax(t, 0)

        @pl.when(t + 1 < n_steps)
        def _():
            logits(t + 2, 0)
            softmax(t + 1, 1)

        return carry

    lax.fori_loop(0, pl.cdiv(n_steps, 2), pair, 0)


def _attend_single(c, logits_fn, vt_ref, s_refs, cmax_refs, m_ref, acc_ref):
    _logits_pass([c], logits_fn, s_refs[0], cmax_refs[0])
    _softmax_pass([c], vt_ref, s_refs[0], cmax_refs[0], m_ref, acc_ref)


def _softmax_finish(o_ref, acc_ref):
    tq = o_ref.shape[1]
    out = acc_ref[:HEAD_DIM, :] / acc_ref[HEAD_DIM:HEAD_DIM + 1, :]
    for hh in range(GROUP):
        o_ref[hh * HEAD_DIM:(hh + 1) * HEAD_DIM, :] = out[:, hh * tq:(hh + 1) * tq].astype(o_ref.dtype)


def _dsa_body(qit_ref, wt_ref, ki_ref, qt_ref, k_ref, vt_ref, o_ref,
              keys_ref, gmax_ref, thr_ref, qall_ref, sa_ref, sb_ref, cma_ref, cmb_ref,
              m_ref, acc_ref, *, topk):
    i = pl.program_id(0)
    g = pl.program_id(1)
    tq = CHUNK
    n_chunks = i + 1
    sub = 128

    def causal(c, rows, row_off=0):
        kpos = c * CHUNK + row_off + lax.broadcasted_iota(jnp.int32, (rows, tq), 0)
        qpos = i * tq + lax.broadcasted_iota(jnp.int32, (rows, tq), 1)
        return kpos <= qpos

    @pl.when(g == 0)
    def _():
        gmax_ref[...] = jnp.full(gmax_ref.shape, INT_MIN, jnp.int32)

        def score_chunk(c, carry):
            for part in range(CHUNK // sub):
                r0 = pl.multiple_of(c * CHUNK + part * sub, sub)
                kit = ki_ref[pl.ds(r0, sub), :]
                sc = jnp.zeros((sub, tq), F32)
                for h in range(IDX_HEADS):
                    d = jnp.dot(kit, qit_ref[h * IDX_DIM:(h + 1) * IDX_DIM, :], preferred_element_type=F32)
                    sc = sc + wt_ref[h:h + 1, :] * jnp.maximum(d, 0.0)
                sc = jnp.where(causal(c, sub, part * sub), sc, -jnp.inf)
                bits = pltpu.bitcast(sc, jnp.int32)
                key = bits ^ ((bits >> 31) & INT_MAX)
                keys_ref[pl.ds(r0, sub), :] = key
                gsl = slice(part * sub, (part + 1) * sub)
                gmax_ref[gsl, :] = jnp.maximum(gmax_ref[gsl, :], key)
            return carry

        lax.fori_loop(0, n_chunks, score_chunk, 0)

        def clear_chunk(c, carry):
            keys_ref[pl.ds(pl.multiple_of(c * CHUNK, CHUNK), CHUNK), :] = jnp.full((CHUNK, tq), INT_MIN, jnp.int32)
            return carry

        lax.fori_loop(n_chunks, pl.cdiv(n_chunks, CHUNKS_PER_STEP) * CHUNKS_PER_STEP, clear_chunk, 0)

        def count_ge(t):
            def count_chunk(c, acc):
                blk = keys_ref[pl.ds(pl.multiple_of(c * CHUNK, CHUNK), CHUNK), :]
                ge = jnp.where(blk >= t, 1, 0).astype(jnp.int32)
                return acc + jnp.sum(ge.reshape(CHUNK // SUBLANES, SUBLANES, tq), axis=0)

            part = lax.fori_loop(0, n_chunks, count_chunk, jnp.zeros((SUBLANES, tq), jnp.int32))
            return jnp.sum(part, axis=0, keepdims=True)

        gm = gmax_ref[...]
        lo = jnp.min(gm, axis=0, keepdims=True)
        top = jnp.max(gm, axis=0, keepdims=True)
        hi = jnp.where(top == INT_MAX, top, top + 1)
        done = (count_ge(lo) == topk).astype(jnp.int32)

        def unfinished(carry):
            it, lo, hi, done = carry
            return jnp.logical_and(it < 34, jnp.min(done) == 0)

        def bisect(carry):
            it, lo, hi, done = carry
            mid = (lo >> 1) + (hi >> 1) + (lo & hi & 1)
            cnt = count_ge(mid)
            active = done == 0
            ok = cnt >= topk
            lo = jnp.where(active & ok, mid, lo)
            hi = jnp.where(active & jnp.logical_not(ok), mid, hi)
            closed = hi <= lo + 1
            done = jnp.where(active & ((cnt == topk) | closed), 1, done)
            return it + 1, lo, hi, done

        _, lo, _, _ = lax.while_loop(unfinished, bisect, (jnp.int32(0), lo, hi, done))
        thr_ref[...] = lo

    _load_q_group(qt_ref, qall_ref)
    _softmax_init(m_ref, acc_ref)
    thr = thr_ref[...]

    def logits(c):
        r0 = pl.multiple_of(c * CHUNK, CHUNK)
        sel = (keys_ref[pl.ds(r0, CHUNK), :] >= thr) & causal(c, CHUNK)
        s = jnp.dot(k_ref[pl.ds(r0, CHUNK), :], qall_ref[...], preferred_element_type=F32)
        return _add_group_bias(s, jnp.where(sel, 0.0, NEG))

    _attend_range(n_chunks, logits, vt_ref, (sa_ref, sb_ref), (cma_ref, cmb_ref), m_ref, acc_ref)
    _softmax_finish(o_ref, acc_ref)


def _dsa(qit, wt, ki, qt, k, vtt, topk):
    s = ki.shape[0]
    tq = CHUNK
    assert s % (tq * CHUNKS_PER_STEP) == 0 and topk <= CHUNK
    gw = GROUP * HEAD_DIM
    return pl.pallas_call(
        functools.partial(_dsa_body, topk=topk),
        grid=(s // tq, N_KV_HEADS),
        in_specs=[
            pl.BlockSpec((IDX_HEADS * IDX_DIM, tq), lambda i, g: (0, i)),
            pl.BlockSpec((IDX_HEADS, tq), lambda i, g: (0, i)),
            pl.BlockSpec((s, IDX_DIM), lambda i, g: (0, 0), pipeline_mode=pl.Buffered(1)),
            pl.BlockSpec((gw, tq), lambda i, g: (g, i)),
            pl.BlockSpec((s, HEAD_DIM), lambda i, g: (0, g)),
            pl.BlockSpec((s // tq, V_ROWS, tq), lambda i, g: (0, g, 0)),
        ],
        out_specs=pl.BlockSpec((gw, tq), lambda i, g: (g, i)),
        out_shape=jax.ShapeDtypeStruct((N_MAIN_HEADS * HEAD_DIM, s), BF16),
        scratch_shapes=[
            pltpu.VMEM((s, tq), jnp.int32),
            pltpu.VMEM((CHUNK, tq), jnp.int32),
            pltpu.VMEM((1, tq), jnp.int32),
            pltpu.VMEM((HEAD_DIM, GROUP * tq), BF16),
            pltpu.VMEM((CHUNKS_PER_STEP * CHUNK, GROUP * tq), F32),
            pltpu.VMEM((CHUNKS_PER_STEP * CHUNK, GROUP * tq), F32),
            pltpu.VMEM((1, GROUP * tq), F32),
            pltpu.VMEM((1, GROUP * tq), F32),
            pltpu.VMEM((1, GROUP * tq), F32),
            pltpu.VMEM((V_ROWS, GROUP * tq), F32),
        ],
        compiler_params=_params("arbitrary", "arbitrary"),
        name="dsa",
    )(qit, wt, ki, qt, k, vtt)


def _moba_body(qt_ref, k_ref, vt_ref, km_ref, o_ref, qaug_ref, sa_ref, sb_ref, cma_ref, cmb_ref,
               m_ref, acc_ref, *, n_sel):
    cur = pl.program_id(0)
    tq = CHUNK
    nb = km_ref.shape[0]
    width = GROUP * tq
    for hh in range(GROUP):
        qaug_ref[:HEAD_DIM, hh * tq:(hh + 1) * tq] = qt_ref[hh * HEAD_DIM:(hh + 1) * HEAD_DIM, :]
    q_all = qaug_ref[:HEAD_DIM, :]

    blk_id = lax.broadcasted_iota(jnp.int32, (nb, width), 0)
    past = blk_id < cur
    gate = jnp.dot(km_ref[...], q_all, preferred_element_type=F32)
    gate = jnp.where(past, gate, -jnp.inf)
    chosen = jnp.zeros((nb, width), jnp.bool_)
    for _ in range(n_sel):
        best = jnp.max(gate, axis=0, keepdims=True)
        first = jnp.min(jnp.where(gate == best, blk_id, nb), axis=0, keepdims=True)
        pick = blk_id == first
        chosen = chosen | pick
        gate = jnp.where(pick, -jnp.inf, gate)
    qaug_ref[HEAD_DIM:HEAD_DIM + nb, :] = jnp.where(chosen & past, 0.0, NEG).astype(BF16)
    if nb < HEAD_DIM:
        qaug_ref[HEAD_DIM + nb:, :] = jnp.zeros((HEAD_DIM - nb, width), BF16)

    _softmax_init(m_ref, acc_ref)
    state = (vt_ref, (sa_ref, sb_ref), (cma_ref, cmb_ref), m_ref, acc_ref)

    def past_logits(c):
        r0 = pl.multiple_of(c * CHUNK, CHUNK)
        return jnp.dot(k_ref[pl.ds(r0, CHUNK), :], qaug_ref[...], preferred_element_type=F32)

    _attend_range(cur, past_logits, *state)

    tri = (lax.broadcasted_iota(jnp.int32, (tq, tq), 0) <= lax.broadcasted_iota(jnp.int32, (tq, tq), 1))

    def own_logits(c):
        r0 = pl.multiple_of(c * CHUNK, CHUNK)
        s = jnp.dot(k_ref[pl.ds(r0, CHUNK), :HEAD_DIM], q_all, preferred_element_type=F32)
        return _add_group_bias(s, jnp.where(tri, 0.0, NEG))

    _attend_single(cur, own_logits, *state)
    _softmax_finish(o_ref, acc_ref)


def _moba(qt, k_aug, vtt, kmeans, n_sel):
    s = k_aug.shape[0]
    assert MOBA_BLOCK == CHUNK and s % (CHUNK * CHUNKS_PER_STEP) == 0
    tq = CHUNK
    nb = s // tq
    assert nb <= HEAD_DIM
    gw = GROUP * HEAD_DIM
    return pl.pallas_call(
        functools.partial(_moba_body, n_sel=n_sel),
        grid=(nb, N_KV_HEADS),
        in_specs=[
            pl.BlockSpec((gw, tq), lambda i, g: (g, i)),
            pl.BlockSpec((s, 2 * HEAD_DIM), lambda i, g: (0, g)),
            pl.BlockSpec((nb, V_ROWS, tq), lambda i, g: (0, g, 0)),
            pl.BlockSpec((nb, HEAD_DIM), lambda i, g: (0, g)),
        ],
        out_specs=pl.BlockSpec((gw, tq), lambda i, g: (g, i)),
        out_shape=jax.ShapeDtypeStruct((N_MAIN_HEADS * HEAD_DIM, s), BF16),
        scratch_shapes=[
            pltpu.VMEM((2 * HEAD_DIM, GROUP * tq), BF16),
            pltpu.VMEM((CHUNKS_PER_STEP * CHUNK, GROUP * tq), F32),
            pltpu.VMEM((CHUNKS_PER_STEP * CHUNK, GROUP * tq), F32),
            pltpu.VMEM((1, GROUP * tq), F32),
            pltpu.VMEM((1, GROUP * tq), F32),
            pltpu.VMEM((1, GROUP * tq), F32),
            pltpu.VMEM((V_ROWS, GROUP * tq), F32),
        ],
        compiler_params=_params("parallel", "arbitrary"),
        name="moba",
    )(qt, k_aug, vtt, kmeans)


def _mem_attn_body(q_ref, kt_ref, v_ref, o_ref):
    for h in range(N_MEM_HEADS):
        sl = slice(h * HEAD_DIM, (h + 1) * HEAD_DIM)
        s = jnp.dot(q_ref[:, sl], kt_ref[sl, :], preferred_element_type=F32)
        p = jnp.exp(s - jnp.max(s, axis=-1, keepdims=True))
        o = jnp.dot(p.astype(BF16), v_ref[:, sl], preferred_element_type=F32)
        o_ref[:, sl] = (o / jnp.sum(p, axis=-1, keepdims=True)).astype(o_ref.dtype)


def _mem_attn(qm, kmt, vm, *, tq=512):
    s, w = qm.shape
    m = vm.shape[0]
    tq = min(tq, s)
    assert s % tq == 0
    return pl.pallas_call(
        _mem_attn_body,
        grid=(s // tq,),
        in_specs=[pl.BlockSpec((tq, w), lambda i: (i, 0)),
                  pl.BlockSpec((w, m), lambda i: (0, 0)),
                  pl.BlockSpec((m, w), lambda i: (0, 0))],
        out_specs=pl.BlockSpec((tq, w), lambda i: (i, 0)),
        out_shape=jax.ShapeDtypeStruct((s, w), BF16),
        compiler_params=_params("parallel"),
        name="mem_attn",
    )(qm, kmt, vm)


def _out_proj_body(x_ref, om_ref, oq_ref, w1_ref, w2_ref, o_ref):
    o_ref[...] = (x_ref[...]
                  + jnp.dot(om_ref[...], w1_ref[...], preferred_element_type=F32)
                  + jnp.dot(oq_ref[...], w2_ref[...], preferred_element_type=F32))


def _out_proj(x, o_main, o_mem, w1, w2, *, tm=512):
    s, d = x.shape
    tm = min(tm, s)
    assert s % tm == 0
    row = lambda i: (i, 0)
    const = lambda i: (0, 0)
    return pl.pallas_call(
        _out_proj_body,
        grid=(s // tm,),
        in_specs=[pl.BlockSpec((tm, d), row),
                  pl.BlockSpec((tm, o_main.shape[1]), row),
                  pl.BlockSpec((tm, o_mem.shape[1]), row),
                  pl.BlockSpec(w1.shape, const, pipeline_mode=pl.Buffered(1)),
                  pl.BlockSpec(w2.shape, const, pipeline_mode=pl.Buffered(1))],
        out_specs=pl.BlockSpec((tm, d), row),
        out_shape=jax.ShapeDtypeStruct((s, d), F32),
        compiler_params=_params("parallel"),
        name="out_proj",
    )(x, o_main, o_mem, w1, w2)


def _rope_tables(positions, dim):
    inv = 1.0 / (ROPE_THETA ** (jnp.arange(0, dim, 2, dtype=F32) / dim))
    ang = positions.astype(F32)[:, None] * inv
    c, s = jnp.cos(ang), jnp.sin(ang)
    reps = LANES // dim
    return jnp.tile(jnp.concatenate([c, c], -1), (1, reps)), jnp.tile(jnp.concatenate([-s, s], -1), (1, reps))


def _value_chunks(v):
    s = v.shape[0]
    vt = jnp.transpose(v.reshape(s // CHUNK, CHUNK, N_KV_HEADS, HEAD_DIM), (0, 2, 3, 1))
    ones = jnp.ones((s // CHUNK, N_KV_HEADS, ONES_ROWS, CHUNK), v.dtype)
    return jnp.concatenate([vt, ones], axis=2).reshape(s // CHUNK, N_KV_HEADS * V_ROWS, CHUNK)


def _keys_with_block_onehot(k):
    s = k.shape[0]
    onehot = (jnp.arange(s)[:, None] // MOBA_BLOCK == jnp.arange(HEAD_DIM)[None, :]).astype(k.dtype)
    onehot = jnp.broadcast_to(onehot[:, None, :], (s, N_KV_HEADS, HEAD_DIM))
    return jnp.concatenate([k.reshape(s, N_KV_HEADS, HEAD_DIM), onehot], axis=2).reshape(s, -1)


def kernel(x, mem, positions, ffn1_norm, ffn1_w_gate_up, ffn1_w_down, attn_norm, mem_norm, a_w_in, idx_k_norm, b_w_in, w_mem_kv, w_out, ffn2_norm, ffn2_w_gate_up, ffn2_w_down, kv_norm, w_kv_shared, final_norm):
    b, s, d = x.shape
    assert b == 1 and mem.shape[0] == 1
    depth = ffn1_norm.shape[0]
    n_a = a_w_in.shape[0]
    main_w = N_MAIN_HEADS * HEAD_DIM
    kv_w = N_KV_HEADS * HEAD_DIM
    idx_w = IDX_HEADS * IDX_DIM
    mem_w = N_MEM_HEADS * HEAD_DIM
    topk = min(IDX_TOPK_MAX, s // 4)
    nb = s // MOBA_BLOCK
    n_sel = min(MOBA_TOPK_MAX, max(nb - 1, 1))

    cos, sin = _rope_tables(positions[0], HEAD_DIM)
    cosi, sini = _rope_tables(positions[0], IDX_DIM)
    tables = (cos, sin, cosi, sini)
    mem_tables = tuple(t[:mem.shape[1]] for t in tables)
    no_gk = jnp.zeros((1, LANES), F32)

    xs = x[0]
    mem2 = mem[0]
    k_sh = vtt_sh = kmeans = None
    for i in range(depth):
        if i == n_a:
            wk = w_kv_shared[:, :kv_w].astype(BF16)
            wv = w_kv_shared[:, kv_w:].astype(BF16)
            k_sh, v_sh, km = _kv_shared(xs, kv_norm, cos, sin, wk, wv)
            vtt_sh = _value_chunks(v_sh)
            k_sh = _keys_with_block_onehot(k_sh)
            kmeans = km.reshape(nb, kv_w).astype(BF16)

        last = i == depth - 1
        xs = _ffn(xs, ffn1_norm[i], ffn1_w_gate_up[i].astype(BF16), ffn1_w_down[i].astype(BF16))

        wm = w_mem_kv[i].astype(BF16)
        mk, mv = _proj(mem2, mem_norm[i], mem_tables, no_gk, [wm[:, :mem_w], wm[:, mem_w:]],
                       ["plain", "plain"], [BF16, BF16])
        if i < n_a:
            wa = a_w_in[i]
            o0 = 0
            ws = []
            for width in (main_w, kv_w, kv_w, idx_w, IDX_DIM + IDX_HEADS, mem_w):
                ws.append(wa[:, o0:o0 + width])
                o0 += width
            w_kiwi = jnp.pad(ws[4], ((0, 0), (0, LANES - ws[4].shape[1])))
            ws = [w.astype(BF16) for w in (ws[0], ws[1], ws[2], ws[3], w_kiwi, ws[5])]
            gk = jnp.pad(idx_k_norm[i], (0, LANES - IDX_DIM)).reshape(1, LANES)
            q, k, v, qi, kiwi, qm = _proj(
                xs, attn_norm[i], tables, gk, ws,
                ["rope_scaled", "rope", "plain", "rope_idx", "kiwi", "scaled"],
                [BF16, BF16, BF16, BF16, F32, BF16])
            ki = kiwi[:, :IDX_DIM].astype(BF16)
            wt = kiwi[:, IDX_DIM:IDX_DIM + IDX_HEADS].T
            ot = _dsa(qi.T, wt, ki, q.T, k, _value_chunks(v), topk)
        else:
            wb = b_w_in[i - n_a]
            ws = [wb[:, :main_w].astype(BF16), wb[:, main_w:].astype(BF16)]
            q, qm = _proj(xs, attn_norm[i], tables, no_gk, ws, ["rope_scaled", "scaled"], [BF16, BF16])
            ot = _moba(q.T, k_sh, vtt_sh, kmeans, n_sel)
        o_mem = _mem_attn(qm, mk.T, mv)
        wo = w_out[i].astype(BF16)
        xs = _out_proj(xs, ot.T, o_mem, wo[:main_w], wo[main_w:])

        xs = _ffn(xs, ffn2_norm[i], ffn2_w_gate_up[i].astype(BF16), ffn2_w_down[i].astype(BF16),
                  final_norm if last else None)
    return xs[None]
```

```python
import functools

import jax
import jax.numpy as jnp
import numpy as np
from jax import lax
from jax.experimental import pallas as pl
from jax.experimental.pallas import tpu as pltpu

HEAD_DIM = 128
N_MAIN_HEADS = 12
N_KV_HEADS = 4
GROUP = N_MAIN_HEADS // N_KV_HEADS
N_MEM_HEADS = 4
IDX_HEADS = 16
IDX_DIM = 64
IDX_TOPK_MAX = 256
MOBA_BLOCK = 256
MOBA_TOPK_MAX = 3
ROPE_THETA = 10000.0
RMS_EPS = 1e-6

LANES = 128
SUBLANES = 8
VMEM_LIMIT = 56 * 1024 * 1024
NEG = -1e30
LOG2_E = 1.4426950408889634
INT_MIN = -2 ** 31
INT_MAX = 2 ** 31 - 1
BF16_ROWS = 2 * SUBLANES
COARSE_KEY_POS_INF = 0x7F80
COARSE_KEY_NEG_INF = -0x7F81
CHUNK = 256
CHUNKS_PER_STEP = 4
ONES_ROWS = 16
V_ROWS = HEAD_DIM + ONES_ROWS

F32 = jnp.float32
BF16 = jnp.bfloat16


def _params(*sem):
    return pltpu.CompilerParams(dimension_semantics=sem, vmem_limit_bytes=VMEM_LIMIT)


def _rms(x, gain):
    return x * lax.rsqrt(jnp.mean(x * x, axis=-1, keepdims=True) + RMS_EPS) * gain


def _ffn_body(x_ref, g_ref, wg_ref, wu_ref, wd_ref, pg_ref, o_ref, h_ref, *, final_norm):
    j = pl.program_id(1)

    @pl.when(j == 0)
    def _():
        h_ref[...] = _rms(x_ref[...], g_ref[...]).astype(BF16)
        o_ref[...] = jnp.zeros_like(o_ref)

    h = h_ref[...]
    gate = jnp.dot(h, wg_ref[...], preferred_element_type=F32)
    up = jnp.dot(h, wu_ref[...], preferred_element_type=F32)
    act = (gate * (1.0 / (1.0 + jnp.exp(-gate))) * up).astype(BF16)
    o_ref[...] += jnp.dot(act, wd_ref[...], preferred_element_type=F32)

    @pl.when(j == pl.num_programs(1) - 1)
    def _():
        y = x_ref[...] + 0.5 * o_ref[...]
        if final_norm:
            y = _rms(y, pg_ref[...])
        o_ref[...] = y


def _ffn(x, gain, w_gate_up, w_down, post_gain=None, *, tm=512, tf=512):
    s, d = x.shape
    f = w_down.shape[0]
    tm = min(tm, s)
    tf = min(tf, f)
    assert s % tm == 0 and f % tf == 0
    nf = f // tf
    final_norm = post_gain is not None
    pg = post_gain if final_norm else gain
    return pl.pallas_call(
        functools.partial(_ffn_body, final_norm=final_norm),
        grid=(s // tm, nf),
        in_specs=[
            pl.BlockSpec((tm, d), lambda i, j: (i, 0)),
            pl.BlockSpec((1, d), lambda i, j: (0, 0)),
            pl.BlockSpec((d, tf), lambda i, j: (0, j)),
            pl.BlockSpec((d, tf), lambda i, j: (0, j + nf)),
            pl.BlockSpec((tf, d), lambda i, j: (j, 0)),
            pl.BlockSpec((1, d), lambda i, j: (0, 0)),
        ],
        out_specs=pl.BlockSpec((tm, d), lambda i, j: (i, 0)),
        out_shape=jax.ShapeDtypeStruct((s, d), F32),
        scratch_shapes=[pltpu.VMEM((tm, d), BF16)],
        compiler_params=_params("parallel", "arbitrary"),
        name="ffn",
    )(x, gain.reshape(1, d), w_gate_up, w_gate_up, w_down, pg.reshape(1, d))


def _rope_heads(y, cos, sin, o_ref, scale):
    for h in range(y.shape[1] // HEAD_DIM):
        t = y[:, h * HEAD_DIM:(h + 1) * HEAD_DIM]
        r = t * cos + pltpu.roll(t, HEAD_DIM // 2, 1) * sin
        if scale != 1.0:
            r = r * scale
        o_ref[:, h * HEAD_DIM:(h + 1) * HEAD_DIM] = r.astype(o_ref.dtype)


def _rot_idx(t):
    lane = lax.broadcasted_iota(jnp.int32, t.shape, 1)
    first_half = (lane & (IDX_DIM // 2)) == 0
    return jnp.where(first_half, pltpu.roll(t, LANES - IDX_DIM // 2, 1),
                     pltpu.roll(t, IDX_DIM // 2, 1))


def _proj_body(*refs, kinds, q_scale, idx_w_scale):
    x_ref, g_ref, cos_ref, sin_ref, cosi_ref, sini_ref, gk_ref = refs[:7]
    n = len(kinds)
    w_refs = refs[7:7 + n]
    o_refs = refs[7 + n:7 + 2 * n]
    h = _rms(x_ref[...], g_ref[...]).astype(BF16)
    for kind, w_ref, o_ref in zip(kinds, w_refs, o_refs):
        y = jnp.dot(h, w_ref[...], preferred_element_type=F32)
        if kind == "plain":
            o_ref[...] = y.astype(o_ref.dtype)
        elif kind == "scaled":
            o_ref[...] = (y * q_scale).astype(o_ref.dtype)
        elif kind == "rope":
            _rope_heads(y, cos_ref[...], sin_ref[...], o_ref, 1.0)
        elif kind == "rope_scaled":
            _rope_heads(y, cos_ref[...], sin_ref[...], o_ref, q_scale * LOG2_E)
        elif kind == "rope_idx":
            cosi, sini = cosi_ref[...], sini_ref[...]
            for c in range(y.shape[1] // LANES):
                t = y[:, c * LANES:(c + 1) * LANES]
                o_ref[:, c * LANES:(c + 1) * LANES] = (t * cosi + _rot_idx(t) * sini).astype(o_ref.dtype)
        elif kind == "kiwi":
            lane = lax.broadcasted_iota(jnp.int32, y.shape, 1)
            is_k = lane < IDX_DIM
            kk = jnp.where(is_k, y, 0.0)
            ms = jnp.sum(kk * kk, axis=-1, keepdims=True) * (1.0 / IDX_DIM)
            kn = kk * lax.rsqrt(ms + RMS_EPS) * gk_ref[...]
            kr = kn * cosi_ref[...] + _rot_idx(kn) * sini_ref[...]
            o_ref[...] = jnp.where(is_k, kr, y * idx_w_scale)
        else:
            raise ValueError(kind)


def _proj(x, gain, tables, gk, weights, kinds, out_dtypes, *, tm=256):
    s, d = x.shape
    tm = min(tm, s)
    assert s % tm == 0
    cos, sin, cosi, sini = tables
    row = lambda i: (i, 0)
    const = lambda i: (0, 0)
    in_specs = [pl.BlockSpec((tm, d), row), pl.BlockSpec((1, d), const)]
    in_specs += [pl.BlockSpec((tm, LANES), row)] * 4
    in_specs += [pl.BlockSpec((1, LANES), const)]
    in_specs += [pl.BlockSpec(w.shape, const, pipeline_mode=pl.Buffered(1)) for w in weights]
    out_specs = [pl.BlockSpec((tm, w.shape[1]), row) for w in weights]
    out_shape = [jax.ShapeDtypeStruct((s, w.shape[1]), dt) for w, dt in zip(weights, out_dtypes)]
    return pl.pallas_call(
        functools.partial(_proj_body, kinds=tuple(kinds), q_scale=HEAD_DIM ** -0.5,
                          idx_w_scale=(IDX_HEADS ** -0.5) * (IDX_DIM ** -0.5)),
        grid=(s // tm,),
        in_specs=in_specs,
        out_specs=out_specs,
        out_shape=out_shape,
        compiler_params=_params("parallel"),
        name="proj",
    )(x, gain.reshape(1, d), cos, sin, cosi, sini, gk, *weights)


def _kv_shared_body(x_ref, g_ref, cos_ref, sin_ref, wk_ref, wv_ref, k_ref, v_ref, km_ref):
    h = _rms(x_ref[...], g_ref[...]).astype(BF16)
    yk = jnp.dot(h, wk_ref[...], preferred_element_type=F32)
    cos, sin = cos_ref[...], sin_ref[...]
    tm = yk.shape[0]
    for hd in range(yk.shape[1] // HEAD_DIM):
        t = yk[:, hd * HEAD_DIM:(hd + 1) * HEAD_DIM]
        r = t * cos + pltpu.roll(t, HEAD_DIM // 2, 1) * sin
        k_ref[:, hd * HEAD_DIM:(hd + 1) * HEAD_DIM] = r.astype(k_ref.dtype)
        for b in range(tm // MOBA_BLOCK):
            blk = r[b * MOBA_BLOCK:(b + 1) * MOBA_BLOCK]
            km_ref[b, :, hd * HEAD_DIM:(hd + 1) * HEAD_DIM] = (
                jnp.sum(blk, axis=0, keepdims=True) * (1.0 / MOBA_BLOCK))
    v_ref[...] = jnp.dot(h, wv_ref[...], preferred_element_type=F32).astype(v_ref.dtype)


def _kv_shared(x, gain, cos, sin, wk, wv, *, tm=256):
    s, d = x.shape
    assert s % MOBA_BLOCK == 0
    tm = min(tm, s)
    assert tm % MOBA_BLOCK == 0 and s % tm == 0
    nkv = wk.shape[1]
    row = lambda i: (i, 0)
    const = lambda i: (0, 0)
    return pl.pallas_call(
        _kv_shared_body,
        grid=(s // tm,),
        in_specs=[pl.BlockSpec((tm, d), row), pl.BlockSpec((1, d), const),
                  pl.BlockSpec((tm, LANES), row), pl.BlockSpec((tm, LANES), row),
                  pl.BlockSpec(wk.shape, const, pipeline_mode=pl.Buffered(1)),
                  pl.BlockSpec(wv.shape, const, pipeline_mode=pl.Buffered(1))],
        out_specs=[pl.BlockSpec((tm, nkv), row), pl.BlockSpec((tm, nkv), row),
                   pl.BlockSpec((tm // MOBA_BLOCK, 1, nkv), lambda i: (i, 0, 0))],
        out_shape=[jax.ShapeDtypeStruct((s, nkv), BF16), jax.ShapeDtypeStruct((s, nkv), BF16),
                   jax.ShapeDtypeStruct((s // MOBA_BLOCK, 1, nkv), F32)],
        compiler_params=_params("parallel"),
        name="kv_shared",
    )(x, gain.reshape(1, d), cos, sin, wk, wv)


def _load_q_group(qt_ref, qall_ref):
    tq = qt_ref.shape[1]
    for hh in range(GROUP):
        qall_ref[:, hh * tq:(hh + 1) * tq] = qt_ref[hh * HEAD_DIM:(hh + 1) * HEAD_DIM, :]


def _softmax_init(m_ref, acc_ref):
    m_ref[...] = jnp.full(m_ref.shape, NEG, F32)
    acc_ref[...] = jnp.zeros(acc_ref.shape, F32)


def _add_group_bias(s, b):
    tq = b.shape[1]
    return jnp.concatenate([s[:, hh * tq:(hh + 1) * tq] + b for hh in range(GROUP)], axis=1)


def _logits_pass(chunk_ids, logits_fn, s_ref, cmax_ref):
    cmax = None
    for j, c in enumerate(chunk_ids):
        s = logits_fn(c)
        s_ref[j * CHUNK:(j + 1) * CHUNK, :] = s
        cm = jnp.max(s, axis=0, keepdims=True)
        cmax = cm if cmax is None else jnp.maximum(cmax, cm)
    cmax_ref[...] = cmax


def _softmax_pass(chunk_ids, vt_ref, s_ref, cmax_ref, m_ref, acc_ref):
    m_old = m_ref[...]
    m_new = jnp.maximum(m_old, cmax_ref[...])
    alpha = jnp.exp2(m_old - m_new)
    pv = None
    for j, c in enumerate(chunk_ids):
        p = jnp.exp2(s_ref[j * CHUNK:(j + 1) * CHUNK, :] - m_new)
        d = jnp.dot(vt_ref[c], p.astype(BF16), preferred_element_type=F32)
        pv = d if pv is None else pv + d
    m_ref[...] = m_new
    acc_ref[...] = alpha * acc_ref[...] + pv


def _attend_range(n_chunks, logits_fn, vt_ref, s_refs, cmax_refs, m_ref, acc_ref):
    n_steps = pl.cdiv(n_chunks, CHUNKS_PER_STEP)
    last = jnp.maximum(n_steps - 1, 0)

    def chunks(step):
        return [step * CHUNKS_PER_STEP + j for j in range(CHUNKS_PER_STEP)]

    def logits(step, slot):
        _logits_pass(chunks(jnp.minimum(step, last)), logits_fn, s_refs[slot], cmax_refs[slot])

    def softmax(step, slot):
        _softmax_pass(chunks(step), vt_ref, s_refs[slot], cmax_refs[slot], m_ref, acc_ref)

    logits(0, 0)

    def pair(u, carry):
        t = 2 * u
        logits(t + 1, 1)
        softmax(t, 0)

        @pl.when(t + 1 < n_steps)
        def _():
            logits(t + 2, 0)
            softmax(t + 1, 1)

        return carry

    lax.fori_loop(0, pl.cdiv(n_steps, 2), pair, 0)


def _attend_single(c, logits_fn, vt_ref, s_refs, cmax_refs, m_ref, acc_ref):
    _logits_pass([c], logits_fn, s_refs[0], cmax_refs[0])
    _softmax_pass([c], vt_ref, s_refs[0], cmax_refs[0], m_ref, acc_ref)


def _softmax_finish(o_ref, acc_ref):
    tq = o_ref.shape[1]
    out = acc_ref[:HEAD_DIM, :] / acc_ref[HEAD_DIM:HEAD_DIM + 1, :]
    for hh in range(GROUP):
        o_ref[hh * HEAD_DIM:(hh + 1) * HEAD_DIM, :] = out[:, hh * tq:(hh + 1) * tq].astype(o_ref.dtype)


def _dsa_body(qit_ref, wt_ref, ki_ref, qt_ref, k_ref, vt_ref, o_ref,
              keys_ref, coarse_ref, gmax_ref, thr_ref, qall_ref, sa_ref, sb_ref, cma_ref, cmb_ref,
              m_ref, acc_ref, *, topk):
    i = pl.program_id(0)
    g = pl.program_id(1)
    tq = CHUNK
    n_chunks = i + 1
    sub = 128

    def causal(c, rows, row_off=0):
        kpos = c * CHUNK + row_off + lax.broadcasted_iota(jnp.int32, (rows, tq), 0)
        qpos = i * tq + lax.broadcasted_iota(jnp.int32, (rows, tq), 1)
        return kpos <= qpos

    @pl.when(g == 0)
    def _():
        gmax_ref[...] = jnp.full(gmax_ref.shape, INT_MIN, jnp.int32)

        def score_chunk(c, carry):
            for part in range(CHUNK // sub):
                r0 = pl.multiple_of(c * CHUNK + part * sub, sub)
                kit = ki_ref[pl.ds(r0, sub), :]
                sc = jnp.zeros((sub, tq), F32)
                for h in range(IDX_HEADS):
                    d = jnp.dot(kit, qit_ref[h * IDX_DIM:(h + 1) * IDX_DIM, :], preferred_element_type=F32)
                    sc = sc + wt_ref[h:h + 1, :] * jnp.maximum(d, 0.0)
                sc = jnp.where(causal(c, sub, part * sub), sc, -jnp.inf)
                bits = pltpu.bitcast(sc, jnp.int32)
                key = bits ^ ((bits >> 31) & INT_MAX)
                keys_ref[pl.ds(r0, sub), :] = key
                coarse_ref[pl.ds(r0, sub), :] = sc.astype(BF16)
                gsl = slice(part * sub, (part + 1) * sub)
                gmax_ref[gsl, :] = jnp.maximum(gmax_ref[gsl, :], key)
            return carry

        lax.fori_loop(0, n_chunks, score_chunk, 0)

        def clear_chunk(c, carry):
            keys_ref[pl.ds(pl.multiple_of(c * CHUNK, CHUNK), CHUNK), :] = jnp.full((CHUNK, tq), INT_MIN, jnp.int32)
            return carry

        lax.fori_loop(n_chunks, pl.cdiv(n_chunks, CHUNKS_PER_STEP) * CHUNKS_PER_STEP, clear_chunk, 0)

        def count_ge(t):
            def count_chunk(c, acc):
                blk = keys_ref[pl.ds(pl.multiple_of(c * CHUNK, CHUNK), CHUNK), :]
                ge = jnp.where(blk >= t, 1, 0).astype(jnp.int32)
                return acc + jnp.sum(ge.reshape(CHUNK // SUBLANES, SUBLANES, tq), axis=0)

            part = lax.fori_loop(0, n_chunks, count_chunk, jnp.zeros((SUBLANES, tq), jnp.int32))
            return jnp.sum(part, axis=0, keepdims=True)

        def count_coarse_ge(t16):
            one, zero = jnp.ones((), BF16), jnp.zeros((), BF16)

            def count_chunk(c, acc):
                blk = coarse_ref[pl.ds(pl.multiple_of(c * CHUNK, CHUNK), CHUNK), :]
                x = jnp.where(blk >= t16, one, zero)
                rows = CHUNK
                while rows > BF16_ROWS:
                    rows //= 2
                    x = x[:rows] + x[rows:]
                return acc + x.astype(F32)

            part = lax.fori_loop(0, n_chunks, count_chunk, jnp.zeros((BF16_ROWS, tq), F32))
            return jnp.sum(part, axis=0, keepdims=True)

        def key_to_f32(k):
            return pltpu.bitcast(k ^ ((k >> 31) & INT_MAX), F32)

        def f32_to_key(x):
            bits = pltpu.bitcast(x, jnp.int32)
            return bits ^ ((bits >> 31) & INT_MAX)

        def coarse_key(x16):
            b = pltpu.bitcast(x16.astype(F32), jnp.int32) >> 16
            return b ^ ((b >> 15) & 0x7FFF)

        def coarse_val(k16):
            b = k16 ^ ((k16 >> 15) & 0x7FFF)
            return pltpu.bitcast(b << 16, F32).astype(BF16)

        gm = gmax_ref[...]
        lo = jnp.min(gm, axis=0, keepdims=True)
        top = jnp.max(gm, axis=0, keepdims=True)
        hi = jnp.where(top == INT_MAX, top, top + 1)

        lo16 = coarse_key(key_to_f32(lo).astype(BF16))
        hi16 = coarse_key(key_to_f32(top).astype(BF16)) + 1

        def coarse_open(carry):
            lo16, hi16 = carry
            return jnp.max(hi16 - lo16) > 1

        def coarse_step(carry):
            lo16, hi16 = carry
            mid = (lo16 + hi16) >> 1
            ok = count_coarse_ge(coarse_val(mid)) >= topk
            is_open = hi16 > lo16 + 1
            return (jnp.where(is_open & ok, mid, lo16),
                    jnp.where(is_open & jnp.logical_not(ok), mid, hi16))

        lo16, _ = lax.while_loop(coarse_open, coarse_step, (lo16, hi16))

        below = coarse_val(jnp.maximum(lo16 - 1, COARSE_KEY_NEG_INF)).astype(F32)
        above = coarse_val(jnp.minimum(lo16 + 1, COARSE_KEY_POS_INF)).astype(F32)
        lo = jnp.maximum(lo, f32_to_key(below))
        hi = jnp.minimum(hi, f32_to_key(above))
        done = (count_ge(lo) == topk).astype(jnp.int32)

        def unfinished(carry):
            it, lo, hi, done = carry
            return jnp.logical_and(it < 34, jnp.min(done) == 0)

        def bisect(carry):
            it, lo, hi, done = carry
            mid = (lo >> 1) + (hi >> 1) + (lo & hi & 1)
            cnt = count_ge(mid)
            active = done == 0
            ok = cnt >= topk
            lo = jnp.where(active & ok, mid, lo)
            hi = jnp.where(active & jnp.logical_not(ok), mid, hi)
            closed = hi <= lo + 1
            done = jnp.where(active & ((cnt == topk) | closed), 1, done)
            return it + 1, lo, hi, done

        _, lo, _, _ = lax.while_loop(unfinished, bisect, (jnp.int32(0), lo, hi, done))
        thr_ref[...] = lo

    _load_q_group(qt_ref, qall_ref)
    _softmax_init(m_ref, acc_ref)
    thr = thr_ref[...]

    def logits(c):
        r0 = pl.multiple_of(c * CHUNK, CHUNK)
        sel = (keys_ref[pl.ds(r0, CHUNK), :] >= thr) & causal(c, CHUNK)
        s = jnp.dot(k_ref[pl.ds(r0, CHUNK), :], qall_ref[...], preferred_element_type=F32)
        return _add_group_bias(s, jnp.where(sel, 0.0, NEG))

    _attend_range(n_chunks, logits, vt_ref, (sa_ref, sb_ref), (cma_ref, cmb_ref), m_ref, acc_ref)
    _softmax_finish(o_ref, acc_ref)


def _dsa(qit, wt, ki, qt, k, vtt, topk):
    s = ki.shape[0]
    tq = CHUNK
    assert s % (tq * CHUNKS_PER_STEP) == 0 and topk <= CHUNK
    gw = GROUP * HEAD_DIM
    return pl.pallas_call(
        functools.partial(_dsa_body, topk=topk),
        grid=(s // tq, N_KV_HEADS),
        in_specs=[
            pl.BlockSpec((IDX_HEADS * IDX_DIM, tq), lambda i, g: (0, i)),
            pl.BlockSpec((IDX_HEADS, tq), lambda i, g: (0, i)),
            pl.BlockSpec((s, IDX_DIM), lambda i, g: (0, 0), pipeline_mode=pl.Buffered(1)),
            pl.BlockSpec((gw, tq), lambda i, g: (g, i)),
            pl.BlockSpec((s, HEAD_DIM), lambda i, g: (0, g)),
            pl.BlockSpec((s // tq, V_ROWS, tq), lambda i, g: (0, g, 0)),
        ],
        out_specs=pl.BlockSpec((gw, tq), lambda i, g: (g, i)),
        out_shape=jax.ShapeDtypeStruct((N_MAIN_HEADS * HEAD_DIM, s), BF16),
        scratch_shapes=[
            pltpu.VMEM((s, tq), jnp.int32),
            pltpu.VMEM((s, tq), BF16),
            pltpu.VMEM((CHUNK, tq), jnp.int32),
            pltpu.VMEM((1, tq), jnp.int32),
            pltpu.VMEM((HEAD_DIM, GROUP * tq), BF16),
            pltpu.VMEM((CHUNKS_PER_STEP * CHUNK, GROUP * tq), F32),
            pltpu.VMEM((CHUNKS_PER_STEP * CHUNK, GROUP * tq), F32),
            pltpu.VMEM((1, GROUP * tq), F32),
            pltpu.VMEM((1, GROUP * tq), F32),
            pltpu.VMEM((1, GROUP * tq), F32),
            pltpu.VMEM((V_ROWS, GROUP * tq), F32),
        ],
        compiler_params=_params("arbitrary", "arbitrary"),
        name="dsa",
    )(qit, wt, ki, qt, k, vtt)


def _moba_body(qt_ref, k_ref, vt_ref, km_ref, o_ref, qaug_ref, sa_ref, sb_ref, cma_ref, cmb_ref,
               m_ref, acc_ref, *, n_sel):
    cur = pl.program_id(0)
    tq = CHUNK
    nb = km_ref.shape[0]
    width = GROUP * tq
    for hh in range(GROUP):
        qaug_ref[:HEAD_DIM, hh * tq:(hh + 1) * tq] = qt_ref[hh * HEAD_DIM:(hh + 1) * HEAD_DIM, :]
    q_all = qaug_ref[:HEAD_DIM, :]

    blk_id = lax.broadcasted_iota(jnp.int32, (nb, width), 0)
    past = blk_id < cur
    gate = jnp.dot(km_ref[...], q_all, preferred_element_type=F32)
    gate = jnp.where(past, gate, -jnp.inf)
    chosen = jnp.zeros((nb, width), jnp.bool_)
    for _ in range(n_sel):
        best = jnp.max(gate, axis=0, keepdims=True)
        first = jnp.min(jnp.where(gate == best, blk_id, nb), axis=0, keepdims=True)
        pick = blk_id == first
        chosen = chosen | pick
        gate = jnp.where(pick, -jnp.inf, gate)
    qaug_ref[HEAD_DIM:HEAD_DIM + nb, :] = jnp.where(chosen & past, 0.0, NEG).astype(BF16)
    if nb < HEAD_DIM:
        qaug_ref[HEAD_DIM + nb:, :] = jnp.zeros((HEAD_DIM - nb, width), BF16)

    _softmax_init(m_ref, acc_ref)
    state = (vt_ref, (sa_ref, sb_ref), (cma_ref, cmb_ref), m_ref, acc_ref)

    def past_logits(c):
        r0 = pl.multiple_of(c * CHUNK, CHUNK)
        return jnp.dot(k_ref[pl.ds(r0, CHUNK), :], qaug_ref[...], preferred_element_type=F32)

    _attend_range(cur, past_logits, *state)

    tri = (lax.broadcasted_iota(jnp.int32, (tq, tq), 0) <= lax.broadcasted_iota(jnp.int32, (tq, tq), 1))

    def own_logits(c):
        r0 = pl.multiple_of(c * CHUNK, CHUNK)
        s = jnp.dot(k_ref[pl.ds(r0, CHUNK), :HEAD_DIM], q_all, preferred_element_type=F32)
        return _add_group_bias(s, jnp.where(tri, 0.0, NEG))

    _attend_single(cur, own_logits, *state)
    _softmax_finish(o_ref, acc_ref)


def _moba(qt, k_aug, vtt, kmeans, n_sel):
    s = k_aug.shape[0]
    assert MOBA_BLOCK == CHUNK and s % (CHUNK * CHUNKS_PER_STEP) == 0
    tq = CHUNK
    nb = s // tq
    assert nb <= HEAD_DIM
    gw = GROUP * HEAD_DIM
    return pl.pallas_call(
        functools.partial(_moba_body, n_sel=n_sel),
        grid=(nb, N_KV_HEADS),
        in_specs=[
            pl.BlockSpec((gw, tq), lambda i, g: (g, i)),
            pl.BlockSpec((s, 2 * HEAD_DIM), lambda i, g: (0, g)),
            pl.BlockSpec((nb, V_ROWS, tq), lambda i, g: (0, g, 0)),
            pl.BlockSpec((nb, HEAD_DIM), lambda i, g: (0, g)),
        ],
        out_specs=pl.BlockSpec((gw, tq), lambda i, g: (g, i)),
        out_shape=jax.ShapeDtypeStruct((N_MAIN_HEADS * HEAD_DIM, s), BF16),
        scratch_shapes=[
            pltpu.VMEM((2 * HEAD_DIM, GROUP * tq), BF16),
            pltpu.VMEM((CHUNKS_PER_STEP * CHUNK, GROUP * tq), F32),
            pltpu.VMEM((CHUNKS_PER_STEP * CHUNK, GROUP * tq), F32),
            pltpu.VMEM((1, GROUP * tq), F32),
            pltpu.VMEM((1, GROUP * tq), F32),
            pltpu.VMEM((1, GROUP * tq), F32),
            pltpu.VMEM((V_ROWS, GROUP * tq), F32),
        ],
        compiler_params=_params("parallel", "arbitrary"),
        name="moba",
    )(qt, k_aug, vtt, kmeans)


def _mem_attn_body(q_ref, kt_ref, v_ref, o_ref):
    for h in range(N_MEM_HEADS):
        sl = slice(h * HEAD_DIM, (h + 1) * HEAD_DIM)
        s = jnp.dot(q_ref[:, sl], kt_ref[sl, :], preferred_element_type=F32)
        p = jnp.exp(s - jnp.max(s, axis=-1, keepdims=True))
        o = jnp.dot(p.astype(BF16), v_ref[:, sl], preferred_element_type=F32)
        o_ref[:, sl] = (o / jnp.sum(p, axis=-1, keepdims=True)).astype(o_ref.dtype)


def _mem_attn(qm, kmt, vm, *, tq=512):
    s, w = qm.shape
    m = vm.shape[0]
    tq = min(tq, s)
    assert s % tq == 0
    return pl.pallas_call(
        _mem_attn_body,
        grid=(s // tq,),
        in_specs=[pl.BlockSpec((tq, w), lambda i: (i, 0)),
                  pl.BlockSpec((w, m), lambda i: (0, 0)),
                  pl.BlockSpec((m, w), lambda i: (0, 0))],
        out_specs=pl.BlockSpec((tq, w), lambda i: (i, 0)),
        out_shape=jax.ShapeDtypeStruct((s, w), BF16),
        compiler_params=_params("parallel"),
        name="mem_attn",
    )(qm, kmt, vm)


def _out_proj_body(x_ref, om_ref, oq_ref, w1_ref, w2_ref, o_ref):
    o_ref[...] = (x_ref[...]
                  + jnp.dot(om_ref[...], w1_ref[...], preferred_element_type=F32)
                  + jnp.dot(oq_ref[...], w2_ref[...], preferred_element_type=F32))


def _out_proj(x, o_main, o_mem, w1, w2, *, tm=512):
    s, d = x.shape
    tm = min(tm, s)
    assert s % tm == 0
    row = lambda i: (i, 0)
    const = lambda i: (0, 0)
    return pl.pallas_call(
        _out_proj_body,
        grid=(s // tm,),
        in_specs=[pl.BlockSpec((tm, d), row),
                  pl.BlockSpec((tm, o_main.shape[1]), row),
                  pl.BlockSpec((tm, o_mem.shape[1]), row),
                  pl.BlockSpec(w1.shape, const, pipeline_mode=pl.Buffered(1)),
                  pl.BlockSpec(w2.shape, const, pipeline_mode=pl.Buffered(1))],
        out_specs=pl.BlockSpec((tm, d), row),
        out_shape=jax.ShapeDtypeStruct((s, d), F32),
        compiler_params=_params("parallel"),
        name="out_proj",
    )(x, o_main, o_mem, w1, w2)


def _rope_tables(positions, dim):
    inv = 1.0 / (ROPE_THETA ** (jnp.arange(0, dim, 2, dtype=F32) / dim))
    ang = positions.astype(F32)[:, None] * inv
    c, s = jnp.cos(ang), jnp.sin(ang)
    reps = LANES // dim
    return jnp.tile(jnp.concatenate([c, c], -1), (1, reps)), jnp.tile(jnp.concatenate([-s, s], -1), (1, reps))


def _value_chunks(v):
    s = v.shape[0]
    vt = jnp.transpose(v.reshape(s // CHUNK, CHUNK, N_KV_HEADS, HEAD_DIM), (0, 2, 3, 1))
    ones = jnp.ones((s // CHUNK, N_KV_HEADS, ONES_ROWS, CHUNK), v.dtype)
    return jnp.concatenate([vt, ones], axis=2).reshape(s // CHUNK, N_KV_HEADS * V_ROWS, CHUNK)


def _keys_with_block_onehot(k):
    s = k.shape[0]
    onehot = (jnp.arange(s)[:, None] // MOBA_BLOCK == jnp.arange(HEAD_DIM)[None, :]).astype(k.dtype)
    onehot = jnp.broadcast_to(onehot[:, None, :], (s, N_KV_HEADS, HEAD_DIM))
    return jnp.concatenate([k.reshape(s, N_KV_HEADS, HEAD_DIM), onehot], axis=2).reshape(s, -1)


def kernel(x, mem, positions, ffn1_norm, ffn1_w_gate_up, ffn1_w_down, attn_norm, mem_norm, a_w_in, idx_k_norm, b_w_in, w_mem_kv, w_out, ffn2_norm, ffn2_w_gate_up, ffn2_w_down, kv_norm, w_kv_shared, final_norm):
    b, s, d = x.shape
    assert b == 1 and mem.shape[0] == 1
    depth = ffn1_norm.shape[0]
    n_a = a_w_in.shape[0]
    main_w = N_MAIN_HEADS * HEAD_DIM
    kv_w = N_KV_HEADS * HEAD_DIM
    idx_w = IDX_HEADS * IDX_DIM
    mem_w = N_MEM_HEADS * HEAD_DIM
    topk = min(IDX_TOPK_MAX, s // 4)
    nb = s // MOBA_BLOCK
    n_sel = min(MOBA_TOPK_MAX, max(nb - 1, 1))

    cos, sin = _rope_tables(positions[0], HEAD_DIM)
    cosi, sini = _rope_tables(positions[0], IDX_DIM)
    tables = (cos, sin, cosi, sini)
    mem_tables = tuple(t[:mem.shape[1]] for t in tables)
    no_gk = jnp.zeros((1, LANES), F32)

    xs = x[0]
    mem2 = mem[0]
    k_sh = vtt_sh = kmeans = None
    for i in range(depth):
        if i == n_a:
            wk = w_kv_shared[:, :kv_w].astype(BF16)
            wv = w_kv_shared[:, kv_w:].astype(BF16)
            k_sh, v_sh, km = _kv_shared(xs, kv_norm, cos, sin, wk, wv)
            vtt_sh = _value_chunks(v_sh)
            k_sh = _keys_with_block_onehot(k_sh)
            kmeans = km.reshape(nb, kv_w).astype(BF16)

        last = i == depth - 1
        xs = _ffn(xs, ffn1_norm[i], ffn1_w_gate_up[i].astype(BF16), ffn1_w_down[i].astype(BF16))

        wm = w_mem_kv[i].astype(BF16)
        mk, mv = _proj(mem2, mem_norm[i], mem_tables, no_gk, [wm[:, :mem_w], wm[:, mem_w:]],
                       ["plain", "plain"], [BF16, BF16])
        if i < n_a:
            wa = a_w_in[i]
            o0 = 0
            ws = []
            for width in (main_w, kv_w, kv_w, idx_w, IDX_DIM + IDX_HEADS, mem_w):
                ws.append(wa[:, o0:o0 + width])
                o0 += width
            w_kiwi = jnp.pad(ws[4], ((0, 0), (0, LANES - ws[4].shape[1])))
            ws = [w.astype(BF16) for w in (ws[0], ws[1], ws[2], ws[3], w_kiwi, ws[5])]
            gk = jnp.pad(idx_k_norm[i], (0, LANES - IDX_DIM)).reshape(1, LANES)
            q, k, v, qi, kiwi, qm = _proj(
                xs, attn_norm[i], tables, gk, ws,
                ["rope_scaled", "rope", "plain", "rope_idx", "kiwi", "scaled"],
                [BF16, BF16, BF16, BF16, F32, BF16])
            ki = kiwi[:, :IDX_DIM].astype(BF16)
            wt = kiwi[:, IDX_DIM:IDX_DIM + IDX_HEADS].T
            ot = _dsa(qi.T, wt, ki, q.T, k, _value_chunks(v), topk)
        else:
            wb = b_w_in[i - n_a]
            ws = [wb[:, :main_w].astype(BF16), wb[:, main_w:].astype(BF16)]
            q, qm = _proj(xs, attn_norm[i], tables, no_gk, ws, ["rope_scaled", "scaled"], [BF16, BF16])
            ot = _moba(q.T, k_sh, vtt_sh, kmeans, n_sel)
        o_mem = _mem_attn(qm, mk.T, mv)
        wo = w_out[i].astype(BF16)
        xs = _out_proj(xs, ot.T, o_mem, wo[:main_w], wo[main_w:])

        xs = _ffn(xs, ffn2_norm[i], ffn2_w_gate_up[i].astype(BF16), ffn2_w_down[i].astype(BF16),
                  final_norm if last else None)
    return xs[None]
```

```python
import functools

import jax
import jax.numpy as jnp
import numpy as np
from jax import lax
from jax.experimental import pallas as pl
from jax.experimental.pallas import tpu as pltpu

HEAD_DIM = 128
N_MAIN_HEADS = 12
N_KV_HEADS = 4
GROUP = N_MAIN_HEADS // N_KV_HEADS
N_MEM_HEADS = 4
IDX_HEADS = 16
IDX_DIM = 64
IDX_TOPK_MAX = 256
MOBA_BLOCK = 256
MOBA_TOPK_MAX = 3
ROPE_THETA = 10000.0
RMS_EPS = 1e-6

LANES = 128
SUBLANES = 8
VMEM_LIMIT = 56 * 1024 * 1024
NEG = -1e30
LOG2_E = 1.4426950408889634
INT_MIN = -2 ** 31
INT_MAX = 2 ** 31 - 1
CHUNK = 256
CHUNKS_PER_STEP = 4
ONES_ROWS = 16
V_ROWS = HEAD_DIM + ONES_ROWS

F32 = jnp.float32
BF16 = jnp.bfloat16


def _params(*sem):
    return pltpu.CompilerParams(dimension_semantics=sem, vmem_limit_bytes=VMEM_LIMIT)


def _rms(x, gain):
    return x * lax.rsqrt(jnp.mean(x * x, axis=-1, keepdims=True) + RMS_EPS) * gain


def _ffn_body(x_ref, g_ref, wg_ref, wu_ref, wd_ref, pg_ref, o_ref, h_ref, *, final_norm):
    j = pl.program_id(1)

    @pl.when(j == 0)
    def _():
        h_ref[...] = _rms(x_ref[...], g_ref[...]).astype(BF16)
        o_ref[...] = jnp.zeros_like(o_ref)

    h = h_ref[...]
    gate = jnp.dot(h, wg_ref[...], preferred_element_type=F32)
    up = jnp.dot(h, wu_ref[...], preferred_element_type=F32)
    act = (gate * (1.0 / (1.0 + jnp.exp(-gate))) * up).astype(BF16)
    o_ref[...] += jnp.dot(act, wd_ref[...], preferred_element_type=F32)

    @pl.when(j == pl.num_programs(1) - 1)
    def _():
        y = x_ref[...] + 0.5 * o_ref[...]
        if final_norm:
            y = _rms(y, pg_ref[...])
        o_ref[...] = y


def _ffn(x, gain, w_gate_up, w_down, post_gain=None, *, tm=512, tf=512):
    s, d = x.shape
    f = w_down.shape[0]
    tm = min(tm, s)
    tf = min(tf, f)
    assert s % tm == 0 and f % tf == 0
    nf = f // tf
    final_norm = post_gain is not None
    pg = post_gain if final_norm else gain
    return pl.pallas_call(
        functools.partial(_ffn_body, final_norm=final_norm),
        grid=(s // tm, nf),
        in_specs=[
            pl.BlockSpec((tm, d), lambda i, j: (i, 0)),
            pl.BlockSpec((1, d), lambda i, j: (0, 0)),
            pl.BlockSpec((d, tf), lambda i, j: (0, j)),
            pl.BlockSpec((d, tf), lambda i, j: (0, j + nf)),
            pl.BlockSpec((tf, d), lambda i, j: (j, 0)),
            pl.BlockSpec((1, d), lambda i, j: (0, 0)),
        ],
        out_specs=pl.BlockSpec((tm, d), lambda i, j: (i, 0)),
        out_shape=jax.ShapeDtypeStruct((s, d), F32),
        scratch_shapes=[pltpu.VMEM((tm, d), BF16)],
        compiler_params=_params("parallel", "arbitrary"),
        name="ffn",
    )(x, gain.reshape(1, d), w_gate_up, w_gate_up, w_down, pg.reshape(1, d))


def _rope_heads(y, cos, sin, o_ref, scale, transposed=False):
    for h in range(y.shape[1] // HEAD_DIM):
        t = y[:, h * HEAD_DIM:(h + 1) * HEAD_DIM]
        r = t * cos + pltpu.roll(t, HEAD_DIM // 2, 1) * sin
        if scale != 1.0:
            r = r * scale
        if transposed:
            o_ref[h * HEAD_DIM:(h + 1) * HEAD_DIM, :] = r.T.astype(o_ref.dtype)
        else:
            o_ref[:, h * HEAD_DIM:(h + 1) * HEAD_DIM] = r.astype(o_ref.dtype)


def _store_value_chunks(y, o_ref):
    for b in range(y.shape[0] // CHUNK):
        for g in range(N_KV_HEADS):
            blk = y[b * CHUNK:(b + 1) * CHUNK, g * HEAD_DIM:(g + 1) * HEAD_DIM]
            o_ref[b, g * V_ROWS:g * V_ROWS + HEAD_DIM, :] = blk.T.astype(o_ref.dtype)
            o_ref[b, g * V_ROWS + HEAD_DIM:(g + 1) * V_ROWS, :] = jnp.ones((ONES_ROWS, CHUNK), o_ref.dtype)


def _rot_idx(t):
    lane = lax.broadcasted_iota(jnp.int32, t.shape, 1)
    first_half = (lane & (IDX_DIM // 2)) == 0
    return jnp.where(first_half, pltpu.roll(t, LANES - IDX_DIM // 2, 1),
                     pltpu.roll(t, IDX_DIM // 2, 1))


def _proj_body(*refs, kinds, q_scale, idx_w_scale):
    x_ref, g_ref, cos_ref, sin_ref, cosi_ref, sini_ref, gk_ref = refs[:7]
    n = len(kinds)
    w_refs = refs[7:7 + n]
    o_refs = refs[7 + n:7 + 2 * n]
    h = _rms(x_ref[...], g_ref[...]).astype(BF16)
    for kind, w_ref, o_ref in zip(kinds, w_refs, o_refs):
        y = jnp.dot(h, w_ref[...], preferred_element_type=F32)
        if kind == "plain":
            o_ref[...] = y.astype(o_ref.dtype)
        elif kind == "scaled":
            o_ref[...] = (y * q_scale).astype(o_ref.dtype)
        elif kind == "rope":
            _rope_heads(y, cos_ref[...], sin_ref[...], o_ref, 1.0)
        elif kind == "rope_scaled_t":
            _rope_heads(y, cos_ref[...], sin_ref[...], o_ref, q_scale * LOG2_E, transposed=True)
        elif kind == "rope_idx_t":
            cosi, sini = cosi_ref[...], sini_ref[...]
            for c in range(y.shape[1] // LANES):
                t = y[:, c * LANES:(c + 1) * LANES]
                o_ref[c * LANES:(c + 1) * LANES, :] = (t * cosi + _rot_idx(t) * sini).T.astype(o_ref.dtype)
        elif kind == "value_chunks":
            _store_value_chunks(y, o_ref)
        elif kind == "kiwi":
            lane = lax.broadcasted_iota(jnp.int32, y.shape, 1)
            is_k = lane < IDX_DIM
            kk = jnp.where(is_k, y, 0.0)
            ms = jnp.sum(kk * kk, axis=-1, keepdims=True) * (1.0 / IDX_DIM)
            kn = kk * lax.rsqrt(ms + RMS_EPS) * gk_ref[...]
            kr = kn * cosi_ref[...] + _rot_idx(kn) * sini_ref[...]
            o_ref[...] = jnp.where(is_k, kr, y * idx_w_scale)
        else:
            raise ValueError(kind)


def _proj(x, gain, tables, gk, weights, kinds, out_dtypes, *, tm=256):
    s, d = x.shape
    tm = min(tm, s)
    assert s % tm == 0
    cos, sin, cosi, sini = tables
    row = lambda i: (i, 0)
    const = lambda i: (0, 0)
    in_specs = [pl.BlockSpec((tm, d), row), pl.BlockSpec((1, d), const)]
    in_specs += [pl.BlockSpec((tm, LANES), row)] * 4
    in_specs += [pl.BlockSpec((1, LANES), const)]
    in_specs += [pl.BlockSpec(w.shape, const, pipeline_mode=pl.Buffered(1)) for w in weights]
    out_specs, out_shape = [], []
    for w, kind, dt in zip(weights, kinds, out_dtypes):
        n = w.shape[1]
        if kind.endswith("_t"):
            out_specs.append(pl.BlockSpec((n, tm), lambda i: (0, i)))
            out_shape.append(jax.ShapeDtypeStruct((n, s), dt))
        elif kind == "value_chunks":
            assert tm % CHUNK == 0 and n == N_KV_HEADS * HEAD_DIM
            out_specs.append(pl.BlockSpec((tm // CHUNK, N_KV_HEADS * V_ROWS, CHUNK), lambda i: (i, 0, 0)))
            out_shape.append(jax.ShapeDtypeStruct((s // CHUNK, N_KV_HEADS * V_ROWS, CHUNK), dt))
        else:
            out_specs.append(pl.BlockSpec((tm, n), row))
            out_shape.append(jax.ShapeDtypeStruct((s, n), dt))
    return pl.pallas_call(
        functools.partial(_proj_body, kinds=tuple(kinds), q_scale=HEAD_DIM ** -0.5,
                          idx_w_scale=(IDX_HEADS ** -0.5) * (IDX_DIM ** -0.5)),
        grid=(s // tm,),
        in_specs=in_specs,
        out_specs=out_specs,
        out_shape=out_shape,
        compiler_params=_params("parallel"),
        name="proj",
    )(x, gain.reshape(1, d), cos, sin, cosi, sini, gk, *weights)


def _kv_shared_body(x_ref, g_ref, cos_ref, sin_ref, wk_ref, wv_ref, k_ref, v_ref, km_ref):
    h = _rms(x_ref[...], g_ref[...]).astype(BF16)
    yk = jnp.dot(h, wk_ref[...], preferred_element_type=F32)
    cos, sin = cos_ref[...], sin_ref[...]
    tm = yk.shape[0]
    first_blk = pl.program_id(0) * (tm // MOBA_BLOCK)
    lane = lax.broadcasted_iota(jnp.int32, (MOBA_BLOCK, HEAD_DIM), 1)
    for hd in range(yk.shape[1] // HEAD_DIM):
        t = yk[:, hd * HEAD_DIM:(hd + 1) * HEAD_DIM]
        r = t * cos + pltpu.roll(t, HEAD_DIM // 2, 1) * sin
        k_ref[:, 2 * hd * HEAD_DIM:(2 * hd + 1) * HEAD_DIM] = r.astype(k_ref.dtype)
        for b in range(tm // MOBA_BLOCK):
            rows = slice(b * MOBA_BLOCK, (b + 1) * MOBA_BLOCK)
            k_ref[rows, (2 * hd + 1) * HEAD_DIM:(2 * hd + 2) * HEAD_DIM] = (
                jnp.where(lane == first_blk + b, 1.0, 0.0).astype(k_ref.dtype))
            km_ref[b, :, hd * HEAD_DIM:(hd + 1) * HEAD_DIM] = (
                jnp.sum(r[rows], axis=0, keepdims=True) * (1.0 / MOBA_BLOCK))
    _store_value_chunks(jnp.dot(h, wv_ref[...], preferred_element_type=F32), v_ref)


def _kv_shared(x, gain, cos, sin, wk, wv, *, tm=256):
    s, d = x.shape
    assert s % MOBA_BLOCK == 0
    tm = min(tm, s)
    assert tm % MOBA_BLOCK == 0 and s % tm == 0
    nkv = wk.shape[1]
    row = lambda i: (i, 0)
    const = lambda i: (0, 0)
    return pl.pallas_call(
        _kv_shared_body,
        grid=(s // tm,),
        in_specs=[pl.BlockSpec((tm, d), row), pl.BlockSpec((1, d), const),
                  pl.BlockSpec((tm, LANES), row), pl.BlockSpec((tm, LANES), row),
                  pl.BlockSpec(wk.shape, const, pipeline_mode=pl.Buffered(1)),
                  pl.BlockSpec(wv.shape, const, pipeline_mode=pl.Buffered(1))],
        out_specs=[pl.BlockSpec((tm, 2 * nkv), row),
                   pl.BlockSpec((tm // CHUNK, N_KV_HEADS * V_ROWS, CHUNK), lambda i: (i, 0, 0)),
                   pl.BlockSpec((tm // MOBA_BLOCK, 1, nkv), lambda i: (i, 0, 0))],
        out_shape=[jax.ShapeDtypeStruct((s, 2 * nkv), BF16),
                   jax.ShapeDtypeStruct((s // CHUNK, N_KV_HEADS * V_ROWS, CHUNK), BF16),
                   jax.ShapeDtypeStruct((s // MOBA_BLOCK, 1, nkv), F32)],
        compiler_params=_params("parallel"),
        name="kv_shared",
    )(x, gain.reshape(1, d), cos, sin, wk, wv)


def _load_q_group(qt_ref, qall_ref):
    tq = qt_ref.shape[1]
    for hh in range(GROUP):
        qall_ref[:, hh * tq:(hh + 1) * tq] = qt_ref[hh * HEAD_DIM:(hh + 1) * HEAD_DIM, :]


def _softmax_init(m_ref, acc_ref):
    m_ref[...] = jnp.full(m_ref.shape, NEG, F32)
    acc_ref[...] = jnp.zeros(acc_ref.shape, F32)


def _add_group_bias(s, b):
    tq = b.shape[1]
    return jnp.concatenate([s[:, hh * tq:(hh + 1) * tq] + b for hh in range(GROUP)], axis=1)


def _logits_pass(chunk_ids, logits_fn, s_ref, cmax_ref):
    cmax = None
    for j, c in enumerate(chunk_ids):
        s = logits_fn(c)
        s_ref[j * CHUNK:(j + 1) * CHUNK, :] = s
        cm = jnp.max(s, axis=0, keepdims=True)
        cmax = cm if cmax is None else jnp.maximum(cmax, cm)
    cmax_ref[...] = cmax


def _softmax_pass(chunk_ids, vt_ref, s_ref, cmax_ref, m_ref, acc_ref):
    m_old = m_ref[...]
    m_new = jnp.maximum(m_old, cmax_ref[...])
    alpha = jnp.exp2(m_old - m_new)
    pv = None
    for j, c in enumerate(chunk_ids):
        p = jnp.exp2(s_ref[j * CHUNK:(j + 1) * CHUNK, :] - m_new)
        d = jnp.dot(vt_ref[c], p.astype(BF16), preferred_element_type=F32)
        pv = d if pv is None else pv + d
    m_ref[...] = m_new
    acc_ref[...] = alpha * acc_ref[...] + pv


def _attend_range(n_chunks, logits_fn, vt_ref, s_refs, cmax_refs, m_ref, acc_ref):
    n_steps = pl.cdiv(n_chunks, CHUNKS_PER_STEP)
    last = jnp.maximum(n_steps - 1, 0)

    def chunks(step):
        return [step * CHUNKS_PER_STEP + j for j in range(CHUNKS_PER_STEP)]

    def logits(step, slot):
        _logits_pass(chunks(jnp.minimum(step, last)), logits_fn, s_refs[slot], cmax_refs[slot])

    def softmax(step, slot):
        _softmax_pass(chunks(step), vt_ref, s_refs[slot], cmax_refs[slot], m_ref, acc_ref)

    logits(0, 0)

    def pair(u, carry):
        t = 2 * u
        logits(t + 1, 1)
        softmax(t, 0)

        @pl.when(t + 1 < n_steps)
        def _():
            logits(t + 2, 0)
            softmax(t + 1, 1)

        return carry

    lax.fori_loop(0, pl.cdiv(n_steps, 2), pair, 0)


def _attend_single(c, logits_fn, vt_ref, s_refs, cmax_refs, m_ref, acc_ref):
    _logits_pass([c], logits_fn, s_refs[0], cmax_refs[0])
    _softmax_pass([c], vt_ref, s_refs[0], cmax_refs[0], m_ref, acc_ref)


def _softmax_finish(o_ref, acc_ref):
    tq = o_ref.shape[0]
    out = acc_ref[:HEAD_DIM, :] / acc_ref[HEAD_DIM:HEAD_DIM + 1, :]
    for hh in range(GROUP):
        o_ref[:, hh * HEAD_DIM:(hh + 1) * HEAD_DIM] = out[:, hh * tq:(hh + 1) * tq].T.astype(o_ref.dtype)


def _dsa_body(qit_ref, wt_ref, ki_ref, qt_ref, k_ref, vt_ref, o_ref,
              keys_ref, gmax_ref, thr_ref, qall_ref, sa_ref, sb_ref, cma_ref, cmb_ref,
              m_ref, acc_ref, *, topk):
    i = pl.program_id(0)
    g = pl.program_id(1)
    tq = CHUNK
    n_chunks = i + 1
    sub = 128

    def causal(c, rows, row_off=0):
        kpos = c * CHUNK + row_off + lax.broadcasted_iota(jnp.int32, (rows, tq), 0)
        qpos = i * tq + lax.broadcasted_iota(jnp.int32, (rows, tq), 1)
        return kpos <= qpos

    @pl.when(g == 0)
    def _():
        gmax_ref[...] = jnp.full(gmax_ref.shape, INT_MIN, jnp.int32)

        def score_chunk(c, carry):
            for part in range(CHUNK // sub):
                r0 = pl.multiple_of(c * CHUNK + part * sub, sub)
                kit = ki_ref[pl.ds(r0, sub), :]
                sc = jnp.zeros((sub, tq), F32)
                for h in range(IDX_HEADS):
                    d = jnp.dot(kit, qit_ref[h * IDX_DIM:(h + 1) * IDX_DIM, :], preferred_element_type=F32)
                    sc = sc + wt_ref[h:h + 1, :] * jnp.maximum(d, 0.0)
                sc = jnp.where(causal(c, sub, part * sub), sc, -jnp.inf)
                bits = pltpu.bitcast(sc, jnp.int32)
                key = bits ^ ((bits >> 31) & INT_MAX)
                keys_ref[pl.ds(r0, sub), :] = key
                gsl = slice(part * sub, (part + 1) * sub)
                gmax_ref[gsl, :] = jnp.maximum(gmax_ref[gsl, :], key)
            return carry

        lax.fori_loop(0, n_chunks, score_chunk, 0)

        def clear_chunk(c, carry):
            keys_ref[pl.ds(pl.multiple_of(c * CHUNK, CHUNK), CHUNK), :] = jnp.full((CHUNK, tq), INT_MIN, jnp.int32)
            return carry

        lax.fori_loop(n_chunks, pl.cdiv(n_chunks, CHUNKS_PER_STEP) * CHUNKS_PER_STEP, clear_chunk, 0)

        def count_ge(t):
            def count_chunk(c, acc):
                blk = keys_ref[pl.ds(pl.multiple_of(c * CHUNK, CHUNK), CHUNK), :]
                ge = jnp.where(blk >= t, 1, 0).astype(jnp.int32)
                return acc + jnp.sum(ge.reshape(CHUNK // SUBLANES, SUBLANES, tq), axis=0)

            part = lax.fori_loop(0, n_chunks, count_chunk, jnp.zeros((SUBLANES, tq), jnp.int32))
            return jnp.sum(part, axis=0, keepdims=True)

        gm = gmax_ref[...]
        lo = jnp.min(gm, axis=0, keepdims=True)
        top = jnp.max(gm, axis=0, keepdims=True)
        hi = jnp.where(top == INT_MAX, top, top + 1)
        done = (count_ge(lo) == topk).astype(jnp.int32)

        def unfinished(carry):
            it, lo, hi, done = carry
            return jnp.logical_and(it < 34, jnp.min(done) == 0)

        def bisect(carry):
            it, lo, hi, done = carry
            mid = (lo >> 1) + (hi >> 1) + (lo & hi & 1)
            cnt = count_ge(mid)
            active = done == 0
            ok = cnt >= topk
            lo = jnp.where(active & ok, mid, lo)
            hi = jnp.where(active & jnp.logical_not(ok), mid, hi)
            closed = hi <= lo + 1
            done = jnp.where(active & ((cnt == topk) | closed), 1, done)
            return it + 1, lo, hi, done

        _, lo, _, _ = lax.while_loop(unfinished, bisect, (jnp.int32(0), lo, hi, done))
        thr_ref[...] = lo

    _load_q_group(qt_ref, qall_ref)
    _softmax_init(m_ref, acc_ref)
    thr = thr_ref[...]

    def logits(c):
        r0 = pl.multiple_of(c * CHUNK, CHUNK)
        sel = (keys_ref[pl.ds(r0, CHUNK), :] >= thr) & causal(c, CHUNK)
        s = jnp.dot(k_ref[pl.ds(r0, CHUNK), :], qall_ref[...], preferred_element_type=F32)
        return _add_group_bias(s, jnp.where(sel, 0.0, NEG))

    _attend_range(n_chunks, logits, vt_ref, (sa_ref, sb_ref), (cma_ref, cmb_ref), m_ref, acc_ref)
    _softmax_finish(o_ref, acc_ref)


def _dsa(qit, wt, ki, qt, k, vtt, topk):
    s = ki.shape[0]
    tq = CHUNK
    assert s % (tq * CHUNKS_PER_STEP) == 0 and topk <= CHUNK
    gw = GROUP * HEAD_DIM
    return pl.pallas_call(
        functools.partial(_dsa_body, topk=topk),
        grid=(s // tq, N_KV_HEADS),
        in_specs=[
            pl.BlockSpec((IDX_HEADS * IDX_DIM, tq), lambda i, g: (0, i)),
            pl.BlockSpec((IDX_HEADS, tq), lambda i, g: (0, i)),
            pl.BlockSpec((s, IDX_DIM), lambda i, g: (0, 0), pipeline_mode=pl.Buffered(1)),
            pl.BlockSpec((gw, tq), lambda i, g: (g, i)),
            pl.BlockSpec((s, HEAD_DIM), lambda i, g: (0, g)),
            pl.BlockSpec((s // tq, V_ROWS, tq), lambda i, g: (0, g, 0)),
        ],
        out_specs=pl.BlockSpec((tq, gw), lambda i, g: (i, g)),
        out_shape=jax.ShapeDtypeStruct((s, N_MAIN_HEADS * HEAD_DIM), BF16),
        scratch_shapes=[
            pltpu.VMEM((s, tq), jnp.int32),
            pltpu.VMEM((CHUNK, tq), jnp.int32),
            pltpu.VMEM((1, tq), jnp.int32),
            pltpu.VMEM((HEAD_DIM, GROUP * tq), BF16),
            pltpu.VMEM((CHUNKS_PER_STEP * CHUNK, GROUP * tq), F32),
            pltpu.VMEM((CHUNKS_PER_STEP * CHUNK, GROUP * tq), F32),
            pltpu.VMEM((1, GROUP * tq), F32),
            pltpu.VMEM((1, GROUP * tq), F32),
            pltpu.VMEM((1, GROUP * tq), F32),
            pltpu.VMEM((V_ROWS, GROUP * tq), F32),
        ],
        compiler_params=_params("arbitrary", "arbitrary"),
        name="dsa",
    )(qit, wt, ki, qt, k, vtt)


def _moba_body(qt_ref, k_ref, vt_ref, km_ref, o_ref, qaug_ref, sa_ref, sb_ref, cma_ref, cmb_ref,
               m_ref, acc_ref, *, n_sel):
    cur = pl.program_id(0)
    tq = CHUNK
    nb = km_ref.shape[0]
    width = GROUP * tq
    for hh in range(GROUP):
        qaug_ref[:HEAD_DIM, hh * tq:(hh + 1) * tq] = qt_ref[hh * HEAD_DIM:(hh + 1) * HEAD_DIM, :]
    q_all = qaug_ref[:HEAD_DIM, :]

    blk_id = lax.broadcasted_iota(jnp.int32, (nb, width), 0)
    past = blk_id < cur
    gate = jnp.dot(km_ref[...], q_all, preferred_element_type=F32)
    gate = jnp.where(past, gate, -jnp.inf)
    chosen = jnp.zeros((nb, width), jnp.bool_)
    for _ in range(n_sel):
        best = jnp.max(gate, axis=0, keepdims=True)
        first = jnp.min(jnp.where(gate == best, blk_id, nb), axis=0, keepdims=True)
        pick = blk_id == first
        chosen = chosen | pick
        gate = jnp.where(pick, -jnp.inf, gate)
    qaug_ref[HEAD_DIM:HEAD_DIM + nb, :] = jnp.where(chosen & past, 0.0, NEG).astype(BF16)
    if nb < HEAD_DIM:
        qaug_ref[HEAD_DIM + nb:, :] = jnp.zeros((HEAD_DIM - nb, width), BF16)

    _softmax_init(m_ref, acc_ref)
    state = (vt_ref, (sa_ref, sb_ref), (cma_ref, cmb_ref), m_ref, acc_ref)

    def past_logits(c):
        r0 = pl.multiple_of(c * CHUNK, CHUNK)
        return jnp.dot(k_ref[pl.ds(r0, CHUNK), :], qaug_ref[...], preferred_element_type=F32)

    _attend_range(cur, past_logits, *state)

    tri = (lax.broadcasted_iota(jnp.int32, (tq, tq), 0) <= lax.broadcasted_iota(jnp.int32, (tq, tq), 1))

    def own_logits(c):
        r0 = pl.multiple_of(c * CHUNK, CHUNK)
        s = jnp.dot(k_ref[pl.ds(r0, CHUNK), :HEAD_DIM], q_all, preferred_element_type=F32)
        return _add_group_bias(s, jnp.where(tri, 0.0, NEG))

    _attend_single(cur, own_logits, *state)
    _softmax_finish(o_ref, acc_ref)


def _moba(qt, k_aug, vtt, kmeans, n_sel):
    s = k_aug.shape[0]
    assert MOBA_BLOCK == CHUNK and s % (CHUNK * CHUNKS_PER_STEP) == 0
    tq = CHUNK
    nb = s // tq
    assert nb <= HEAD_DIM
    gw = GROUP * HEAD_DIM
    return pl.pallas_call(
        functools.partial(_moba_body, n_sel=n_sel),
        grid=(nb, N_KV_HEADS),
        in_specs=[
            pl.BlockSpec((gw, tq), lambda i, g: (g, i)),
            pl.BlockSpec((s, 2 * HEAD_DIM), lambda i, g: (0, g)),
            pl.BlockSpec((nb, V_ROWS, tq), lambda i, g: (0, g, 0)),
            pl.BlockSpec((nb, HEAD_DIM), lambda i, g: (0, g)),
        ],
        out_specs=pl.BlockSpec((tq, gw), lambda i, g: (i, g)),
        out_shape=jax.ShapeDtypeStruct((s, N_MAIN_HEADS * HEAD_DIM), BF16),
        scratch_shapes=[
            pltpu.VMEM((2 * HEAD_DIM, GROUP * tq), BF16),
            pltpu.VMEM((CHUNKS_PER_STEP * CHUNK, GROUP * tq), F32),
            pltpu.VMEM((CHUNKS_PER_STEP * CHUNK, GROUP * tq), F32),
            pltpu.VMEM((1, GROUP * tq), F32),
            pltpu.VMEM((1, GROUP * tq), F32),
            pltpu.VMEM((1, GROUP * tq), F32),
            pltpu.VMEM((V_ROWS, GROUP * tq), F32),
        ],
        compiler_params=_params("parallel", "arbitrary"),
        name="moba",
    )(qt, k_aug, vtt, kmeans)


def _mem_attn_body(q_ref, kt_ref, v_ref, o_ref):
    for h in range(N_MEM_HEADS):
        sl = slice(h * HEAD_DIM, (h + 1) * HEAD_DIM)
        s = jnp.dot(q_ref[:, sl], kt_ref[sl, :], preferred_element_type=F32)
        p = jnp.exp(s - jnp.max(s, axis=-1, keepdims=True))
        o = jnp.dot(p.astype(BF16), v_ref[:, sl], preferred_element_type=F32)
        o_ref[:, sl] = (o / jnp.sum(p, axis=-1, keepdims=True)).astype(o_ref.dtype)


def _mem_attn(qm, kmt, vm, *, tq=512):
    s, w = qm.shape
    m = vm.shape[0]
    tq = min(tq, s)
    assert s % tq == 0
    return pl.pallas_call(
        _mem_attn_body,
        grid=(s // tq,),
        in_specs=[pl.BlockSpec((tq, w), lambda i: (i, 0)),
                  pl.BlockSpec((w, m), lambda i: (0, 0)),
                  pl.BlockSpec((m, w), lambda i: (0, 0))],
        out_specs=pl.BlockSpec((tq, w), lambda i: (i, 0)),
        out_shape=jax.ShapeDtypeStruct((s, w), BF16),
        compiler_params=_params("parallel"),
        name="mem_attn",
    )(qm, kmt, vm)


def _out_proj_body(x_ref, om_ref, oq_ref, w1_ref, w2_ref, o_ref):
    o_ref[...] = (x_ref[...]
                  + jnp.dot(om_ref[...], w1_ref[...], preferred_element_type=F32)
                  + jnp.dot(oq_ref[...], w2_ref[...], preferred_element_type=F32))


def _out_proj(x, o_main, o_mem, w1, w2, *, tm=512):
    s, d = x.shape
    tm = min(tm, s)
    assert s % tm == 0
    row = lambda i: (i, 0)
    const = lambda i: (0, 0)
    return pl.pallas_call(
        _out_proj_body,
        grid=(s // tm,),
        in_specs=[pl.BlockSpec((tm, d), row),
                  pl.BlockSpec((tm, o_main.shape[1]), row),
                  pl.BlockSpec((tm, o_mem.shape[1]), row),
                  pl.BlockSpec(w1.shape, const, pipeline_mode=pl.Buffered(1)),
                  pl.BlockSpec(w2.shape, const, pipeline_mode=pl.Buffered(1))],
        out_specs=pl.BlockSpec((tm, d), row),
        out_shape=jax.ShapeDtypeStruct((s, d), F32),
        compiler_params=_params("parallel"),
        name="out_proj",
    )(x, o_main, o_mem, w1, w2)


def _rope_tables(positions, dim):
    inv = 1.0 / (ROPE_THETA ** (jnp.arange(0, dim, 2, dtype=F32) / dim))
    ang = positions.astype(F32)[:, None] * inv
    c, s = jnp.cos(ang), jnp.sin(ang)
    reps = LANES // dim
    return jnp.tile(jnp.concatenate([c, c], -1), (1, reps)), jnp.tile(jnp.concatenate([-s, s], -1), (1, reps))


def kernel(x, mem, positions, ffn1_norm, ffn1_w_gate_up, ffn1_w_down, attn_norm, mem_norm, a_w_in, idx_k_norm, b_w_in, w_mem_kv, w_out, ffn2_norm, ffn2_w_gate_up, ffn2_w_down, kv_norm, w_kv_shared, final_norm):
    b, s, d = x.shape
    assert b == 1 and mem.shape[0] == 1
    depth = ffn1_norm.shape[0]
    n_a = a_w_in.shape[0]
    main_w = N_MAIN_HEADS * HEAD_DIM
    kv_w = N_KV_HEADS * HEAD_DIM
    idx_w = IDX_HEADS * IDX_DIM
    mem_w = N_MEM_HEADS * HEAD_DIM
    topk = min(IDX_TOPK_MAX, s // 4)
    nb = s // MOBA_BLOCK
    n_sel = min(MOBA_TOPK_MAX, max(nb - 1, 1))

    cos, sin = _rope_tables(positions[0], HEAD_DIM)
    cosi, sini = _rope_tables(positions[0], IDX_DIM)
    tables = (cos, sin, cosi, sini)
    mem_tables = tuple(t[:mem.shape[1]] for t in tables)
    no_gk = jnp.zeros((1, LANES), F32)

    xs = x[0]
    mem2 = mem[0]
    k_sh = vtt_sh = kmeans = None
    for i in range(depth):
        if i == n_a:
            wk = w_kv_shared[:, :kv_w].astype(BF16)
            wv = w_kv_shared[:, kv_w:].astype(BF16)
            k_sh, vtt_sh, km = _kv_shared(xs, kv_norm, cos, sin, wk, wv)
            kmeans = km.reshape(nb, kv_w).astype(BF16)

        last = i == depth - 1
        xs = _ffn(xs, ffn1_norm[i], ffn1_w_gate_up[i].astype(BF16), ffn1_w_down[i].astype(BF16))

        wm = w_mem_kv[i].astype(BF16)
        mk, mv = _proj(mem2, mem_norm[i], mem_tables, no_gk, [wm[:, :mem_w], wm[:, mem_w:]],
                       ["plain", "plain"], [BF16, BF16])
        if i < n_a:
            wa = a_w_in[i]
            o0 = 0
            ws = []
            for width in (main_w, kv_w, kv_w, idx_w, IDX_DIM + IDX_HEADS, mem_w):
                ws.append(wa[:, o0:o0 + width])
                o0 += width
            w_kiwi = jnp.pad(ws[4], ((0, 0), (0, LANES - ws[4].shape[1])))
            ws = [w.astype(BF16) for w in (ws[0], ws[1], ws[2], ws[3], w_kiwi, ws[5])]
            gk = jnp.pad(idx_k_norm[i], (0, LANES - IDX_DIM)).reshape(1, LANES)
            qt, k, vtt, qit, kiwi, qm = _proj(
                xs, attn_norm[i], tables, gk, ws,
                ["rope_scaled_t", "rope", "value_chunks", "rope_idx_t", "kiwi", "scaled"],
                [BF16, BF16, BF16, BF16, F32, BF16])
            ki = kiwi[:, :IDX_DIM].astype(BF16)
            wt = kiwi[:, IDX_DIM:IDX_DIM + IDX_HEADS].T
            o_main = _dsa(qit, wt, ki, qt, k, vtt, topk)
        else:
            wb = b_w_in[i - n_a]
            ws = [wb[:, :main_w].astype(BF16), wb[:, main_w:].astype(BF16)]
            qt, qm = _proj(xs, attn_norm[i], tables, no_gk, ws, ["rope_scaled_t", "scaled"], [BF16, BF16])
            o_main = _moba(qt, k_sh, vtt_sh, kmeans, n_sel)
        o_mem = _mem_attn(qm, mk.T, mv)
        wo = w_out[i].astype(BF16)
        xs = _out_proj(xs, o_main, o_mem, wo[:main_w], wo[main_w:])

        xs = _ffn(xs, ffn2_norm[i], ffn2_w_gate_up[i].astype(BF16), ffn2_w_down[i].astype(BF16),
                  final_norm if last else None)
    return xs[None]
```

```python
import functools

import jax
import jax.numpy as jnp
import numpy as np
from jax import lax
from jax.experimental import pallas as pl
from jax.experimental.pallas import tpu as pltpu

HEAD_DIM = 128
N_MAIN_HEADS = 12
N_KV_HEADS = 4
GROUP = N_MAIN_HEADS // N_KV_HEADS
N_MEM_HEADS = 4
IDX_HEADS = 16
IDX_DIM = 64
IDX_TOPK_MAX = 256
MOBA_BLOCK = 256
MOBA_TOPK_MAX = 3
ROPE_THETA = 10000.0
RMS_EPS = 1e-6

LANES = 128
SUBLANES = 8
VMEM_LIMIT = 56 * 1024 * 1024
NEG = -1e30
LOG2_E = 1.4426950408889634
INT_MIN = -2 ** 31
INT_MAX = 2 ** 31 - 1
CHUNK = 256
CHUNKS_PER_STEP = 4
ONES_ROWS = 16
V_ROWS = HEAD_DIM + ONES_ROWS

F32 = jnp.float32
BF16 = jnp.bfloat16


def _params(*sem):
    return pltpu.CompilerParams(dimension_semantics=sem, vmem_limit_bytes=VMEM_LIMIT)


def _rms(x, gain):
    return x * lax.rsqrt(jnp.mean(x * x, axis=-1, keepdims=True) + RMS_EPS) * gain


def _ffn_body(x_ref, g_ref, wg_ref, wu_ref, wd_ref, pg_ref, o_ref, h_ref, *, final_norm):
    j = pl.program_id(1)

    @pl.when(j == 0)
    def _():
        x = x_ref[...]
        h_ref[...] = _rms(x, g_ref[...]).astype(BF16)
        o_ref[...] = x

    h = h_ref[...]
    gate = jnp.dot(h, wg_ref[...], preferred_element_type=F32)
    up = jnp.dot(h, wu_ref[...], preferred_element_type=F32)
    act = (gate * (0.5 / (1.0 + jnp.exp(-gate))) * up).astype(BF16)
    o_ref[...] += jnp.dot(act, wd_ref[...], preferred_element_type=F32)

    if final_norm:
        @pl.when(j == pl.num_programs(1) - 1)
        def _():
            o_ref[...] = _rms(o_ref[...], pg_ref[...])


def _ffn(x, gain, w_gate_up, w_down, post_gain=None, *, tm=1024, tf=512):
    s, d = x.shape
    f = w_down.shape[0]
    tm = min(tm, s)
    tf = min(tf, f)
    assert s % tm == 0 and f % tf == 0
    nf = f // tf
    final_norm = post_gain is not None
    pg = post_gain if final_norm else gain
    return pl.pallas_call(
        functools.partial(_ffn_body, final_norm=final_norm),
        grid=(s // tm, nf),
        in_specs=[
            pl.BlockSpec((tm, d), lambda i, j: (i, 0), pipeline_mode=pl.Buffered(1)),
            pl.BlockSpec((1, d), lambda i, j: (0, 0)),
            pl.BlockSpec((d, tf), lambda i, j: (0, j)),
            pl.BlockSpec((d, tf), lambda i, j: (0, j + nf)),
            pl.BlockSpec((tf, d), lambda i, j: (j, 0)),
            pl.BlockSpec((1, d), lambda i, j: (0, 0)),
        ],
        out_specs=pl.BlockSpec((tm, d), lambda i, j: (i, 0)),
        out_shape=jax.ShapeDtypeStruct((s, d), F32),
        scratch_shapes=[pltpu.VMEM((tm, d), BF16)],
        compiler_params=_params("parallel", "arbitrary"),
        name="ffn",
    )(x, gain.reshape(1, d), w_gate_up, w_gate_up, w_down, pg.reshape(1, d))


def _rope_heads(y, cos, sin, o_ref, scale, transposed=False):
    for h in range(y.shape[1] // HEAD_DIM):
        t = y[:, h * HEAD_DIM:(h + 1) * HEAD_DIM]
        r = t * cos + pltpu.roll(t, HEAD_DIM // 2, 1) * sin
        if scale != 1.0:
            r = r * scale
        if transposed:
            o_ref[h * HEAD_DIM:(h + 1) * HEAD_DIM, :] = r.T.astype(o_ref.dtype)
        else:
            o_ref[:, h * HEAD_DIM:(h + 1) * HEAD_DIM] = r.astype(o_ref.dtype)


def _store_value_chunks(y, o_ref):
    for b in range(y.shape[0] // CHUNK):
        for g in range(N_KV_HEADS):
            blk = y[b * CHUNK:(b + 1) * CHUNK, g * HEAD_DIM:(g + 1) * HEAD_DIM]
            o_ref[b, g * V_ROWS:g * V_ROWS + HEAD_DIM, :] = blk.T.astype(o_ref.dtype)
            o_ref[b, g * V_ROWS + HEAD_DIM:(g + 1) * V_ROWS, :] = jnp.ones((ONES_ROWS, CHUNK), o_ref.dtype)


def _rot_idx(t):
    lane = lax.broadcasted_iota(jnp.int32, t.shape, 1)
    first_half = (lane & (IDX_DIM // 2)) == 0
    return jnp.where(first_half, pltpu.roll(t, LANES - IDX_DIM // 2, 1),
                     pltpu.roll(t, IDX_DIM // 2, 1))


def _proj_body(*refs, kinds, q_scale, idx_w_scale):
    x_ref, g_ref, cos_ref, sin_ref, cosi_ref, sini_ref, gk_ref = refs[:7]
    n = len(kinds)
    w_refs = refs[7:7 + n]
    o_refs = refs[7 + n:7 + 2 * n]
    h = _rms(x_ref[...], g_ref[...]).astype(BF16)
    for kind, w_ref, o_ref in zip(kinds, w_refs, o_refs):
        y = jnp.dot(h, w_ref[...], preferred_element_type=F32)
        if kind == "plain":
            o_ref[...] = y.astype(o_ref.dtype)
        elif kind == "scaled":
            o_ref[...] = (y * q_scale).astype(o_ref.dtype)
        elif kind == "rope":
            _rope_heads(y, cos_ref[...], sin_ref[...], o_ref, 1.0)
        elif kind == "rope_scaled_t":
            _rope_heads(y, cos_ref[...], sin_ref[...], o_ref, q_scale * LOG2_E, transposed=True)
        elif kind == "rope_idx_t":
            cosi, sini = cosi_ref[...], sini_ref[...]
            for c in range(y.shape[1] // LANES):
                t = y[:, c * LANES:(c + 1) * LANES]
                o_ref[c * LANES:(c + 1) * LANES, :] = (t * cosi + _rot_idx(t) * sini).T.astype(o_ref.dtype)
        elif kind == "value_chunks":
            _store_value_chunks(y, o_ref)
        elif kind == "kiwi":
            lane = lax.broadcasted_iota(jnp.int32, y.shape, 1)
            is_k = lane < IDX_DIM
            kk = jnp.where(is_k, y, 0.0)
            ms = jnp.sum(kk * kk, axis=-1, keepdims=True) * (1.0 / IDX_DIM)
            kn = kk * lax.rsqrt(ms + RMS_EPS) * gk_ref[...]
            kr = kn * cosi_ref[...] + _rot_idx(kn) * sini_ref[...]
            o_ref[...] = jnp.where(is_k, kr, y * idx_w_scale)
        else:
            raise ValueError(kind)


def _proj(x, gain, tables, gk, weights, kinds, out_dtypes, *, tm=256):
    s, d = x.shape
    tm = min(tm, s)
    assert s % tm == 0
    cos, sin, cosi, sini = tables
    row = lambda i: (i, 0)
    const = lambda i: (0, 0)
    in_specs = [pl.BlockSpec((tm, d), row), pl.BlockSpec((1, d), const)]
    in_specs += [pl.BlockSpec((tm, LANES), row)] * 4
    in_specs += [pl.BlockSpec((1, LANES), const)]
    in_specs += [pl.BlockSpec(w.shape, const, pipeline_mode=pl.Buffered(1)) for w in weights]
    out_specs, out_shape = [], []
    for w, kind, dt in zip(weights, kinds, out_dtypes):
        n = w.shape[1]
        if kind.endswith("_t"):
            out_specs.append(pl.BlockSpec((n, tm), lambda i: (0, i)))
            out_shape.append(jax.ShapeDtypeStruct((n, s), dt))
        elif kind == "value_chunks":
            assert tm % CHUNK == 0 and n == N_KV_HEADS * HEAD_DIM
            out_specs.append(pl.BlockSpec((tm // CHUNK, N_KV_HEADS * V_ROWS, CHUNK), lambda i: (i, 0, 0)))
            out_shape.append(jax.ShapeDtypeStruct((s // CHUNK, N_KV_HEADS * V_ROWS, CHUNK), dt))
        else:
            out_specs.append(pl.BlockSpec((tm, n), row))
            out_shape.append(jax.ShapeDtypeStruct((s, n), dt))
    return pl.pallas_call(
        functools.partial(_proj_body, kinds=tuple(kinds), q_scale=HEAD_DIM ** -0.5,
                          idx_w_scale=(IDX_HEADS ** -0.5) * (IDX_DIM ** -0.5)),
        grid=(s // tm,),
        in_specs=in_specs,
        out_specs=out_specs,
        out_shape=out_shape,
        compiler_params=_params("parallel"),
        name="proj",
    )(x, gain.reshape(1, d), cos, sin, cosi, sini, gk, *weights)


def _kv_shared_body(x_ref, g_ref, cos_ref, sin_ref, wk_ref, wv_ref, k_ref, v_ref, km_ref):
    h = _rms(x_ref[...], g_ref[...]).astype(BF16)
    yk = jnp.dot(h, wk_ref[...], preferred_element_type=F32)
    cos, sin = cos_ref[...], sin_ref[...]
    tm = yk.shape[0]
    first_blk = pl.program_id(0) * (tm // MOBA_BLOCK)
    lane = lax.broadcasted_iota(jnp.int32, (MOBA_BLOCK, HEAD_DIM), 1)
    for hd in range(yk.shape[1] // HEAD_DIM):
        t = yk[:, hd * HEAD_DIM:(hd + 1) * HEAD_DIM]
        r = t * cos + pltpu.roll(t, HEAD_DIM // 2, 1) * sin
        k_ref[:, 2 * hd * HEAD_DIM:(2 * hd + 1) * HEAD_DIM] = r.astype(k_ref.dtype)
        for b in range(tm // MOBA_BLOCK):
            rows = slice(b * MOBA_BLOCK, (b + 1) * MOBA_BLOCK)
            k_ref[rows, (2 * hd + 1) * HEAD_DIM:(2 * hd + 2) * HEAD_DIM] = (
                jnp.where(lane == first_blk + b, 1.0, 0.0).astype(k_ref.dtype))
            km_ref[b, :, hd * HEAD_DIM:(hd + 1) * HEAD_DIM] = (
                jnp.sum(r[rows], axis=0, keepdims=True) * (1.0 / MOBA_BLOCK))
    _store_value_chunks(jnp.dot(h, wv_ref[...], preferred_element_type=F32), v_ref)


def _kv_shared(x, gain, cos, sin, wk, wv, *, tm=256):
    s, d = x.shape
    assert s % MOBA_BLOCK == 0
    tm = min(tm, s)
    assert tm % MOBA_BLOCK == 0 and s % tm == 0
    nkv = wk.shape[1]
    row = lambda i: (i, 0)
    const = lambda i: (0, 0)
    return pl.pallas_call(
        _kv_shared_body,
        grid=(s // tm,),
        in_specs=[pl.BlockSpec((tm, d), row), pl.BlockSpec((1, d), const),
                  pl.BlockSpec((tm, LANES), row), pl.BlockSpec((tm, LANES), row),
                  pl.BlockSpec(wk.shape, const, pipeline_mode=pl.Buffered(1)),
                  pl.BlockSpec(wv.shape, const, pipeline_mode=pl.Buffered(1))],
        out_specs=[pl.BlockSpec((tm, 2 * nkv), row),
                   pl.BlockSpec((tm // CHUNK, N_KV_HEADS * V_ROWS, CHUNK), lambda i: (i, 0, 0)),
                   pl.BlockSpec((tm // MOBA_BLOCK, 1, nkv), lambda i: (i, 0, 0))],
        out_shape=[jax.ShapeDtypeStruct((s, 2 * nkv), BF16),
                   jax.ShapeDtypeStruct((s // CHUNK, N_KV_HEADS * V_ROWS, CHUNK), BF16),
                   jax.ShapeDtypeStruct((s // MOBA_BLOCK, 1, nkv), F32)],
        compiler_params=_params("parallel"),
        name="kv_shared",
    )(x, gain.reshape(1, d), cos, sin, wk, wv)


def _load_q_group(qt_ref, qall_ref):
    tq = qt_ref.shape[1]
    for hh in range(GROUP):
        qall_ref[:, hh * tq:(hh + 1) * tq] = qt_ref[hh * HEAD_DIM:(hh + 1) * HEAD_DIM, :]


def _softmax_init(m_ref, acc_ref):
    m_ref[...] = jnp.full(m_ref.shape, NEG, F32)
    acc_ref[...] = jnp.zeros(acc_ref.shape, F32)


def _add_group_bias(s, b):
    tq = b.shape[1]
    return jnp.concatenate([s[:, hh * tq:(hh + 1) * tq] + b for hh in range(GROUP)], axis=1)


def _logits_pass(chunk_ids, logits_fn, s_ref, cmax_ref):
    cmax = None
    for j, c in enumerate(chunk_ids):
        s = logits_fn(c)
        s_ref[j * CHUNK:(j + 1) * CHUNK, :] = s
        cm = jnp.max(s, axis=0, keepdims=True)
        cmax = cm if cmax is None else jnp.maximum(cmax, cm)
    cmax_ref[...] = cmax


def _softmax_pass(chunk_ids, vt_ref, s_ref, cmax_ref, m_ref, acc_ref):
    m_old = m_ref[...]
    m_new = jnp.maximum(m_old, cmax_ref[...])
    alpha = jnp.exp2(m_old - m_new)
    pv = None
    for j, c in enumerate(chunk_ids):
        p = jnp.exp2(s_ref[j * CHUNK:(j + 1) * CHUNK, :] - m_new)
        d = jnp.dot(vt_ref[c], p.astype(BF16), preferred_element_type=F32)
        pv = d if pv is None else pv + d
    m_ref[...] = m_new
    acc_ref[...] = alpha * acc_ref[...] + pv


def _attend_range(n_chunks, logits_fn, vt_ref, s_refs, cmax_refs, m_ref, acc_ref):
    n_steps = pl.cdiv(n_chunks, CHUNKS_PER_STEP)
    last = jnp.maximum(n_steps - 1, 0)

    def chunks(step):
        return [step * CHUNKS_PER_STEP + j for j in range(CHUNKS_PER_STEP)]

    def logits(step, slot):
        _logits_pass(chunks(jnp.minimum(step, last)), logits_fn, s_refs[slot], cmax_refs[slot])

    def softmax(step, slot):
        _softmax_pass(chunks(step), vt_ref, s_refs[slot], cmax_refs[slot], m_ref, acc_ref)

    logits(0, 0)

    def pair(u, carry):
        t = 2 * u
        logits(t + 1, 1)
        softmax(t, 0)

        @pl.when(t + 1 < n_steps)
        def _():
            logits(t + 2, 0)
            softmax(t + 1, 1)

        return carry

    lax.fori_loop(0, pl.cdiv(n_steps, 2), pair, 0)


def _attend_single(c, logits_fn, vt_ref, s_refs, cmax_refs, m_ref, acc_ref):
    _logits_pass([c], logits_fn, s_refs[0], cmax_refs[0])
    _softmax_pass([c], vt_ref, s_refs[0], cmax_refs[0], m_ref, acc_ref)


def _softmax_finish(o_ref, acc_ref):
    tq = o_ref.shape[0]
    out = acc_ref[:HEAD_DIM, :] / acc_ref[HEAD_DIM:HEAD_DIM + 1, :]
    for hh in range(GROUP):
        o_ref[:, hh * HEAD_DIM:(hh + 1) * HEAD_DIM] = out[:, hh * tq:(hh + 1) * tq].T.astype(o_ref.dtype)


def _dsa_body(qit_ref, wt_ref, ki_ref, qt_ref, k_ref, vt_ref, o_ref,
              keys_ref, gmax_ref, thr_ref, qall_ref, sa_ref, sb_ref, cma_ref, cmb_ref,
              m_ref, acc_ref, *, topk):
    i = pl.program_id(0)
    g = pl.program_id(1)
    tq = CHUNK
    n_chunks = i + 1
    sub = 128

    def causal(c, rows, row_off=0):
        kpos = c * CHUNK + row_off + lax.broadcasted_iota(jnp.int32, (rows, tq), 0)
        qpos = i * tq + lax.broadcasted_iota(jnp.int32, (rows, tq), 1)
        return kpos <= qpos

    @pl.when(g == 0)
    def _():
        gmax_ref[...] = jnp.full(gmax_ref.shape, INT_MIN, jnp.int32)

        def score_chunk(c, carry):
            for part in range(CHUNK // sub):
                r0 = pl.multiple_of(c * CHUNK + part * sub, sub)
                kit = ki_ref[pl.ds(r0, sub), :]
                sc = jnp.zeros((sub, tq), F32)
                for h in range(IDX_HEADS):
                    d = jnp.dot(kit, qit_ref[h * IDX_DIM:(h + 1) * IDX_DIM, :], preferred_element_type=F32)
                    sc = sc + wt_ref[h:h + 1, :] * jnp.maximum(d, 0.0)
                sc = jnp.where(causal(c, sub, part * sub), sc, -jnp.inf)
                bits = pltpu.bitcast(sc, jnp.int32)
                key = bits ^ ((bits >> 31) & INT_MAX)
                keys_ref[pl.ds(r0, sub), :] = key
                gsl = slice(part * sub, (part + 1) * sub)
                gmax_ref[gsl, :] = jnp.maximum(gmax_ref[gsl, :], key)
            return carry

        lax.fori_loop(0, n_chunks, score_chunk, 0)

        def clear_chunk(c, carry):
            keys_ref[pl.ds(pl.multiple_of(c * CHUNK, CHUNK), CHUNK), :] = jnp.full((CHUNK, tq), INT_MIN, jnp.int32)
            return carry

        lax.fori_loop(n_chunks, pl.cdiv(n_chunks, CHUNKS_PER_STEP) * CHUNKS_PER_STEP, clear_chunk, 0)

        def count_ge(t):
            def count_chunk(c, acc):
                blk = keys_ref[pl.ds(pl.multiple_of(c * CHUNK, CHUNK), CHUNK), :]
                ge = jnp.where(blk >= t, 1, 0).astype(jnp.int32)
                return acc + jnp.sum(ge.reshape(CHUNK // SUBLANES, SUBLANES, tq), axis=0)

            part = lax.fori_loop(0, n_chunks, count_chunk, jnp.zeros((SUBLANES, tq), jnp.int32))
            return jnp.sum(part, axis=0, keepdims=True)

        gm = gmax_ref[...]
        lo = jnp.min(gm, axis=0, keepdims=True)
        top = jnp.max(gm, axis=0, keepdims=True)
        hi = jnp.where(top == INT_MAX, top, top + 1)
        done = (count_ge(lo) == topk).astype(F32)

        def unfinished(carry):
            it, lo, hi, done = carry
            return jnp.logical_and(it < 34, jnp.min(done) == 0.0)

        def bisect(carry):
            it, lo, hi, done = carry
            mid = (lo >> 1) + (hi >> 1) + (lo & hi & 1)
            cnt = count_ge(mid)
            active = done == 0.0
            ok = cnt >= topk
            lo = jnp.where(active & ok, mid, lo)
            hi = jnp.where(active & jnp.logical_not(ok), mid, hi)
            closed = hi <= lo + 1
            done = jnp.where(active & ((cnt == topk) | closed), 1.0, done)
            return it + 1, lo, hi, done

        _, lo, _, _ = lax.while_loop(unfinished, bisect, (jnp.int32(0), lo, hi, done))
        thr_ref[...] = lo

    _load_q_group(qt_ref, qall_ref)
    _softmax_init(m_ref, acc_ref)
    thr = thr_ref[...]

    def logits(c):
        r0 = pl.multiple_of(c * CHUNK, CHUNK)
        sel = (keys_ref[pl.ds(r0, CHUNK), :] >= thr) & causal(c, CHUNK)
        s = jnp.dot(k_ref[pl.ds(r0, CHUNK), :], qall_ref[...], preferred_element_type=F32)
        return _add_group_bias(s, jnp.where(sel, 0.0, NEG))

    _attend_range(n_chunks, logits, vt_ref, (sa_ref, sb_ref), (cma_ref, cmb_ref), m_ref, acc_ref)
    _softmax_finish(o_ref, acc_ref)


def _dsa(qit, wt, ki, qt, k, vtt, topk):
    s = ki.shape[0]
    tq = CHUNK
    assert s % (tq * CHUNKS_PER_STEP) == 0 and topk <= CHUNK
    gw = GROUP * HEAD_DIM
    return pl.pallas_call(
        functools.partial(_dsa_body, topk=topk),
        grid=(s // tq, N_KV_HEADS),
        in_specs=[
            pl.BlockSpec((IDX_HEADS * IDX_DIM, tq), lambda i, g: (0, i)),
            pl.BlockSpec((IDX_HEADS, tq), lambda i, g: (0, i)),
            pl.BlockSpec((s, IDX_DIM), lambda i, g: (0, 0), pipeline_mode=pl.Buffered(1)),
            pl.BlockSpec((gw, tq), lambda i, g: (g, i)),
            pl.BlockSpec((s, HEAD_DIM), lambda i, g: (0, g)),
            pl.BlockSpec((s // tq, V_ROWS, tq), lambda i, g: (0, g, 0)),
        ],
        out_specs=pl.BlockSpec((tq, gw), lambda i, g: (i, g)),
        out_shape=jax.ShapeDtypeStruct((s, N_MAIN_HEADS * HEAD_DIM), BF16),
        scratch_shapes=[
            pltpu.VMEM((s, tq), jnp.int32),
            pltpu.VMEM((CHUNK, tq), jnp.int32),
            pltpu.VMEM((1, tq), jnp.int32),
            pltpu.VMEM((HEAD_DIM, GROUP * tq), BF16),
            pltpu.VMEM((CHUNKS_PER_STEP * CHUNK, GROUP * tq), F32),
            pltpu.VMEM((CHUNKS_PER_STEP * CHUNK, GROUP * tq), F32),
            pltpu.VMEM((1, GROUP * tq), F32),
            pltpu.VMEM((1, GROUP * tq), F32),
            pltpu.VMEM((1, GROUP * tq), F32),
            pltpu.VMEM((V_ROWS, GROUP * tq), F32),
        ],
        compiler_params=_params("arbitrary", "arbitrary"),
        name="dsa",
    )(qit, wt, ki, qt, k, vtt)


def _moba_body(qt_ref, k_ref, vt_ref, km_ref, o_ref, qaug_ref, sa_ref, sb_ref, cma_ref, cmb_ref,
               m_ref, acc_ref, *, n_sel):
    cur = pl.program_id(0)
    tq = CHUNK
    nb = km_ref.shape[0]
    width = GROUP * tq
    for hh in range(GROUP):
        qaug_ref[:HEAD_DIM, hh * tq:(hh + 1) * tq] = qt_ref[hh * HEAD_DIM:(hh + 1) * HEAD_DIM, :]
    q_all = qaug_ref[:HEAD_DIM, :]

    blk_id = lax.broadcasted_iota(jnp.int32, (nb, width), 0)
    past = blk_id < cur
    gate = jnp.dot(km_ref[...], q_all, preferred_element_type=F32)
    gate = jnp.where(past, gate, -jnp.inf)
    chosen = jnp.zeros((nb, width), jnp.bool_)
    for _ in range(n_sel):
        best = jnp.max(gate, axis=0, keepdims=True)
        first = jnp.min(jnp.where(gate == best, blk_id, nb), axis=0, keepdims=True)
        pick = blk_id == first
        chosen = chosen | pick
        gate = jnp.where(pick, -jnp.inf, gate)
    qaug_ref[HEAD_DIM:HEAD_DIM + nb, :] = jnp.where(chosen & past, 0.0, NEG).astype(BF16)
    if nb < HEAD_DIM:
        qaug_ref[HEAD_DIM + nb:, :] = jnp.zeros((HEAD_DIM - nb, width), BF16)

    _softmax_init(m_ref, acc_ref)
    state = (vt_ref, (sa_ref, sb_ref), (cma_ref, cmb_ref), m_ref, acc_ref)

    def past_logits(c):
        r0 = pl.multiple_of(c * CHUNK, CHUNK)
        return jnp.dot(k_ref[pl.ds(r0, CHUNK), :], qaug_ref[...], preferred_element_type=F32)

    _attend_range(cur, past_logits, *state)

    tri = (lax.broadcasted_iota(jnp.int32, (tq, tq), 0) <= lax.broadcasted_iota(jnp.int32, (tq, tq), 1))

    def own_logits(c):
        r0 = pl.multiple_of(c * CHUNK, CHUNK)
        s = jnp.dot(k_ref[pl.ds(r0, CHUNK), :HEAD_DIM], q_all, preferred_element_type=F32)
        return _add_group_bias(s, jnp.where(tri, 0.0, NEG))

    _attend_single(cur, own_logits, *state)
    _softmax_finish(o_ref, acc_ref)


def _moba(qt, k_aug, vtt, kmeans, n_sel):
    s = k_aug.shape[0]
    assert MOBA_BLOCK == CHUNK and s % (CHUNK * CHUNKS_PER_STEP) == 0
    tq = CHUNK
    nb = s // tq
    assert nb <= HEAD_DIM
    gw = GROUP * HEAD_DIM
    return pl.pallas_call(
        functools.partial(_moba_body, n_sel=n_sel),
        grid=(nb, N_KV_HEADS),
        in_specs=[
            pl.BlockSpec((gw, tq), lambda i, g: (g, i)),
            pl.BlockSpec((s, 2 * HEAD_DIM), lambda i, g: (0, g)),
            pl.BlockSpec((nb, V_ROWS, tq), lambda i, g: (0, g, 0)),
            pl.BlockSpec((nb, HEAD_DIM), lambda i, g: (0, g)),
        ],
        out_specs=pl.BlockSpec((tq, gw), lambda i, g: (i, g)),
        out_shape=jax.ShapeDtypeStruct((s, N_MAIN_HEADS * HEAD_DIM), BF16),
        scratch_shapes=[
            pltpu.VMEM((2 * HEAD_DIM, GROUP * tq), BF16),
            pltpu.VMEM((CHUNKS_PER_STEP * CHUNK, GROUP * tq), F32),
            pltpu.VMEM((CHUNKS_PER_STEP * CHUNK, GROUP * tq), F32),
            pltpu.VMEM((1, GROUP * tq), F32),
            pltpu.VMEM((1, GROUP * tq), F32),
            pltpu.VMEM((1, GROUP * tq), F32),
            pltpu.VMEM((V_ROWS, GROUP * tq), F32),
        ],
        compiler_params=_params("parallel", "arbitrary"),
        name="moba",
    )(qt, k_aug, vtt, kmeans)


def _mem_attn_body(q_ref, kt_ref, v_ref, o_ref):
    for h in range(N_MEM_HEADS):
        sl = slice(h * HEAD_DIM, (h + 1) * HEAD_DIM)
        s = jnp.dot(q_ref[:, sl], kt_ref[sl, :], preferred_element_type=F32)
        p = jnp.exp(s - jnp.max(s, axis=-1, keepdims=True))
        o = jnp.dot(p.astype(BF16), v_ref[:, sl], preferred_element_type=F32)
        o_ref[:, sl] = (o / jnp.sum(p, axis=-1, keepdims=True)).astype(o_ref.dtype)


def _mem_attn(qm, kmt, vm, *, tq=512):
    s, w = qm.shape
    m = vm.shape[0]
    tq = min(tq, s)
    assert s % tq == 0
    return pl.pallas_call(
        _mem_attn_body,
        grid=(s // tq,),
        in_specs=[pl.BlockSpec((tq, w), lambda i: (i, 0)),
                  pl.BlockSpec((w, m), lambda i: (0, 0)),
                  pl.BlockSpec((m, w), lambda i: (0, 0))],
        out_specs=pl.BlockSpec((tq, w), lambda i: (i, 0)),
        out_shape=jax.ShapeDtypeStruct((s, w), BF16),
        compiler_params=_params("parallel"),
        name="mem_attn",
    )(qm, kmt, vm)


def _out_proj_body(x_ref, om_ref, oq_ref, w1_ref, w2_ref, o_ref):
    o_ref[...] = (x_ref[...]
                  + jnp.dot(om_ref[...], w1_ref[...], preferred_element_type=F32)
                  + jnp.dot(oq_ref[...], w2_ref[...], preferred_element_type=F32))


def _out_proj(x, o_main, o_mem, w1, w2, *, tm=512):
    s, d = x.shape
    tm = min(tm, s)
    assert s % tm == 0
    row = lambda i: (i, 0)
    const = lambda i: (0, 0)
    return pl.pallas_call(
        _out_proj_body,
        grid=(s // tm,),
        in_specs=[pl.BlockSpec((tm, d), row),
                  pl.BlockSpec((tm, o_main.shape[1]), row),
                  pl.BlockSpec((tm, o_mem.shape[1]), row),
                  pl.BlockSpec(w1.shape, const, pipeline_mode=pl.Buffered(1)),
                  pl.BlockSpec(w2.shape, const, pipeline_mode=pl.Buffered(1))],
        out_specs=pl.BlockSpec((tm, d), row),
        out_shape=jax.ShapeDtypeStruct((s, d), F32),
        compiler_params=_params("parallel"),
        name="out_proj",
    )(x, o_main, o_mem, w1, w2)


def _rope_tables(positions, dim):
    inv = 1.0 / (ROPE_THETA ** (jnp.arange(0, dim, 2, dtype=F32) / dim))
    ang = positions.astype(F32)[:, None] * inv
    c, s = jnp.cos(ang), jnp.sin(ang)
    reps = LANES // dim
    return jnp.tile(jnp.concatenate([c, c], -1), (1, reps)), jnp.tile(jnp.concatenate([-s, s], -1), (1, reps))


def kernel(x, mem, positions, ffn1_norm, ffn1_w_gate_up, ffn1_w_down, attn_norm, mem_norm, a_w_in, idx_k_norm, b_w_in, w_mem_kv, w_out, ffn2_norm, ffn2_w_gate_up, ffn2_w_down, kv_norm, w_kv_shared, final_norm):
    b, s, d = x.shape
    assert b == 1 and mem.shape[0] == 1
    depth = ffn1_norm.shape[0]
    n_a = a_w_in.shape[0]
    main_w = N_MAIN_HEADS * HEAD_DIM
    kv_w = N_KV_HEADS * HEAD_DIM
    idx_w = IDX_HEADS * IDX_DIM
    mem_w = N_MEM_HEADS * HEAD_DIM
    topk = min(IDX_TOPK_MAX, s // 4)
    nb = s // MOBA_BLOCK
    n_sel = min(MOBA_TOPK_MAX, max(nb - 1, 1))

    cos, sin = _rope_tables(positions[0], HEAD_DIM)
    cosi, sini = _rope_tables(positions[0], IDX_DIM)
    tables = (cos, sin, cosi, sini)
    mem_tables = tuple(t[:mem.shape[1]] for t in tables)
    no_gk = jnp.zeros((1, LANES), F32)

    xs = x[0]
    mem2 = mem[0]
    k_sh = vtt_sh = kmeans = None
    for i in range(depth):
        if i == n_a:
            wk = w_kv_shared[:, :kv_w].astype(BF16)
            wv = w_kv_shared[:, kv_w:].astype(BF16)
            k_sh, vtt_sh, km = _kv_shared(xs, kv_norm, cos, sin, wk, wv)
            kmeans = km.reshape(nb, kv_w).astype(BF16)

        last = i == depth - 1
        xs = _ffn(xs, ffn1_norm[i], ffn1_w_gate_up[i].astype(BF16), ffn1_w_down[i].astype(BF16))

        wm = w_mem_kv[i].astype(BF16)
        mk, mv = _proj(mem2, mem_norm[i], mem_tables, no_gk, [wm[:, :mem_w], wm[:, mem_w:]],
                       ["plain", "plain"], [BF16, BF16])
        if i < n_a:
            wa = a_w_in[i]
            o0 = 0
            ws = []
            for width in (main_w, kv_w, kv_w, idx_w, IDX_DIM + IDX_HEADS, mem_w):
                ws.append(wa[:, o0:o0 + width])
                o0 += width
            w_kiwi = jnp.pad(ws[4], ((0, 0), (0, LANES - ws[4].shape[1])))
            ws = [w.astype(BF16) for w in (ws[0], ws[1], ws[2], ws[3], w_kiwi, ws[5])]
            gk = jnp.pad(idx_k_norm[i], (0, LANES - IDX_DIM)).reshape(1, LANES)
            qt, k, vtt, qit, kiwi, qm = _proj(
                xs, attn_norm[i], tables, gk, ws,
                ["rope_scaled_t", "rope", "value_chunks", "rope_idx_t", "kiwi", "scaled"],
                [BF16, BF16, BF16, BF16, F32, BF16])
            ki = kiwi[:, :IDX_DIM].astype(BF16)
            wt = kiwi[:, IDX_DIM:IDX_DIM + IDX_HEADS].T
            o_main = _dsa(qit, wt, ki, qt, k, vtt, topk)
        else:
            wb = b_w_in[i - n_a]
            ws = [wb[:, :main_w].astype(BF16), wb[:, main_w:].astype(BF16)]
            qt, qm = _proj(xs, attn_norm[i], tables, no_gk, ws, ["rope_scaled_t", "scaled"], [BF16, BF16])
            o_main = _moba(qt, k_sh, vtt_sh, kmeans, n_sel)
        o_mem = _mem_attn(qm, mk.T, mv)
        wo = w_out[i].astype(BF16)
        xs = _out_proj(xs, o_main, o_mem, wo[:main_w], wo[main_w:])

        xs = _ffn(xs, ffn2_norm[i], ffn2_w_gate_up[i].astype(BF16), ffn2_w_down[i].astype(BF16),
                  final_norm if last else None)
    return xs[None]
```

```python
import functools

import jax
import jax.numpy as jnp
import numpy as np
from jax import lax
from jax.experimental import pallas as pl
from jax.experimental.pallas import tpu as pltpu

HEAD_DIM = 128
N_MAIN_HEADS = 12
N_KV_HEADS = 4
GROUP = N_MAIN_HEADS // N_KV_HEADS
N_MEM_HEADS = 4
IDX_HEADS = 16
IDX_DIM = 64
IDX_TOPK_MAX = 256
MOBA_BLOCK = 256
MOBA_TOPK_MAX = 3
ROPE_THETA = 10000.0
RMS_EPS = 1e-6

LANES = 128
SUBLANES = 8
VMEM_LIMIT = 56 * 1024 * 1024
NEG = -1e30
LOG2_E = 1.4426950408889634
INT_MIN = -2 ** 31
INT_MAX = 2 ** 31 - 1
CHUNK = 256
CHUNKS_PER_STEP = 4
CAND_LEVELS = 2
CAND_MAX = 16
MAX_PASSES = 40
BF16_TILE_ROWS = 2 * SUBLANES
ONES_ROWS = BF16_TILE_ROWS
V_ROWS = HEAD_DIM + ONES_ROWS

F32 = jnp.float32
BF16 = jnp.bfloat16


def _params(*sem):
    return pltpu.CompilerParams(dimension_semantics=sem, vmem_limit_bytes=VMEM_LIMIT)


def _rms(x, gain):
    return x * lax.rsqrt(jnp.mean(x * x, axis=-1, keepdims=True) + RMS_EPS) * gain


def _ffn_body(x_ref, g_ref, wg_ref, wu_ref, wd_ref, pg_ref, o_ref, h_ref, *, final_norm):
    j = pl.program_id(1)

    @pl.when(j == 0)
    def _():
        x = x_ref[...]
        h_ref[...] = _rms(x, g_ref[...]).astype(BF16)
        o_ref[...] = x

    h = h_ref[...]
    gate = jnp.dot(h, wg_ref[...], preferred_element_type=F32)
    up = jnp.dot(h, wu_ref[...], preferred_element_type=F32)
    act = (gate * (0.5 / (1.0 + jnp.exp(-gate))) * up).astype(BF16)
    o_ref[...] += jnp.dot(act, wd_ref[...], preferred_element_type=F32)

    if final_norm:
        @pl.when(j == pl.num_programs(1) - 1)
        def _():
            o_ref[...] = _rms(o_ref[...], pg_ref[...])


def _ffn(x, gain, w_gate_up, w_down, post_gain=None, *, tm=512, tf=512):
    s, d = x.shape
    f = w_down.shape[0]
    tm = min(tm, s)
    tf = min(tf, f)
    assert s % tm == 0 and f % tf == 0
    nf = f // tf
    final_norm = post_gain is not None
    pg = post_gain if final_norm else gain
    return pl.pallas_call(
        functools.partial(_ffn_body, final_norm=final_norm),
        grid=(s // tm, nf),
        in_specs=[
            pl.BlockSpec((tm, d), lambda i, j: (i, 0)),
            pl.BlockSpec((1, d), lambda i, j: (0, 0)),
            pl.BlockSpec((d, tf), lambda i, j: (0, j)),
            pl.BlockSpec((d, tf), lambda i, j: (0, j + nf)),
            pl.BlockSpec((tf, d), lambda i, j: (j, 0)),
            pl.BlockSpec((1, d), lambda i, j: (0, 0)),
        ],
        out_specs=pl.BlockSpec((tm, d), lambda i, j: (i, 0)),
        out_shape=jax.ShapeDtypeStruct((s, d), F32),
        scratch_shapes=[pltpu.VMEM((tm, d), BF16)],
        compiler_params=_params("parallel", "arbitrary"),
        name="ffn",
    )(x, gain.reshape(1, d), w_gate_up, w_gate_up, w_down, pg.reshape(1, d))


def _rope_heads(y, cos, sin, o_ref, scale, transposed=False):
    for h in range(y.shape[1] // HEAD_DIM):
        t = y[:, h * HEAD_DIM:(h + 1) * HEAD_DIM]
        r = t * cos + pltpu.roll(t, HEAD_DIM // 2, 1) * sin
        if scale != 1.0:
            r = r * scale
        if transposed:
            o_ref[h * HEAD_DIM:(h + 1) * HEAD_DIM, :] = r.T.astype(o_ref.dtype)
        else:
            o_ref[:, h * HEAD_DIM:(h + 1) * HEAD_DIM] = r.astype(o_ref.dtype)


def _store_value_chunks(y, o_ref):
    for b in range(y.shape[0] // CHUNK):
        for g in range(N_KV_HEADS):
            blk = y[b * CHUNK:(b + 1) * CHUNK, g * HEAD_DIM:(g + 1) * HEAD_DIM]
            o_ref[b, g * V_ROWS:g * V_ROWS + HEAD_DIM, :] = blk.T.astype(o_ref.dtype)
            o_ref[b, g * V_ROWS + HEAD_DIM:(g + 1) * V_ROWS, :] = jnp.ones((ONES_ROWS, CHUNK), o_ref.dtype)


def _rot_idx(t):
    lane = lax.broadcasted_iota(jnp.int32, t.shape, 1)
    first_half = (lane & (IDX_DIM // 2)) == 0
    return jnp.where(first_half, pltpu.roll(t, LANES - IDX_DIM // 2, 1),
                     pltpu.roll(t, IDX_DIM // 2, 1))


def _proj_body(*refs, kinds, q_scale, idx_w_scale):
    x_ref, g_ref, cos_ref, sin_ref, cosi_ref, sini_ref, gk_ref = refs[:7]
    n = len(kinds)
    w_refs = refs[7:7 + n]
    o_refs = refs[7 + n:7 + 2 * n]
    h = _rms(x_ref[...], g_ref[...]).astype(BF16)
    for kind, w_ref, o_ref in zip(kinds, w_refs, o_refs):
        y = jnp.dot(h, w_ref[...], preferred_element_type=F32)
        if kind == "plain":
            o_ref[...] = y.astype(o_ref.dtype)
        elif kind == "scaled":
            o_ref[...] = (y * q_scale).astype(o_ref.dtype)
        elif kind == "rope":
            _rope_heads(y, cos_ref[...], sin_ref[...], o_ref, 1.0)
        elif kind == "rope_scaled_t":
            _rope_heads(y, cos_ref[...], sin_ref[...], o_ref, q_scale * LOG2_E, transposed=True)
        elif kind == "rope_idx_t":
            cosi, sini = cosi_ref[...], sini_ref[...]
            for c in range(y.shape[1] // LANES):
                t = y[:, c * LANES:(c + 1) * LANES]
                o_ref[c * LANES:(c + 1) * LANES, :] = (t * cosi + _rot_idx(t) * sini).T.astype(o_ref.dtype)
        elif kind == "value_chunks":
            _store_value_chunks(y, o_ref)
        elif kind == "kiwi":
            lane = lax.broadcasted_iota(jnp.int32, y.shape, 1)
            is_k = lane < IDX_DIM
            kk = jnp.where(is_k, y, 0.0)
            ms = jnp.sum(kk * kk, axis=-1, keepdims=True) * (1.0 / IDX_DIM)
            kn = kk * lax.rsqrt(ms + RMS_EPS) * gk_ref[...]
            kr = kn * cosi_ref[...] + _rot_idx(kn) * sini_ref[...]
            o_ref[...] = jnp.where(is_k, kr, y * idx_w_scale)
        else:
            raise ValueError(kind)


def _proj(x, gain, tables, gk, weights, kinds, out_dtypes, *, tm=256):
    s, d = x.shape
    tm = min(tm, s)
    assert s % tm == 0
    cos, sin, cosi, sini = tables
    row = lambda i: (i, 0)
    const = lambda i: (0, 0)
    in_specs = [pl.BlockSpec((tm, d), row), pl.BlockSpec((1, d), const)]
    in_specs += [pl.BlockSpec((tm, LANES), row)] * 4
    in_specs += [pl.BlockSpec((1, LANES), const)]
    in_specs += [pl.BlockSpec(w.shape, const, pipeline_mode=pl.Buffered(1)) for w in weights]
    out_specs, out_shape = [], []
    for w, kind, dt in zip(weights, kinds, out_dtypes):
        n = w.shape[1]
        if kind.endswith("_t"):
            out_specs.append(pl.BlockSpec((n, tm), lambda i: (0, i)))
            out_shape.append(jax.ShapeDtypeStruct((n, s), dt))
        elif kind == "value_chunks":
            assert tm % CHUNK == 0 and n == N_KV_HEADS * HEAD_DIM
            out_specs.append(pl.BlockSpec((tm // CHUNK, N_KV_HEADS * V_ROWS, CHUNK), lambda i: (i, 0, 0)))
            out_shape.append(jax.ShapeDtypeStruct((s // CHUNK, N_KV_HEADS * V_ROWS, CHUNK), dt))
        else:
            out_specs.append(pl.BlockSpec((tm, n), row))
            out_shape.append(jax.ShapeDtypeStruct((s, n), dt))
    return pl.pallas_call(
        functools.partial(_proj_body, kinds=tuple(kinds), q_scale=HEAD_DIM ** -0.5,
                          idx_w_scale=(IDX_HEADS ** -0.5) * (IDX_DIM ** -0.5)),
        grid=(s // tm,),
        in_specs=in_specs,
        out_specs=out_specs,
        out_shape=out_shape,
        compiler_params=_params("parallel"),
        name="proj",
    )(x, gain.reshape(1, d), cos, sin, cosi, sini, gk, *weights)


def _kv_shared_body(x_ref, g_ref, cos_ref, sin_ref, wk_ref, wv_ref, k_ref, v_ref, km_ref):
    h = _rms(x_ref[...], g_ref[...]).astype(BF16)
    yk = jnp.dot(h, wk_ref[...], preferred_element_type=F32)
    cos, sin = cos_ref[...], sin_ref[...]
    tm = yk.shape[0]
    first_blk = pl.program_id(0) * (tm // MOBA_BLOCK)
    lane = lax.broadcasted_iota(jnp.int32, (MOBA_BLOCK, HEAD_DIM), 1)
    for hd in range(yk.shape[1] // HEAD_DIM):
        t = yk[:, hd * HEAD_DIM:(hd + 1) * HEAD_DIM]
        r = t * cos + pltpu.roll(t, HEAD_DIM // 2, 1) * sin
        k_ref[:, 2 * hd * HEAD_DIM:(2 * hd + 1) * HEAD_DIM] = r.astype(k_ref.dtype)
        for b in range(tm // MOBA_BLOCK):
            rows = slice(b * MOBA_BLOCK, (b + 1) * MOBA_BLOCK)
            k_ref[rows, (2 * hd + 1) * HEAD_DIM:(2 * hd + 2) * HEAD_DIM] = (
                jnp.where(lane == first_blk + b, 1.0, 0.0).astype(k_ref.dtype))
            km_ref[b, :, hd * HEAD_DIM:(hd + 1) * HEAD_DIM] = (
                jnp.sum(r[rows], axis=0, keepdims=True) * (1.0 / MOBA_BLOCK))
    _store_value_chunks(jnp.dot(h, wv_ref[...], preferred_element_type=F32), v_ref)


def _kv_shared(x, gain, cos, sin, wk, wv):
    s, d = x.shape
    tm = MOBA_BLOCK
    assert s % tm == 0 and CHUNK == MOBA_BLOCK
    n = s // tm
    nkv = wk.shape[1]
    src = lambda i: (jnp.minimum(i, n - 1), 0)
    row = lambda i: (i, 0)
    const = lambda i: (0, 0)
    return pl.pallas_call(
        _kv_shared_body,
        grid=(n + 1,),
        in_specs=[pl.BlockSpec((tm, d), src), pl.BlockSpec((1, d), const),
                  pl.BlockSpec((tm, LANES), src), pl.BlockSpec((tm, LANES), src),
                  pl.BlockSpec(wk.shape, const, pipeline_mode=pl.Buffered(1)),
                  pl.BlockSpec(wv.shape, const, pipeline_mode=pl.Buffered(1))],
        out_specs=[pl.BlockSpec((tm, 2 * nkv), row),
                   pl.BlockSpec((1, N_KV_HEADS * V_ROWS, CHUNK), lambda i: (i, 0, 0)),
                   pl.BlockSpec((1, 1, nkv), lambda i: (i, 0, 0))],
        out_shape=[jax.ShapeDtypeStruct((s + tm, 2 * nkv), BF16),
                   jax.ShapeDtypeStruct((n + 1, N_KV_HEADS * V_ROWS, CHUNK), BF16),
                   jax.ShapeDtypeStruct((n + 1, 1, nkv), F32)],
        compiler_params=_params("parallel"),
        name="kv_shared",
    )(x, gain.reshape(1, d), cos, sin, wk, wv)


def _load_q_group(qt_ref, qall_ref):
    tq = qt_ref.shape[1]
    for hh in range(GROUP):
        qall_ref[:, hh * tq:(hh + 1) * tq] = qt_ref[hh * HEAD_DIM:(hh + 1) * HEAD_DIM, :]


def _softmax_init(m_ref, acc_ref):
    m_ref[...] = jnp.full(m_ref.shape, NEG, F32)
    acc_ref[...] = jnp.zeros(acc_ref.shape, F32)


def _add_group_bias(s, b):
    tq = b.shape[1]
    return jnp.concatenate([s[:, hh * tq:(hh + 1) * tq] + b for hh in range(GROUP)], axis=1)


def _logits_pass(chunk_ids, logits_fn, s_ref, cmax_ref):
    cmax = None
    for j, c in enumerate(chunk_ids):
        s = logits_fn(c)
        s_ref[j * CHUNK:(j + 1) * CHUNK, :] = s
        cm = jnp.max(s, axis=0, keepdims=True)
        cmax = cm if cmax is None else jnp.maximum(cmax, cm)
    cmax_ref[...] = cmax


def _softmax_pass(chunk_ids, vt_ref, s_ref, cmax_ref, m_ref, acc_ref):
    m_old = m_ref[...]
    m_new = jnp.maximum(m_old, cmax_ref[...])
    alpha = jnp.exp2(m_old - m_new)
    pv = None
    for j, c in enumerate(chunk_ids):
        p = jnp.exp2(s_ref[j * CHUNK:(j + 1) * CHUNK, :] - m_new)
        d = jnp.dot(vt_ref[c], p.astype(BF16), preferred_element_type=F32)
        pv = d if pv is None else pv + d
    m_ref[...] = m_new
    acc_ref[...] = alpha * acc_ref[...] + pv


def _attend_range(n_chunks, logits_fn, vt_ref, s_refs, cmax_refs, m_ref, acc_ref):
    n_steps = pl.cdiv(n_chunks, CHUNKS_PER_STEP)
    last_chunk = vt_ref.shape[0] - 1

    def chunks(step):
        return [step * CHUNKS_PER_STEP + j for j in range(CHUNKS_PER_STEP)]

    def logits(step, slot):
        _logits_pass(chunks(step), logits_fn, s_refs[slot], cmax_refs[slot])

    def softmax(step, slot):
        ids = [jnp.minimum(c, last_chunk) for c in chunks(step)]
        _softmax_pass(ids, vt_ref, s_refs[slot], cmax_refs[slot], m_ref, acc_ref)

    @pl.when(n_steps > 0)
    def _():
        logits(0, 0)

    def pair(u, carry):
        t = 2 * u
        logits(t + 1, 1)
        softmax(t, 0)

        @pl.when(t + 1 < n_steps)
        def _():
            logits(t + 2, 0)
            softmax(t + 1, 1)

        return carry

    lax.fori_loop(0, pl.cdiv(n_steps, 2), pair, 0)


def _attend_single(c, logits_fn, vt_ref, s_refs, cmax_refs, m_ref, acc_ref):
    _logits_pass([c], logits_fn, s_refs[0], cmax_refs[0])
    _softmax_pass([c], vt_ref, s_refs[0], cmax_refs[0], m_ref, acc_ref)


def _softmax_finish(o_ref, acc_ref):
    tq = o_ref.shape[0]
    out = acc_ref[:HEAD_DIM, :] / acc_ref[HEAD_DIM:HEAD_DIM + 1, :]
    for hh in range(GROUP):
        o_ref[:, hh * HEAD_DIM:(hh + 1) * HEAD_DIM] = out[:, hh * tq:(hh + 1) * tq].T.astype(o_ref.dtype)


def _dsa_body(qit_ref, wt_ref, ki_ref, qt_ref, k_ref, vt_ref, o_ref,
              keys_ref, gmax_ref, cand_ref, thr_ref, qall_ref, sa_ref, sb_ref, cma_ref, cmb_ref,
              m_ref, acc_ref, *, topk):
    i = pl.program_id(0)
    g = pl.program_id(1)
    tq = CHUNK
    n_chunks = i + 1
    sub = 128

    def causal(c, rows, row_off=0):
        kpos = c * CHUNK + row_off + lax.broadcasted_iota(jnp.int32, (rows, tq), 0)
        qpos = i * tq + lax.broadcasted_iota(jnp.int32, (rows, tq), 1)
        return kpos <= qpos

    @pl.when(g == 0)
    def _():
        gmax_ref[...] = jnp.full(gmax_ref.shape, INT_MIN, jnp.int32)

        def score_chunk(c, carry):
            for part in range(CHUNK // sub):
                r0 = pl.multiple_of(c * CHUNK + part * sub, sub)
                kit = ki_ref[pl.ds(r0, sub), :]
                sc = jnp.zeros((sub, tq), F32)
                for h in range(IDX_HEADS):
                    d = jnp.dot(kit, qit_ref[h * IDX_DIM:(h + 1) * IDX_DIM, :], preferred_element_type=F32)
                    sc = sc + wt_ref[h:h + 1, :] * jnp.maximum(d, 0.0)
                sc = jnp.where(causal(c, sub, part * sub), sc, -jnp.inf)
                bits = pltpu.bitcast(sc, jnp.int32)
                key = bits ^ ((bits >> 31) & INT_MAX)
                keys_ref[pl.ds(r0, sub), :] = key
                gsl = slice(part * sub, (part + 1) * sub)
                gmax_ref[gsl, :] = jnp.maximum(gmax_ref[gsl, :], key)
            return carry

        lax.fori_loop(0, n_chunks, score_chunk, 0)

        def count_rows(load, n_blocks, t):
            def count_block(b, acc):
                ge = jnp.where(load(b) >= t, 1, 0).astype(jnp.int32)
                return acc + jnp.sum(ge.reshape(CHUNK // SUBLANES, SUBLANES, tq), axis=0)

            part = lax.fori_loop(0, n_blocks, count_block, jnp.zeros((SUBLANES, tq), jnp.int32))
            return jnp.sum(part, axis=0, keepdims=True)

        def key_chunk(c):
            return keys_ref[pl.ds(pl.multiple_of(c * CHUNK, CHUNK), CHUNK), :]

        def count_all(t):
            return count_rows(key_chunk, n_chunks, t)

        def count_cand(t):
            return count_rows(lambda b: cand_ref[b], CAND_LEVELS, t)

        def any_set(flags):
            return jnp.max(flags) > 0.0

        def bisect_pass(count_fn, base, live, state):
            lo, hi, c_lo, c_hi, done = state
            mid = (lo >> 1) + (hi >> 1) + (lo & hi & 1)
            cnt = count_fn(mid) + base
            up = live & (cnt >= topk)
            down = live & (cnt < topk)
            lo, c_lo = jnp.where(up, mid, lo), jnp.where(up, cnt, c_lo)
            hi, c_hi = jnp.where(down, mid, hi), jnp.where(down, cnt, c_hi)
            finished = live & ((cnt == topk) | (hi <= lo + 1))
            return lo, hi, c_lo, c_hi, jnp.where(finished, 1.0, done)

        def bisect_while(count_fn, base, live_fn, keep_going, state):
            def cond(carry):
                it, state = carry
                return jnp.logical_and(it < MAX_PASSES, keep_going(state))

            def body(carry):
                it, state = carry
                return it + 1, bisect_pass(count_fn, base, live_fn(state), state)

            return lax.while_loop(cond, body, (jnp.int32(0), state))[1]

        def in_bracket(state):
            return state[2] - state[3]

        def active(state):
            return state[4] == 0.0

        gm = gmax_ref[...]
        lo = jnp.min(gm, axis=0, keepdims=True)
        top = jnp.max(gm, axis=0, keepdims=True)
        hi = jnp.where(top == INT_MAX, top, top + 1)
        c_lo = count_all(lo)
        state = (lo, hi, c_lo, jnp.zeros_like(c_lo), (c_lo == topk).astype(F32))

        def wide(state):
            return any_set(jnp.where(active(state) & (in_bracket(state) > CAND_MAX), 1.0, 0.0))

        state = bisect_while(count_all, 0, active, wide, state)

        lo, hi, c_lo, c_hi, _ = state
        cand_ref[...] = jnp.full(cand_ref.shape, INT_MIN, jnp.int32)

        def capture(c, carry):
            blk = key_chunk(c)
            x = jnp.where((blk >= lo) & (blk < hi), blk, INT_MIN)
            for level in range(CAND_LEVELS):
                held = cand_ref[level]
                cand_ref[level] = jnp.maximum(held, x)
                x = jnp.minimum(held, x)
            return carry

        lax.fori_loop(0, n_chunks, capture, 0)
        captured = count_cand(lo) == in_bracket(state)

        def live_captured(state):
            return active(state) & captured

        state = bisect_while(count_cand, c_hi, live_captured,
                             lambda st: any_set(jnp.where(live_captured(st), 1.0, 0.0)), state)

        state = bisect_while(count_all, 0, active,
                             lambda st: any_set(jnp.where(active(st), 1.0, 0.0)), state)
        thr_ref[...] = state[0]

    _load_q_group(qt_ref, qall_ref)
    _softmax_init(m_ref, acc_ref)
    thr = thr_ref[...]

    def logits(c):
        r0 = pl.multiple_of(jnp.minimum(c, n_chunks - 1) * CHUNK, CHUNK)
        sel = (keys_ref[pl.ds(r0, CHUNK), :] >= thr) & causal(c, CHUNK)
        s = jnp.dot(k_ref[pl.ds(r0, CHUNK), :], qall_ref[...], preferred_element_type=F32)
        return _add_group_bias(s, jnp.where(sel, 0.0, NEG))

    _attend_range(n_chunks, logits, vt_ref, (sa_ref, sb_ref), (cma_ref, cmb_ref), m_ref, acc_ref)
    _softmax_finish(o_ref, acc_ref)


def _dsa(qit, wt, ki, qt, k, vtt, topk):
    s = ki.shape[0]
    tq = CHUNK
    assert s % (tq * CHUNKS_PER_STEP) == 0 and topk <= CHUNK
    gw = GROUP * HEAD_DIM
    return pl.pallas_call(
        functools.partial(_dsa_body, topk=topk),
        grid=(s // tq, N_KV_HEADS),
        in_specs=[
            pl.BlockSpec((IDX_HEADS * IDX_DIM, tq), lambda i, g: (0, i)),
            pl.BlockSpec((IDX_HEADS, tq), lambda i, g: (0, i)),
            pl.BlockSpec((s, IDX_DIM), lambda i, g: (0, 0), pipeline_mode=pl.Buffered(1)),
            pl.BlockSpec((gw, tq), lambda i, g: (g, i)),
            pl.BlockSpec((s, HEAD_DIM), lambda i, g: (0, g)),
            pl.BlockSpec((s // tq, V_ROWS, tq), lambda i, g: (0, g, 0)),
        ],
        out_specs=pl.BlockSpec((tq, gw), lambda i, g: (i, g)),
        out_shape=jax.ShapeDtypeStruct((s, N_MAIN_HEADS * HEAD_DIM), BF16),
        scratch_shapes=[
            pltpu.VMEM((s, tq), jnp.int32),
            pltpu.VMEM((CHUNK, tq), jnp.int32),
            pltpu.VMEM((CAND_LEVELS, CHUNK, tq), jnp.int32),
            pltpu.VMEM((1, tq), jnp.int32),
            pltpu.VMEM((HEAD_DIM, GROUP * tq), BF16),
            pltpu.VMEM((CHUNKS_PER_STEP * CHUNK, GROUP * tq), F32),
            pltpu.VMEM((CHUNKS_PER_STEP * CHUNK, GROUP * tq), F32),
            pltpu.VMEM((1, GROUP * tq), F32),
            pltpu.VMEM((1, GROUP * tq), F32),
            pltpu.VMEM((1, GROUP * tq), F32),
            pltpu.VMEM((V_ROWS, GROUP * tq), F32),
        ],
        compiler_params=_params("arbitrary", "arbitrary"),
        name="dsa",
    )(qit, wt, ki, qt, k, vtt)


def _moba_body(qt_ref, k_ref, vt_ref, km_ref, o_ref, qaug_ref, sa_ref, sb_ref, cma_ref, cmb_ref,
               m_ref, acc_ref, *, n_sel):
    cur = pl.program_id(0)
    tq = CHUNK
    nb = km_ref.shape[0]
    width = GROUP * tq
    for hh in range(GROUP):
        qaug_ref[:HEAD_DIM, hh * tq:(hh + 1) * tq] = qt_ref[hh * HEAD_DIM:(hh + 1) * HEAD_DIM, :]
    q_all = qaug_ref[:HEAD_DIM, :]

    blk_id = lax.broadcasted_iota(jnp.int32, (nb, width), 0)
    past = blk_id < cur
    gate = jnp.dot(km_ref[...], q_all, preferred_element_type=F32)
    gate = jnp.where(past, gate, -jnp.inf)
    chosen = jnp.zeros((nb, width), jnp.bool_)
    for _ in range(n_sel):
        best = jnp.max(gate, axis=0, keepdims=True)
        first = jnp.min(jnp.where(gate == best, blk_id, nb), axis=0, keepdims=True)
        pick = blk_id == first
        chosen = chosen | pick
        gate = jnp.where(pick, -jnp.inf, gate)
    qaug_ref[HEAD_DIM:HEAD_DIM + nb, :] = jnp.where(chosen & past, 0.0, NEG).astype(BF16)
    pad = BF16_TILE_ROWS
    qaug_ref[HEAD_DIM + nb:HEAD_DIM + nb + pad, :] = jnp.full((pad, width), NEG, BF16)
    if nb + pad < HEAD_DIM:
        qaug_ref[HEAD_DIM + nb + pad:, :] = jnp.zeros((HEAD_DIM - nb - pad, width), BF16)

    _softmax_init(m_ref, acc_ref)
    state = (vt_ref, (sa_ref, sb_ref), (cma_ref, cmb_ref), m_ref, acc_ref)

    def past_logits(c):
        r0 = pl.multiple_of(jnp.where(c < cur, c, nb) * CHUNK, CHUNK)
        return jnp.dot(k_ref[pl.ds(r0, CHUNK), :], qaug_ref[...], preferred_element_type=F32)

    _attend_range(cur, past_logits, *state)

    tri = (lax.broadcasted_iota(jnp.int32, (tq, tq), 0) <= lax.broadcasted_iota(jnp.int32, (tq, tq), 1))

    def own_logits(c):
        r0 = pl.multiple_of(c * CHUNK, CHUNK)
        s = jnp.dot(k_ref[pl.ds(r0, CHUNK), :HEAD_DIM], q_all, preferred_element_type=F32)
        return _add_group_bias(s, jnp.where(tri, 0.0, NEG))

    _attend_single(cur, own_logits, *state)
    _softmax_finish(o_ref, acc_ref)


def _moba(qt, k_aug, vtt, kmeans, n_sel):
    s = k_aug.shape[0] - CHUNK
    assert MOBA_BLOCK == CHUNK and s % (CHUNK * CHUNKS_PER_STEP) == 0
    tq = CHUNK
    nb = s // tq
    assert nb + BF16_TILE_ROWS <= HEAD_DIM and vtt.shape[0] == nb + 1
    gw = GROUP * HEAD_DIM
    return pl.pallas_call(
        functools.partial(_moba_body, n_sel=n_sel),
        grid=(nb, N_KV_HEADS),
        in_specs=[
            pl.BlockSpec((gw, tq), lambda i, g: (g, i)),
            pl.BlockSpec((s + CHUNK, 2 * HEAD_DIM), lambda i, g: (0, g)),
            pl.BlockSpec((nb + 1, V_ROWS, tq), lambda i, g: (0, g, 0)),
            pl.BlockSpec((nb, HEAD_DIM), lambda i, g: (0, g)),
        ],
        out_specs=pl.BlockSpec((tq, gw), lambda i, g: (i, g)),
        out_shape=jax.ShapeDtypeStruct((s, N_MAIN_HEADS * HEAD_DIM), BF16),
        scratch_shapes=[
            pltpu.VMEM((2 * HEAD_DIM, GROUP * tq), BF16),
            pltpu.VMEM((CHUNKS_PER_STEP * CHUNK, GROUP * tq), F32),
            pltpu.VMEM((CHUNKS_PER_STEP * CHUNK, GROUP * tq), F32),
            pltpu.VMEM((1, GROUP * tq), F32),
            pltpu.VMEM((1, GROUP * tq), F32),
            pltpu.VMEM((1, GROUP * tq), F32),
            pltpu.VMEM((V_ROWS, GROUP * tq), F32),
        ],
        compiler_params=_params("parallel", "arbitrary"),
        name="moba",
    )(qt, k_aug, vtt, kmeans)


def _mem_attn_body(q_ref, kt_ref, v_ref, o_ref):
    for h in range(N_MEM_HEADS):
        sl = slice(h * HEAD_DIM, (h + 1) * HEAD_DIM)
        s = jnp.dot(q_ref[:, sl], kt_ref[sl, :], preferred_element_type=F32)
        p = jnp.exp(s - jnp.max(s, axis=-1, keepdims=True))
        o = jnp.dot(p.astype(BF16), v_ref[:, sl], preferred_element_type=F32)
        o_ref[:, sl] = (o / jnp.sum(p, axis=-1, keepdims=True)).astype(o_ref.dtype)


def _mem_attn(qm, kmt, vm, *, tq=512):
    s, w = qm.shape
    m = vm.shape[0]
    tq = min(tq, s)
    assert s % tq == 0
    return pl.pallas_call(
        _mem_attn_body,
        grid=(s // tq,),
        in_specs=[pl.BlockSpec((tq, w), lambda i: (i, 0)),
                  pl.BlockSpec((w, m), lambda i: (0, 0)),
                  pl.BlockSpec((m, w), lambda i: (0, 0))],
        out_specs=pl.BlockSpec((tq, w), lambda i: (i, 0)),
        out_shape=jax.ShapeDtypeStruct((s, w), BF16),
        compiler_params=_params("parallel"),
        name="mem_attn",
    )(qm, kmt, vm)


def _out_proj_body(x_ref, om_ref, oq_ref, w1_ref, w2_ref, o_ref):
    o_ref[...] = (x_ref[...]
                  + jnp.dot(om_ref[...], w1_ref[...], preferred_element_type=F32)
                  + jnp.dot(oq_ref[...], w2_ref[...], preferred_element_type=F32))


def _out_proj(x, o_main, o_mem, w1, w2, *, tm=512):
    s, d = x.shape
    tm = min(tm, s)
    assert s % tm == 0
    row = lambda i: (i, 0)
    const = lambda i: (0, 0)
    return pl.pallas_call(
        _out_proj_body,
        grid=(s // tm,),
        in_specs=[pl.BlockSpec((tm, d), row),
                  pl.BlockSpec((tm, o_main.shape[1]), row),
                  pl.BlockSpec((tm, o_mem.shape[1]), row),
                  pl.BlockSpec(w1.shape, const, pipeline_mode=pl.Buffered(1)),
                  pl.BlockSpec(w2.shape, const, pipeline_mode=pl.Buffered(1))],
        out_specs=pl.BlockSpec((tm, d), row),
        out_shape=jax.ShapeDtypeStruct((s, d), F32),
        compiler_params=_params("parallel"),
        name="out_proj",
    )(x, o_main, o_mem, w1, w2)


def _rope_tables(positions, dim):
    inv = 1.0 / (ROPE_THETA ** (jnp.arange(0, dim, 2, dtype=F32) / dim))
    ang = positions.astype(F32)[:, None] * inv
    c, s = jnp.cos(ang), jnp.sin(ang)
    reps = LANES // dim
    return jnp.tile(jnp.concatenate([c, c], -1), (1, reps)), jnp.tile(jnp.concatenate([-s, s], -1), (1, reps))


def kernel(x, mem, positions, ffn1_norm, ffn1_w_gate_up, ffn1_w_down, attn_norm, mem_norm, a_w_in, idx_k_norm, b_w_in, w_mem_kv, w_out, ffn2_norm, ffn2_w_gate_up, ffn2_w_down, kv_norm, w_kv_shared, final_norm):
    b, s, d = x.shape
    assert b == 1 and mem.shape[0] == 1
    depth = ffn1_norm.shape[0]
    n_a = a_w_in.shape[0]
    main_w = N_MAIN_HEADS * HEAD_DIM
    kv_w = N_KV_HEADS * HEAD_DIM
    idx_w = IDX_HEADS * IDX_DIM
    mem_w = N_MEM_HEADS * HEAD_DIM
    topk = min(IDX_TOPK_MAX, s // 4)
    nb = s // MOBA_BLOCK
    n_sel = min(MOBA_TOPK_MAX, max(nb - 1, 1))

    cos, sin = _rope_tables(positions[0], HEAD_DIM)
    cosi, sini = _rope_tables(positions[0], IDX_DIM)
    tables = (cos, sin, cosi, sini)
    mem_tables = tuple(t[:mem.shape[1]] for t in tables)
    no_gk = jnp.zeros((1, LANES), F32)

    xs = x[0]
    mem2 = mem[0]
    k_sh = vtt_sh = kmeans = None
    for i in range(depth):
        if i == n_a:
            wk = w_kv_shared[:, :kv_w].astype(BF16)
            wv = w_kv_shared[:, kv_w:].astype(BF16)
            k_sh, vtt_sh, km = _kv_shared(xs, kv_norm, cos, sin, wk, wv)
            kmeans = km[:nb].reshape(nb, kv_w).astype(BF16)

        last = i == depth - 1
        xs = _ffn(xs, ffn1_norm[i], ffn1_w_gate_up[i].astype(BF16), ffn1_w_down[i].astype(BF16))

        wm = w_mem_kv[i].astype(BF16)
        mk, mv = _proj(mem2, mem_norm[i], mem_tables, no_gk, [wm[:, :mem_w], wm[:, mem_w:]],
                       ["plain", "plain"], [BF16, BF16])
        if i < n_a:
            wa = a_w_in[i]
            o0 = 0
            ws = []
            for width in (main_w, kv_w, kv_w, idx_w, IDX_DIM + IDX_HEADS, mem_w):
                ws.append(wa[:, o0:o0 + width])
                o0 += width
            w_kiwi = jnp.pad(ws[4], ((0, 0), (0, LANES - ws[4].shape[1])))
            ws = [w.astype(BF16) for w in (ws[0], ws[1], ws[2], ws[3], w_kiwi, ws[5])]
            gk = jnp.pad(idx_k_norm[i], (0, LANES - IDX_DIM)).reshape(1, LANES)
            qt, k, vtt, qit, kiwi, qm = _proj(
                xs, attn_norm[i], tables, gk, ws,
                ["rope_scaled_t", "rope", "value_chunks", "rope_idx_t", "kiwi", "scaled"],
                [BF16, BF16, BF16, BF16, F32, BF16])
            ki = kiwi[:, :IDX_DIM].astype(BF16)
            wt = kiwi[:, IDX_DIM:IDX_DIM + IDX_HEADS].T
            o_main = _dsa(qit, wt, ki, qt, k, vtt, topk)
        else:
            wb = b_w_in[i - n_a]
            ws = [wb[:, :main_w].astype(BF16), wb[:, main_w:].astype(BF16)]
            qt, qm = _proj(xs, attn_norm[i], tables, no_gk, ws, ["rope_scaled_t", "scaled"], [BF16, BF16])
            o_main = _moba(qt, k_sh, vtt_sh, kmeans, n_sel)
        o_mem = _mem_attn(qm, mk.T, mv)
        wo = w_out[i].astype(BF16)
        xs = _out_proj(xs, o_main, o_mem, wo[:main_w], wo[main_w:])

        xs = _ffn(xs, ffn2_norm[i], ffn2_w_gate_up[i].astype(BF16), ffn2_w_down[i].astype(BF16),
                  final_norm if last else None)
    return xs[None]
```

```python
import functools

import jax
import jax.numpy as jnp
import numpy as np
from jax import lax
from jax.experimental import pallas as pl
from jax.experimental.pallas import tpu as pltpu

HEAD_DIM = 128
N_MAIN_HEADS = 12
N_KV_HEADS = 4
GROUP = N_MAIN_HEADS // N_KV_HEADS
N_MEM_HEADS = 4
IDX_HEADS = 16
IDX_DIM = 64
IDX_TOPK_MAX = 256
MOBA_BLOCK = 256
MOBA_TOPK_MAX = 3
ROPE_THETA = 10000.0
RMS_EPS = 1e-6

LANES = 128
SUBLANES = 8
VMEM_LIMIT = 56 * 1024 * 1024
NEG = -1e30
LOG2_E = 1.4426950408889634
INT_MIN = -2 ** 31
INT_MAX = 2 ** 31 - 1
CHUNK = 256
CHUNKS_PER_STEP = 4
CAND_LEVELS = 2
CAND_MAX = 16
MAX_PASSES = 40
COUNT_UNROLL = 4
BF16_TILE_ROWS = 2 * SUBLANES
ONES_ROWS = BF16_TILE_ROWS
V_ROWS = HEAD_DIM + ONES_ROWS

F32 = jnp.float32
BF16 = jnp.bfloat16


def _params(*sem):
    return pltpu.CompilerParams(dimension_semantics=sem, vmem_limit_bytes=VMEM_LIMIT)


def _rms(x, gain):
    return x * lax.rsqrt(jnp.mean(x * x, axis=-1, keepdims=True) + RMS_EPS) * gain


def _ffn_body(x_ref, g_ref, wg_ref, wu_ref, wd_ref, pg_ref, o_ref, h_ref, *, final_norm):
    j = pl.program_id(1)

    @pl.when(j == 0)
    def _():
        x = x_ref[...]
        h_ref[...] = _rms(x, g_ref[...]).astype(BF16)
        o_ref[...] = x

    h = h_ref[...]
    gate = jnp.dot(h, wg_ref[...], preferred_element_type=F32)
    up = jnp.dot(h, wu_ref[...], preferred_element_type=F32)
    act = (gate * (0.5 / (1.0 + jnp.exp(-gate))) * up).astype(BF16)
    o_ref[...] += jnp.dot(act, wd_ref[...], preferred_element_type=F32)

    if final_norm:
        @pl.when(j == pl.num_programs(1) - 1)
        def _():
            o_ref[...] = _rms(o_ref[...], pg_ref[...])


def _ffn(x, gain, w_gate_up, w_down, post_gain=None, *, tm=512, tf=512):
    s, d = x.shape
    f = w_down.shape[0]
    tm = min(tm, s)
    tf = min(tf, f)
    assert s % tm == 0 and f % tf == 0
    nf = f // tf
    final_norm = post_gain is not None
    pg = post_gain if final_norm else gain
    return pl.pallas_call(
        functools.partial(_ffn_body, final_norm=final_norm),
        grid=(s // tm, nf),
        in_specs=[
            pl.BlockSpec((tm, d), lambda i, j: (i, 0)),
            pl.BlockSpec((1, d), lambda i, j: (0, 0)),
            pl.BlockSpec((d, tf), lambda i, j: (0, j)),
            pl.BlockSpec((d, tf), lambda i, j: (0, j + nf)),
            pl.BlockSpec((tf, d), lambda i, j: (j, 0)),
            pl.BlockSpec((1, d), lambda i, j: (0, 0)),
        ],
        out_specs=pl.BlockSpec((tm, d), lambda i, j: (i, 0)),
        out_shape=jax.ShapeDtypeStruct((s, d), F32),
        scratch_shapes=[pltpu.VMEM((tm, d), BF16)],
        compiler_params=_params("parallel", "arbitrary"),
        name="ffn",
    )(x, gain.reshape(1, d), w_gate_up, w_gate_up, w_down, pg.reshape(1, d))


def _rope_heads(y, cos, sin, o_ref, scale, transposed=False):
    for h in range(y.shape[1] // HEAD_DIM):
        t = y[:, h * HEAD_DIM:(h + 1) * HEAD_DIM]
        r = t * cos + pltpu.roll(t, HEAD_DIM // 2, 1) * sin
        if scale != 1.0:
            r = r * scale
        if transposed:
            o_ref[h * HEAD_DIM:(h + 1) * HEAD_DIM, :] = r.T.astype(o_ref.dtype)
        else:
            o_ref[:, h * HEAD_DIM:(h + 1) * HEAD_DIM] = r.astype(o_ref.dtype)


def _store_value_chunks(y, o_ref):
    for b in range(y.shape[0] // CHUNK):
        for g in range(N_KV_HEADS):
            blk = y[b * CHUNK:(b + 1) * CHUNK, g * HEAD_DIM:(g + 1) * HEAD_DIM]
            o_ref[b, g * V_ROWS:g * V_ROWS + HEAD_DIM, :] = blk.T.astype(o_ref.dtype)
            o_ref[b, g * V_ROWS + HEAD_DIM:(g + 1) * V_ROWS, :] = jnp.ones((ONES_ROWS, CHUNK), o_ref.dtype)


def _rot_idx(t):
    lane = lax.broadcasted_iota(jnp.int32, t.shape, 1)
    first_half = (lane & (IDX_DIM // 2)) == 0
    return jnp.where(first_half, pltpu.roll(t, LANES - IDX_DIM // 2, 1),
                     pltpu.roll(t, IDX_DIM // 2, 1))


def _proj_body(*refs, kinds, q_scale, idx_w_scale):
    x_ref, g_ref, cos_ref, sin_ref, cosi_ref, sini_ref, gk_ref = refs[:7]
    n = len(kinds)
    w_refs = refs[7:7 + n]
    o_refs = refs[7 + n:7 + 2 * n]
    h = _rms(x_ref[...], g_ref[...]).astype(BF16)
    for kind, w_ref, o_ref in zip(kinds, w_refs, o_refs):
        y = jnp.dot(h, w_ref[...], preferred_element_type=F32)
        if kind == "plain":
            o_ref[...] = y.astype(o_ref.dtype)
        elif kind == "scaled":
            o_ref[...] = (y * q_scale).astype(o_ref.dtype)
        elif kind == "rope":
            _rope_heads(y, cos_ref[...], sin_ref[...], o_ref, 1.0)
        elif kind == "rope_scaled_t":
            _rope_heads(y, cos_ref[...], sin_ref[...], o_ref, q_scale * LOG2_E, transposed=True)
        elif kind == "rope_idx_t":
            cosi, sini = cosi_ref[...], sini_ref[...]
            for c in range(y.shape[1] // LANES):
                t = y[:, c * LANES:(c + 1) * LANES]
                o_ref[c * LANES:(c + 1) * LANES, :] = (t * cosi + _rot_idx(t) * sini).T.astype(o_ref.dtype)
        elif kind == "value_chunks":
            _store_value_chunks(y, o_ref)
        elif kind == "kiwi":
            lane = lax.broadcasted_iota(jnp.int32, y.shape, 1)
            is_k = lane < IDX_DIM
            kk = jnp.where(is_k, y, 0.0)
            ms = jnp.sum(kk * kk, axis=-1, keepdims=True) * (1.0 / IDX_DIM)
            kn = kk * lax.rsqrt(ms + RMS_EPS) * gk_ref[...]
            kr = kn * cosi_ref[...] + _rot_idx(kn) * sini_ref[...]
            o_ref[...] = jnp.where(is_k, kr, y * idx_w_scale)
        else:
            raise ValueError(kind)


def _proj(x, gain, tables, gk, weights, kinds, out_dtypes, *, tm=256):
    s, d = x.shape
    tm = min(tm, s)
    assert s % tm == 0
    cos, sin, cosi, sini = tables
    row = lambda i: (i, 0)
    const = lambda i: (0, 0)
    in_specs = [pl.BlockSpec((tm, d), row), pl.BlockSpec((1, d), const)]
    in_specs += [pl.BlockSpec((tm, LANES), row)] * 4
    in_specs += [pl.BlockSpec((1, LANES), const)]
    in_specs += [pl.BlockSpec(w.shape, const, pipeline_mode=pl.Buffered(1)) for w in weights]
    out_specs, out_shape = [], []
    for w, kind, dt in zip(weights, kinds, out_dtypes):
        n = w.shape[1]
        if kind.endswith("_t"):
            out_specs.append(pl.BlockSpec((n, tm), lambda i: (0, i)))
            out_shape.append(jax.ShapeDtypeStruct((n, s), dt))
        elif kind == "value_chunks":
            assert tm % CHUNK == 0 and n == N_KV_HEADS * HEAD_DIM
            out_specs.append(pl.BlockSpec((tm // CHUNK, N_KV_HEADS * V_ROWS, CHUNK), lambda i: (i, 0, 0)))
            out_shape.append(jax.ShapeDtypeStruct((s // CHUNK, N_KV_HEADS * V_ROWS, CHUNK), dt))
        else:
            out_specs.append(pl.BlockSpec((tm, n), row))
            out_shape.append(jax.ShapeDtypeStruct((s, n), dt))
    return pl.pallas_call(
        functools.partial(_proj_body, kinds=tuple(kinds), q_scale=HEAD_DIM ** -0.5,
                          idx_w_scale=(IDX_HEADS ** -0.5) * (IDX_DIM ** -0.5)),
        grid=(s // tm,),
        in_specs=in_specs,
        out_specs=out_specs,
        out_shape=out_shape,
        compiler_params=_params("parallel"),
        name="proj",
    )(x, gain.reshape(1, d), cos, sin, cosi, sini, gk, *weights)


def _kv_shared_body(x_ref, g_ref, cos_ref, sin_ref, wk_ref, wv_ref, k_ref, v_ref, km_ref):
    h = _rms(x_ref[...], g_ref[...]).astype(BF16)
    yk = jnp.dot(h, wk_ref[...], preferred_element_type=F32)
    cos, sin = cos_ref[...], sin_ref[...]
    tm = yk.shape[0]
    first_blk = pl.program_id(0) * (tm // MOBA_BLOCK)
    lane = lax.broadcasted_iota(jnp.int32, (MOBA_BLOCK, HEAD_DIM), 1)
    for hd in range(yk.shape[1] // HEAD_DIM):
        t = yk[:, hd * HEAD_DIM:(hd + 1) * HEAD_DIM]
        r = t * cos + pltpu.roll(t, HEAD_DIM // 2, 1) * sin
        k_ref[:, 2 * hd * HEAD_DIM:(2 * hd + 1) * HEAD_DIM] = r.astype(k_ref.dtype)
        for b in range(tm // MOBA_BLOCK):
            rows = slice(b * MOBA_BLOCK, (b + 1) * MOBA_BLOCK)
            k_ref[rows, (2 * hd + 1) * HEAD_DIM:(2 * hd + 2) * HEAD_DIM] = (
                jnp.where(lane == first_blk + b, 1.0, 0.0).astype(k_ref.dtype))
            km_ref[b, :, hd * HEAD_DIM:(hd + 1) * HEAD_DIM] = (
                jnp.sum(r[rows], axis=0, keepdims=True) * (1.0 / MOBA_BLOCK))
    _store_value_chunks(jnp.dot(h, wv_ref[...], preferred_element_type=F32), v_ref)


def _kv_shared(x, gain, cos, sin, wk, wv):
    s, d = x.shape
    tm = MOBA_BLOCK
    assert s % tm == 0 and CHUNK == MOBA_BLOCK
    n = s // tm
    nkv = wk.shape[1]
    src = lambda i: (jnp.minimum(i, n - 1), 0)
    row = lambda i: (i, 0)
    const = lambda i: (0, 0)
    return pl.pallas_call(
        _kv_shared_body,
        grid=(n + 1,),
        in_specs=[pl.BlockSpec((tm, d), src), pl.BlockSpec((1, d), const),
                  pl.BlockSpec((tm, LANES), src), pl.BlockSpec((tm, LANES), src),
                  pl.BlockSpec(wk.shape, const, pipeline_mode=pl.Buffered(1)),
                  pl.BlockSpec(wv.shape, const, pipeline_mode=pl.Buffered(1))],
        out_specs=[pl.BlockSpec((tm, 2 * nkv), row),
                   pl.BlockSpec((1, N_KV_HEADS * V_ROWS, CHUNK), lambda i: (i, 0, 0)),
                   pl.BlockSpec((1, 1, nkv), lambda i: (i, 0, 0))],
        out_shape=[jax.ShapeDtypeStruct((s + tm, 2 * nkv), BF16),
                   jax.ShapeDtypeStruct((n + 1, N_KV_HEADS * V_ROWS, CHUNK), BF16),
                   jax.ShapeDtypeStruct((n + 1, 1, nkv), F32)],
        compiler_params=_params("parallel"),
        name="kv_shared",
    )(x, gain.reshape(1, d), cos, sin, wk, wv)


def _load_q_group(qt_ref, qall_ref):
    tq = qt_ref.shape[1]
    for hh in range(GROUP):
        qall_ref[:, hh * tq:(hh + 1) * tq] = qt_ref[hh * HEAD_DIM:(hh + 1) * HEAD_DIM, :]


def _softmax_init(m_ref, acc_ref):
    m_ref[...] = jnp.full(m_ref.shape, NEG, F32)
    acc_ref[...] = jnp.zeros(acc_ref.shape, F32)


def _add_group_bias(s, b):
    tq = b.shape[1]
    return jnp.concatenate([s[:, hh * tq:(hh + 1) * tq] + b for hh in range(GROUP)], axis=1)


def _logits_pass(chunk_ids, logits_fn, s_ref, cmax_ref):
    cmax = None
    for j, c in enumerate(chunk_ids):
        s = logits_fn(c)
        s_ref[j * CHUNK:(j + 1) * CHUNK, :] = s
        cm = jnp.max(s, axis=0, keepdims=True)
        cmax = cm if cmax is None else jnp.maximum(cmax, cm)
    cmax_ref[...] = cmax


def _softmax_pass(chunk_ids, vt_ref, s_ref, cmax_ref, m_ref, acc_ref):
    m_old = m_ref[...]
    m_new = jnp.maximum(m_old, cmax_ref[...])
    alpha = jnp.exp2(m_old - m_new)
    pv = None
    for j, c in enumerate(chunk_ids):
        p = jnp.exp2(s_ref[j * CHUNK:(j + 1) * CHUNK, :] - m_new)
        d = jnp.dot(vt_ref[c], p.astype(BF16), preferred_element_type=F32)
        pv = d if pv is None else pv + d
    m_ref[...] = m_new
    acc_ref[...] = alpha * acc_ref[...] + pv


def _attend_range(n_chunks, logits_fn, vt_ref, s_refs, cmax_refs, m_ref, acc_ref):
    n_steps = pl.cdiv(n_chunks, CHUNKS_PER_STEP)
    last_chunk = vt_ref.shape[0] - 1

    def chunks(step):
        return [step * CHUNKS_PER_STEP + j for j in range(CHUNKS_PER_STEP)]

    def logits(step, slot):
        _logits_pass(chunks(step), logits_fn, s_refs[slot], cmax_refs[slot])

    def softmax(step, slot):
        ids = [jnp.minimum(c, last_chunk) for c in chunks(step)]
        _softmax_pass(ids, vt_ref, s_refs[slot], cmax_refs[slot], m_ref, acc_ref)

    @pl.when(n_steps > 0)
    def _():
        logits(0, 0)

    def pair(u, carry):
        t = 2 * u
        logits(t + 1, 1)
        softmax(t, 0)

        @pl.when(t + 1 < n_steps)
        def _():
            logits(t + 2, 0)
            softmax(t + 1, 1)

        return carry

    lax.fori_loop(0, pl.cdiv(n_steps, 2), pair, 0)


def _softmax_finish(o_ref, acc_ref):
    tq = o_ref.shape[0]
    out = acc_ref[:HEAD_DIM, :] / acc_ref[HEAD_DIM:HEAD_DIM + 1, :]
    for hh in range(GROUP):
        o_ref[:, hh * HEAD_DIM:(hh + 1) * HEAD_DIM] = out[:, hh * tq:(hh + 1) * tq].T.astype(o_ref.dtype)


def _dsa_body(qit_ref, wt_ref, ki_ref, qt_ref, k_ref, vt_ref, o_ref,
              keys_ref, gmax_ref, cand_ref, thr_ref, qall_ref, sa_ref, sb_ref, cma_ref, cmb_ref,
              m_ref, acc_ref, *, topk):
    i = pl.program_id(0)
    g = pl.program_id(1)
    tq = CHUNK
    n_chunks = i + 1
    sub = 128

    def causal(c, rows, row_off=0):
        kpos = c * CHUNK + row_off + lax.broadcasted_iota(jnp.int32, (rows, tq), 0)
        qpos = i * tq + lax.broadcasted_iota(jnp.int32, (rows, tq), 1)
        return kpos <= qpos

    @pl.when(g == 0)
    def _():
        gmax_ref[...] = jnp.full(gmax_ref.shape, INT_MIN, jnp.int32)

        def score_chunk(c, carry):
            for part in range(CHUNK // sub):
                r0 = pl.multiple_of(c * CHUNK + part * sub, sub)
                kit = ki_ref[pl.ds(r0, sub), :]
                sc = jnp.zeros((sub, tq), F32)
                for h in range(IDX_HEADS):
                    d = jnp.dot(kit, qit_ref[h * IDX_DIM:(h + 1) * IDX_DIM, :], preferred_element_type=F32)
                    sc = sc + wt_ref[h:h + 1, :] * jnp.maximum(d, 0.0)
                sc = jnp.where(causal(c, sub, part * sub), sc, -jnp.inf)
                bits = pltpu.bitcast(sc, jnp.int32)
                key = bits ^ ((bits >> 31) & INT_MAX)
                keys_ref[pl.ds(r0, sub), :] = key
                gsl = slice(part * sub, (part + 1) * sub)
                gmax_ref[gsl, :] = jnp.maximum(gmax_ref[gsl, :], key)
            return carry

        def score_pair(u, carry):
            score_chunk(2 * u, carry)
            return score_chunk(2 * u + 1, carry)

        lax.fori_loop(0, n_chunks // 2, score_pair, 0)

        @pl.when(n_chunks % 2 == 1)
        def _():
            score_chunk(n_chunks - 1, 0)

        def count_rows(load, n_blocks, t):
            def count_block(b, acc):
                ge = jnp.where(load(b) >= t, 1, 0).astype(jnp.int32)
                return acc + jnp.sum(ge.reshape(CHUNK // SUBLANES, SUBLANES, tq), axis=0)

            part = jnp.zeros((SUBLANES, tq), jnp.int32)
            if isinstance(n_blocks, int):
                for b in range(n_blocks):
                    part = count_block(b, part)
            else:
                def count_group(u, acc):
                    for j in range(COUNT_UNROLL):
                        acc = count_block(u * COUNT_UNROLL + j, acc)
                    return acc

                n_groups = n_blocks // COUNT_UNROLL
                part = lax.fori_loop(0, n_groups, count_group, part)
                part = lax.fori_loop(n_groups * COUNT_UNROLL, n_blocks, count_block, part)
            return jnp.sum(part, axis=0, keepdims=True)

        def key_chunk(c):
            return keys_ref[pl.ds(pl.multiple_of(c * CHUNK, CHUNK), CHUNK), :]

        def count_all(t):
            return count_rows(key_chunk, n_chunks, t)

        def count_cand(t):
            return count_rows(lambda b: cand_ref[b], CAND_LEVELS, t)

        def any_set(flags):
            return jnp.max(flags) > 0.0

        def bisect_pass(count_fn, base, live, state):
            lo, hi, c_lo, c_hi, done = state
            mid = (lo >> 1) + (hi >> 1) + (lo & hi & 1)
            cnt = count_fn(mid) + base
            up = live & (cnt >= topk)
            down = live & (cnt < topk)
            lo, c_lo = jnp.where(up, mid, lo), jnp.where(up, cnt, c_lo)
            hi, c_hi = jnp.where(down, mid, hi), jnp.where(down, cnt, c_hi)
            finished = live & ((cnt == topk) | (hi <= lo + 1))
            return lo, hi, c_lo, c_hi, jnp.where(finished, 1.0, done)

        def bisect_while(count_fn, base, live_fn, keep_going, state):
            def cond(carry):
                it, state = carry
                return jnp.logical_and(it < MAX_PASSES, keep_going(state))

            def body(carry):
                it, state = carry
                return it + 1, bisect_pass(count_fn, base, live_fn(state), state)

            return lax.while_loop(cond, body, (jnp.int32(0), state))[1]

        def in_bracket(state):
            return state[2] - state[3]

        def active(state):
            return state[4] == 0.0

        gm = gmax_ref[...]
        lo = jnp.min(gm, axis=0, keepdims=True)
        top = jnp.max(gm, axis=0, keepdims=True)
        hi = jnp.where(top == INT_MAX, top, top + 1)
        c_lo = count_all(lo)
        state = (lo, hi, c_lo, jnp.zeros_like(c_lo), (c_lo == topk).astype(F32))

        def wide(state):
            return any_set(jnp.where(active(state) & (in_bracket(state) > CAND_MAX), 1.0, 0.0))

        state = bisect_while(count_all, 0, active, wide, state)

        lo, hi, c_lo, c_hi, _ = state
        cand_ref[...] = jnp.full(cand_ref.shape, INT_MIN, jnp.int32)

        def capture(c, carry):
            blk = key_chunk(c)
            x = jnp.where((blk >= lo) & (blk < hi), blk, INT_MIN)
            for level in range(CAND_LEVELS):
                held = cand_ref[level]
                cand_ref[level] = jnp.maximum(held, x)
                x = jnp.minimum(held, x)
            return carry

        lax.fori_loop(0, n_chunks, capture, 0)
        captured = count_cand(lo) == in_bracket(state)

        def live_captured(state):
            return active(state) & captured

        state = bisect_while(count_cand, c_hi, live_captured,
                             lambda st: any_set(jnp.where(live_captured(st), 1.0, 0.0)), state)

        state = bisect_while(count_all, 0, active,
                             lambda st: any_set(jnp.where(active(st), 1.0, 0.0)), state)
        thr_ref[...] = state[0]

    _load_q_group(qt_ref, qall_ref)
    _softmax_init(m_ref, acc_ref)
    thr = thr_ref[...]

    def logits(c):
        r0 = pl.multiple_of(jnp.minimum(c, n_chunks - 1) * CHUNK, CHUNK)
        sel = (keys_ref[pl.ds(r0, CHUNK), :] >= thr) & causal(c, CHUNK)
        s = jnp.dot(k_ref[pl.ds(r0, CHUNK), :], qall_ref[...], preferred_element_type=F32)
        return _add_group_bias(s, jnp.where(sel, 0.0, NEG))

    _attend_range(n_chunks, logits, vt_ref, (sa_ref, sb_ref), (cma_ref, cmb_ref), m_ref, acc_ref)
    _softmax_finish(o_ref, acc_ref)


def _dsa(qit, wt, ki, qt, k, vtt, topk):
    s = ki.shape[0]
    tq = CHUNK
    assert s % (tq * CHUNKS_PER_STEP) == 0 and topk <= CHUNK
    gw = GROUP * HEAD_DIM
    return pl.pallas_call(
        functools.partial(_dsa_body, topk=topk),
        grid=(s // tq, N_KV_HEADS),
        in_specs=[
            pl.BlockSpec((IDX_HEADS * IDX_DIM, tq), lambda i, g: (0, i)),
            pl.BlockSpec((IDX_HEADS, tq), lambda i, g: (0, i)),
            pl.BlockSpec((s, IDX_DIM), lambda i, g: (0, 0), pipeline_mode=pl.Buffered(1)),
            pl.BlockSpec((gw, tq), lambda i, g: (g, i)),
            pl.BlockSpec((s, HEAD_DIM), lambda i, g: (0, g)),
            pl.BlockSpec((s // tq, V_ROWS, tq), lambda i, g: (0, g, 0)),
        ],
        out_specs=pl.BlockSpec((tq, gw), lambda i, g: (i, g)),
        out_shape=jax.ShapeDtypeStruct((s, N_MAIN_HEADS * HEAD_DIM), BF16),
        scratch_shapes=[
            pltpu.VMEM((s, tq), jnp.int32),
            pltpu.VMEM((CHUNK, tq), jnp.int32),
            pltpu.VMEM((CAND_LEVELS, CHUNK, tq), jnp.int32),
            pltpu.VMEM((1, tq), jnp.int32),
            pltpu.VMEM((HEAD_DIM, GROUP * tq), BF16),
            pltpu.VMEM((CHUNKS_PER_STEP * CHUNK, GROUP * tq), F32),
            pltpu.VMEM((CHUNKS_PER_STEP * CHUNK, GROUP * tq), F32),
            pltpu.VMEM((1, GROUP * tq), F32),
            pltpu.VMEM((1, GROUP * tq), F32),
            pltpu.VMEM((1, GROUP * tq), F32),
            pltpu.VMEM((V_ROWS, GROUP * tq), F32),
        ],
        compiler_params=_params("arbitrary", "arbitrary"),
        name="dsa",
    )(qit, wt, ki, qt, k, vtt)


def _moba_body(qt_ref, k_ref, vt_ref, km_ref, o_ref, qaug_ref, sa_ref, sb_ref, cma_ref, cmb_ref,
               m_ref, acc_ref, *, n_sel):
    cur = pl.program_id(0)
    tq = CHUNK
    nb = km_ref.shape[0]
    width = GROUP * tq
    for hh in range(GROUP):
        qaug_ref[:HEAD_DIM, hh * tq:(hh + 1) * tq] = qt_ref[hh * HEAD_DIM:(hh + 1) * HEAD_DIM, :]
    q_all = qaug_ref[:HEAD_DIM, :]

    blk_id = lax.broadcasted_iota(jnp.int32, (nb, width), 0)
    past = blk_id < cur
    gate = jnp.dot(km_ref[...], q_all, preferred_element_type=F32)
    gate = jnp.where(past, gate, -jnp.inf)
    chosen = jnp.zeros((nb, width), jnp.bool_)
    for _ in range(n_sel):
        best = jnp.max(gate, axis=0, keepdims=True)
        first = jnp.min(jnp.where(gate == best, blk_id, nb), axis=0, keepdims=True)
        pick = blk_id == first
        chosen = chosen | pick
        gate = jnp.where(pick, -jnp.inf, gate)
    qaug_ref[HEAD_DIM:HEAD_DIM + nb, :] = jnp.where(chosen & past, 0.0, NEG).astype(BF16)
    pad = BF16_TILE_ROWS
    qaug_ref[HEAD_DIM + nb:HEAD_DIM + nb + pad, :] = jnp.full((pad, width), NEG, BF16)
    if nb + pad < HEAD_DIM:
        qaug_ref[HEAD_DIM + nb + pad:, :] = jnp.zeros((HEAD_DIM - nb - pad, width), BF16)

    _softmax_init(m_ref, acc_ref)
    state = (vt_ref, (sa_ref, sb_ref), (cma_ref, cmb_ref), m_ref, acc_ref)

    def past_logits(c):
        r0 = pl.multiple_of(jnp.where(c < cur, c, nb) * CHUNK, CHUNK)
        return jnp.dot(k_ref[pl.ds(r0, CHUNK), :], qaug_ref[...], preferred_element_type=F32)

    _attend_range(cur, past_logits, *state)

    tri = (lax.broadcasted_iota(jnp.int32, (tq, tq), 0) <= lax.broadcasted_iota(jnp.int32, (tq, tq), 1))

    def own_logits(c):
        r0 = pl.multiple_of(c * CHUNK, CHUNK)
        s = jnp.dot(k_ref[pl.ds(r0, CHUNK), :HEAD_DIM], q_all, preferred_element_type=F32)
        return _add_group_bias(s, jnp.where(tri, 0.0, NEG))

    _logits_pass([cur], own_logits, sa_ref, cma_ref)
    _softmax_pass([cur], vt_ref, sa_ref, cma_ref, m_ref, acc_ref)
    _softmax_finish(o_ref, acc_ref)


def _moba(qt, k_aug, vtt, kmeans, n_sel):
    s = k_aug.shape[0] - CHUNK
    assert MOBA_BLOCK == CHUNK and s % (CHUNK * CHUNKS_PER_STEP) == 0
    tq = CHUNK
    nb = s // tq
    assert nb + BF16_TILE_ROWS <= HEAD_DIM and vtt.shape[0] == nb + 1
    gw = GROUP * HEAD_DIM
    return pl.pallas_call(
        functools.partial(_moba_body, n_sel=n_sel),
        grid=(nb, N_KV_HEADS),
        in_specs=[
            pl.BlockSpec((gw, tq), lambda i, g: (g, i)),
            pl.BlockSpec((s + CHUNK, 2 * HEAD_DIM), lambda i, g: (0, g)),
            pl.BlockSpec((nb + 1, V_ROWS, tq), lambda i, g: (0, g, 0)),
            pl.BlockSpec((nb, HEAD_DIM), lambda i, g: (0, g)),
        ],
        out_specs=pl.BlockSpec((tq, gw), lambda i, g: (i, g)),
        out_shape=jax.ShapeDtypeStruct((s, N_MAIN_HEADS * HEAD_DIM), BF16),
        scratch_shapes=[
            pltpu.VMEM((2 * HEAD_DIM, GROUP * tq), BF16),
            pltpu.VMEM((CHUNKS_PER_STEP * CHUNK, GROUP * tq), F32),
            pltpu.VMEM((CHUNKS_PER_STEP * CHUNK, GROUP * tq), F32),
            pltpu.VMEM((1, GROUP * tq), F32),
            pltpu.VMEM((1, GROUP * tq), F32),
            pltpu.VMEM((1, GROUP * tq), F32),
            pltpu.VMEM((V_ROWS, GROUP * tq), F32),
        ],
        compiler_params=_params("parallel", "arbitrary"),
        name="moba",
    )(qt, k_aug, vtt, kmeans)


def _mem_attn_body(q_ref, kt_ref, v_ref, o_ref):
    for h in range(N_MEM_HEADS):
        sl = slice(h * HEAD_DIM, (h + 1) * HEAD_DIM)
        s = jnp.dot(q_ref[:, sl], kt_ref[sl, :], preferred_element_type=F32)
        p = jnp.exp(s - jnp.max(s, axis=-1, keepdims=True))
        o = jnp.dot(p.astype(BF16), v_ref[:, sl], preferred_element_type=F32)
        o_ref[:, sl] = (o / jnp.sum(p, axis=-1, keepdims=True)).astype(o_ref.dtype)


def _mem_attn(qm, kmt, vm, *, tq=512):
    s, w = qm.shape
    m = vm.shape[0]
    tq = min(tq, s)
    assert s % tq == 0
    return pl.pallas_call(
        _mem_attn_body,
        grid=(s // tq,),
        in_specs=[pl.BlockSpec((tq, w), lambda i: (i, 0)),
                  pl.BlockSpec((w, m), lambda i: (0, 0)),
                  pl.BlockSpec((m, w), lambda i: (0, 0))],
        out_specs=pl.BlockSpec((tq, w), lambda i: (i, 0)),
        out_shape=jax.ShapeDtypeStruct((s, w), BF16),
        compiler_params=_params("parallel"),
        name="mem_attn",
    )(qm, kmt, vm)


def _out_proj_body(x_ref, om_ref, oq_ref, w1_ref, w2_ref, o_ref):
    o_ref[...] = (x_ref[...]
                  + jnp.dot(om_ref[...], w1_ref[...], preferred_element_type=F32)
                  + jnp.dot(oq_ref[...], w2_ref[...], preferred_element_type=F32))


def _out_proj(x, o_main, o_mem, w1, w2, *, tm=512):
    s, d = x.shape
    tm = min(tm, s)
    assert s % tm == 0
    row = lambda i: (i, 0)
    const = lambda i: (0, 0)
    return pl.pallas_call(
        _out_proj_body,
        grid=(s // tm,),
        in_specs=[pl.BlockSpec((tm, d), row),
                  pl.BlockSpec((tm, o_main.shape[1]), row),
                  pl.BlockSpec((tm, o_mem.shape[1]), row),
                  pl.BlockSpec(w1.shape, const, pipeline_mode=pl.Buffered(1)),
                  pl.BlockSpec(w2.shape, const, pipeline_mode=pl.Buffered(1))],
        out_specs=pl.BlockSpec((tm, d), row),
        out_shape=jax.ShapeDtypeStruct((s, d), F32),
        compiler_params=_params("parallel"),
        name="out_proj",
    )(x, o_main, o_mem, w1, w2)


def _rope_tables(positions, dim):
    inv = 1.0 / (ROPE_THETA ** (jnp.arange(0, dim, 2, dtype=F32) / dim))
    ang = positions.astype(F32)[:, None] * inv
    c, s = jnp.cos(ang), jnp.sin(ang)
    reps = LANES // dim
    return jnp.tile(jnp.concatenate([c, c], -1), (1, reps)), jnp.tile(jnp.concatenate([-s, s], -1), (1, reps))


def kernel(x, mem, positions, ffn1_norm, ffn1_w_gate_up, ffn1_w_down, attn_norm, mem_norm, a_w_in, idx_k_norm, b_w_in, w_mem_kv, w_out, ffn2_norm, ffn2_w_gate_up, ffn2_w_down, kv_norm, w_kv_shared, final_norm):
    b, s, d = x.shape
    assert b == 1 and mem.shape[0] == 1
    depth = ffn1_norm.shape[0]
    n_a = a_w_in.shape[0]
    main_w = N_MAIN_HEADS * HEAD_DIM
    kv_w = N_KV_HEADS * HEAD_DIM
    idx_w = IDX_HEADS * IDX_DIM
    mem_w = N_MEM_HEADS * HEAD_DIM
    topk = min(IDX_TOPK_MAX, s // 4)
    nb = s // MOBA_BLOCK
    n_sel = min(MOBA_TOPK_MAX, max(nb - 1, 1))

    cos, sin = _rope_tables(positions[0], HEAD_DIM)
    cosi, sini = _rope_tables(positions[0], IDX_DIM)
    tables = (cos, sin, cosi, sini)
    mem_tables = tuple(t[:mem.shape[1]] for t in tables)
    no_gk = jnp.zeros((1, LANES), F32)

    xs = x[0]
    mem2 = mem[0]
    k_sh = vtt_sh = kmeans = None
    for i in range(depth):
        if i == n_a:
            wk = w_kv_shared[:, :kv_w].astype(BF16)
            wv = w_kv_shared[:, kv_w:].astype(BF16)
            k_sh, vtt_sh, km = _kv_shared(xs, kv_norm, cos, sin, wk, wv)
            kmeans = km[:nb].reshape(nb, kv_w).astype(BF16)

        last = i == depth - 1
        xs = _ffn(xs, ffn1_norm[i], ffn1_w_gate_up[i].astype(BF16), ffn1_w_down[i].astype(BF16))

        wm = w_mem_kv[i].astype(BF16)
        mk, mv = _proj(mem2, mem_norm[i], mem_tables, no_gk, [wm[:, :mem_w], wm[:, mem_w:]],
                       ["plain", "plain"], [BF16, BF16])
        if i < n_a:
            wa = a_w_in[i]
            o0 = 0
            ws = []
            for width in (main_w, kv_w, kv_w, idx_w, IDX_DIM + IDX_HEADS, mem_w):
                ws.append(wa[:, o0:o0 + width])
                o0 += width
            w_kiwi = jnp.pad(ws[4], ((0, 0), (0, LANES - ws[4].shape[1])))
            ws = [w.astype(BF16) for w in (ws[0], ws[1], ws[2], ws[3], w_kiwi, ws[5])]
            gk = jnp.pad(idx_k_norm[i], (0, LANES - IDX_DIM)).reshape(1, LANES)
            qt, k, vtt, qit, kiwi, qm = _proj(
                xs, attn_norm[i], tables, gk, ws,
                ["rope_scaled_t", "rope", "value_chunks", "rope_idx_t", "kiwi", "scaled"],
                [BF16, BF16, BF16, BF16, F32, BF16])
            ki = kiwi[:, :IDX_DIM].astype(BF16)
            wt = kiwi[:, IDX_DIM:IDX_DIM + IDX_HEADS].T
            o_main = _dsa(qit, wt, ki, qt, k, vtt, topk)
        else:
            wb = b_w_in[i - n_a]
            ws = [wb[:, :main_w].astype(BF16), wb[:, main_w:].astype(BF16)]
            qt, qm = _proj(xs, attn_norm[i], tables, no_gk, ws, ["rope_scaled_t", "scaled"], [BF16, BF16])
            o_main = _moba(qt, k_sh, vtt_sh, kmeans, n_sel)
        o_mem = _mem_attn(qm, mk.T, mv)
        wo = w_out[i].astype(BF16)
        xs = _out_proj(xs, o_main, o_mem, wo[:main_w], wo[main_w:])

        xs = _ffn(xs, ffn2_norm[i], ffn2_w_gate_up[i].astype(BF16), ffn2_w_down[i].astype(BF16),
                  final_norm if last else None)
    return xs[None]
```

```python
import functools
from typing import Any, NamedTuple

import jax
import jax.numpy as jnp
import numpy as np
from jax import lax
from jax.experimental import pallas as pl
from jax.experimental.pallas import tpu as pltpu

HEAD_DIM = 128
N_MAIN_HEADS = 12
N_KV_HEADS = 4
GROUP = N_MAIN_HEADS // N_KV_HEADS
N_MEM_HEADS = 4
IDX_HEADS = 16
IDX_DIM = 64
IDX_TOPK_MAX = 256
MOBA_BLOCK = 256
MOBA_TOPK_MAX = 3
ROPE_THETA = 10000.0
RMS_EPS = 1e-6

LANES = 128
SUBLANES = 8
VMEM_LIMIT = 56 * 1024 * 1024
NEG = -1e30
LOG2_E = 1.4426950408889634
INT_MIN = -2 ** 31
INT_MAX = 2 ** 31 - 1
CHUNK = 256
CHUNKS_PER_STEP = 4
CAND_LEVELS = 2
CAND_MAX = 16
MAX_PASSES = 40
COUNT_UNROLL = 4
BF16_TILE_ROWS = 2 * SUBLANES
ONES_ROWS = BF16_TILE_ROWS
V_ROWS = HEAD_DIM + ONES_ROWS

F32 = jnp.float32
BF16 = jnp.bfloat16


def _params(*sem):
    return pltpu.CompilerParams(dimension_semantics=sem, vmem_limit_bytes=VMEM_LIMIT)


def _rms(x, gain):
    return x * lax.rsqrt(jnp.mean(x * x, axis=-1, keepdims=True) + RMS_EPS) * gain


def _ffn_body(x_ref, g_ref, wg_ref, wu_ref, wd_ref, pg_ref, o_ref, h_ref, *, final_norm):
    j = pl.program_id(1)

    @pl.when(j == 0)
    def _():
        x = x_ref[...]
        h_ref[...] = _rms(x, g_ref[...]).astype(BF16)
        o_ref[...] = x

    h = h_ref[...]
    gate = jnp.dot(h, wg_ref[...], preferred_element_type=F32)
    up = jnp.dot(h, wu_ref[...], preferred_element_type=F32)
    act = (gate * (0.5 / (1.0 + jnp.exp(-gate))) * up).astype(BF16)
    o_ref[...] += jnp.dot(act, wd_ref[...], preferred_element_type=F32)

    if final_norm:
        @pl.when(j == pl.num_programs(1) - 1)
        def _():
            o_ref[...] = _rms(o_ref[...], pg_ref[...])


def _ffn(x, gain, w_gate_up, w_down, post_gain=None, *, tm=512, tf=512):
    s, d = x.shape
    f = w_down.shape[0]
    tm = min(tm, s)
    tf = min(tf, f)
    assert s % tm == 0 and f % tf == 0
    nf = f // tf
    final_norm = post_gain is not None
    pg = post_gain if final_norm else gain
    return pl.pallas_call(
        functools.partial(_ffn_body, final_norm=final_norm),
        grid=(s // tm, nf),
        in_specs=[
            pl.BlockSpec((tm, d), lambda i, j: (i, 0)),
            pl.BlockSpec((1, d), lambda i, j: (0, 0)),
            pl.BlockSpec((d, tf), lambda i, j: (0, j)),
            pl.BlockSpec((d, tf), lambda i, j: (0, j + nf)),
            pl.BlockSpec((tf, d), lambda i, j: (j, 0)),
            pl.BlockSpec((1, d), lambda i, j: (0, 0)),
        ],
        out_specs=pl.BlockSpec((tm, d), lambda i, j: (i, 0)),
        out_shape=jax.ShapeDtypeStruct((s, d), F32),
        scratch_shapes=[pltpu.VMEM((tm, d), BF16)],
        compiler_params=_params("parallel", "arbitrary"),
        name="ffn",
    )(x, gain.reshape(1, d), w_gate_up, w_gate_up, w_down, pg.reshape(1, d))


def _rope_heads(y, cos, sin, o_ref, scale, transposed=False):
    for h in range(y.shape[1] // HEAD_DIM):
        t = y[:, h * HEAD_DIM:(h + 1) * HEAD_DIM]
        r = t * cos + pltpu.roll(t, HEAD_DIM // 2, 1) * sin
        if scale != 1.0:
            r = r * scale
        if transposed:
            o_ref[h * HEAD_DIM:(h + 1) * HEAD_DIM, :] = r.T.astype(o_ref.dtype)
        else:
            o_ref[:, h * HEAD_DIM:(h + 1) * HEAD_DIM] = r.astype(o_ref.dtype)


def _store_value_chunks(y, o_ref):
    for b in range(y.shape[0] // CHUNK):
        for g in range(N_KV_HEADS):
            blk = y[b * CHUNK:(b + 1) * CHUNK, g * HEAD_DIM:(g + 1) * HEAD_DIM]
            o_ref[b, g * V_ROWS:g * V_ROWS + HEAD_DIM, :] = blk.T.astype(o_ref.dtype)
            o_ref[b, g * V_ROWS + HEAD_DIM:(g + 1) * V_ROWS, :] = jnp.ones((ONES_ROWS, CHUNK), o_ref.dtype)


def _rot_idx(t):
    lane = lax.broadcasted_iota(jnp.int32, t.shape, 1)
    first_half = (lane & (IDX_DIM // 2)) == 0
    return jnp.where(first_half, pltpu.roll(t, LANES - IDX_DIM // 2, 1),
                     pltpu.roll(t, IDX_DIM // 2, 1))


class _Piece(NamedTuple):
    kind: str
    start: int
    width: int
    lo: int
    hi: int
    dtype: Any


def _proj_body(*refs, pieces, q_scale, idx_w_scale):
    x_ref, g_ref, cos_ref, sin_ref, cosi_ref, sini_ref, gk_ref, w_ref = refs[:8]
    o_refs = refs[8:]
    h = _rms(x_ref[...], g_ref[...]).astype(BF16)
    products = {}
    for piece, o_ref in zip(pieces, o_refs):
        cols = (piece.start, piece.width)
        if cols not in products:
            products[cols] = jnp.dot(h, w_ref[:, piece.start:piece.start + piece.width],
                                     preferred_element_type=F32)
        y = products[cols]
        if (piece.lo, piece.hi) != (0, piece.width):
            y = y[:, piece.lo:piece.hi]
        kind = piece.kind
        if kind == "plain":
            o_ref[...] = y.astype(o_ref.dtype)
        elif kind == "scaled":
            o_ref[...] = (y * q_scale).astype(o_ref.dtype)
        elif kind == "rope":
            _rope_heads(y, cos_ref[...], sin_ref[...], o_ref, 1.0)
        elif kind == "rope_scaled_t":
            _rope_heads(y, cos_ref[...], sin_ref[...], o_ref, q_scale * LOG2_E, transposed=True)
        elif kind == "rope_idx_t":
            cosi, sini = cosi_ref[...], sini_ref[...]
            for c in range(y.shape[1] // LANES):
                t = y[:, c * LANES:(c + 1) * LANES]
                o_ref[c * LANES:(c + 1) * LANES, :] = (t * cosi + _rot_idx(t) * sini).T.astype(o_ref.dtype)
        elif kind == "value_chunks":
            _store_value_chunks(y, o_ref)
        elif kind == "kiwi":
            lane = lax.broadcasted_iota(jnp.int32, y.shape, 1)
            is_k = lane < IDX_DIM
            kk = jnp.where(is_k, y, 0.0)
            ms = jnp.sum(kk * kk, axis=-1, keepdims=True) * (1.0 / IDX_DIM)
            kn = kk * lax.rsqrt(ms + RMS_EPS) * gk_ref[...]
            kr = kn * cosi_ref[...] + _rot_idx(kn) * sini_ref[...]
            o_ref[...] = jnp.where(is_k, kr, y * idx_w_scale)
        else:
            raise ValueError(kind)


def _proj(x, gain, tables, gk, w, pieces, *, tm=256):
    s, d = x.shape
    tm = min(tm, s)
    assert s % tm == 0
    cos, sin, cosi, sini = tables
    row = lambda i: (i, 0)
    const = lambda i: (0, 0)
    in_specs = [pl.BlockSpec((tm, d), row), pl.BlockSpec((1, d), const)]
    in_specs += [pl.BlockSpec((tm, LANES), row)] * 4
    in_specs += [pl.BlockSpec((1, LANES), const)]
    in_specs += [pl.BlockSpec(w.shape, const, pipeline_mode=pl.Buffered(1))]
    out_specs, out_shape = [], []
    for p in pieces:
        assert p.start % LANES == 0 and p.width % LANES == 0 and p.start + p.width <= w.shape[1]
        n = p.hi - p.lo
        if p.kind.endswith("_t"):
            out_specs.append(pl.BlockSpec((n, tm), lambda i: (0, i)))
            out_shape.append(jax.ShapeDtypeStruct((n, s), p.dtype))
        elif p.kind == "value_chunks":
            assert tm % CHUNK == 0 and n == N_KV_HEADS * HEAD_DIM
            out_specs.append(pl.BlockSpec((tm // CHUNK, N_KV_HEADS * V_ROWS, CHUNK), lambda i: (i, 0, 0)))
            out_shape.append(jax.ShapeDtypeStruct((s // CHUNK, N_KV_HEADS * V_ROWS, CHUNK), p.dtype))
        else:
            out_specs.append(pl.BlockSpec((tm, n), row))
            out_shape.append(jax.ShapeDtypeStruct((s, n), p.dtype))
    return pl.pallas_call(
        functools.partial(_proj_body, pieces=tuple(pieces), q_scale=HEAD_DIM ** -0.5,
                          idx_w_scale=(IDX_HEADS ** -0.5) * (IDX_DIM ** -0.5)),
        grid=(s // tm,),
        in_specs=in_specs,
        out_specs=out_specs,
        out_shape=out_shape,
        compiler_params=_params("parallel"),
        name="proj",
    )(x, gain.reshape(1, d), cos, sin, cosi, sini, gk, w)


def _kv_shared_body(x_ref, g_ref, cos_ref, sin_ref, wk_ref, wv_ref, k_ref, v_ref, km_ref):
    h = _rms(x_ref[...], g_ref[...]).astype(BF16)
    yk = jnp.dot(h, wk_ref[...], preferred_element_type=F32)
    cos, sin = cos_ref[...], sin_ref[...]
    tm = yk.shape[0]
    first_blk = pl.program_id(0) * (tm // MOBA_BLOCK)
    lane = lax.broadcasted_iota(jnp.int32, (MOBA_BLOCK, HEAD_DIM), 1)
    for hd in range(yk.shape[1] // HEAD_DIM):
        t = yk[:, hd * HEAD_DIM:(hd + 1) * HEAD_DIM]
        r = t * cos + pltpu.roll(t, HEAD_DIM // 2, 1) * sin
        k_ref[:, 2 * hd * HEAD_DIM:(2 * hd + 1) * HEAD_DIM] = r.astype(k_ref.dtype)
        for b in range(tm // MOBA_BLOCK):
            rows = slice(b * MOBA_BLOCK, (b + 1) * MOBA_BLOCK)
            k_ref[rows, (2 * hd + 1) * HEAD_DIM:(2 * hd + 2) * HEAD_DIM] = (
                jnp.where(lane == first_blk + b, 1.0, 0.0).astype(k_ref.dtype))
            km_ref[b, :, hd * HEAD_DIM:(hd + 1) * HEAD_DIM] = (
                jnp.sum(r[rows], axis=0, keepdims=True) * (1.0 / MOBA_BLOCK))
    _store_value_chunks(jnp.dot(h, wv_ref[...], preferred_element_type=F32), v_ref)


def _kv_shared(x, gain, cos, sin, wk, wv):
    s, d = x.shape
    tm = MOBA_BLOCK
    assert s % tm == 0 and CHUNK == MOBA_BLOCK
    n = s // tm
    nkv = wk.shape[1]
    src = lambda i: (jnp.minimum(i, n - 1), 0)
    row = lambda i: (i, 0)
    const = lambda i: (0, 0)
    return pl.pallas_call(
        _kv_shared_body,
        grid=(n + 1,),
        in_specs=[pl.BlockSpec((tm, d), src), pl.BlockSpec((1, d), const),
                  pl.BlockSpec((tm, LANES), src), pl.BlockSpec((tm, LANES), src),
                  pl.BlockSpec(wk.shape, const, pipeline_mode=pl.Buffered(1)),
                  pl.BlockSpec(wv.shape, const, pipeline_mode=pl.Buffered(1))],
        out_specs=[pl.BlockSpec((tm, 2 * nkv), row),
                   pl.BlockSpec((1, N_KV_HEADS * V_ROWS, CHUNK), lambda i: (i, 0, 0)),
                   pl.BlockSpec((1, 1, nkv), lambda i: (i, 0, 0))],
        out_shape=[jax.ShapeDtypeStruct((s + tm, 2 * nkv), BF16),
                   jax.ShapeDtypeStruct((n + 1, N_KV_HEADS * V_ROWS, CHUNK), BF16),
                   jax.ShapeDtypeStruct((n + 1, 1, nkv), F32)],
        compiler_params=_params("parallel"),
        name="kv_shared",
    )(x, gain.reshape(1, d), cos, sin, wk, wv)


def _load_q_group(qt_ref, qall_ref):
    tq = qt_ref.shape[1]
    for hh in range(GROUP):
        qall_ref[:, hh * tq:(hh + 1) * tq] = qt_ref[hh * HEAD_DIM:(hh + 1) * HEAD_DIM, :]


def _softmax_init(m_ref, acc_ref):
    m_ref[...] = jnp.full(m_ref.shape, NEG, F32)
    acc_ref[...] = jnp.zeros(acc_ref.shape, F32)


def _add_group_bias(s, b):
    tq = b.shape[1]
    return jnp.concatenate([s[:, hh * tq:(hh + 1) * tq] + b for hh in range(GROUP)], axis=1)


def _logits_pass(chunk_ids, logits_fn, s_ref, cmax_ref):
    cmax = None
    for j, c in enumerate(chunk_ids):
        s = logits_fn(c)
        s_ref[j * CHUNK:(j + 1) * CHUNK, :] = s
        cm = jnp.max(s, axis=0, keepdims=True)
        cmax = cm if cmax is None else jnp.maximum(cmax, cm)
    cmax_ref[...] = cmax


def _softmax_pass(chunk_ids, vt_ref, s_ref, cmax_ref, m_ref, acc_ref):
    m_old = m_ref[...]
    m_new = jnp.maximum(m_old, cmax_ref[...])
    alpha = jnp.exp2(m_old - m_new)
    pv = None
    for j, c in enumerate(chunk_ids):
        p = jnp.exp2(s_ref[j * CHUNK:(j + 1) * CHUNK, :] - m_new)
        d = jnp.dot(vt_ref[c], p.astype(BF16), preferred_element_type=F32)
        pv = d if pv is None else pv + d
    m_ref[...] = m_new
    acc_ref[...] = alpha * acc_ref[...] + pv


def _attend_range(n_chunks, logits_fn, vt_ref, s_refs, cmax_refs, m_ref, acc_ref, *, fuse_pairs):
    n_steps = pl.cdiv(n_chunks, CHUNKS_PER_STEP)
    last_chunk = vt_ref.shape[0] - 1

    def chunks(step):
        return [step * CHUNKS_PER_STEP + j for j in range(CHUNKS_PER_STEP)]

    def logits(step, slot):
        _logits_pass(chunks(step), logits_fn, s_refs[slot], cmax_refs[slot])

    def softmax(step, slot):
        ids = [jnp.minimum(c, last_chunk) for c in chunks(step)]
        _softmax_pass(ids, vt_ref, s_refs[slot], cmax_refs[slot], m_ref, acc_ref)

    @pl.when(n_steps > 0)
    def _():
        logits(0, 0)

    def pair(u, carry):
        t = 2 * u
        if fuse_pairs:
            @pl.when(t + 1 < n_steps)
            def _():
                logits(t + 1, 1)
                softmax(t, 0)
                logits(t + 2, 0)
                softmax(t + 1, 1)

            @pl.when(t + 1 >= n_steps)
            def _():
                softmax(t, 0)
        else:
            logits(t + 1, 1)
            softmax(t, 0)

            @pl.when(t + 1 < n_steps)
            def _():
                logits(t + 2, 0)
                softmax(t + 1, 1)

        return carry

    lax.fori_loop(0, pl.cdiv(n_steps, 2), pair, 0)


def _softmax_finish(o_ref, acc_ref):
    tq = o_ref.shape[0]
    out = acc_ref[:HEAD_DIM, :] / acc_ref[HEAD_DIM:HEAD_DIM + 1, :]
    for hh in range(GROUP):
        o_ref[:, hh * HEAD_DIM:(hh + 1) * HEAD_DIM] = out[:, hh * tq:(hh + 1) * tq].T.astype(o_ref.dtype)


def _dsa_body(qit_ref, wt_ref, ki_ref, qt_ref, k_ref, vt_ref, o_ref,
              keys_ref, gmax_ref, cand_ref, thr_ref, qall_ref, sa_ref, sb_ref, cma_ref, cmb_ref,
              m_ref, acc_ref, *, topk):
    i = pl.program_id(0)
    g = pl.program_id(1)
    tq = CHUNK
    n_chunks = i + 1
    sub = 128

    def causal(c, rows, row_off=0):
        kpos = c * CHUNK + row_off + lax.broadcasted_iota(jnp.int32, (rows, tq), 0)
        qpos = i * tq + lax.broadcasted_iota(jnp.int32, (rows, tq), 1)
        return kpos <= qpos

    @pl.when(g == 0)
    def _():
        gmax_ref[...] = jnp.full(gmax_ref.shape, INT_MIN, jnp.int32)

        def score_chunk(c, carry):
            for part in range(CHUNK // sub):
                r0 = pl.multiple_of(c * CHUNK + part * sub, sub)
                kit = ki_ref[pl.ds(r0, sub), :]
                sc = jnp.zeros((sub, tq), F32)
                for h in range(IDX_HEADS):
                    d = jnp.dot(kit, qit_ref[h * IDX_DIM:(h + 1) * IDX_DIM, :], preferred_element_type=F32)
                    sc = sc + wt_ref[h:h + 1, :] * jnp.maximum(d, 0.0)
                sc = jnp.where(causal(c, sub, part * sub), sc, -jnp.inf)
                bits = pltpu.bitcast(sc, jnp.int32)
                key = bits ^ ((bits >> 31) & INT_MAX)
                keys_ref[pl.ds(r0, sub), :] = key
                gsl = slice(part * sub, (part + 1) * sub)
                gmax_ref[gsl, :] = jnp.maximum(gmax_ref[gsl, :], key)
            return carry

        def score_pair(u, carry):
            score_chunk(2 * u, carry)
            return score_chunk(2 * u + 1, carry)

        lax.fori_loop(0, n_chunks // 2, score_pair, 0)

        @pl.when(n_chunks % 2 == 1)
        def _():
            score_chunk(n_chunks - 1, 0)

        def count_rows(load, n_blocks, t):
            def count_block(b, acc):
                ge = jnp.where(load(b) >= t, 1, 0).astype(jnp.int32)
                return acc + jnp.sum(ge.reshape(CHUNK // SUBLANES, SUBLANES, tq), axis=0)

            part = jnp.zeros((SUBLANES, tq), jnp.int32)
            if isinstance(n_blocks, int):
                for b in range(n_blocks):
                    part = count_block(b, part)
            else:
                def count_group(u, acc):
                    for j in range(COUNT_UNROLL):
                        acc = count_block(u * COUNT_UNROLL + j, acc)
                    return acc

                n_groups = n_blocks // COUNT_UNROLL
                part = lax.fori_loop(0, n_groups, count_group, part)
                part = lax.fori_loop(n_groups * COUNT_UNROLL, n_blocks, count_block, part)
            return jnp.sum(part, axis=0, keepdims=True)

        def key_chunk(c):
            return keys_ref[pl.ds(pl.multiple_of(c * CHUNK, CHUNK), CHUNK), :]

        def count_all(t):
            return count_rows(key_chunk, n_chunks, t)

        def count_cand(t):
            return count_rows(lambda b: cand_ref[b], CAND_LEVELS, t)

        def any_set(flags):
            return jnp.max(flags) > 0.0

        def bisect_pass(count_fn, base, live, state):
            lo, hi, c_lo, c_hi, done = state
            mid = (lo >> 1) + (hi >> 1) + (lo & hi & 1)
            cnt = count_fn(mid) + base
            up = live & (cnt >= topk)
            down = live & (cnt < topk)
            lo, c_lo = jnp.where(up, mid, lo), jnp.where(up, cnt, c_lo)
            hi, c_hi = jnp.where(down, mid, hi), jnp.where(down, cnt, c_hi)
            finished = live & ((cnt == topk) | (hi <= lo + 1))
            return lo, hi, c_lo, c_hi, jnp.where(finished, 1.0, done)

        def bisect_while(count_fn, base, live_fn, keep_going, state):
            def cond(carry):
                it, state = carry
                return jnp.logical_and(it < MAX_PASSES, keep_going(state))

            def body(carry):
                it, state = carry
                return it + 1, bisect_pass(count_fn, base, live_fn(state), state)

            return lax.while_loop(cond, body, (jnp.int32(0), state))[1]

        def in_bracket(state):
            return state[2] - state[3]

        def active(state):
            return state[4] == 0.0

        gm = gmax_ref[...]
        lo = jnp.min(gm, axis=0, keepdims=True)
        top = jnp.max(gm, axis=0, keepdims=True)
        hi = jnp.where(top == INT_MAX, top, top + 1)
        c_lo = count_all(lo)
        state = (lo, hi, c_lo, jnp.zeros_like(c_lo), (c_lo == topk).astype(F32))

        def wide(state):
            return any_set(jnp.where(active(state) & (in_bracket(state) > CAND_MAX), 1.0, 0.0))

        state = bisect_while(count_all, 0, active, wide, state)

        lo, hi, c_lo, c_hi, _ = state
        cand_ref[...] = jnp.full(cand_ref.shape, INT_MIN, jnp.int32)

        def capture(c, carry):
            blk = key_chunk(c)
            x = jnp.where((blk >= lo) & (blk < hi), blk, INT_MIN)
            for level in range(CAND_LEVELS):
                held = cand_ref[level]
                cand_ref[level] = jnp.maximum(held, x)
                x = jnp.minimum(held, x)
            return carry

        lax.fori_loop(0, n_chunks, capture, 0)
        captured = count_cand(lo) == in_bracket(state)

        def live_captured(state):
            return active(state) & captured

        state = bisect_while(count_cand, c_hi, live_captured,
                             lambda st: any_set(jnp.where(live_captured(st), 1.0, 0.0)), state)

        state = bisect_while(count_all, 0, active,
                             lambda st: any_set(jnp.where(active(st), 1.0, 0.0)), state)
        thr_ref[...] = state[0]

    _load_q_group(qt_ref, qall_ref)
    _softmax_init(m_ref, acc_ref)
    thr = thr_ref[...]

    def logits(c):
        r0 = pl.multiple_of(jnp.minimum(c, n_chunks - 1) * CHUNK, CHUNK)
        sel = (keys_ref[pl.ds(r0, CHUNK), :] >= thr) & causal(c, CHUNK)
        s = jnp.dot(k_ref[pl.ds(r0, CHUNK), :], qall_ref[...], preferred_element_type=F32)
        return _add_group_bias(s, jnp.where(sel, 0.0, NEG))

    _attend_range(n_chunks, logits, vt_ref, (sa_ref, sb_ref), (cma_ref, cmb_ref), m_ref, acc_ref,
                  fuse_pairs=True)
    _softmax_finish(o_ref, acc_ref)


def _dsa(qit, wt, ki, qt, k, vtt, topk):
    s = ki.shape[0]
    tq = CHUNK
    assert s % (tq * CHUNKS_PER_STEP) == 0 and topk <= CHUNK
    gw = GROUP * HEAD_DIM
    return pl.pallas_call(
        functools.partial(_dsa_body, topk=topk),
        grid=(s // tq, N_KV_HEADS),
        in_specs=[
            pl.BlockSpec((IDX_HEADS * IDX_DIM, tq), lambda i, g: (0, i)),
            pl.BlockSpec((IDX_HEADS, tq), lambda i, g: (0, i)),
            pl.BlockSpec((s, IDX_DIM), lambda i, g: (0, 0), pipeline_mode=pl.Buffered(1)),
            pl.BlockSpec((gw, tq), lambda i, g: (g, i)),
            pl.BlockSpec((s, HEAD_DIM), lambda i, g: (0, g)),
            pl.BlockSpec((s // tq, V_ROWS, tq), lambda i, g: (0, g, 0)),
        ],
        out_specs=pl.BlockSpec((tq, gw), lambda i, g: (i, g)),
        out_shape=jax.ShapeDtypeStruct((s, N_MAIN_HEADS * HEAD_DIM), BF16),
        scratch_shapes=[
            pltpu.VMEM((s, tq), jnp.int32),
            pltpu.VMEM((CHUNK, tq), jnp.int32),
            pltpu.VMEM((CAND_LEVELS, CHUNK, tq), jnp.int32),
            pltpu.VMEM((1, tq), jnp.int32),
            pltpu.VMEM((HEAD_DIM, GROUP * tq), BF16),
            pltpu.VMEM((CHUNKS_PER_STEP * CHUNK, GROUP * tq), F32),
            pltpu.VMEM((CHUNKS_PER_STEP * CHUNK, GROUP * tq), F32),
            pltpu.VMEM((1, GROUP * tq), F32),
            pltpu.VMEM((1, GROUP * tq), F32),
            pltpu.VMEM((1, GROUP * tq), F32),
            pltpu.VMEM((V_ROWS, GROUP * tq), F32),
        ],
        compiler_params=_params("arbitrary", "arbitrary"),
        name="dsa",
    )(qit, wt, ki, qt, k, vtt)


def _moba_body(qt_ref, k_ref, vt_ref, km_ref, o_ref, qaug_ref, sa_ref, sb_ref, cma_ref, cmb_ref,
               m_ref, acc_ref, *, n_sel):
    cur = pl.program_id(0)
    tq = CHUNK
    nb = km_ref.shape[0]
    width = GROUP * tq
    for hh in range(GROUP):
        qaug_ref[:HEAD_DIM, hh * tq:(hh + 1) * tq] = qt_ref[hh * HEAD_DIM:(hh + 1) * HEAD_DIM, :]
    q_all = qaug_ref[:HEAD_DIM, :]

    blk_id = lax.broadcasted_iota(jnp.int32, (nb, width), 0)
    past = blk_id < cur
    gate = jnp.dot(km_ref[...], q_all, preferred_element_type=F32)
    gate = jnp.where(past, gate, -jnp.inf)
    chosen = jnp.zeros((nb, width), jnp.bool_)
    for _ in range(n_sel):
        best = jnp.max(gate, axis=0, keepdims=True)
        first = jnp.min(jnp.where(gate == best, blk_id, nb), axis=0, keepdims=True)
        pick = blk_id == first
        chosen = chosen | pick
        gate = jnp.where(pick, -jnp.inf, gate)
    qaug_ref[HEAD_DIM:HEAD_DIM + nb, :] = jnp.where(chosen & past, 0.0, NEG).astype(BF16)
    pad = BF16_TILE_ROWS
    qaug_ref[HEAD_DIM + nb:HEAD_DIM + nb + pad, :] = jnp.full((pad, width), NEG, BF16)
    if nb + pad < HEAD_DIM:
        qaug_ref[HEAD_DIM + nb + pad:, :] = jnp.zeros((HEAD_DIM - nb - pad, width), BF16)

    _softmax_init(m_ref, acc_ref)
    state = (vt_ref, (sa_ref, sb_ref), (cma_ref, cmb_ref), m_ref, acc_ref)

    def past_logits(c):
        r0 = pl.multiple_of(jnp.where(c < cur, c, nb) * CHUNK, CHUNK)
        return jnp.dot(k_ref[pl.ds(r0, CHUNK), :], qaug_ref[...], preferred_element_type=F32)

    _attend_range(cur, past_logits, *state, fuse_pairs=False)

    tri = (lax.broadcasted_iota(jnp.int32, (tq, tq), 0) <= lax.broadcasted_iota(jnp.int32, (tq, tq), 1))

    def own_logits(c):
        r0 = pl.multiple_of(c * CHUNK, CHUNK)
        s = jnp.dot(k_ref[pl.ds(r0, CHUNK), :HEAD_DIM], q_all, preferred_element_type=F32)
        return _add_group_bias(s, jnp.where(tri, 0.0, NEG))

    _logits_pass([cur], own_logits, sa_ref, cma_ref)
    _softmax_pass([cur], vt_ref, sa_ref, cma_ref, m_ref, acc_ref)
    _softmax_finish(o_ref, acc_ref)


def _moba(qt, k_aug, vtt, kmeans, n_sel):
    s = k_aug.shape[0] - CHUNK
    assert MOBA_BLOCK == CHUNK and s % (CHUNK * CHUNKS_PER_STEP) == 0
    tq = CHUNK
    nb = s // tq
    assert nb + BF16_TILE_ROWS <= HEAD_DIM and vtt.shape[0] == nb + 1
    gw = GROUP * HEAD_DIM
    return pl.pallas_call(
        functools.partial(_moba_body, n_sel=n_sel),
        grid=(nb, N_KV_HEADS),
        in_specs=[
            pl.BlockSpec((gw, tq), lambda i, g: (g, i)),
            pl.BlockSpec((s + CHUNK, 2 * HEAD_DIM), lambda i, g: (0, g)),
            pl.BlockSpec((nb + 1, V_ROWS, tq), lambda i, g: (0, g, 0)),
            pl.BlockSpec((nb, HEAD_DIM), lambda i, g: (0, g)),
        ],
        out_specs=pl.BlockSpec((tq, gw), lambda i, g: (i, g)),
        out_shape=jax.ShapeDtypeStruct((s, N_MAIN_HEADS * HEAD_DIM), BF16),
        scratch_shapes=[
            pltpu.VMEM((2 * HEAD_DIM, GROUP * tq), BF16),
            pltpu.VMEM((CHUNKS_PER_STEP * CHUNK, GROUP * tq), F32),
            pltpu.VMEM((CHUNKS_PER_STEP * CHUNK, GROUP * tq), F32),
            pltpu.VMEM((1, GROUP * tq), F32),
            pltpu.VMEM((1, GROUP * tq), F32),
            pltpu.VMEM((1, GROUP * tq), F32),
            pltpu.VMEM((V_ROWS, GROUP * tq), F32),
        ],
        compiler_params=_params("parallel", "arbitrary"),
        name="moba",
    )(qt, k_aug, vtt, kmeans)


def _mem_attn_body(q_ref, kt_ref, v_ref, o_ref):
    for h in range(N_MEM_HEADS):
        sl = slice(h * HEAD_DIM, (h + 1) * HEAD_DIM)
        s = jnp.dot(q_ref[:, sl], kt_ref[sl, :], preferred_element_type=F32)
        p = jnp.exp(s - jnp.max(s, axis=-1, keepdims=True))
        o = jnp.dot(p.astype(BF16), v_ref[:, sl], preferred_element_type=F32)
        o_ref[:, sl] = (o / jnp.sum(p, axis=-1, keepdims=True)).astype(o_ref.dtype)


def _mem_attn(qm, kmt, vm, *, tq=512):
    s, w = qm.shape
    m = vm.shape[0]
    tq = min(tq, s)
    assert s % tq == 0
    return pl.pallas_call(
        _mem_attn_body,
        grid=(s // tq,),
        in_specs=[pl.BlockSpec((tq, w), lambda i: (i, 0)),
                  pl.BlockSpec((w, m), lambda i: (0, 0)),
                  pl.BlockSpec((m, w), lambda i: (0, 0))],
        out_specs=pl.BlockSpec((tq, w), lambda i: (i, 0)),
        out_shape=jax.ShapeDtypeStruct((s, w), BF16),
        compiler_params=_params("parallel"),
        name="mem_attn",
    )(qm, kmt, vm)


def _out_proj_body(x_ref, om_ref, oq_ref, w_ref, o_ref):
    n_main = om_ref.shape[1]
    o_ref[...] = (x_ref[...]
                  + jnp.dot(om_ref[...], w_ref[:n_main, :], preferred_element_type=F32)
                  + jnp.dot(oq_ref[...], w_ref[n_main:, :], preferred_element_type=F32))


def _out_proj(x, o_main, o_mem, w, *, tm=512):
    s, d = x.shape
    tm = min(tm, s)
    assert s % tm == 0 and w.shape[0] == o_main.shape[1] + o_mem.shape[1]
    row = lambda i: (i, 0)
    const = lambda i: (0, 0)
    return pl.pallas_call(
        _out_proj_body,
        grid=(s // tm,),
        in_specs=[pl.BlockSpec((tm, d), row),
                  pl.BlockSpec((tm, o_main.shape[1]), row),
                  pl.BlockSpec((tm, o_mem.shape[1]), row),
                  pl.BlockSpec(w.shape, const, pipeline_mode=pl.Buffered(1))],
        out_specs=pl.BlockSpec((tm, d), row),
        out_shape=jax.ShapeDtypeStruct((s, d), F32),
        compiler_params=_params("parallel"),
        name="out_proj",
    )(x, o_main, o_mem, w)


def _rope_tables(positions, dim):
    inv = 1.0 / (ROPE_THETA ** (jnp.arange(0, dim, 2, dtype=F32) / dim))
    ang = positions.astype(F32)[:, None] * inv
    c, s = jnp.cos(ang), jnp.sin(ang)
    reps = LANES // dim
    return jnp.tile(jnp.concatenate([c, c], -1), (1, reps)), jnp.tile(jnp.concatenate([-s, s], -1), (1, reps))


def kernel(x, mem, positions, ffn1_norm, ffn1_w_gate_up, ffn1_w_down, attn_norm, mem_norm, a_w_in, idx_k_norm, b_w_in, w_mem_kv, w_out, ffn2_norm, ffn2_w_gate_up, ffn2_w_down, kv_norm, w_kv_shared, final_norm):
    b, s, d = x.shape
    assert b == 1 and mem.shape[0] == 1
    depth = ffn1_norm.shape[0]
    n_a = a_w_in.shape[0]
    main_w = N_MAIN_HEADS * HEAD_DIM
    kv_w = N_KV_HEADS * HEAD_DIM
    idx_w = IDX_HEADS * IDX_DIM
    mem_w = N_MEM_HEADS * HEAD_DIM
    topk = min(IDX_TOPK_MAX, s // 4)
    nb = s // MOBA_BLOCK
    n_sel = min(MOBA_TOPK_MAX, max(nb - 1, 1))

    cos, sin = _rope_tables(positions[0], HEAD_DIM)
    cosi, sini = _rope_tables(positions[0], IDX_DIM)
    tables = (cos, sin, cosi, sini)
    mem_tables = tuple(t[:mem.shape[1]] for t in tables)
    no_gk = jnp.zeros((1, LANES), F32)

    xs = x[0]
    mem2 = mem[0]
    k_sh = vtt_sh = kmeans = None
    for i in range(depth):
        if i == n_a:
            wk = w_kv_shared[:, :kv_w].astype(BF16)
            wv = w_kv_shared[:, kv_w:].astype(BF16)
            k_sh, vtt_sh, km = _kv_shared(xs, kv_norm, cos, sin, wk, wv)
            kmeans = km[:nb].reshape(nb, kv_w).astype(BF16)

        last = i == depth - 1
        xs = _ffn(xs, ffn1_norm[i], ffn1_w_gate_up[i].astype(BF16), ffn1_w_down[i].astype(BF16))

        wm = w_mem_kv[i].astype(BF16)
        mk, mv = _proj(mem2, mem_norm[i], mem_tables, no_gk, wm,
                       [_Piece("plain", 0, mem_w, 0, mem_w, BF16),
                        _Piece("plain", mem_w, mem_w, 0, mem_w, BF16)])
        if i < n_a:
            tail = main_w + 2 * kv_w + idx_w
            qm_lo = IDX_DIM + IDX_HEADS
            assert a_w_in.shape[2] == tail + qm_lo + mem_w and qm_lo <= LANES
            tail_w = -(-(qm_lo + mem_w) // LANES) * LANES
            wa = jnp.pad(a_w_in[i].astype(BF16), ((0, 0), (0, tail + tail_w - a_w_in.shape[2])))
            gk = jnp.pad(idx_k_norm[i], (0, LANES - IDX_DIM)).reshape(1, LANES)
            qt, k, vtt, qit, kiwi, qm = _proj(
                xs, attn_norm[i], tables, gk, wa,
                [_Piece("rope_scaled_t", 0, main_w, 0, main_w, BF16),
                 _Piece("rope", main_w, kv_w, 0, kv_w, BF16),
                 _Piece("value_chunks", main_w + kv_w, kv_w, 0, kv_w, BF16),
                 _Piece("rope_idx_t", main_w + 2 * kv_w, idx_w, 0, idx_w, BF16),
                 _Piece("kiwi", tail, tail_w, 0, LANES, F32),
                 _Piece("scaled", tail, tail_w, qm_lo, qm_lo + mem_w, BF16)])
            ki = kiwi[:, :IDX_DIM].astype(BF16)
            wt = kiwi[:, IDX_DIM:IDX_DIM + IDX_HEADS].T
            o_main = _dsa(qit, wt, ki, qt, k, vtt, topk)
        else:
            qt, qm = _proj(xs, attn_norm[i], tables, no_gk, b_w_in[i - n_a].astype(BF16),
                           [_Piece("rope_scaled_t", 0, main_w, 0, main_w, BF16),
                            _Piece("scaled", main_w, mem_w, 0, mem_w, BF16)])
            o_main = _moba(qt, k_sh, vtt_sh, kmeans, n_sel)
        o_mem = _mem_attn(qm, mk.T, mv)
        xs = _out_proj(xs, o_main, o_mem, w_out[i].astype(BF16))

        xs = _ffn(xs, ffn2_norm[i], ffn2_w_gate_up[i].astype(BF16), ffn2_w_down[i].astype(BF16),
                  final_norm if last else None)
    return xs[None]
```

```python
import functools
from typing import Any, NamedTuple

import jax
import jax.numpy as jnp
import numpy as np
from jax import lax
from jax.experimental import pallas as pl
from jax.experimental.pallas import tpu as pltpu

HEAD_DIM = 128
N_MAIN_HEADS = 12
N_KV_HEADS = 4
GROUP = N_MAIN_HEADS // N_KV_HEADS
N_MEM_HEADS = 4
IDX_HEADS = 16
IDX_DIM = 64
IDX_TOPK_MAX = 256
MOBA_BLOCK = 256
MOBA_TOPK_MAX = 3
ROPE_THETA = 10000.0
RMS_EPS = 1e-6

LANES = 128
SUBLANES = 8
VMEM_LIMIT = 56 * 1024 * 1024
NEG = -1e30
LOG2_E = 1.4426950408889634
INT_MIN = -2 ** 31
INT_MAX = 2 ** 31 - 1
CHUNK = 256
CHUNKS_PER_STEP = 4
CAND_LEVELS = 2
CAND_MAX = 16
MAX_PASSES = 40
COUNT_UNROLL = 4
BF16_TILE_ROWS = 2 * SUBLANES
ONES_ROWS = BF16_TILE_ROWS
V_ROWS = HEAD_DIM + ONES_ROWS

F32 = jnp.float32
BF16 = jnp.bfloat16


def _params(*sem):
    return pltpu.CompilerParams(dimension_semantics=sem, vmem_limit_bytes=VMEM_LIMIT)


def _rms(x, gain):
    return x * lax.rsqrt(jnp.mean(x * x, axis=-1, keepdims=True) + RMS_EPS) * gain


def _ffn_body(x_ref, g_ref, wg_ref, wu_ref, wd_ref, pg_ref, o_ref, h_ref, *, final_norm):
    j = pl.program_id(1)

    @pl.when(j == 0)
    def _():
        x = x_ref[...]
        h_ref[...] = _rms(x, g_ref[...]).astype(BF16)
        o_ref[...] = x

    h = h_ref[...]
    gate = jnp.dot(h, wg_ref[...], preferred_element_type=F32)
    up = jnp.dot(h, wu_ref[...], preferred_element_type=F32)
    act = (gate * (0.5 / (1.0 + jnp.exp(-gate))) * up).astype(BF16)
    o_ref[...] += jnp.dot(act, wd_ref[...], preferred_element_type=F32)

    if final_norm:
        @pl.when(j == pl.num_programs(1) - 1)
        def _():
            o_ref[...] = _rms(o_ref[...], pg_ref[...])


def _ffn(x, gain, w_gate_up, w_down, layer, post_gain=None, *, tm=512, tf=512):
    s, d = x.shape
    f = w_down.shape[1]
    tm = min(tm, s)
    tf = min(tf, f)
    assert s % tm == 0 and f % tf == 0
    nf = f // tf
    final_norm = post_gain is not None
    pg = post_gain if final_norm else gain
    return pl.pallas_call(
        functools.partial(_ffn_body, final_norm=final_norm),
        grid=(s // tm, nf),
        in_specs=[
            pl.BlockSpec((tm, d), lambda i, j: (i, 0)),
            pl.BlockSpec((1, d), lambda i, j: (0, 0)),
            pl.BlockSpec((None, d, tf), lambda i, j: (layer, 0, j)),
            pl.BlockSpec((None, d, tf), lambda i, j: (layer, 0, j + nf)),
            pl.BlockSpec((None, tf, d), lambda i, j: (layer, j, 0)),
            pl.BlockSpec((1, d), lambda i, j: (0, 0)),
        ],
        out_specs=pl.BlockSpec((tm, d), lambda i, j: (i, 0)),
        out_shape=jax.ShapeDtypeStruct((s, d), F32),
        scratch_shapes=[pltpu.VMEM((tm, d), BF16)],
        compiler_params=_params("parallel", "arbitrary"),
        name="ffn",
    )(x, gain.reshape(1, d), w_gate_up, w_gate_up, w_down, pg.reshape(1, d))


def _rope_heads(y, cos, sin, o_ref, scale, transposed=False):
    for h in range(y.shape[1] // HEAD_DIM):
        t = y[:, h * HEAD_DIM:(h + 1) * HEAD_DIM]
        r = t * cos + pltpu.roll(t, HEAD_DIM // 2, 1) * sin
        if scale != 1.0:
            r = r * scale
        if transposed:
            o_ref[h * HEAD_DIM:(h + 1) * HEAD_DIM, :] = r.T.astype(o_ref.dtype)
        else:
            o_ref[:, h * HEAD_DIM:(h + 1) * HEAD_DIM] = r.astype(o_ref.dtype)


def _store_value_chunks(y, o_ref):
    for b in range(y.shape[0] // CHUNK):
        for g in range(N_KV_HEADS):
            blk = y[b * CHUNK:(b + 1) * CHUNK, g * HEAD_DIM:(g + 1) * HEAD_DIM]
            o_ref[b, g * V_ROWS:g * V_ROWS + HEAD_DIM, :] = blk.T.astype(o_ref.dtype)
            o_ref[b, g * V_ROWS + HEAD_DIM:(g + 1) * V_ROWS, :] = jnp.ones((ONES_ROWS, CHUNK), o_ref.dtype)


def _rot_idx(t):
    lane = lax.broadcasted_iota(jnp.int32, t.shape, 1)
    first_half = (lane & (IDX_DIM // 2)) == 0
    return jnp.where(first_half, pltpu.roll(t, LANES - IDX_DIM // 2, 1),
                     pltpu.roll(t, IDX_DIM // 2, 1))


class _Piece(NamedTuple):
    kind: str
    start: int
    width: int
    lo: int
    hi: int
    dtype: Any


def _proj_body(*refs, pieces, q_scale, idx_w_scale):
    x_ref, g_ref, cos_ref, sin_ref, cosi_ref, sini_ref, gk_ref, w_ref = refs[:8]
    o_refs = refs[8:]
    h = _rms(x_ref[...], g_ref[...]).astype(BF16)
    products = {}
    for piece, o_ref in zip(pieces, o_refs):
        cols = (piece.start, piece.width)
        if cols not in products:
            products[cols] = jnp.dot(h, w_ref[:, piece.start:piece.start + piece.width],
                                     preferred_element_type=F32)
        y = products[cols]
        if (piece.lo, piece.hi) != (0, piece.width):
            y = y[:, piece.lo:piece.hi]
        kind = piece.kind
        if kind == "plain":
            o_ref[...] = y.astype(o_ref.dtype)
        elif kind == "scaled":
            o_ref[...] = (y * q_scale).astype(o_ref.dtype)
        elif kind == "rope":
            _rope_heads(y, cos_ref[...], sin_ref[...], o_ref, 1.0)
        elif kind == "rope_scaled_t":
            _rope_heads(y, cos_ref[...], sin_ref[...], o_ref, q_scale * LOG2_E, transposed=True)
        elif kind == "rope_idx_t":
            cosi, sini = cosi_ref[...], sini_ref[...]
            for c in range(y.shape[1] // LANES):
                t = y[:, c * LANES:(c + 1) * LANES]
                o_ref[c * LANES:(c + 1) * LANES, :] = (t * cosi + _rot_idx(t) * sini).T.astype(o_ref.dtype)
        elif kind == "value_chunks":
            _store_value_chunks(y, o_ref)
        elif kind == "kiwi":
            lane = lax.broadcasted_iota(jnp.int32, y.shape, 1)
            is_k = lane < IDX_DIM
            kk = jnp.where(is_k, y, 0.0)
            ms = jnp.sum(kk * kk, axis=-1, keepdims=True) * (1.0 / IDX_DIM)
            kn = kk * lax.rsqrt(ms + RMS_EPS) * gk_ref[...]
            kr = kn * cosi_ref[...] + _rot_idx(kn) * sini_ref[...]
            o_ref[...] = jnp.where(is_k, kr, y * idx_w_scale)
        else:
            raise ValueError(kind)


def _layer_weight_spec(w, layer):
    if w.ndim == 2:
        return pl.BlockSpec(w.shape, lambda i: (0, 0), pipeline_mode=pl.Buffered(1))
    return pl.BlockSpec((None,) + w.shape[1:], lambda i: (layer, 0, 0), pipeline_mode=pl.Buffered(1))


def _proj(x, gain, tables, gk, w, pieces, *, layer=0, tm=256):
    s, d = x.shape
    tm = min(tm, s)
    assert s % tm == 0
    cos, sin, cosi, sini = tables
    row = lambda i: (i, 0)
    const = lambda i: (0, 0)
    in_specs = [pl.BlockSpec((tm, d), row), pl.BlockSpec((1, d), const)]
    in_specs += [pl.BlockSpec((tm, LANES), row)] * 4
    in_specs += [pl.BlockSpec((1, LANES), const)]
    in_specs += [_layer_weight_spec(w, layer)]
    out_specs, out_shape = [], []
    for p in pieces:
        assert p.start % LANES == 0 and p.width % LANES == 0 and p.start + p.width <= w.shape[-1]
        n = p.hi - p.lo
        if p.kind.endswith("_t"):
            out_specs.append(pl.BlockSpec((n, tm), lambda i: (0, i)))
            out_shape.append(jax.ShapeDtypeStruct((n, s), p.dtype))
        elif p.kind == "value_chunks":
            assert tm % CHUNK == 0 and n == N_KV_HEADS * HEAD_DIM
            out_specs.append(pl.BlockSpec((tm // CHUNK, N_KV_HEADS * V_ROWS, CHUNK), lambda i: (i, 0, 0)))
            out_shape.append(jax.ShapeDtypeStruct((s // CHUNK, N_KV_HEADS * V_ROWS, CHUNK), p.dtype))
        else:
            out_specs.append(pl.BlockSpec((tm, n), row))
            out_shape.append(jax.ShapeDtypeStruct((s, n), p.dtype))
    return pl.pallas_call(
        functools.partial(_proj_body, pieces=tuple(pieces), q_scale=HEAD_DIM ** -0.5,
                          idx_w_scale=(IDX_HEADS ** -0.5) * (IDX_DIM ** -0.5)),
        grid=(s // tm,),
        in_specs=in_specs,
        out_specs=out_specs,
        out_shape=out_shape,
        compiler_params=_params("parallel"),
        name="proj",
    )(x, gain.reshape(1, d), cos, sin, cosi, sini, gk, w)


def _kv_shared_body(x_ref, g_ref, cos_ref, sin_ref, wk_ref, wv_ref, k_ref, v_ref, km_ref):
    h = _rms(x_ref[...], g_ref[...]).astype(BF16)
    yk = jnp.dot(h, wk_ref[...], preferred_element_type=F32)
    cos, sin = cos_ref[...], sin_ref[...]
    tm = yk.shape[0]
    first_blk = pl.program_id(0) * (tm // MOBA_BLOCK)
    lane = lax.broadcasted_iota(jnp.int32, (MOBA_BLOCK, HEAD_DIM), 1)
    for hd in range(yk.shape[1] // HEAD_DIM):
        t = yk[:, hd * HEAD_DIM:(hd + 1) * HEAD_DIM]
        r = t * cos + pltpu.roll(t, HEAD_DIM // 2, 1) * sin
        k_ref[:, 2 * hd * HEAD_DIM:(2 * hd + 1) * HEAD_DIM] = r.astype(k_ref.dtype)
        for b in range(tm // MOBA_BLOCK):
            rows = slice(b * MOBA_BLOCK, (b + 1) * MOBA_BLOCK)
            k_ref[rows, (2 * hd + 1) * HEAD_DIM:(2 * hd + 2) * HEAD_DIM] = (
                jnp.where(lane == first_blk + b, 1.0, 0.0).astype(k_ref.dtype))
            km_ref[b, :, hd * HEAD_DIM:(hd + 1) * HEAD_DIM] = (
                jnp.sum(r[rows], axis=0, keepdims=True) * (1.0 / MOBA_BLOCK))
    _store_value_chunks(jnp.dot(h, wv_ref[...], preferred_element_type=F32), v_ref)


def _kv_shared(x, gain, cos, sin, wk, wv):
    s, d = x.shape
    tm = MOBA_BLOCK
    assert s % tm == 0 and CHUNK == MOBA_BLOCK
    n = s // tm
    nkv = wk.shape[1]
    src = lambda i: (jnp.minimum(i, n - 1), 0)
    row = lambda i: (i, 0)
    const = lambda i: (0, 0)
    return pl.pallas_call(
        _kv_shared_body,
        grid=(n + 1,),
        in_specs=[pl.BlockSpec((tm, d), src), pl.BlockSpec((1, d), const),
                  pl.BlockSpec((tm, LANES), src), pl.BlockSpec((tm, LANES), src),
                  pl.BlockSpec(wk.shape, const, pipeline_mode=pl.Buffered(1)),
                  pl.BlockSpec(wv.shape, const, pipeline_mode=pl.Buffered(1))],
        out_specs=[pl.BlockSpec((tm, 2 * nkv), row),
                   pl.BlockSpec((1, N_KV_HEADS * V_ROWS, CHUNK), lambda i: (i, 0, 0)),
                   pl.BlockSpec((1, 1, nkv), lambda i: (i, 0, 0))],
        out_shape=[jax.ShapeDtypeStruct((s + tm, 2 * nkv), BF16),
                   jax.ShapeDtypeStruct((n + 1, N_KV_HEADS * V_ROWS, CHUNK), BF16),
                   jax.ShapeDtypeStruct((n + 1, 1, nkv), F32)],
        compiler_params=_params("parallel"),
        name="kv_shared",
    )(x, gain.reshape(1, d), cos, sin, wk, wv)


def _load_q_group(qt_ref, qall_ref):
    tq = qt_ref.shape[1]
    for hh in range(GROUP):
        qall_ref[:, hh * tq:(hh + 1) * tq] = qt_ref[hh * HEAD_DIM:(hh + 1) * HEAD_DIM, :]


def _softmax_init(m_ref, acc_ref):
    m_ref[...] = jnp.full(m_ref.shape, NEG, F32)
    acc_ref[...] = jnp.zeros(acc_ref.shape, F32)


def _add_group_bias(s, b):
    tq = b.shape[1]
    return jnp.concatenate([s[:, hh * tq:(hh + 1) * tq] + b for hh in range(GROUP)], axis=1)


def _logits_pass(chunk_ids, logits_fn, s_ref, cmax_ref):
    cmax = None
    for j, c in enumerate(chunk_ids):
        s = logits_fn(c)
        s_ref[j * CHUNK:(j + 1) * CHUNK, :] = s
        cm = jnp.max(s, axis=0, keepdims=True)
        cmax = cm if cmax is None else jnp.maximum(cmax, cm)
    cmax_ref[...] = cmax


def _softmax_pass(chunk_ids, vt_ref, s_ref, cmax_ref, m_ref, acc_ref):
    m_old = m_ref[...]
    m_new = jnp.maximum(m_old, cmax_ref[...])
    alpha = jnp.exp2(m_old - m_new)
    pv = None
    for j, c in enumerate(chunk_ids):
        p = jnp.exp2(s_ref[j * CHUNK:(j + 1) * CHUNK, :] - m_new)
        d = jnp.dot(vt_ref[c], p.astype(BF16), preferred_element_type=F32)
        pv = d if pv is None else pv + d
    m_ref[...] = m_new
    acc_ref[...] = alpha * acc_ref[...] + pv


def _attend_range(n_chunks, logits_fn, vt_ref, s_refs, cmax_refs, m_ref, acc_ref, *, fuse_pairs):
    n_steps = pl.cdiv(n_chunks, CHUNKS_PER_STEP)
    last_chunk = vt_ref.shape[0] - 1

    def chunks(step):
        return [step * CHUNKS_PER_STEP + j for j in range(CHUNKS_PER_STEP)]

    def logits(step, slot):
        _logits_pass(chunks(step), logits_fn, s_refs[slot], cmax_refs[slot])

    def softmax(step, slot):
        ids = [jnp.minimum(c, last_chunk) for c in chunks(step)]
        _softmax_pass(ids, vt_ref, s_refs[slot], cmax_refs[slot], m_ref, acc_ref)

    @pl.when(n_steps > 0)
    def _():
        logits(0, 0)

    def pair(u, carry):
        t = 2 * u
        if fuse_pairs:
            @pl.when(t + 1 < n_steps)
            def _():
                logits(t + 1, 1)
                softmax(t, 0)
                logits(t + 2, 0)
                softmax(t + 1, 1)

            @pl.when(t + 1 >= n_steps)
            def _():
                softmax(t, 0)
        else:
            logits(t + 1, 1)
            softmax(t, 0)

            @pl.when(t + 1 < n_steps)
            def _():
                logits(t + 2, 0)
                softmax(t + 1, 1)

        return carry

    lax.fori_loop(0, pl.cdiv(n_steps, 2), pair, 0)


def _softmax_finish(o_ref, acc_ref):
    tq = o_ref.shape[0]
    out = acc_ref[:HEAD_DIM, :] / acc_ref[HEAD_DIM:HEAD_DIM + 1, :]
    for hh in range(GROUP):
        o_ref[:, hh * HEAD_DIM:(hh + 1) * HEAD_DIM] = out[:, hh * tq:(hh + 1) * tq].T.astype(o_ref.dtype)


def _dsa_body(qit_ref, wt_ref, ki_ref, qt_ref, k_ref, vt_ref, o_ref,
              keys_ref, gmax_ref, cand_ref, thr_ref, qall_ref, sa_ref, sb_ref, cma_ref, cmb_ref,
              m_ref, acc_ref, *, topk):
    i = pl.program_id(0)
    g = pl.program_id(1)
    tq = CHUNK
    n_chunks = i + 1
    sub = 128

    def causal(c, rows, row_off=0):
        kpos = c * CHUNK + row_off + lax.broadcasted_iota(jnp.int32, (rows, tq), 0)
        qpos = i * tq + lax.broadcasted_iota(jnp.int32, (rows, tq), 1)
        return kpos <= qpos

    @pl.when(g == 0)
    def _():
        gmax_ref[...] = jnp.full(gmax_ref.shape, INT_MIN, jnp.int32)

        def score_chunk(c, carry):
            for part in range(CHUNK // sub):
                r0 = pl.multiple_of(c * CHUNK + part * sub, sub)
                kit = ki_ref[pl.ds(r0, sub), :]
                sc = jnp.zeros((sub, tq), F32)
                for h in range(IDX_HEADS):
                    d = jnp.dot(kit, qit_ref[h * IDX_DIM:(h + 1) * IDX_DIM, :], preferred_element_type=F32)
                    sc = sc + wt_ref[h:h + 1, :] * jnp.maximum(d, 0.0)
                sc = jnp.where(causal(c, sub, part * sub), sc, -jnp.inf)
                bits = pltpu.bitcast(sc, jnp.int32)
                key = bits ^ ((bits >> 31) & INT_MAX)
                keys_ref[pl.ds(r0, sub), :] = key
                gsl = slice(part * sub, (part + 1) * sub)
                gmax_ref[gsl, :] = jnp.maximum(gmax_ref[gsl, :], key)
            return carry

        def score_pair(u, carry):
            score_chunk(2 * u, carry)
            return score_chunk(2 * u + 1, carry)

        lax.fori_loop(0, n_chunks // 2, score_pair, 0)

        @pl.when(n_chunks % 2 == 1)
        def _():
            score_chunk(n_chunks - 1, 0)

        def count_rows(load, n_blocks, t):
            def count_block(b, acc):
                ge = jnp.where(load(b) >= t, 1, 0).astype(jnp.int32)
                return acc + jnp.sum(ge.reshape(CHUNK // SUBLANES, SUBLANES, tq), axis=0)

            part = jnp.zeros((SUBLANES, tq), jnp.int32)
            if isinstance(n_blocks, int):
                for b in range(n_blocks):
                    part = count_block(b, part)
            else:
                def count_group(u, acc):
                    for j in range(COUNT_UNROLL):
                        acc = count_block(u * COUNT_UNROLL + j, acc)
                    return acc

                n_groups = n_blocks // COUNT_UNROLL
                part = lax.fori_loop(0, n_groups, count_group, part)
                part = lax.fori_loop(n_groups * COUNT_UNROLL, n_blocks, count_block, part)
            return jnp.sum(part, axis=0, keepdims=True)

        def key_chunk(c):
            return keys_ref[pl.ds(pl.multiple_of(c * CHUNK, CHUNK), CHUNK), :]

        def count_all(t):
            return count_rows(key_chunk, n_chunks, t)

        def count_cand(t):
            return count_rows(lambda b: cand_ref[b], CAND_LEVELS, t)

        def any_set(flags):
            return jnp.max(flags) > 0.0

        def bisect_pass(count_fn, base, live, state):
            lo, hi, c_lo, c_hi, done = state
            mid = (lo >> 1) + (hi >> 1) + (lo & hi & 1)
            cnt = count_fn(mid) + base
            up = live & (cnt >= topk)
            down = live & (cnt < topk)
            lo, c_lo = jnp.where(up, mid, lo), jnp.where(up, cnt, c_lo)
            hi, c_hi = jnp.where(down, mid, hi), jnp.where(down, cnt, c_hi)
            finished = live & ((cnt == topk) | (hi <= lo + 1))
            return lo, hi, c_lo, c_hi, jnp.where(finished, 1.0, done)

        def bisect_while(count_fn, base, live_fn, keep_going, state):
            def cond(carry):
                it, state = carry
                return jnp.logical_and(it < MAX_PASSES, keep_going(state))

            def body(carry):
                it, state = carry
                return it + 1, bisect_pass(count_fn, base, live_fn(state), state)

            return lax.while_loop(cond, body, (jnp.int32(0), state))[1]

        def in_bracket(state):
            return state[2] - state[3]

        def active(state):
            return state[4] == 0.0

        gm = gmax_ref[...]
        lo = jnp.min(gm, axis=0, keepdims=True)
        top = jnp.max(gm, axis=0, keepdims=True)
        hi = jnp.where(top == INT_MAX, top, top + 1)
        c_lo = count_all(lo)
        state = (lo, hi, c_lo, jnp.zeros_like(c_lo), (c_lo == topk).astype(F32))

        def wide(state):
            return any_set(jnp.where(active(state) & (in_bracket(state) > CAND_MAX), 1.0, 0.0))

        state = bisect_while(count_all, 0, active, wide, state)

        lo, hi, c_lo, c_hi, _ = state
        cand_ref[...] = jnp.full(cand_ref.shape, INT_MIN, jnp.int32)

        def capture(c, carry):
            blk = key_chunk(c)
            x = jnp.where((blk >= lo) & (blk < hi), blk, INT_MIN)
            for level in range(CAND_LEVELS):
                held = cand_ref[level]
                cand_ref[level] = jnp.maximum(held, x)
                x = jnp.minimum(held, x)
            return carry

        lax.fori_loop(0, n_chunks, capture, 0)
        captured = count_cand(lo) == in_bracket(state)

        def live_captured(state):
            return active(state) & captured

        state = bisect_while(count_cand, c_hi, live_captured,
                             lambda st: any_set(jnp.where(live_captured(st), 1.0, 0.0)), state)

        state = bisect_while(count_all, 0, active,
                             lambda st: any_set(jnp.where(active(st), 1.0, 0.0)), state)
        thr_ref[...] = state[0]

    _load_q_group(qt_ref, qall_ref)
    _softmax_init(m_ref, acc_ref)
    thr = thr_ref[...]

    def logits(c):
        r0 = pl.multiple_of(jnp.minimum(c, n_chunks - 1) * CHUNK, CHUNK)
        sel = (keys_ref[pl.ds(r0, CHUNK), :] >= thr) & causal(c, CHUNK)
        s = jnp.dot(k_ref[pl.ds(r0, CHUNK), :], qall_ref[...], preferred_element_type=F32)
        return _add_group_bias(s, jnp.where(sel, 0.0, NEG))

    _attend_range(n_chunks, logits, vt_ref, (sa_ref, sb_ref), (cma_ref, cmb_ref), m_ref, acc_ref,
                  fuse_pairs=True)
    _softmax_finish(o_ref, acc_ref)


def _dsa(qit, wt, ki, qt, k, vtt, topk):
    s = ki.shape[0]
    tq = CHUNK
    assert s % (tq * CHUNKS_PER_STEP) == 0 and topk <= CHUNK
    gw = GROUP * HEAD_DIM
    return pl.pallas_call(
        functools.partial(_dsa_body, topk=topk),
        grid=(s // tq, N_KV_HEADS),
        in_specs=[
            pl.BlockSpec((IDX_HEADS * IDX_DIM, tq), lambda i, g: (0, i)),
            pl.BlockSpec((IDX_HEADS, tq), lambda i, g: (0, i)),
            pl.BlockSpec((s, IDX_DIM), lambda i, g: (0, 0), pipeline_mode=pl.Buffered(1)),
            pl.BlockSpec((gw, tq), lambda i, g: (g, i)),
            pl.BlockSpec((s, HEAD_DIM), lambda i, g: (0, g)),
            pl.BlockSpec((s // tq, V_ROWS, tq), lambda i, g: (0, g, 0)),
        ],
        out_specs=pl.BlockSpec((tq, gw), lambda i, g: (i, g)),
        out_shape=jax.ShapeDtypeStruct((s, N_MAIN_HEADS * HEAD_DIM), BF16),
        scratch_shapes=[
            pltpu.VMEM((s, tq), jnp.int32),
            pltpu.VMEM((CHUNK, tq), jnp.int32),
            pltpu.VMEM((CAND_LEVELS, CHUNK, tq), jnp.int32),
            pltpu.VMEM((1, tq), jnp.int32),
            pltpu.VMEM((HEAD_DIM, GROUP * tq), BF16),
            pltpu.VMEM((CHUNKS_PER_STEP * CHUNK, GROUP * tq), F32),
            pltpu.VMEM((CHUNKS_PER_STEP * CHUNK, GROUP * tq), F32),
            pltpu.VMEM((1, GROUP * tq), F32),
            pltpu.VMEM((1, GROUP * tq), F32),
            pltpu.VMEM((1, GROUP * tq), F32),
            pltpu.VMEM((V_ROWS, GROUP * tq), F32),
        ],
        compiler_params=_params("arbitrary", "arbitrary"),
        name="dsa",
    )(qit, wt, ki, qt, k, vtt)


def _moba_body(qt_ref, k_ref, vt_ref, km_ref, o_ref, qaug_ref, sa_ref, sb_ref, cma_ref, cmb_ref,
               m_ref, acc_ref, *, n_sel):
    cur = pl.program_id(0)
    tq = CHUNK
    nb = km_ref.shape[0]
    width = GROUP * tq
    for hh in range(GROUP):
        qaug_ref[:HEAD_DIM, hh * tq:(hh + 1) * tq] = qt_ref[hh * HEAD_DIM:(hh + 1) * HEAD_DIM, :]
    q_all = qaug_ref[:HEAD_DIM, :]

    blk_id = lax.broadcasted_iota(jnp.int32, (nb, width), 0)
    past = blk_id < cur
    gate = jnp.dot(km_ref[...], q_all, preferred_element_type=F32)
    gate = jnp.where(past, gate, -jnp.inf)
    chosen = jnp.zeros((nb, width), jnp.bool_)
    for _ in range(n_sel):
        best = jnp.max(gate, axis=0, keepdims=True)
        first = jnp.min(jnp.where(gate == best, blk_id, nb), axis=0, keepdims=True)
        pick = blk_id == first
        chosen = chosen | pick
        gate = jnp.where(pick, -jnp.inf, gate)
    qaug_ref[HEAD_DIM:HEAD_DIM + nb, :] = jnp.where(chosen & past, 0.0, NEG).astype(BF16)
    pad = BF16_TILE_ROWS
    qaug_ref[HEAD_DIM + nb:HEAD_DIM + nb + pad, :] = jnp.full((pad, width), NEG, BF16)
    if nb + pad < HEAD_DIM:
        qaug_ref[HEAD_DIM + nb + pad:, :] = jnp.zeros((HEAD_DIM - nb - pad, width), BF16)

    _softmax_init(m_ref, acc_ref)
    state = (vt_ref, (sa_ref, sb_ref), (cma_ref, cmb_ref), m_ref, acc_ref)

    def past_logits(c):
        r0 = pl.multiple_of(jnp.where(c < cur, c, nb) * CHUNK, CHUNK)
        return jnp.dot(k_ref[pl.ds(r0, CHUNK), :], qaug_ref[...], preferred_element_type=F32)

    _attend_range(cur, past_logits, *state, fuse_pairs=False)

    tri = (lax.broadcasted_iota(jnp.int32, (tq, tq), 0) <= lax.broadcasted_iota(jnp.int32, (tq, tq), 1))

    def own_logits(c):
        r0 = pl.multiple_of(c * CHUNK, CHUNK)
        s = jnp.dot(k_ref[pl.ds(r0, CHUNK), :HEAD_DIM], q_all, preferred_element_type=F32)
        return _add_group_bias(s, jnp.where(tri, 0.0, NEG))

    _logits_pass([cur], own_logits, sa_ref, cma_ref)
    _softmax_pass([cur], vt_ref, sa_ref, cma_ref, m_ref, acc_ref)
    _softmax_finish(o_ref, acc_ref)


def _moba(qt, k_aug, vtt, kmeans, n_sel):
    s = k_aug.shape[0] - CHUNK
    assert MOBA_BLOCK == CHUNK and s % (CHUNK * CHUNKS_PER_STEP) == 0
    tq = CHUNK
    nb = s // tq
    assert nb + BF16_TILE_ROWS <= HEAD_DIM and vtt.shape[0] == nb + 1
    gw = GROUP * HEAD_DIM
    return pl.pallas_call(
        functools.partial(_moba_body, n_sel=n_sel),
        grid=(nb, N_KV_HEADS),
        in_specs=[
            pl.BlockSpec((gw, tq), lambda i, g: (g, i)),
            pl.BlockSpec((s + CHUNK, 2 * HEAD_DIM), lambda i, g: (0, g)),
            pl.BlockSpec((nb + 1, V_ROWS, tq), lambda i, g: (0, g, 0)),
            pl.BlockSpec((nb, HEAD_DIM), lambda i, g: (0, g)),
        ],
        out_specs=pl.BlockSpec((tq, gw), lambda i, g: (i, g)),
        out_shape=jax.ShapeDtypeStruct((s, N_MAIN_HEADS * HEAD_DIM), BF16),
        scratch_shapes=[
            pltpu.VMEM((2 * HEAD_DIM, GROUP * tq), BF16),
            pltpu.VMEM((CHUNKS_PER_STEP * CHUNK, GROUP * tq), F32),
            pltpu.VMEM((CHUNKS_PER_STEP * CHUNK, GROUP * tq), F32),
            pltpu.VMEM((1, GROUP * tq), F32),
            pltpu.VMEM((1, GROUP * tq), F32),
            pltpu.VMEM((1, GROUP * tq), F32),
            pltpu.VMEM((V_ROWS, GROUP * tq), F32),
        ],
        compiler_params=_params("parallel", "arbitrary"),
        name="moba",
    )(qt, k_aug, vtt, kmeans)


def _mem_attn_body(q_ref, kt_ref, v_ref, o_ref):
    for h in range(N_MEM_HEADS):
        sl = slice(h * HEAD_DIM, (h + 1) * HEAD_DIM)
        s = jnp.dot(q_ref[:, sl], kt_ref[sl, :], preferred_element_type=F32)
        p = jnp.exp(s - jnp.max(s, axis=-1, keepdims=True))
        o = jnp.dot(p.astype(BF16), v_ref[:, sl], preferred_element_type=F32)
        o_ref[:, sl] = (o / jnp.sum(p, axis=-1, keepdims=True)).astype(o_ref.dtype)


def _mem_attn(qm, kmt, vm, *, tq=512):
    s, w = qm.shape
    m = vm.shape[0]
    tq = min(tq, s)
    assert s % tq == 0
    return pl.pallas_call(
        _mem_attn_body,
        grid=(s // tq,),
        in_specs=[pl.BlockSpec((tq, w), lambda i: (i, 0)),
                  pl.BlockSpec((w, m), lambda i: (0, 0)),
                  pl.BlockSpec((m, w), lambda i: (0, 0))],
        out_specs=pl.BlockSpec((tq, w), lambda i: (i, 0)),
        out_shape=jax.ShapeDtypeStruct((s, w), BF16),
        compiler_params=_params("parallel"),
        name="mem_attn",
    )(qm, kmt, vm)


def _out_proj_body(x_ref, om_ref, oq_ref, w_ref, o_ref):
    n_main = om_ref.shape[1]
    o_ref[...] = (x_ref[...]
                  + jnp.dot(om_ref[...], w_ref[:n_main, :], preferred_element_type=F32)
                  + jnp.dot(oq_ref[...], w_ref[n_main:, :], preferred_element_type=F32))


def _out_proj(x, o_main, o_mem, w, *, layer=0, tm=512):
    s, d = x.shape
    tm = min(tm, s)
    assert s % tm == 0 and w.shape[-2] == o_main.shape[1] + o_mem.shape[1]
    row = lambda i: (i, 0)
    const = lambda i: (0, 0)
    return pl.pallas_call(
        _out_proj_body,
        grid=(s // tm,),
        in_specs=[pl.BlockSpec((tm, d), row),
                  pl.BlockSpec((tm, o_main.shape[1]), row),
                  pl.BlockSpec((tm, o_mem.shape[1]), row),
                  _layer_weight_spec(w, layer)],
        out_specs=pl.BlockSpec((tm, d), row),
        out_shape=jax.ShapeDtypeStruct((s, d), F32),
        compiler_params=_params("parallel"),
        name="out_proj",
    )(x, o_main, o_mem, w)


def _rope_tables(positions, dim):
    inv = 1.0 / (ROPE_THETA ** (jnp.arange(0, dim, 2, dtype=F32) / dim))
    ang = positions.astype(F32)[:, None] * inv
    c, s = jnp.cos(ang), jnp.sin(ang)
    reps = LANES // dim
    return jnp.tile(jnp.concatenate([c, c], -1), (1, reps)), jnp.tile(jnp.concatenate([-s, s], -1), (1, reps))


def kernel(x, mem, positions, ffn1_norm, ffn1_w_gate_up, ffn1_w_down, attn_norm, mem_norm, a_w_in, idx_k_norm, b_w_in, w_mem_kv, w_out, ffn2_norm, ffn2_w_gate_up, ffn2_w_down, kv_norm, w_kv_shared, final_norm):
    b, s, d = x.shape
    assert b == 1 and mem.shape[0] == 1
    depth = ffn1_norm.shape[0]
    n_a = a_w_in.shape[0]
    main_w = N_MAIN_HEADS * HEAD_DIM
    kv_w = N_KV_HEADS * HEAD_DIM
    idx_w = IDX_HEADS * IDX_DIM
    mem_w = N_MEM_HEADS * HEAD_DIM
    topk = min(IDX_TOPK_MAX, s // 4)
    nb = s // MOBA_BLOCK
    n_sel = min(MOBA_TOPK_MAX, max(nb - 1, 1))

    cos, sin = _rope_tables(positions[0], HEAD_DIM)
    cosi, sini = _rope_tables(positions[0], IDX_DIM)
    tables = (cos, sin, cosi, sini)
    mem_tables = tuple(t[:mem.shape[1]] for t in tables)
    no_gk = jnp.zeros((1, LANES), F32)

    ffn_w = [w.astype(BF16) for w in (ffn1_w_gate_up, ffn1_w_down, ffn2_w_gate_up, ffn2_w_down)]
    w_mem_all = w_mem_kv.astype(BF16)
    w_out_all = w_out.astype(BF16)

    xs = x[0]
    mem2 = mem[0]
    k_sh = vtt_sh = kmeans = None
    for i in range(depth):
        if i == n_a:
            wk = w_kv_shared[:, :kv_w].astype(BF16)
            wv = w_kv_shared[:, kv_w:].astype(BF16)
            k_sh, vtt_sh, km = _kv_shared(xs, kv_norm, cos, sin, wk, wv)
            kmeans = km[:nb].reshape(nb, kv_w).astype(BF16)

        last = i == depth - 1
        xs = _ffn(xs, ffn1_norm[i], ffn_w[0], ffn_w[1], i)

        mk, mv = _proj(mem2, mem_norm[i], mem_tables, no_gk, w_mem_all,
                       [_Piece("plain", 0, mem_w, 0, mem_w, BF16),
                        _Piece("plain", mem_w, mem_w, 0, mem_w, BF16)], layer=i)
        if i < n_a:
            tail = main_w + 2 * kv_w + idx_w
            qm_lo = IDX_DIM + IDX_HEADS
            assert a_w_in.shape[2] == tail + qm_lo + mem_w and qm_lo <= LANES
            tail_w = -(-(qm_lo + mem_w) // LANES) * LANES
            wa = jnp.pad(a_w_in[i].astype(BF16), ((0, 0), (0, tail + tail_w - a_w_in.shape[2])))
            gk = jnp.pad(idx_k_norm[i], (0, LANES - IDX_DIM)).reshape(1, LANES)
            qt, k, vtt, qit, kiwi, qm = _proj(
                xs, attn_norm[i], tables, gk, wa,
                [_Piece("rope_scaled_t", 0, main_w, 0, main_w, BF16),
                 _Piece("rope", main_w, kv_w, 0, kv_w, BF16),
                 _Piece("value_chunks", main_w + kv_w, kv_w, 0, kv_w, BF16),
                 _Piece("rope_idx_t", main_w + 2 * kv_w, idx_w, 0, idx_w, BF16),
                 _Piece("kiwi", tail, tail_w, 0, LANES, F32),
                 _Piece("scaled", tail, tail_w, qm_lo, qm_lo + mem_w, BF16)])
            ki = kiwi[:, :IDX_DIM].astype(BF16)
            wt = kiwi[:, IDX_DIM:IDX_DIM + IDX_HEADS].T
            o_main = _dsa(qit, wt, ki, qt, k, vtt, topk)
        else:
            qt, qm = _proj(xs, attn_norm[i], tables, no_gk, b_w_in[i - n_a].astype(BF16),
                           [_Piece("rope_scaled_t", 0, main_w, 0, main_w, BF16),
                            _Piece("scaled", main_w, mem_w, 0, mem_w, BF16)])
            o_main = _moba(qt, k_sh, vtt_sh, kmeans, n_sel)
        o_mem = _mem_attn(qm, mk.T, mv)
        xs = _out_proj(xs, o_main, o_mem, w_out_all, layer=i)

        xs = _ffn(xs, ffn2_norm[i], ffn_w[2], ffn_w[3], i, final_norm if last else None)
    return xs[None]
```

```python
import functools
from typing import Any, NamedTuple

import jax
import jax.numpy as jnp
import numpy as np
from jax import lax
from jax.experimental import pallas as pl
from jax.experimental.pallas import tpu as pltpu

HEAD_DIM = 128
N_MAIN_HEADS = 12
N_KV_HEADS = 4
GROUP = N_MAIN_HEADS // N_KV_HEADS
N_MEM_HEADS = 4
IDX_HEADS = 16
IDX_DIM = 64
IDX_TOPK_MAX = 256
MOBA_BLOCK = 256
MOBA_TOPK_MAX = 3
ROPE_THETA = 10000.0
RMS_EPS = 1e-6

LANES = 128
SUBLANES = 8
VMEM_LIMIT = 56 * 1024 * 1024
NEG = -1e30
LOG2_E = 1.4426950408889634
INT_MIN = -2 ** 31
INT_MAX = 2 ** 31 - 1
CHUNK = 256
CHUNKS_PER_STEP = 4
CAND_LEVELS = 2
CAND_MAX = 16
MAX_PASSES = 40
COUNT_UNROLL = 4
BF16_TILE_ROWS = 2 * SUBLANES
ONES_ROWS = BF16_TILE_ROWS
V_ROWS = HEAD_DIM + ONES_ROWS

F32 = jnp.float32
BF16 = jnp.bfloat16


def _params(*sem):
    return pltpu.CompilerParams(dimension_semantics=sem, vmem_limit_bytes=VMEM_LIMIT)


def _rms(x, gain):
    return x * lax.rsqrt(jnp.mean(x * x, axis=-1, keepdims=True) + RMS_EPS) * gain


def _ffn_body(x_ref, g_ref, wg_ref, wu_ref, wd_ref, pg_ref, o_ref, h_ref, *, final_norm):
    j = pl.program_id(1)

    @pl.when(j == 0)
    def _():
        x = x_ref[...]
        h_ref[...] = _rms(x, g_ref[...]).astype(BF16)
        o_ref[...] = x

    h = h_ref[...]
    gate = jnp.dot(h, wg_ref[...], preferred_element_type=F32)
    up = jnp.dot(h, wu_ref[...], preferred_element_type=F32)
    act = (gate * (0.5 / (1.0 + jnp.exp(-gate))) * up).astype(BF16)
    o_ref[...] += jnp.dot(act, wd_ref[...], preferred_element_type=F32)

    if final_norm:
        @pl.when(j == pl.num_programs(1) - 1)
        def _():
            o_ref[...] = _rms(o_ref[...], pg_ref[...])


def _ffn(x, gain, w_gate_up, w_down, layer, post_gain=None, *, tm=512, tf=512):
    s, d = x.shape
    f = w_down.shape[1]
    tm = min(tm, s)
    tf = min(tf, f)
    assert s % tm == 0 and f % tf == 0
    nf = f // tf
    final_norm = post_gain is not None
    pg = post_gain if final_norm else gain
    return pl.pallas_call(
        functools.partial(_ffn_body, final_norm=final_norm),
        grid=(s // tm, nf),
        in_specs=[
            pl.BlockSpec((tm, d), lambda i, j: (i, 0)),
            pl.BlockSpec((1, d), lambda i, j: (0, 0)),
            pl.BlockSpec((None, d, tf), lambda i, j: (layer, 0, j)),
            pl.BlockSpec((None, d, tf), lambda i, j: (layer, 0, j + nf)),
            pl.BlockSpec((None, tf, d), lambda i, j: (layer, j, 0)),
            pl.BlockSpec((1, d), lambda i, j: (0, 0)),
        ],
        out_specs=pl.BlockSpec((tm, d), lambda i, j: (i, 0)),
        out_shape=jax.ShapeDtypeStruct((s, d), F32),
        scratch_shapes=[pltpu.VMEM((tm, d), BF16)],
        compiler_params=_params("parallel", "arbitrary"),
        name="ffn",
    )(x, gain.reshape(1, d), w_gate_up, w_gate_up, w_down, pg.reshape(1, d))


def _rope_heads(y, cos, sin, o_ref, scale, transposed=False):
    for h in range(y.shape[1] // HEAD_DIM):
        t = y[:, h * HEAD_DIM:(h + 1) * HEAD_DIM]
        r = t * cos + pltpu.roll(t, HEAD_DIM // 2, 1) * sin
        if scale != 1.0:
            r = r * scale
        if transposed:
            o_ref[h * HEAD_DIM:(h + 1) * HEAD_DIM, :] = r.T.astype(o_ref.dtype)
        else:
            o_ref[:, h * HEAD_DIM:(h + 1) * HEAD_DIM] = r.astype(o_ref.dtype)


def _store_value_chunks(y, o_ref):
    for b in range(y.shape[0] // CHUNK):
        for g in range(N_KV_HEADS):
            blk = y[b * CHUNK:(b + 1) * CHUNK, g * HEAD_DIM:(g + 1) * HEAD_DIM]
            o_ref[b, g * V_ROWS:g * V_ROWS + HEAD_DIM, :] = blk.T.astype(o_ref.dtype)
            o_ref[b, g * V_ROWS + HEAD_DIM:(g + 1) * V_ROWS, :] = jnp.ones((ONES_ROWS, CHUNK), o_ref.dtype)


def _rot_idx(t):
    lane = lax.broadcasted_iota(jnp.int32, t.shape, 1)
    first_half = (lane & (IDX_DIM // 2)) == 0
    return jnp.where(first_half, pltpu.roll(t, LANES - IDX_DIM // 2, 1),
                     pltpu.roll(t, IDX_DIM // 2, 1))


class _Piece(NamedTuple):
    kind: str
    start: int
    width: int
    lo: int
    hi: int
    dtype: Any


def _proj_body(*refs, pieces, q_scale, idx_w_scale):
    x_ref, g_ref, cos_ref, sin_ref, cosi_ref, sini_ref, gk_ref, w_ref = refs[:8]
    o_refs = refs[8:]
    h = _rms(x_ref[...], g_ref[...]).astype(BF16)
    products = {}
    for piece, o_ref in zip(pieces, o_refs):
        cols = (piece.start, piece.width)
        if cols not in products:
            products[cols] = jnp.dot(h, w_ref[:, piece.start:piece.start + piece.width],
                                     preferred_element_type=F32)
        y = products[cols]
        if (piece.lo, piece.hi) != (0, piece.width):
            y = y[:, piece.lo:piece.hi]
        kind = piece.kind
        if kind == "plain":
            o_ref[...] = y.astype(o_ref.dtype)
        elif kind == "scaled":
            o_ref[...] = (y * q_scale).astype(o_ref.dtype)
        elif kind == "rope":
            _rope_heads(y, cos_ref[...], sin_ref[...], o_ref, 1.0)
        elif kind == "rope_scaled_t":
            _rope_heads(y, cos_ref[...], sin_ref[...], o_ref, q_scale * LOG2_E, transposed=True)
        elif kind == "rope_idx_t":
            cosi, sini = cosi_ref[...], sini_ref[...]
            for c in range(y.shape[1] // LANES):
                t = y[:, c * LANES:(c + 1) * LANES]
                o_ref[c * LANES:(c + 1) * LANES, :] = (t * cosi + _rot_idx(t) * sini).T.astype(o_ref.dtype)
        elif kind == "value_chunks":
            _store_value_chunks(y, o_ref)
        elif kind == "kiwi":
            lane = lax.broadcasted_iota(jnp.int32, y.shape, 1)
            is_k = lane < IDX_DIM
            kk = jnp.where(is_k, y, 0.0)
            ms = jnp.sum(kk * kk, axis=-1, keepdims=True) * (1.0 / IDX_DIM)
            kn = kk * lax.rsqrt(ms + RMS_EPS) * gk_ref[...]
            kr = kn * cosi_ref[...] + _rot_idx(kn) * sini_ref[...]
            o_ref[...] = jnp.where(is_k, kr, y * idx_w_scale)
        else:
            raise ValueError(kind)


def _layer_weight_spec(w, layer):
    if w.ndim == 2:
        return pl.BlockSpec(w.shape, lambda i: (0, 0), pipeline_mode=pl.Buffered(1))
    return pl.BlockSpec((None,) + w.shape[1:], lambda i: (layer, 0, 0), pipeline_mode=pl.Buffered(1))


def _proj(x, gain, tables, gk, w, pieces, *, layer=0, tm=256):
    s, d = x.shape
    tm = min(tm, s)
    assert s % tm == 0
    cos, sin, cosi, sini = tables
    row = lambda i: (i, 0)
    const = lambda i: (0, 0)
    in_specs = [pl.BlockSpec((tm, d), row), pl.BlockSpec((1, d), const)]
    in_specs += [pl.BlockSpec((tm, LANES), row)] * 4
    in_specs += [pl.BlockSpec((1, LANES), const)]
    in_specs += [_layer_weight_spec(w, layer)]
    out_specs, out_shape = [], []
    for p in pieces:
        assert p.start % LANES == 0 and p.width % LANES == 0 and p.start + p.width <= w.shape[-1]
        n = p.hi - p.lo
        if p.kind.endswith("_t"):
            out_specs.append(pl.BlockSpec((n, tm), lambda i: (0, i)))
            out_shape.append(jax.ShapeDtypeStruct((n, s), p.dtype))
        elif p.kind == "value_chunks":
            assert tm % CHUNK == 0 and n == N_KV_HEADS * HEAD_DIM
            out_specs.append(pl.BlockSpec((tm // CHUNK, N_KV_HEADS * V_ROWS, CHUNK), lambda i: (i, 0, 0)))
            out_shape.append(jax.ShapeDtypeStruct((s // CHUNK, N_KV_HEADS * V_ROWS, CHUNK), p.dtype))
        else:
            out_specs.append(pl.BlockSpec((tm, n), row))
            out_shape.append(jax.ShapeDtypeStruct((s, n), p.dtype))
    return pl.pallas_call(
        functools.partial(_proj_body, pieces=tuple(pieces), q_scale=HEAD_DIM ** -0.5,
                          idx_w_scale=(IDX_HEADS ** -0.5) * (IDX_DIM ** -0.5)),
        grid=(s // tm,),
        in_specs=in_specs,
        out_specs=out_specs,
        out_shape=out_shape,
        compiler_params=_params("parallel"),
        name="proj",
    )(x, gain.reshape(1, d), cos, sin, cosi, sini, gk, w)


def _kv_shared_body(x_ref, g_ref, cos_ref, sin_ref, wk_ref, wv_ref, k_ref, v_ref, km_ref):
    h = _rms(x_ref[...], g_ref[...]).astype(BF16)
    yk = jnp.dot(h, wk_ref[...], preferred_element_type=F32)
    cos, sin = cos_ref[...], sin_ref[...]
    tm = yk.shape[0]
    first_blk = pl.program_id(0) * (tm // MOBA_BLOCK)
    lane = lax.broadcasted_iota(jnp.int32, (MOBA_BLOCK, HEAD_DIM), 1)
    for hd in range(yk.shape[1] // HEAD_DIM):
        t = yk[:, hd * HEAD_DIM:(hd + 1) * HEAD_DIM]
        r = t * cos + pltpu.roll(t, HEAD_DIM // 2, 1) * sin
        k_ref[:, 2 * hd * HEAD_DIM:(2 * hd + 1) * HEAD_DIM] = r.astype(k_ref.dtype)
        for b in range(tm // MOBA_BLOCK):
            rows = slice(b * MOBA_BLOCK, (b + 1) * MOBA_BLOCK)
            k_ref[rows, (2 * hd + 1) * HEAD_DIM:(2 * hd + 2) * HEAD_DIM] = (
                jnp.where(lane == first_blk + b, 1.0, 0.0).astype(k_ref.dtype))
            km_ref[b, :, hd * HEAD_DIM:(hd + 1) * HEAD_DIM] = (
                jnp.sum(r[rows], axis=0, keepdims=True) * (1.0 / MOBA_BLOCK))
    _store_value_chunks(jnp.dot(h, wv_ref[...], preferred_element_type=F32), v_ref)


def _kv_shared(x, gain, cos, sin, wk, wv):
    s, d = x.shape
    tm = MOBA_BLOCK
    assert s % tm == 0 and CHUNK == MOBA_BLOCK
    n = s // tm
    nkv = wk.shape[1]
    src = lambda i: (jnp.minimum(i, n - 1), 0)
    row = lambda i: (i, 0)
    const = lambda i: (0, 0)
    return pl.pallas_call(
        _kv_shared_body,
        grid=(n + 1,),
        in_specs=[pl.BlockSpec((tm, d), src), pl.BlockSpec((1, d), const),
                  pl.BlockSpec((tm, LANES), src), pl.BlockSpec((tm, LANES), src),
                  pl.BlockSpec(wk.shape, const, pipeline_mode=pl.Buffered(1)),
                  pl.BlockSpec(wv.shape, const, pipeline_mode=pl.Buffered(1))],
        out_specs=[pl.BlockSpec((tm, 2 * nkv), row),
                   pl.BlockSpec((1, N_KV_HEADS * V_ROWS, CHUNK), lambda i: (i, 0, 0)),
                   pl.BlockSpec((1, 1, nkv), lambda i: (i, 0, 0))],
        out_shape=[jax.ShapeDtypeStruct((s + tm, 2 * nkv), BF16),
                   jax.ShapeDtypeStruct((n + 1, N_KV_HEADS * V_ROWS, CHUNK), BF16),
                   jax.ShapeDtypeStruct((n + 1, 1, nkv), F32)],
        compiler_params=_params("parallel"),
        name="kv_shared",
    )(x, gain.reshape(1, d), cos, sin, wk, wv)


def _load_q_group(qt_ref, qall_ref):
    tq = qt_ref.shape[1]
    for hh in range(GROUP):
        qall_ref[:, hh * tq:(hh + 1) * tq] = qt_ref[hh * HEAD_DIM:(hh + 1) * HEAD_DIM, :]


def _softmax_init(m_ref, acc_ref):
    m_ref[...] = jnp.full(m_ref.shape, NEG, F32)
    acc_ref[...] = jnp.zeros(acc_ref.shape, F32)


def _add_group_bias(s, b):
    tq = b.shape[1]
    return jnp.concatenate([s[:, hh * tq:(hh + 1) * tq] + b for hh in range(GROUP)], axis=1)


def _logits_pass(chunk_ids, logits_fn, s_ref, cmax_ref):
    cmax = None
    for j, c in enumerate(chunk_ids):
        s = logits_fn(c)
        s_ref[j * CHUNK:(j + 1) * CHUNK, :] = s
        cm = jnp.max(s, axis=0, keepdims=True)
        cmax = cm if cmax is None else jnp.maximum(cmax, cm)
    cmax_ref[...] = cmax


def _softmax_pass(chunk_ids, vt_ref, s_ref, cmax_ref, m_ref, acc_ref):
    m_old = m_ref[...]
    m_new = jnp.maximum(m_old, cmax_ref[...])
    alpha = jnp.exp2(m_old - m_new)
    pv = None
    for j, c in enumerate(chunk_ids):
        p = jnp.exp2(s_ref[j * CHUNK:(j + 1) * CHUNK, :] - m_new)
        d = jnp.dot(vt_ref[c], p.astype(BF16), preferred_element_type=F32)
        pv = d if pv is None else pv + d
    m_ref[...] = m_new
    acc_ref[...] = alpha * acc_ref[...] + pv


def _attend_range(n_chunks, logits_fn, vt_ref, s_refs, cmax_refs, m_ref, acc_ref, *, fuse_pairs):
    n_steps = pl.cdiv(n_chunks, CHUNKS_PER_STEP)
    last_chunk = vt_ref.shape[0] - 1

    def chunks(step):
        return [step * CHUNKS_PER_STEP + j for j in range(CHUNKS_PER_STEP)]

    def logits(step, slot):
        _logits_pass(chunks(step), logits_fn, s_refs[slot], cmax_refs[slot])

    def softmax(step, slot):
        ids = [jnp.minimum(c, last_chunk) for c in chunks(step)]
        _softmax_pass(ids, vt_ref, s_refs[slot], cmax_refs[slot], m_ref, acc_ref)

    @pl.when(n_steps > 0)
    def _():
        logits(0, 0)

    def pair(u, carry):
        t = 2 * u
        if fuse_pairs:
            @pl.when(t + 1 < n_steps)
            def _():
                logits(t + 1, 1)
                softmax(t, 0)
                logits(t + 2, 0)
                softmax(t + 1, 1)

            @pl.when(t + 1 >= n_steps)
            def _():
                softmax(t, 0)
        else:
            logits(t + 1, 1)
            softmax(t, 0)

            @pl.when(t + 1 < n_steps)
            def _():
                logits(t + 2, 0)
                softmax(t + 1, 1)

        return carry

    lax.fori_loop(0, pl.cdiv(n_steps, 2), pair, 0)


def _softmax_finish(o_ref, acc_ref):
    tq = o_ref.shape[0]
    out = acc_ref[:HEAD_DIM, :] / acc_ref[HEAD_DIM:HEAD_DIM + 1, :]
    for hh in range(GROUP):
        o_ref[:, hh * HEAD_DIM:(hh + 1) * HEAD_DIM] = out[:, hh * tq:(hh + 1) * tq].T.astype(o_ref.dtype)


def _dsa_body(qit_ref, wt_ref, ki_ref, qt_ref, k_ref, vt_ref, o_ref,
              keys_ref, gmax_ref, cand_ref, thr_ref, qall_ref, sa_ref, sb_ref, cma_ref, cmb_ref,
              m_ref, acc_ref, *, topk):
    i = pl.program_id(0)
    g = pl.program_id(1)
    tq = CHUNK
    n_chunks = i + 1
    sub = 128

    def causal(c, rows, row_off=0):
        kpos = c * CHUNK + row_off + lax.broadcasted_iota(jnp.int32, (rows, tq), 0)
        qpos = i * tq + lax.broadcasted_iota(jnp.int32, (rows, tq), 1)
        return kpos <= qpos

    @pl.when(g == 0)
    def _():
        gmax_ref[...] = jnp.full(gmax_ref.shape, INT_MIN, jnp.int32)

        def score_chunk(c, carry):
            for part in range(CHUNK // sub):
                r0 = pl.multiple_of(c * CHUNK + part * sub, sub)
                kit = ki_ref[pl.ds(r0, sub), :]
                sc = jnp.zeros((sub, tq), F32)
                for h in range(IDX_HEADS):
                    d = jnp.dot(kit, qit_ref[h * IDX_DIM:(h + 1) * IDX_DIM, :], preferred_element_type=F32)
                    sc = sc + wt_ref[h:h + 1, :] * jnp.maximum(d, 0.0)
                sc = jnp.where(sc == 0.0, 0.0, sc)
                sc = jnp.where(causal(c, sub, part * sub), sc, -jnp.inf)
                bits = pltpu.bitcast(sc, jnp.int32)
                key = bits ^ ((bits >> 31) & INT_MAX)
                keys_ref[pl.ds(r0, sub), :] = key
                gsl = slice(part * sub, (part + 1) * sub)
                gmax_ref[gsl, :] = jnp.maximum(gmax_ref[gsl, :], key)
            return carry

        def score_pair(u, carry):
            score_chunk(2 * u, carry)
            return score_chunk(2 * u + 1, carry)

        lax.fori_loop(0, n_chunks // 2, score_pair, 0)

        @pl.when(n_chunks % 2 == 1)
        def _():
            score_chunk(n_chunks - 1, 0)

        def count_rows(load, n_blocks, t):
            def count_block(b, acc):
                ge = jnp.where(load(b) >= t, 1, 0).astype(jnp.int32)
                return acc + jnp.sum(ge.reshape(CHUNK // SUBLANES, SUBLANES, tq), axis=0)

            part = jnp.zeros((SUBLANES, tq), jnp.int32)
            if isinstance(n_blocks, int):
                for b in range(n_blocks):
                    part = count_block(b, part)
            else:
                def count_group(u, acc):
                    for j in range(COUNT_UNROLL):
                        acc = count_block(u * COUNT_UNROLL + j, acc)
                    return acc

                n_groups = n_blocks // COUNT_UNROLL
                part = lax.fori_loop(0, n_groups, count_group, part)
                part = lax.fori_loop(n_groups * COUNT_UNROLL, n_blocks, count_block, part)
            return jnp.sum(part, axis=0, keepdims=True)

        def key_chunk(c):
            return keys_ref[pl.ds(pl.multiple_of(c * CHUNK, CHUNK), CHUNK), :]

        def count_all(t):
            return count_rows(key_chunk, n_chunks, t)

        def count_cand(t):
            return count_rows(lambda b: cand_ref[b], CAND_LEVELS, t)

        def any_set(flags):
            return jnp.max(flags) > 0.0

        def bisect_pass(count_fn, base, live, state):
            lo, hi, c_lo, c_hi, done = state
            mid = (lo >> 1) + (hi >> 1) + (lo & hi & 1)
            cnt = count_fn(mid) + base
            up = live & (cnt >= topk)
            down = live & (cnt < topk)
            lo, c_lo = jnp.where(up, mid, lo), jnp.where(up, cnt, c_lo)
            hi, c_hi = jnp.where(down, mid, hi), jnp.where(down, cnt, c_hi)
            finished = live & ((cnt == topk) | (hi <= lo + 1))
            return lo, hi, c_lo, c_hi, jnp.where(finished, 1.0, done)

        def bisect_while(count_fn, base, live_fn, keep_going, state):
            def cond(carry):
                it, state = carry
                return jnp.logical_and(it < MAX_PASSES, keep_going(state))

            def body(carry):
                it, state = carry
                return it + 1, bisect_pass(count_fn, base, live_fn(state), state)

            return lax.while_loop(cond, body, (jnp.int32(0), state))[1]

        def in_bracket(state):
            return state[2] - state[3]

        def active(state):
            return state[4] == 0.0

        gm = gmax_ref[...]
        lo = jnp.min(gm, axis=0, keepdims=True)
        top = jnp.max(gm, axis=0, keepdims=True)
        hi = jnp.where(top == INT_MAX, top, top + 1)
        c_lo = count_all(lo)
        state = (lo, hi, c_lo, jnp.zeros_like(c_lo), (c_lo == topk).astype(F32))

        def wide(state):
            return any_set(jnp.where(active(state) & (in_bracket(state) > CAND_MAX), 1.0, 0.0))

        state = bisect_while(count_all, 0, active, wide, state)

        lo, hi, c_lo, c_hi, _ = state
        cand_ref[...] = jnp.full(cand_ref.shape, INT_MIN, jnp.int32)

        def capture(c, carry):
            blk = key_chunk(c)
            x = jnp.where((blk >= lo) & (blk < hi), blk, INT_MIN)
            for level in range(CAND_LEVELS):
                held = cand_ref[level]
                cand_ref[level] = jnp.maximum(held, x)
                x = jnp.minimum(held, x)
            return carry

        lax.fori_loop(0, n_chunks, capture, 0)
        captured = count_cand(lo) == in_bracket(state)

        def live_captured(state):
            return active(state) & captured

        state = bisect_while(count_cand, c_hi, live_captured,
                             lambda st: any_set(jnp.where(live_captured(st), 1.0, 0.0)), state)

        state = bisect_while(count_all, 0, active,
                             lambda st: any_set(jnp.where(active(st), 1.0, 0.0)), state)
        thr = state[0]
        thr_ref[...] = thr

        surplus = state[2] - topk

        @pl.when(jnp.max(surplus.astype(F32)) > 0.0)
        def _():
            row = lax.broadcasted_iota(jnp.int32, (CHUNK, CHUNK), 0)
            col = lax.broadcasted_iota(jnp.int32, (CHUNK, CHUNK), 1)
            prefix_ones = jnp.where(row >= col, 1.0, 0.0).astype(BF16)

            def count_equal(c, acc):
                return acc + jnp.sum(jnp.where(key_chunk(c) == thr, 1.0, 0.0), axis=0, keepdims=True)

            n_equal = lax.fori_loop(0, n_chunks, count_equal, jnp.zeros((1, tq), F32))
            keep = n_equal - surplus.astype(F32)

            def drop_late_ties(c, seen):
                blk = key_chunk(c)
                tie = blk == thr
                tie_f = jnp.where(tie, 1.0, 0.0)
                rank = seen + jnp.dot(prefix_ones, tie_f.astype(BF16), preferred_element_type=F32)
                r0 = pl.multiple_of(c * CHUNK, CHUNK)
                keys_ref[pl.ds(r0, CHUNK), :] = jnp.where(tie & (rank > keep), blk - 1, blk)
                return seen + jnp.sum(tie_f, axis=0, keepdims=True)

            lax.fori_loop(0, n_chunks, drop_late_ties, jnp.zeros((1, tq), F32))

    _load_q_group(qt_ref, qall_ref)
    _softmax_init(m_ref, acc_ref)
    thr = thr_ref[...]

    def logits(c):
        r0 = pl.multiple_of(jnp.minimum(c, n_chunks - 1) * CHUNK, CHUNK)
        sel = (keys_ref[pl.ds(r0, CHUNK), :] >= thr) & causal(c, CHUNK)
        s = jnp.dot(k_ref[pl.ds(r0, CHUNK), :], qall_ref[...], preferred_element_type=F32)
        return _add_group_bias(s, jnp.where(sel, 0.0, NEG))

    _attend_range(n_chunks, logits, vt_ref, (sa_ref, sb_ref), (cma_ref, cmb_ref), m_ref, acc_ref,
                  fuse_pairs=True)
    _softmax_finish(o_ref, acc_ref)


def _dsa(qit, wt, ki, qt, k, vtt, topk):
    s = ki.shape[0]
    tq = CHUNK
    assert s % (tq * CHUNKS_PER_STEP) == 0 and topk <= CHUNK
    gw = GROUP * HEAD_DIM
    return pl.pallas_call(
        functools.partial(_dsa_body, topk=topk),
        grid=(s // tq, N_KV_HEADS),
        in_specs=[
            pl.BlockSpec((IDX_HEADS * IDX_DIM, tq), lambda i, g: (0, i)),
            pl.BlockSpec((IDX_HEADS, tq), lambda i, g: (0, i)),
            pl.BlockSpec((s, IDX_DIM), lambda i, g: (0, 0), pipeline_mode=pl.Buffered(1)),
            pl.BlockSpec((gw, tq), lambda i, g: (g, i)),
            pl.BlockSpec((s, HEAD_DIM), lambda i, g: (0, g)),
            pl.BlockSpec((s // tq, V_ROWS, tq), lambda i, g: (0, g, 0)),
        ],
        out_specs=pl.BlockSpec((tq, gw), lambda i, g: (i, g)),
        out_shape=jax.ShapeDtypeStruct((s, N_MAIN_HEADS * HEAD_DIM), BF16),
        scratch_shapes=[
            pltpu.VMEM((s, tq), jnp.int32),
            pltpu.VMEM((CHUNK, tq), jnp.int32),
            pltpu.VMEM((CAND_LEVELS, CHUNK, tq), jnp.int32),
            pltpu.VMEM((1, tq), jnp.int32),
            pltpu.VMEM((HEAD_DIM, GROUP * tq), BF16),
            pltpu.VMEM((CHUNKS_PER_STEP * CHUNK, GROUP * tq), F32),
            pltpu.VMEM((CHUNKS_PER_STEP * CHUNK, GROUP * tq), F32),
            pltpu.VMEM((1, GROUP * tq), F32),
            pltpu.VMEM((1, GROUP * tq), F32),
            pltpu.VMEM((1, GROUP * tq), F32),
            pltpu.VMEM((V_ROWS, GROUP * tq), F32),
        ],
        compiler_params=_params("arbitrary", "arbitrary"),
        name="dsa",
    )(qit, wt, ki, qt, k, vtt)


def _moba_body(qt_ref, k_ref, vt_ref, km_ref, o_ref, qaug_ref, sa_ref, sb_ref, cma_ref, cmb_ref,
               m_ref, acc_ref, *, n_sel):
    cur = pl.program_id(0)
    tq = CHUNK
    nb = km_ref.shape[0]
    width = GROUP * tq
    for hh in range(GROUP):
        qaug_ref[:HEAD_DIM, hh * tq:(hh + 1) * tq] = qt_ref[hh * HEAD_DIM:(hh + 1) * HEAD_DIM, :]
    q_all = qaug_ref[:HEAD_DIM, :]

    blk_id = lax.broadcasted_iota(jnp.int32, (nb, width), 0)
    past = blk_id < cur
    gate = jnp.dot(km_ref[...], q_all, preferred_element_type=F32)
    gate = jnp.where(past, gate, -jnp.inf)
    chosen = jnp.zeros((nb, width), jnp.bool_)
    for _ in range(n_sel):
        best = jnp.max(gate, axis=0, keepdims=True)
        first = jnp.min(jnp.where(gate == best, blk_id, nb), axis=0, keepdims=True)
        pick = blk_id == first
        chosen = chosen | pick
        gate = jnp.where(pick, -jnp.inf, gate)
    qaug_ref[HEAD_DIM:HEAD_DIM + nb, :] = jnp.where(chosen & past, 0.0, NEG).astype(BF16)
    pad = BF16_TILE_ROWS
    qaug_ref[HEAD_DIM + nb:HEAD_DIM + nb + pad, :] = jnp.full((pad, width), NEG, BF16)
    if nb + pad < HEAD_DIM:
        qaug_ref[HEAD_DIM + nb + pad:, :] = jnp.zeros((HEAD_DIM - nb - pad, width), BF16)

    _softmax_init(m_ref, acc_ref)
    state = (vt_ref, (sa_ref, sb_ref), (cma_ref, cmb_ref), m_ref, acc_ref)

    def past_logits(c):
        r0 = pl.multiple_of(jnp.where(c < cur, c, nb) * CHUNK, CHUNK)
        return jnp.dot(k_ref[pl.ds(r0, CHUNK), :], qaug_ref[...], preferred_element_type=F32)

    _attend_range(cur, past_logits, *state, fuse_pairs=False)

    tri = (lax.broadcasted_iota(jnp.int32, (tq, tq), 0) <= lax.broadcasted_iota(jnp.int32, (tq, tq), 1))

    def own_logits(c):
        r0 = pl.multiple_of(c * CHUNK, CHUNK)
        s = jnp.dot(k_ref[pl.ds(r0, CHUNK), :HEAD_DIM], q_all, preferred_element_type=F32)
        return _add_group_bias(s, jnp.where(tri, 0.0, NEG))

    _logits_pass([cur], own_logits, sa_ref, cma_ref)
    _softmax_pass([cur], vt_ref, sa_ref, cma_ref, m_ref, acc_ref)
    _softmax_finish(o_ref, acc_ref)


def _moba(qt, k_aug, vtt, kmeans, n_sel):
    s = k_aug.shape[0] - CHUNK
    assert MOBA_BLOCK == CHUNK and s % (CHUNK * CHUNKS_PER_STEP) == 0
    tq = CHUNK
    nb = s // tq
    assert nb + BF16_TILE_ROWS <= HEAD_DIM and vtt.shape[0] == nb + 1
    gw = GROUP * HEAD_DIM
    return pl.pallas_call(
        functools.partial(_moba_body, n_sel=n_sel),
        grid=(nb, N_KV_HEADS),
        in_specs=[
            pl.BlockSpec((gw, tq), lambda i, g: (g, i)),
            pl.BlockSpec((s + CHUNK, 2 * HEAD_DIM), lambda i, g: (0, g)),
            pl.BlockSpec((nb + 1, V_ROWS, tq), lambda i, g: (0, g, 0)),
            pl.BlockSpec((nb, HEAD_DIM), lambda i, g: (0, g)),
        ],
        out_specs=pl.BlockSpec((tq, gw), lambda i, g: (i, g)),
        out_shape=jax.ShapeDtypeStruct((s, N_MAIN_HEADS * HEAD_DIM), BF16),
        scratch_shapes=[
            pltpu.VMEM((2 * HEAD_DIM, GROUP * tq), BF16),
            pltpu.VMEM((CHUNKS_PER_STEP * CHUNK, GROUP * tq), F32),
            pltpu.VMEM((CHUNKS_PER_STEP * CHUNK, GROUP * tq), F32),
            pltpu.VMEM((1, GROUP * tq), F32),
            pltpu.VMEM((1, GROUP * tq), F32),
            pltpu.VMEM((1, GROUP * tq), F32),
            pltpu.VMEM((V_ROWS, GROUP * tq), F32),
        ],
        compiler_params=_params("parallel", "arbitrary"),
        name="moba",
    )(qt, k_aug, vtt, kmeans)


def _mem_attn_body(q_ref, kt_ref, v_ref, o_ref):
    for h in range(N_MEM_HEADS):
        sl = slice(h * HEAD_DIM, (h + 1) * HEAD_DIM)
        s = jnp.dot(q_ref[:, sl], kt_ref[sl, :], preferred_element_type=F32)
        p = jnp.exp(s - jnp.max(s, axis=-1, keepdims=True))
        o = jnp.dot(p.astype(BF16), v_ref[:, sl], preferred_element_type=F32)
        o_ref[:, sl] = (o / jnp.sum(p, axis=-1, keepdims=True)).astype(o_ref.dtype)


def _mem_attn(qm, kmt, vm, *, tq=512):
    s, w = qm.shape
    m = vm.shape[0]
    tq = min(tq, s)
    assert s % tq == 0
    return pl.pallas_call(
        _mem_attn_body,
        grid=(s // tq,),
        in_specs=[pl.BlockSpec((tq, w), lambda i: (i, 0)),
                  pl.BlockSpec((w, m), lambda i: (0, 0)),
                  pl.BlockSpec((m, w), lambda i: (0, 0))],
        out_specs=pl.BlockSpec((tq, w), lambda i: (i, 0)),
        out_shape=jax.ShapeDtypeStruct((s, w), BF16),
        compiler_params=_params("parallel"),
        name="mem_attn",
    )(qm, kmt, vm)


def _out_proj_body(x_ref, om_ref, oq_ref, w_ref, o_ref):
    n_main = om_ref.shape[1]
    o_ref[...] = (x_ref[...]
                  + jnp.dot(om_ref[...], w_ref[:n_main, :], preferred_element_type=F32)
                  + jnp.dot(oq_ref[...], w_ref[n_main:, :], preferred_element_type=F32))


def _out_proj(x, o_main, o_mem, w, *, layer=0, tm=512):
    s, d = x.shape
    tm = min(tm, s)
    assert s % tm == 0 and w.shape[-2] == o_main.shape[1] + o_mem.shape[1]
    row = lambda i: (i, 0)
    const = lambda i: (0, 0)
    return pl.pallas_call(
        _out_proj_body,
        grid=(s // tm,),
        in_specs=[pl.BlockSpec((tm, d), row),
                  pl.BlockSpec((tm, o_main.shape[1]), row),
                  pl.BlockSpec((tm, o_mem.shape[1]), row),
                  _layer_weight_spec(w, layer)],
        out_specs=pl.BlockSpec((tm, d), row),
        out_shape=jax.ShapeDtypeStruct((s, d), F32),
        compiler_params=_params("parallel"),
        name="out_proj",
    )(x, o_main, o_mem, w)


def _rope_tables(positions, dim):
    inv = 1.0 / (ROPE_THETA ** (jnp.arange(0, dim, 2, dtype=F32) / dim))
    ang = positions.astype(F32)[:, None] * inv
    c, s = jnp.cos(ang), jnp.sin(ang)
    reps = LANES // dim
    return jnp.tile(jnp.concatenate([c, c], -1), (1, reps)), jnp.tile(jnp.concatenate([-s, s], -1), (1, reps))


def kernel(x, mem, positions, ffn1_norm, ffn1_w_gate_up, ffn1_w_down, attn_norm, mem_norm, a_w_in, idx_k_norm, b_w_in, w_mem_kv, w_out, ffn2_norm, ffn2_w_gate_up, ffn2_w_down, kv_norm, w_kv_shared, final_norm):
    b, s, d = x.shape
    assert b == 1 and mem.shape[0] == 1
    depth = ffn1_norm.shape[0]
    n_a = a_w_in.shape[0]
    main_w = N_MAIN_HEADS * HEAD_DIM
    kv_w = N_KV_HEADS * HEAD_DIM
    idx_w = IDX_HEADS * IDX_DIM
    mem_w = N_MEM_HEADS * HEAD_DIM
    topk = min(IDX_TOPK_MAX, s // 4)
    nb = s // MOBA_BLOCK
    n_sel = min(MOBA_TOPK_MAX, max(nb - 1, 1))

    cos, sin = _rope_tables(positions[0], HEAD_DIM)
    cosi, sini = _rope_tables(positions[0], IDX_DIM)
    tables = (cos, sin, cosi, sini)
    mem_tables = tuple(t[:mem.shape[1]] for t in tables)
    no_gk = jnp.zeros((1, LANES), F32)

    ffn_w = [w.astype(BF16) for w in (ffn1_w_gate_up, ffn1_w_down, ffn2_w_gate_up, ffn2_w_down)]
    w_mem_all = w_mem_kv.astype(BF16)
    w_out_all = w_out.astype(BF16)

    xs = x[0]
    mem2 = mem[0]
    k_sh = vtt_sh = kmeans = None
    for i in range(depth):
        if i == n_a:
            wk = w_kv_shared[:, :kv_w].astype(BF16)
            wv = w_kv_shared[:, kv_w:].astype(BF16)
            k_sh, vtt_sh, km = _kv_shared(xs, kv_norm, cos, sin, wk, wv)
            kmeans = km[:nb].reshape(nb, kv_w).astype(BF16)

        last = i == depth - 1
        xs = _ffn(xs, ffn1_norm[i], ffn_w[0], ffn_w[1], i)

        mk, mv = _proj(mem2, mem_norm[i], mem_tables, no_gk, w_mem_all,
                       [_Piece("plain", 0, mem_w, 0, mem_w, BF16),
                        _Piece("plain", mem_w, mem_w, 0, mem_w, BF16)], layer=i)
        if i < n_a:
            tail = main_w + 2 * kv_w + idx_w
            qm_lo = IDX_DIM + IDX_HEADS
            assert a_w_in.shape[2] == tail + qm_lo + mem_w and qm_lo <= LANES
            tail_w = -(-(qm_lo + mem_w) // LANES) * LANES
            wa = jnp.pad(a_w_in[i].astype(BF16), ((0, 0), (0, tail + tail_w - a_w_in.shape[2])))
            gk = jnp.pad(idx_k_norm[i], (0, LANES - IDX_DIM)).reshape(1, LANES)
            qt, k, vtt, qit, kiwi, qm = _proj(
                xs, attn_norm[i], tables, gk, wa,
                [_Piece("rope_scaled_t", 0, main_w, 0, main_w, BF16),
                 _Piece("rope", main_w, kv_w, 0, kv_w, BF16),
                 _Piece("value_chunks", main_w + kv_w, kv_w, 0, kv_w, BF16),
                 _Piece("rope_idx_t", main_w + 2 * kv_w, idx_w, 0, idx_w, BF16),
                 _Piece("kiwi", tail, tail_w, 0, LANES, F32),
                 _Piece("scaled", tail, tail_w, qm_lo, qm_lo + mem_w, BF16)])
            ki = kiwi[:, :IDX_DIM].astype(BF16)
            wt = kiwi[:, IDX_DIM:IDX_DIM + IDX_HEADS].T
            o_main = _dsa(qit, wt, ki, qt, k, vtt, topk)
        else:
            qt, qm = _proj(xs, attn_norm[i], tables, no_gk, b_w_in[i - n_a].astype(BF16),
                           [_Piece("rope_scaled_t", 0, main_w, 0, main_w, BF16),
                            _Piece("scaled", main_w, mem_w, 0, mem_w, BF16)])
            o_main = _moba(qt, k_sh, vtt_sh, kmeans, n_sel)
        o_mem = _mem_attn(qm, mk.T, mv)
        xs = _out_proj(xs, o_main, o_mem, w_out_all, layer=i)

        xs = _ffn(xs, ffn2_norm[i], ffn_w[2], ffn_w[3], i, final_norm if last else None)
    return xs[None]
```

```python
import functools
from typing import Any, NamedTuple

import jax
import jax.numpy as jnp
import numpy as np
from jax import lax
from jax.experimental import pallas as pl
from jax.experimental.pallas import tpu as pltpu

HEAD_DIM = 128
N_MAIN_HEADS = 12
N_KV_HEADS = 4
GROUP = N_MAIN_HEADS // N_KV_HEADS
N_MEM_HEADS = 4
IDX_HEADS = 16
IDX_DIM = 64
IDX_TOPK_MAX = 256
MOBA_BLOCK = 256
MOBA_TOPK_MAX = 3
ROPE_THETA = 10000.0
RMS_EPS = 1e-6

LANES = 128
SUBLANES = 8
VMEM_LIMIT = 56 * 1024 * 1024
NEG = -1e30
LOG2_E = 1.4426950408889634
INT_MIN = -2 ** 31
INT_MAX = 2 ** 31 - 1
CHUNK = 256
CHUNKS_PER_STEP = 4
CAND_LEVELS = 2
CAND_MAX = 16
MAX_PASSES = 40
COUNT_UNROLL = 4
BF16_TILE_ROWS = 2 * SUBLANES
ONES_ROWS = BF16_TILE_ROWS
V_ROWS = HEAD_DIM + ONES_ROWS

F32 = jnp.float32
BF16 = jnp.bfloat16


def _params(*sem):
    return pltpu.CompilerParams(dimension_semantics=sem, vmem_limit_bytes=VMEM_LIMIT)


def _rms(x, gain):
    return x * lax.rsqrt(jnp.mean(x * x, axis=-1, keepdims=True) + RMS_EPS) * gain


def _ffn_body(x_ref, g_ref, wg_ref, wu_ref, wd_ref, pg_ref, o_ref, h_ref, *, final_norm):
    j = pl.program_id(1)

    @pl.when(j == 0)
    def _():
        x = x_ref[...]
        h_ref[...] = _rms(x, g_ref[...]).astype(BF16)
        o_ref[...] = x

    h = h_ref[...]
    gate = jnp.dot(h, wg_ref[...], preferred_element_type=F32)
    up = jnp.dot(h, wu_ref[...], preferred_element_type=F32)
    act = (gate * (0.5 / (1.0 + jnp.exp(-gate))) * up).astype(BF16)
    o_ref[...] += jnp.dot(act, wd_ref[...], preferred_element_type=F32)

    if final_norm:
        @pl.when(j == pl.num_programs(1) - 1)
        def _():
            o_ref[...] = _rms(o_ref[...], pg_ref[...])


def _ffn(x, gain, w_gate_up, w_down, layer, post_gain=None, *, tm=512, tf=512):
    s, d = x.shape
    f = w_down.shape[1]
    tm = min(tm, s)
    tf = min(tf, f)
    assert s % tm == 0 and f % tf == 0
    nf = f // tf
    final_norm = post_gain is not None
    pg = post_gain if final_norm else gain
    return pl.pallas_call(
        functools.partial(_ffn_body, final_norm=final_norm),
        grid=(s // tm, nf),
        in_specs=[
            pl.BlockSpec((tm, d), lambda i, j: (i, 0)),
            pl.BlockSpec((1, d), lambda i, j: (0, 0)),
            pl.BlockSpec((None, d, tf), lambda i, j: (layer, 0, j)),
            pl.BlockSpec((None, d, tf), lambda i, j: (layer, 0, j + nf)),
            pl.BlockSpec((None, tf, d), lambda i, j: (layer, j, 0)),
            pl.BlockSpec((1, d), lambda i, j: (0, 0)),
        ],
        out_specs=pl.BlockSpec((tm, d), lambda i, j: (i, 0)),
        out_shape=jax.ShapeDtypeStruct((s, d), F32),
        scratch_shapes=[pltpu.VMEM((tm, d), BF16)],
        compiler_params=_params("parallel", "arbitrary"),
        name="ffn",
    )(x, gain.reshape(1, d), w_gate_up, w_gate_up, w_down, pg.reshape(1, d))


def _rope_heads(y, cos, sin, o_ref, scale, transposed=False):
    for h in range(y.shape[1] // HEAD_DIM):
        t = y[:, h * HEAD_DIM:(h + 1) * HEAD_DIM]
        r = t * cos + pltpu.roll(t, HEAD_DIM // 2, 1) * sin
        if scale != 1.0:
            r = r * scale
        if transposed:
            o_ref[h * HEAD_DIM:(h + 1) * HEAD_DIM, :] = r.T.astype(o_ref.dtype)
        else:
            o_ref[:, h * HEAD_DIM:(h + 1) * HEAD_DIM] = r.astype(o_ref.dtype)


def _store_value_chunks(y, o_ref):
    for b in range(y.shape[0] // CHUNK):
        for g in range(N_KV_HEADS):
            blk = y[b * CHUNK:(b + 1) * CHUNK, g * HEAD_DIM:(g + 1) * HEAD_DIM]
            o_ref[b, g * V_ROWS:g * V_ROWS + HEAD_DIM, :] = blk.T.astype(o_ref.dtype)
            o_ref[b, g * V_ROWS + HEAD_DIM:(g + 1) * V_ROWS, :] = jnp.ones((ONES_ROWS, CHUNK), o_ref.dtype)


def _rot_idx(t):
    lane = lax.broadcasted_iota(jnp.int32, t.shape, 1)
    first_half = (lane & (IDX_DIM // 2)) == 0
    return jnp.where(first_half, pltpu.roll(t, LANES - IDX_DIM // 2, 1),
                     pltpu.roll(t, IDX_DIM // 2, 1))


class _Piece(NamedTuple):
    kind: str
    start: int
    width: int
    lo: int
    hi: int
    dtype: Any


def _proj_body(*refs, pieces, q_scale, idx_w_scale):
    x_ref, g_ref, cos_ref, sin_ref, cosi_ref, sini_ref, gk_ref, w_ref = refs[:8]
    o_refs = refs[8:]
    h = _rms(x_ref[...], g_ref[...]).astype(BF16)
    products = {}
    for piece, o_ref in zip(pieces, o_refs):
        cols = (piece.start, piece.width)
        if cols not in products:
            products[cols] = jnp.dot(h, w_ref[:, piece.start:piece.start + piece.width],
                                     preferred_element_type=F32)
        y = products[cols]
        if (piece.lo, piece.hi) != (0, piece.width):
            y = y[:, piece.lo:piece.hi]
        kind = piece.kind
        if kind == "plain":
            o_ref[...] = y.astype(o_ref.dtype)
        elif kind == "scaled":
            o_ref[...] = (y * q_scale).astype(o_ref.dtype)
        elif kind == "rope":
            _rope_heads(y, cos_ref[...], sin_ref[...], o_ref, 1.0)
        elif kind == "rope_scaled_t":
            _rope_heads(y, cos_ref[...], sin_ref[...], o_ref, q_scale * LOG2_E, transposed=True)
        elif kind == "rope_idx_t":
            cosi, sini = cosi_ref[...], sini_ref[...]
            for c in range(y.shape[1] // LANES):
                t = y[:, c * LANES:(c + 1) * LANES]
                o_ref[c * LANES:(c + 1) * LANES, :] = (t * cosi + _rot_idx(t) * sini).T.astype(o_ref.dtype)
        elif kind == "value_chunks":
            _store_value_chunks(y, o_ref)
        elif kind == "kiwi":
            lane = lax.broadcasted_iota(jnp.int32, y.shape, 1)
            is_k = lane < IDX_DIM
            kk = jnp.where(is_k, y, 0.0)
            ms = jnp.sum(kk * kk, axis=-1, keepdims=True) * (1.0 / IDX_DIM)
            kn = kk * lax.rsqrt(ms + RMS_EPS) * gk_ref[...]
            kr = kn * cosi_ref[...] + _rot_idx(kn) * sini_ref[...]
            o_ref[...] = jnp.where(is_k, kr, y * idx_w_scale)
        else:
            raise ValueError(kind)


def _layer_weight_spec(w, layer):
    if w.ndim == 2:
        return pl.BlockSpec(w.shape, lambda i: (0, 0), pipeline_mode=pl.Buffered(1))
    return pl.BlockSpec((None,) + w.shape[1:], lambda i: (layer, 0, 0), pipeline_mode=pl.Buffered(1))


def _proj(x, gain, tables, gk, w, pieces, *, layer=0, tm=256):
    s, d = x.shape
    tm = min(tm, s)
    assert s % tm == 0
    cos, sin, cosi, sini = tables
    row = lambda i: (i, 0)
    const = lambda i: (0, 0)
    in_specs = [pl.BlockSpec((tm, d), row), pl.BlockSpec((1, d), const)]
    in_specs += [pl.BlockSpec((tm, LANES), row)] * 4
    in_specs += [pl.BlockSpec((1, LANES), const)]
    in_specs += [_layer_weight_spec(w, layer)]
    out_specs, out_shape = [], []
    for p in pieces:
        assert p.start % LANES == 0 and p.width % LANES == 0 and p.start + p.width <= w.shape[-1]
        n = p.hi - p.lo
        if p.kind.endswith("_t"):
            out_specs.append(pl.BlockSpec((n, tm), lambda i: (0, i)))
            out_shape.append(jax.ShapeDtypeStruct((n, s), p.dtype))
        elif p.kind == "value_chunks":
            assert tm % CHUNK == 0 and n == N_KV_HEADS * HEAD_DIM
            out_specs.append(pl.BlockSpec((tm // CHUNK, N_KV_HEADS * V_ROWS, CHUNK), lambda i: (i, 0, 0)))
            out_shape.append(jax.ShapeDtypeStruct((s // CHUNK, N_KV_HEADS * V_ROWS, CHUNK), p.dtype))
        else:
            out_specs.append(pl.BlockSpec((tm, n), row))
            out_shape.append(jax.ShapeDtypeStruct((s, n), p.dtype))
    return pl.pallas_call(
        functools.partial(_proj_body, pieces=tuple(pieces), q_scale=HEAD_DIM ** -0.5,
                          idx_w_scale=(IDX_HEADS ** -0.5) * (IDX_DIM ** -0.5)),
        grid=(s // tm,),
        in_specs=in_specs,
        out_specs=out_specs,
        out_shape=out_shape,
        compiler_params=_params("parallel"),
        name="proj",
    )(x, gain.reshape(1, d), cos, sin, cosi, sini, gk, w)


def _kv_shared_body(x_ref, g_ref, cos_ref, sin_ref, wk_ref, wv_ref, k_ref, v_ref, km_ref):
    h = _rms(x_ref[...], g_ref[...]).astype(BF16)
    yk = jnp.dot(h, wk_ref[...], preferred_element_type=F32)
    cos, sin = cos_ref[...], sin_ref[...]
    tm = yk.shape[0]
    first_blk = pl.program_id(0) * (tm // MOBA_BLOCK)
    lane = lax.broadcasted_iota(jnp.int32, (MOBA_BLOCK, HEAD_DIM), 1)
    for hd in range(yk.shape[1] // HEAD_DIM):
        t = yk[:, hd * HEAD_DIM:(hd + 1) * HEAD_DIM]
        r = t * cos + pltpu.roll(t, HEAD_DIM // 2, 1) * sin
        k_ref[:, 2 * hd * HEAD_DIM:(2 * hd + 1) * HEAD_DIM] = r.astype(k_ref.dtype)
        for b in range(tm // MOBA_BLOCK):
            rows = slice(b * MOBA_BLOCK, (b + 1) * MOBA_BLOCK)
            k_ref[rows, (2 * hd + 1) * HEAD_DIM:(2 * hd + 2) * HEAD_DIM] = (
                jnp.where(lane == first_blk + b, 1.0, 0.0).astype(k_ref.dtype))
            km_ref[b, :, hd * HEAD_DIM:(hd + 1) * HEAD_DIM] = (
                jnp.sum(r[rows], axis=0, keepdims=True) * (1.0 / MOBA_BLOCK))
    _store_value_chunks(jnp.dot(h, wv_ref[...], preferred_element_type=F32), v_ref)


def _kv_shared(x, gain, cos, sin, wk, wv):
    s, d = x.shape
    tm = MOBA_BLOCK
    assert s % tm == 0 and CHUNK == MOBA_BLOCK
    n = s // tm
    nkv = wk.shape[1]
    src = lambda i: (jnp.minimum(i, n - 1), 0)
    row = lambda i: (i, 0)
    const = lambda i: (0, 0)
    return pl.pallas_call(
        _kv_shared_body,
        grid=(n + 1,),
        in_specs=[pl.BlockSpec((tm, d), src), pl.BlockSpec((1, d), const),
                  pl.BlockSpec((tm, LANES), src), pl.BlockSpec((tm, LANES), src),
                  pl.BlockSpec(wk.shape, const, pipeline_mode=pl.Buffered(1)),
                  pl.BlockSpec(wv.shape, const, pipeline_mode=pl.Buffered(1))],
        out_specs=[pl.BlockSpec((tm, 2 * nkv), row),
                   pl.BlockSpec((1, N_KV_HEADS * V_ROWS, CHUNK), lambda i: (i, 0, 0)),
                   pl.BlockSpec((1, 1, nkv), lambda i: (i, 0, 0))],
        out_shape=[jax.ShapeDtypeStruct((s + tm, 2 * nkv), BF16),
                   jax.ShapeDtypeStruct((n + 1, N_KV_HEADS * V_ROWS, CHUNK), BF16),
                   jax.ShapeDtypeStruct((n + 1, 1, nkv), F32)],
        compiler_params=_params("parallel"),
        name="kv_shared",
    )(x, gain.reshape(1, d), cos, sin, wk, wv)


def _load_q_group(qt_ref, qall_ref):
    tq = qt_ref.shape[1]
    for hh in range(GROUP):
        qall_ref[:, hh * tq:(hh + 1) * tq] = qt_ref[hh * HEAD_DIM:(hh + 1) * HEAD_DIM, :]


def _softmax_init(m_ref, acc_ref):
    m_ref[...] = jnp.full(m_ref.shape, NEG, F32)
    acc_ref[...] = jnp.zeros(acc_ref.shape, F32)


def _add_group_bias(s, b):
    tq = b.shape[1]
    return jnp.concatenate([s[:, hh * tq:(hh + 1) * tq] + b for hh in range(GROUP)], axis=1)


def _logits_pass(chunk_ids, logits_fn, s_ref, cmax_ref):
    cmax = None
    for j, c in enumerate(chunk_ids):
        s = logits_fn(c)
        s_ref[j * CHUNK:(j + 1) * CHUNK, :] = s
        cm = jnp.max(s, axis=0, keepdims=True)
        cmax = cm if cmax is None else jnp.maximum(cmax, cm)
    cmax_ref[...] = cmax


def _softmax_pass(chunk_ids, vt_ref, s_ref, cmax_ref, m_ref, acc_ref):
    m_old = m_ref[...]
    m_new = jnp.maximum(m_old, cmax_ref[...])
    alpha = jnp.exp2(m_old - m_new)
    pv = None
    for j, c in enumerate(chunk_ids):
        p = jnp.exp2(s_ref[j * CHUNK:(j + 1) * CHUNK, :] - m_new)
        d = jnp.dot(vt_ref[c], p.astype(BF16), preferred_element_type=F32)
        pv = d if pv is None else pv + d
    m_ref[...] = m_new
    acc_ref[...] = alpha * acc_ref[...] + pv


def _attend_range(n_chunks, logits_fn, vt_ref, s_refs, cmax_refs, m_ref, acc_ref, *, fuse_pairs):
    n_steps = pl.cdiv(n_chunks, CHUNKS_PER_STEP)
    last_chunk = vt_ref.shape[0] - 1

    def chunks(step):
        return [step * CHUNKS_PER_STEP + j for j in range(CHUNKS_PER_STEP)]

    def logits(step, slot):
        _logits_pass(chunks(step), logits_fn, s_refs[slot], cmax_refs[slot])

    def softmax(step, slot):
        ids = [jnp.minimum(c, last_chunk) for c in chunks(step)]
        _softmax_pass(ids, vt_ref, s_refs[slot], cmax_refs[slot], m_ref, acc_ref)

    @pl.when(n_steps > 0)
    def _():
        logits(0, 0)

    def pair(u, carry):
        t = 2 * u
        if fuse_pairs:
            @pl.when(t + 1 < n_steps)
            def _():
                logits(t + 1, 1)
                softmax(t, 0)
                logits(t + 2, 0)
                softmax(t + 1, 1)

            @pl.when(t + 1 >= n_steps)
            def _():
                softmax(t, 0)
        else:
            logits(t + 1, 1)
            softmax(t, 0)

            @pl.when(t + 1 < n_steps)
            def _():
                logits(t + 2, 0)
                softmax(t + 1, 1)

        return carry

    lax.fori_loop(0, pl.cdiv(n_steps, 2), pair, 0)


def _softmax_finish(o_ref, acc_ref):
    tq = o_ref.shape[0]
    out = acc_ref[:HEAD_DIM, :] / acc_ref[HEAD_DIM:HEAD_DIM + 1, :]
    for hh in range(GROUP):
        o_ref[:, hh * HEAD_DIM:(hh + 1) * HEAD_DIM] = out[:, hh * tq:(hh + 1) * tq].T.astype(o_ref.dtype)


def _dsa_body(qit_ref, wt_ref, ki_ref, qt_ref, k_ref, vt_ref, o_ref,
              keys_ref, gmax_ref, cand_ref, thr_ref, qall_ref, sa_ref, sb_ref, cma_ref, cmb_ref,
              m_ref, acc_ref, *, topk):
    i = pl.program_id(0)
    g = pl.program_id(1)
    tq = CHUNK
    n_chunks = i + 1
    sub = 128

    def causal(c, rows, row_off=0):
        kpos = c * CHUNK + row_off + lax.broadcasted_iota(jnp.int32, (rows, tq), 0)
        qpos = i * tq + lax.broadcasted_iota(jnp.int32, (rows, tq), 1)
        return kpos <= qpos

    @pl.when(g == 0)
    def _():
        gmax_ref[...] = jnp.full(gmax_ref.shape, INT_MIN, jnp.int32)

        def score_chunk(c, carry):
            for part in range(CHUNK // sub):
                r0 = pl.multiple_of(c * CHUNK + part * sub, sub)
                kit = ki_ref[pl.ds(r0, sub), :]
                sc = jnp.zeros((sub, tq), F32)
                for h in range(IDX_HEADS):
                    d = jnp.dot(kit, qit_ref[h * IDX_DIM:(h + 1) * IDX_DIM, :], preferred_element_type=F32)
                    sc = sc + wt_ref[h:h + 1, :] * jnp.maximum(d, 0.0)
                sc = jnp.where(sc == 0.0, 0.0, sc)
                sc = jnp.where(causal(c, sub, part * sub), sc, -jnp.inf)
                bits = pltpu.bitcast(sc, jnp.int32)
                key = bits ^ ((bits >> 31) & INT_MAX)
                keys_ref[pl.ds(r0, sub), :] = key
                gsl = slice(part * sub, (part + 1) * sub)
                gmax_ref[gsl, :] = jnp.maximum(gmax_ref[gsl, :], key)
            return carry

        def score_pair(u, carry):
            score_chunk(2 * u, carry)
            return score_chunk(2 * u + 1, carry)

        lax.fori_loop(0, n_chunks // 2, score_pair, 0)

        @pl.when(n_chunks % 2 == 1)
        def _():
            score_chunk(n_chunks - 1, 0)

        def count_rows(load, n_blocks, t):
            def count_block(b, acc):
                ge = jnp.where(load(b) >= t, 1, 0).astype(jnp.int32)
                return acc + jnp.sum(ge.reshape(CHUNK // SUBLANES, SUBLANES, tq), axis=0)

            part = jnp.zeros((SUBLANES, tq), jnp.int32)
            if isinstance(n_blocks, int):
                for b in range(n_blocks):
                    part = count_block(b, part)
            else:
                def count_group(u, acc):
                    for j in range(COUNT_UNROLL):
                        acc = count_block(u * COUNT_UNROLL + j, acc)
                    return acc

                n_groups = n_blocks // COUNT_UNROLL
                part = lax.fori_loop(0, n_groups, count_group, part)
                part = lax.fori_loop(n_groups * COUNT_UNROLL, n_blocks, count_block, part)
            return jnp.sum(part, axis=0, keepdims=True)

        def key_chunk(c):
            return keys_ref[pl.ds(pl.multiple_of(c * CHUNK, CHUNK), CHUNK), :]

        def count_all(t):
            return count_rows(key_chunk, n_chunks, t)

        def count_cand(t):
            return count_rows(lambda b: cand_ref[b], CAND_LEVELS, t)

        def any_set(flags):
            return jnp.max(flags) > 0.0

        def bisect_pass(count_fn, base, live, state):
            lo, hi, c_lo, c_hi, done = state
            mid = (lo >> 1) + (hi >> 1) + (lo & hi & 1)
            cnt = count_fn(mid) + base
            up = live & (cnt >= topk)
            down = live & (cnt < topk)
            lo, c_lo = jnp.where(up, mid, lo), jnp.where(up, cnt, c_lo)
            hi, c_hi = jnp.where(down, mid, hi), jnp.where(down, cnt, c_hi)
            finished = live & ((cnt == topk) | (hi <= lo + 1))
            return lo, hi, c_lo, c_hi, jnp.where(finished, 1.0, done)

        def bisect_while(count_fn, base, live_fn, keep_going, state):
            def cond(carry):
                it, state = carry
                return jnp.logical_and(it < MAX_PASSES, keep_going(state))

            def body(carry):
                it, state = carry
                return it + 1, bisect_pass(count_fn, base, live_fn(state), state)

            return lax.while_loop(cond, body, (jnp.int32(0), state))[1]

        def in_bracket(state):
            return state[2] - state[3]

        def active(state):
            return state[4] == 0.0

        gm = gmax_ref[...]
        lo = jnp.min(gm, axis=0, keepdims=True)
        top = jnp.max(gm, axis=0, keepdims=True)
        hi = jnp.where(top == INT_MAX, top, top + 1)
        c_lo = count_all(lo)
        state = (lo, hi, c_lo, jnp.zeros_like(c_lo), (c_lo == topk).astype(F32))

        def wide(state):
            return any_set(jnp.where(active(state) & (in_bracket(state) > CAND_MAX), 1.0, 0.0))

        state = bisect_while(count_all, 0, active, wide, state)

        lo, hi, c_lo, c_hi, _ = state
        cand_ref[...] = jnp.full(cand_ref.shape, INT_MIN, jnp.int32)

        def capture(c, carry):
            blk = key_chunk(c)
            x = jnp.where((blk >= lo) & (blk < hi), blk, INT_MIN)
            for level in range(CAND_LEVELS):
                held = cand_ref[level]
                cand_ref[level] = jnp.maximum(held, x)
                x = jnp.minimum(held, x)
            return carry

        lax.fori_loop(0, n_chunks, capture, 0)
        captured = count_cand(lo) == in_bracket(state)

        def live_captured(state):
            return active(state) & captured

        state = bisect_while(count_cand, c_hi, live_captured,
                             lambda st: any_set(jnp.where(live_captured(st), 1.0, 0.0)), state)

        state = bisect_while(count_all, 0, active,
                             lambda st: any_set(jnp.where(active(st), 1.0, 0.0)), state)
        thr = state[0]
        thr_ref[...] = thr

        surplus = state[2] - topk

        @pl.when(jnp.max(surplus.astype(F32)) > 0.0)
        def _():
            row = lax.broadcasted_iota(jnp.int32, (CHUNK, CHUNK), 0)
            col = lax.broadcasted_iota(jnp.int32, (CHUNK, CHUNK), 1)
            prefix_ones = jnp.where(row >= col, 1.0, 0.0).astype(BF16)

            def count_equal(c, acc):
                return acc + jnp.sum(jnp.where(key_chunk(c) == thr, 1.0, 0.0), axis=0, keepdims=True)

            n_equal = lax.fori_loop(0, n_chunks, count_equal, jnp.zeros((1, tq), F32))
            keep = n_equal - surplus.astype(F32)

            def drop_late_ties(c, seen):
                blk = key_chunk(c)
                tie = blk == thr
                tie_f = jnp.where(tie, 1.0, 0.0)
                rank = seen + jnp.dot(prefix_ones, tie_f.astype(BF16), preferred_element_type=F32)
                r0 = pl.multiple_of(c * CHUNK, CHUNK)
                keys_ref[pl.ds(r0, CHUNK), :] = jnp.where(tie & (rank > keep), blk - 1, blk)
                return seen + jnp.sum(tie_f, axis=0, keepdims=True)

            lax.fori_loop(0, n_chunks, drop_late_ties, jnp.zeros((1, tq), F32))

    _load_q_group(qt_ref, qall_ref)
    _softmax_init(m_ref, acc_ref)
    thr = thr_ref[...]

    def logits(c):
        r0 = pl.multiple_of(jnp.minimum(c, n_chunks - 1) * CHUNK, CHUNK)
        sel = (keys_ref[pl.ds(r0, CHUNK), :] >= thr) & causal(c, CHUNK)
        s = jnp.dot(k_ref[pl.ds(r0, CHUNK), :], qall_ref[...], preferred_element_type=F32)
        return _add_group_bias(s, jnp.where(sel, 0.0, NEG))

    _attend_range(n_chunks, logits, vt_ref, (sa_ref, sb_ref), (cma_ref, cmb_ref), m_ref, acc_ref,
                  fuse_pairs=True)
    _softmax_finish(o_ref, acc_ref)


def _dsa(qit, wt, ki, qt, k, vtt, topk):
    s = ki.shape[0]
    tq = CHUNK
    assert s % (tq * CHUNKS_PER_STEP) == 0 and topk <= CHUNK
    gw = GROUP * HEAD_DIM
    return pl.pallas_call(
        functools.partial(_dsa_body, topk=topk),
        grid=(s // tq, N_KV_HEADS),
        in_specs=[
            pl.BlockSpec((IDX_HEADS * IDX_DIM, tq), lambda i, g: (0, i)),
            pl.BlockSpec((IDX_HEADS, tq), lambda i, g: (0, i)),
            pl.BlockSpec((s, IDX_DIM), lambda i, g: (0, 0), pipeline_mode=pl.Buffered(1)),
            pl.BlockSpec((gw, tq), lambda i, g: (g, i)),
            pl.BlockSpec((s, HEAD_DIM), lambda i, g: (0, g)),
            pl.BlockSpec((s // tq, V_ROWS, tq), lambda i, g: (0, g, 0)),
        ],
        out_specs=pl.BlockSpec((tq, gw), lambda i, g: (i, g)),
        out_shape=jax.ShapeDtypeStruct((s, N_MAIN_HEADS * HEAD_DIM), BF16),
        scratch_shapes=[
            pltpu.VMEM((s, tq), jnp.int32),
            pltpu.VMEM((CHUNK, tq), jnp.int32),
            pltpu.VMEM((CAND_LEVELS, CHUNK, tq), jnp.int32),
            pltpu.VMEM((1, tq), jnp.int32),
            pltpu.VMEM((HEAD_DIM, GROUP * tq), BF16),
            pltpu.VMEM((CHUNKS_PER_STEP * CHUNK, GROUP * tq), F32),
            pltpu.VMEM((CHUNKS_PER_STEP * CHUNK, GROUP * tq), F32),
            pltpu.VMEM((1, GROUP * tq), F32),
            pltpu.VMEM((1, GROUP * tq), F32),
            pltpu.VMEM((1, GROUP * tq), F32),
            pltpu.VMEM((V_ROWS, GROUP * tq), F32),
        ],
        compiler_params=_params("arbitrary", "arbitrary"),
        name="dsa",
    )(qit, wt, ki, qt, k, vtt)


def _moba_body(qt_ref, k_ref, vt_ref, km_ref, o_ref, qaug_ref, sa_ref, sb_ref, cma_ref, cmb_ref,
               m_ref, acc_ref, *, n_sel):
    cur = pl.program_id(1)
    tq = CHUNK
    nb = km_ref.shape[0]
    width = GROUP * tq
    for hh in range(GROUP):
        qaug_ref[:HEAD_DIM, hh * tq:(hh + 1) * tq] = qt_ref[hh * HEAD_DIM:(hh + 1) * HEAD_DIM, :]
    q_all = qaug_ref[:HEAD_DIM, :]

    blk_id = lax.broadcasted_iota(jnp.int32, (nb, width), 0)
    past = blk_id < cur
    gate = jnp.dot(km_ref[...], q_all, preferred_element_type=F32)
    gate = jnp.where(past, gate, -jnp.inf)
    chosen = jnp.zeros((nb, width), jnp.bool_)
    for _ in range(n_sel):
        best = jnp.max(gate, axis=0, keepdims=True)
        first = jnp.min(jnp.where(gate == best, blk_id, nb), axis=0, keepdims=True)
        pick = blk_id == first
        chosen = chosen | pick
        gate = jnp.where(pick, -jnp.inf, gate)
    qaug_ref[HEAD_DIM:HEAD_DIM + nb, :] = jnp.where(chosen & past, 0.0, NEG).astype(BF16)
    pad = BF16_TILE_ROWS
    qaug_ref[HEAD_DIM + nb:HEAD_DIM + nb + pad, :] = jnp.full((pad, width), NEG, BF16)
    if nb + pad < HEAD_DIM:
        qaug_ref[HEAD_DIM + nb + pad:, :] = jnp.zeros((HEAD_DIM - nb - pad, width), BF16)

    _softmax_init(m_ref, acc_ref)
    state = (vt_ref, (sa_ref, sb_ref), (cma_ref, cmb_ref), m_ref, acc_ref)

    def past_logits(c):
        r0 = pl.multiple_of(jnp.where(c < cur, c, nb) * CHUNK, CHUNK)
        return jnp.dot(k_ref[pl.ds(r0, CHUNK), :], qaug_ref[...], preferred_element_type=F32)

    _attend_range(cur, past_logits, *state, fuse_pairs=False)

    tri = (lax.broadcasted_iota(jnp.int32, (tq, tq), 0) <= lax.broadcasted_iota(jnp.int32, (tq, tq), 1))

    def own_logits(c):
        r0 = pl.multiple_of(c * CHUNK, CHUNK)
        s = jnp.dot(k_ref[pl.ds(r0, CHUNK), :HEAD_DIM], q_all, preferred_element_type=F32)
        return _add_group_bias(s, jnp.where(tri, 0.0, NEG))

    _logits_pass([cur], own_logits, sa_ref, cma_ref)
    _softmax_pass([cur], vt_ref, sa_ref, cma_ref, m_ref, acc_ref)
    _softmax_finish(o_ref, acc_ref)


def _moba(qt, k_aug, vtt, kmeans, n_sel):
    s = k_aug.shape[0] - CHUNK
    assert MOBA_BLOCK == CHUNK and s % (CHUNK * CHUNKS_PER_STEP) == 0
    tq = CHUNK
    nb = s // tq
    assert nb + BF16_TILE_ROWS <= HEAD_DIM and vtt.shape[0] == nb + 1
    gw = GROUP * HEAD_DIM
    return pl.pallas_call(
        functools.partial(_moba_body, n_sel=n_sel),
        grid=(N_KV_HEADS, nb),
        in_specs=[
            pl.BlockSpec((gw, tq), lambda g, i: (g, i)),
            pl.BlockSpec((s + CHUNK, 2 * HEAD_DIM), lambda g, i: (0, g)),
            pl.BlockSpec((nb + 1, V_ROWS, tq), lambda g, i: (0, g, 0)),
            pl.BlockSpec((nb, HEAD_DIM), lambda g, i: (0, g)),
        ],
        out_specs=pl.BlockSpec((tq, gw), lambda g, i: (i, g)),
        out_shape=jax.ShapeDtypeStruct((s, N_MAIN_HEADS * HEAD_DIM), BF16),
        scratch_shapes=[
            pltpu.VMEM((2 * HEAD_DIM, GROUP * tq), BF16),
            pltpu.VMEM((CHUNKS_PER_STEP * CHUNK, GROUP * tq), F32),
            pltpu.VMEM((CHUNKS_PER_STEP * CHUNK, GROUP * tq), F32),
            pltpu.VMEM((1, GROUP * tq), F32),
            pltpu.VMEM((1, GROUP * tq), F32),
            pltpu.VMEM((1, GROUP * tq), F32),
            pltpu.VMEM((V_ROWS, GROUP * tq), F32),
        ],
        compiler_params=_params("parallel", "arbitrary"),
        name="moba",
    )(qt, k_aug, vtt, kmeans)


def _mem_attn_body(q_ref, kt_ref, v_ref, o_ref):
    for h in range(N_MEM_HEADS):
        sl = slice(h * HEAD_DIM, (h + 1) * HEAD_DIM)
        s = jnp.dot(q_ref[:, sl], kt_ref[sl, :], preferred_element_type=F32)
        p = jnp.exp(s - jnp.max(s, axis=-1, keepdims=True))
        o = jnp.dot(p.astype(BF16), v_ref[:, sl], preferred_element_type=F32)
        o_ref[:, sl] = (o / jnp.sum(p, axis=-1, keepdims=True)).astype(o_ref.dtype)


def _mem_attn(qm, kmt, vm, *, tq=512):
    s, w = qm.shape
    m = vm.shape[0]
    tq = min(tq, s)
    assert s % tq == 0
    return pl.pallas_call(
        _mem_attn_body,
        grid=(s // tq,),
        in_specs=[pl.BlockSpec((tq, w), lambda i: (i, 0)),
                  pl.BlockSpec((w, m), lambda i: (0, 0)),
                  pl.BlockSpec((m, w), lambda i: (0, 0))],
        out_specs=pl.BlockSpec((tq, w), lambda i: (i, 0)),
        out_shape=jax.ShapeDtypeStruct((s, w), BF16),
        compiler_params=_params("parallel"),
        name="mem_attn",
    )(qm, kmt, vm)


def _out_proj_body(x_ref, om_ref, oq_ref, w_ref, o_ref):
    n_main = om_ref.shape[1]
    o_ref[...] = (x_ref[...]
                  + jnp.dot(om_ref[...], w_ref[:n_main, :], preferred_element_type=F32)
                  + jnp.dot(oq_ref[...], w_ref[n_main:, :], preferred_element_type=F32))


def _out_proj(x, o_main, o_mem, w, *, layer=0, tm=512):
    s, d = x.shape
    tm = min(tm, s)
    assert s % tm == 0 and w.shape[-2] == o_main.shape[1] + o_mem.shape[1]
    row = lambda i: (i, 0)
    const = lambda i: (0, 0)
    return pl.pallas_call(
        _out_proj_body,
        grid=(s // tm,),
        in_specs=[pl.BlockSpec((tm, d), row),
                  pl.BlockSpec((tm, o_main.shape[1]), row),
                  pl.BlockSpec((tm, o_mem.shape[1]), row),
                  _layer_weight_spec(w, layer)],
        out_specs=pl.BlockSpec((tm, d), row),
        out_shape=jax.ShapeDtypeStruct((s, d), F32),
        compiler_params=_params("parallel"),
        name="out_proj",
    )(x, o_main, o_mem, w)


def _rope_tables(positions, dim):
    inv = 1.0 / (ROPE_THETA ** (jnp.arange(0, dim, 2, dtype=F32) / dim))
    ang = positions.astype(F32)[:, None] * inv
    c, s = jnp.cos(ang), jnp.sin(ang)
    reps = LANES // dim
    return jnp.tile(jnp.concatenate([c, c], -1), (1, reps)), jnp.tile(jnp.concatenate([-s, s], -1), (1, reps))


def kernel(x, mem, positions, ffn1_norm, ffn1_w_gate_up, ffn1_w_down, attn_norm, mem_norm, a_w_in, idx_k_norm, b_w_in, w_mem_kv, w_out, ffn2_norm, ffn2_w_gate_up, ffn2_w_down, kv_norm, w_kv_shared, final_norm):
    b, s, d = x.shape
    assert b == 1 and mem.shape[0] == 1
    depth = ffn1_norm.shape[0]
    n_a = a_w_in.shape[0]
    main_w = N_MAIN_HEADS * HEAD_DIM
    kv_w = N_KV_HEADS * HEAD_DIM
    idx_w = IDX_HEADS * IDX_DIM
    mem_w = N_MEM_HEADS * HEAD_DIM
    topk = min(IDX_TOPK_MAX, s // 4)
    nb = s // MOBA_BLOCK
    n_sel = min(MOBA_TOPK_MAX, max(nb - 1, 1))

    cos, sin = _rope_tables(positions[0], HEAD_DIM)
    cosi, sini = _rope_tables(positions[0], IDX_DIM)
    tables = (cos, sin, cosi, sini)
    mem_tables = tuple(t[:mem.shape[1]] for t in tables)
    no_gk = jnp.zeros((1, LANES), F32)

    ffn_w = [w.astype(BF16) for w in (ffn1_w_gate_up, ffn1_w_down, ffn2_w_gate_up, ffn2_w_down)]
    w_mem_all = w_mem_kv.astype(BF16)
    w_out_all = w_out.astype(BF16)

    xs = x[0]
    mem2 = mem[0]
    k_sh = vtt_sh = kmeans = None
    for i in range(depth):
        if i == n_a:
            wk = w_kv_shared[:, :kv_w].astype(BF16)
            wv = w_kv_shared[:, kv_w:].astype(BF16)
            k_sh, vtt_sh, km = _kv_shared(xs, kv_norm, cos, sin, wk, wv)
            kmeans = km[:nb].reshape(nb, kv_w).astype(BF16)

        last = i == depth - 1
        xs = _ffn(xs, ffn1_norm[i], ffn_w[0], ffn_w[1], i)

        mk, mv = _proj(mem2, mem_norm[i], mem_tables, no_gk, w_mem_all,
                       [_Piece("plain", 0, mem_w, 0, mem_w, BF16),
                        _Piece("plain", mem_w, mem_w, 0, mem_w, BF16)], layer=i)
        if i < n_a:
            tail = main_w + 2 * kv_w + idx_w
            kiwi_w = IDX_DIM + IDX_HEADS
            assert a_w_in.shape[2] == tail + kiwi_w + mem_w and kiwi_w <= LANES
            wa = a_w_in[i].astype(BF16)
            wa = jnp.concatenate([wa[:, :tail + kiwi_w], jnp.zeros((d, LANES - kiwi_w), BF16),
                                  wa[:, tail + kiwi_w:]], axis=1)
            gk = jnp.pad(idx_k_norm[i], (0, LANES - IDX_DIM)).reshape(1, LANES)
            qt, k, vtt, qit, kiwi, qm = _proj(
                xs, attn_norm[i], tables, gk, wa,
                [_Piece("rope_scaled_t", 0, main_w, 0, main_w, BF16),
                 _Piece("rope", main_w, kv_w, 0, kv_w, BF16),
                 _Piece("value_chunks", main_w + kv_w, kv_w, 0, kv_w, BF16),
                 _Piece("rope_idx_t", main_w + 2 * kv_w, idx_w, 0, idx_w, BF16),
                 _Piece("kiwi", tail, LANES, 0, LANES, F32),
                 _Piece("scaled", tail + LANES, mem_w, 0, mem_w, BF16)])
            ki = kiwi[:, :IDX_DIM].astype(BF16)
            wt = kiwi[:, IDX_DIM:IDX_DIM + IDX_HEADS].T
            o_main = _dsa(qit, wt, ki, qt, k, vtt, topk)
        else:
            qt, qm = _proj(xs, attn_norm[i], tables, no_gk, b_w_in[i - n_a].astype(BF16),
                           [_Piece("rope_scaled_t", 0, main_w, 0, main_w, BF16),
                            _Piece("scaled", main_w, mem_w, 0, mem_w, BF16)])
            o_main = _moba(qt, k_sh, vtt_sh, kmeans, n_sel)
        o_mem = _mem_attn(qm, mk.T, mv)
        xs = _out_proj(xs, o_main, o_mem, w_out_all, layer=i)

        xs = _ffn(xs, ffn2_norm[i], ffn_w[2], ffn_w[3], i, final_norm if last else None)
    return xs[None]
```

```python
import functools
from typing import Any, NamedTuple

import jax
import jax.numpy as jnp
import numpy as np
from jax import lax
from jax.experimental import pallas as pl
from jax.experimental.pallas import tpu as pltpu

HEAD_DIM = 128
N_MAIN_HEADS = 12
N_KV_HEADS = 4
GROUP = N_MAIN_HEADS // N_KV_HEADS
N_MEM_HEADS = 4
IDX_HEADS = 16
IDX_DIM = 64
IDX_TOPK_MAX = 256
MOBA_BLOCK = 256
MOBA_TOPK_MAX = 3
ROPE_THETA = 10000.0
RMS_EPS = 1e-6

LANES = 128
SUBLANES = 8
VMEM_LIMIT = 56 * 1024 * 1024
NEG = -1e30
LOG2_E = 1.4426950408889634
INT_MIN = -2 ** 31
INT_MAX = 2 ** 31 - 1
CHUNK = 256
CHUNKS_PER_STEP = 4
CAND_LEVELS = 2
CAND_MAX = 16
MAX_PASSES = 40
COUNT_UNROLL = 4
BF16_TILE_ROWS = 2 * SUBLANES
ONES_ROWS = BF16_TILE_ROWS
V_ROWS = HEAD_DIM + ONES_ROWS

F32 = jnp.float32
BF16 = jnp.bfloat16


def _params(*sem):
    return pltpu.CompilerParams(dimension_semantics=sem, vmem_limit_bytes=VMEM_LIMIT)


def _rms(x, gain):
    return x * lax.rsqrt(jnp.mean(x * x, axis=-1, keepdims=True) + RMS_EPS) * gain


def _ffn_body(x_ref, g_ref, wg_ref, wu_ref, wd_ref, pg_ref, o_ref, h_ref, *, final_norm):
    j = pl.program_id(1)

    @pl.when(j == 0)
    def _():
        x = x_ref[...]
        h_ref[...] = _rms(x, g_ref[...]).astype(BF16)
        o_ref[...] = x

    h = h_ref[...]
    gate = jnp.dot(h, wg_ref[...], preferred_element_type=F32)
    up = jnp.dot(h, wu_ref[...], preferred_element_type=F32)
    act = (gate * (0.5 / (1.0 + jnp.exp(-gate))) * up).astype(BF16)
    o_ref[...] += jnp.dot(act, wd_ref[...], preferred_element_type=F32)

    if final_norm:
        @pl.when(j == pl.num_programs(1) - 1)
        def _():
            o_ref[...] = _rms(o_ref[...], pg_ref[...])


def _ffn(x, gain, w_gate_up, w_down, layer, post_gain=None, *, tm=512, tf=512):
    s, d = x.shape
    f = w_down.shape[1]
    tm = min(tm, s)
    tf = min(tf, f)
    assert s % tm == 0 and f % tf == 0
    nf = f // tf
    final_norm = post_gain is not None
    pg = post_gain if final_norm else gain
    return pl.pallas_call(
        functools.partial(_ffn_body, final_norm=final_norm),
        grid=(s // tm, nf),
        in_specs=[
            pl.BlockSpec((tm, d), lambda i, j: (i, 0)),
            pl.BlockSpec((1, d), lambda i, j: (0, 0)),
            pl.BlockSpec((None, d, tf), lambda i, j: (layer, 0, j)),
            pl.BlockSpec((None, d, tf), lambda i, j: (layer, 0, j + nf)),
            pl.BlockSpec((None, tf, d), lambda i, j: (layer, j, 0)),
            pl.BlockSpec((1, d), lambda i, j: (0, 0)),
        ],
        out_specs=pl.BlockSpec((tm, d), lambda i, j: (i, 0)),
        out_shape=jax.ShapeDtypeStruct((s, d), F32),
        scratch_shapes=[pltpu.VMEM((tm, d), BF16)],
        compiler_params=_params("parallel", "arbitrary"),
        name="ffn",
    )(x, gain.reshape(1, d), w_gate_up, w_gate_up, w_down, pg.reshape(1, d))


def _rope_heads(y, cos, sin, o_ref, scale, transposed=False):
    for h in range(y.shape[1] // HEAD_DIM):
        t = y[:, h * HEAD_DIM:(h + 1) * HEAD_DIM]
        r = t * cos + pltpu.roll(t, HEAD_DIM // 2, 1) * sin
        if scale != 1.0:
            r = r * scale
        if transposed:
            o_ref[h * HEAD_DIM:(h + 1) * HEAD_DIM, :] = r.T.astype(o_ref.dtype)
        else:
            o_ref[:, h * HEAD_DIM:(h + 1) * HEAD_DIM] = r.astype(o_ref.dtype)


def _store_value_chunks(y, o_ref):
    for b in range(y.shape[0] // CHUNK):
        for g in range(N_KV_HEADS):
            blk = y[b * CHUNK:(b + 1) * CHUNK, g * HEAD_DIM:(g + 1) * HEAD_DIM]
            o_ref[b, g * V_ROWS:g * V_ROWS + HEAD_DIM, :] = blk.T.astype(o_ref.dtype)
            o_ref[b, g * V_ROWS + HEAD_DIM:(g + 1) * V_ROWS, :] = jnp.ones((ONES_ROWS, CHUNK), o_ref.dtype)


def _rot_idx(t):
    lane = lax.broadcasted_iota(jnp.int32, t.shape, 1)
    first_half = (lane & (IDX_DIM // 2)) == 0
    return jnp.where(first_half, pltpu.roll(t, LANES - IDX_DIM // 2, 1),
                     pltpu.roll(t, IDX_DIM // 2, 1))


class _Piece(NamedTuple):
    kind: str
    start: int
    width: int
    lo: int
    hi: int
    dtype: Any


def _proj_body(*refs, pieces, q_scale, idx_w_scale):
    x_ref, g_ref, cos_ref, sin_ref, cosi_ref, sini_ref, gk_ref, w_ref = refs[:8]
    o_refs = refs[8:]
    h = _rms(x_ref[...], g_ref[...]).astype(BF16)
    products = {}
    for piece, o_ref in zip(pieces, o_refs):
        cols = (piece.start, piece.width)
        if cols not in products:
            products[cols] = jnp.dot(h, w_ref[:, piece.start:piece.start + piece.width],
                                     preferred_element_type=F32)
        y = products[cols]
        if (piece.lo, piece.hi) != (0, piece.width):
            y = y[:, piece.lo:piece.hi]
        kind = piece.kind
        if kind == "plain":
            o_ref[...] = y.astype(o_ref.dtype)
        elif kind == "scaled":
            o_ref[...] = (y * q_scale).astype(o_ref.dtype)
        elif kind == "rope":
            _rope_heads(y, cos_ref[...], sin_ref[...], o_ref, 1.0)
        elif kind == "rope_scaled_t":
            _rope_heads(y, cos_ref[...], sin_ref[...], o_ref, q_scale * LOG2_E, transposed=True)
        elif kind == "rope_idx_t":
            cosi, sini = cosi_ref[...], sini_ref[...]
            for c in range(y.shape[1] // LANES):
                t = y[:, c * LANES:(c + 1) * LANES]
                o_ref[c * LANES:(c + 1) * LANES, :] = (t * cosi + _rot_idx(t) * sini).T.astype(o_ref.dtype)
        elif kind == "value_chunks":
            _store_value_chunks(y, o_ref)
        elif kind == "kiwi":
            lane = lax.broadcasted_iota(jnp.int32, y.shape, 1)
            is_k = lane < IDX_DIM
            kk = jnp.where(is_k, y, 0.0)
            ms = jnp.sum(kk * kk, axis=-1, keepdims=True) * (1.0 / IDX_DIM)
            kn = kk * lax.rsqrt(ms + RMS_EPS) * gk_ref[...]
            kr = kn * cosi_ref[...] + _rot_idx(kn) * sini_ref[...]
            o_ref[...] = jnp.where(is_k, kr, y * idx_w_scale)
        else:
            raise ValueError(kind)


def _layer_weight_spec(w, layer):
    if w.ndim == 2:
        return pl.BlockSpec(w.shape, lambda i: (0, 0), pipeline_mode=pl.Buffered(1))
    return pl.BlockSpec((None,) + w.shape[1:], lambda i: (layer, 0, 0), pipeline_mode=pl.Buffered(1))


def _proj(x, gain, tables, gk, w, pieces, *, layer=0, tm=256):
    s, d = x.shape
    tm = min(tm, s)
    assert s % tm == 0
    cos, sin, cosi, sini = tables
    row = lambda i: (i, 0)
    const = lambda i: (0, 0)
    in_specs = [pl.BlockSpec((tm, d), row), pl.BlockSpec((1, d), const)]
    in_specs += [pl.BlockSpec((tm, LANES), row)] * 4
    in_specs += [pl.BlockSpec((1, LANES), const)]
    in_specs += [_layer_weight_spec(w, layer)]
    out_specs, out_shape = [], []
    for p in pieces:
        assert p.start % LANES == 0 and p.width % LANES == 0 and p.start + p.width <= w.shape[-1]
        n = p.hi - p.lo
        if p.kind.endswith("_t"):
            out_specs.append(pl.BlockSpec((n, tm), lambda i: (0, i)))
            out_shape.append(jax.ShapeDtypeStruct((n, s), p.dtype))
        elif p.kind == "value_chunks":
            assert tm % CHUNK == 0 and n == N_KV_HEADS * HEAD_DIM
            out_specs.append(pl.BlockSpec((tm // CHUNK, N_KV_HEADS * V_ROWS, CHUNK), lambda i: (i, 0, 0)))
            out_shape.append(jax.ShapeDtypeStruct((s // CHUNK, N_KV_HEADS * V_ROWS, CHUNK), p.dtype))
        else:
            out_specs.append(pl.BlockSpec((tm, n), row))
            out_shape.append(jax.ShapeDtypeStruct((s, n), p.dtype))
    return pl.pallas_call(
        functools.partial(_proj_body, pieces=tuple(pieces), q_scale=HEAD_DIM ** -0.5,
                          idx_w_scale=(IDX_HEADS ** -0.5) * (IDX_DIM ** -0.5)),
        grid=(s // tm,),
        in_specs=in_specs,
        out_specs=out_specs,
        out_shape=out_shape,
        compiler_params=_params("parallel"),
        name="proj",
    )(x, gain.reshape(1, d), cos, sin, cosi, sini, gk, w)


def _kv_shared_body(x_ref, g_ref, cos_ref, sin_ref, wk_ref, wv_ref, k_ref, v_ref, km_ref):
    h = _rms(x_ref[...], g_ref[...]).astype(BF16)
    yk = jnp.dot(h, wk_ref[...], preferred_element_type=F32)
    cos, sin = cos_ref[...], sin_ref[...]
    tm = yk.shape[0]
    first_blk = pl.program_id(0) * (tm // MOBA_BLOCK)
    lane = lax.broadcasted_iota(jnp.int32, (MOBA_BLOCK, HEAD_DIM), 1)
    for hd in range(yk.shape[1] // HEAD_DIM):
        t = yk[:, hd * HEAD_DIM:(hd + 1) * HEAD_DIM]
        r = t * cos + pltpu.roll(t, HEAD_DIM // 2, 1) * sin
        k_ref[:, 2 * hd * HEAD_DIM:(2 * hd + 1) * HEAD_DIM] = r.astype(k_ref.dtype)
        for b in range(tm // MOBA_BLOCK):
            rows = slice(b * MOBA_BLOCK, (b + 1) * MOBA_BLOCK)
            k_ref[rows, (2 * hd + 1) * HEAD_DIM:(2 * hd + 2) * HEAD_DIM] = (
                jnp.where(lane == first_blk + b, 1.0, 0.0).astype(k_ref.dtype))
            km_ref[b, :, hd * HEAD_DIM:(hd + 1) * HEAD_DIM] = (
                jnp.sum(r[rows], axis=0, keepdims=True) * (1.0 / MOBA_BLOCK))
    _store_value_chunks(jnp.dot(h, wv_ref[...], preferred_element_type=F32), v_ref)


def _kv_shared(x, gain, cos, sin, wk, wv):
    s, d = x.shape
    tm = MOBA_BLOCK
    assert s % tm == 0 and CHUNK == MOBA_BLOCK
    n = s // tm
    nkv = wk.shape[1]
    src = lambda i: (jnp.minimum(i, n - 1), 0)
    row = lambda i: (i, 0)
    const = lambda i: (0, 0)
    return pl.pallas_call(
        _kv_shared_body,
        grid=(n + 1,),
        in_specs=[pl.BlockSpec((tm, d), src), pl.BlockSpec((1, d), const),
                  pl.BlockSpec((tm, LANES), src), pl.BlockSpec((tm, LANES), src),
                  pl.BlockSpec(wk.shape, const, pipeline_mode=pl.Buffered(1)),
                  pl.BlockSpec(wv.shape, const, pipeline_mode=pl.Buffered(1))],
        out_specs=[pl.BlockSpec((tm, 2 * nkv), row),
                   pl.BlockSpec((1, N_KV_HEADS * V_ROWS, CHUNK), lambda i: (i, 0, 0)),
                   pl.BlockSpec((1, 1, nkv), lambda i: (i, 0, 0))],
        out_shape=[jax.ShapeDtypeStruct((s + tm, 2 * nkv), BF16),
                   jax.ShapeDtypeStruct((n + 1, N_KV_HEADS * V_ROWS, CHUNK), BF16),
                   jax.ShapeDtypeStruct((n + 1, 1, nkv), F32)],
        compiler_params=_params("parallel"),
        name="kv_shared",
    )(x, gain.reshape(1, d), cos, sin, wk, wv)


def _load_q_group(qt_ref, qall_ref):
    tq = qt_ref.shape[1]
    for hh in range(GROUP):
        qall_ref[:, hh * tq:(hh + 1) * tq] = qt_ref[hh * HEAD_DIM:(hh + 1) * HEAD_DIM, :]


def _softmax_init(m_ref, acc_ref):
    m_ref[...] = jnp.full(m_ref.shape, NEG, F32)
    acc_ref[...] = jnp.zeros(acc_ref.shape, F32)


def _add_group_bias(s, b):
    tq = b.shape[1]
    return jnp.concatenate([s[:, hh * tq:(hh + 1) * tq] + b for hh in range(GROUP)], axis=1)


def _logits_pass(chunk_ids, logits_fn, s_ref, cmax_ref):
    cmax = None
    for j, c in enumerate(chunk_ids):
        s = logits_fn(c)
        s_ref[j * CHUNK:(j + 1) * CHUNK, :] = s
        cm = jnp.max(s, axis=0, keepdims=True)
        cmax = cm if cmax is None else jnp.maximum(cmax, cm)
    cmax_ref[...] = cmax


def _softmax_pass(chunk_ids, vt_ref, s_ref, cmax_ref, m_ref, acc_ref):
    m_old = m_ref[...]
    m_new = jnp.maximum(m_old, cmax_ref[...])
    alpha = jnp.exp2(m_old - m_new)
    pv = None
    for j, c in enumerate(chunk_ids):
        p = jnp.exp2(s_ref[j * CHUNK:(j + 1) * CHUNK, :] - m_new)
        d = jnp.dot(vt_ref[c], p.astype(BF16), preferred_element_type=F32)
        pv = d if pv is None else pv + d
    m_ref[...] = m_new
    acc_ref[...] = alpha * acc_ref[...] + pv


def _attend_range(n_chunks, logits_fn, vt_ref, s_refs, cmax_refs, m_ref, acc_ref, *, fuse_pairs):
    n_steps = pl.cdiv(n_chunks, CHUNKS_PER_STEP)
    last_chunk = vt_ref.shape[0] - 1

    def chunks(step):
        return [step * CHUNKS_PER_STEP + j for j in range(CHUNKS_PER_STEP)]

    def logits(step, slot):
        _logits_pass(chunks(step), logits_fn, s_refs[slot], cmax_refs[slot])

    def softmax(step, slot):
        ids = [jnp.minimum(c, last_chunk) for c in chunks(step)]
        _softmax_pass(ids, vt_ref, s_refs[slot], cmax_refs[slot], m_ref, acc_ref)

    @pl.when(n_steps > 0)
    def _():
        logits(0, 0)

    def pair(u, carry):
        t = 2 * u
        if fuse_pairs:
            @pl.when(t + 1 < n_steps)
            def _():
                logits(t + 1, 1)
                softmax(t, 0)
                logits(t + 2, 0)
                softmax(t + 1, 1)

            @pl.when(t + 1 >= n_steps)
            def _():
                softmax(t, 0)
        else:
            logits(t + 1, 1)
            softmax(t, 0)

            @pl.when(t + 1 < n_steps)
            def _():
                logits(t + 2, 0)
                softmax(t + 1, 1)

        return carry

    lax.fori_loop(0, pl.cdiv(n_steps, 2), pair, 0)


def _softmax_finish(o_ref, acc_ref):
    tq = o_ref.shape[0]
    out = acc_ref[:HEAD_DIM, :] / acc_ref[HEAD_DIM:HEAD_DIM + 1, :]
    for hh in range(GROUP):
        o_ref[:, hh * HEAD_DIM:(hh + 1) * HEAD_DIM] = out[:, hh * tq:(hh + 1) * tq].T.astype(o_ref.dtype)


def _dsa_body(qit_ref, wt_ref, ki_ref, qt_ref, k_ref, vt_ref, o_ref,
              keys_ref, gmax_ref, cand_ref, thr_ref, qall_ref, sa_ref, sb_ref, cma_ref, cmb_ref,
              m_ref, acc_ref, *, topk):
    i = pl.program_id(0)
    g = pl.program_id(1)
    tq = CHUNK
    n_chunks = i + 1
    sub = 128

    def causal(c, rows, row_off=0):
        kpos = c * CHUNK + row_off + lax.broadcasted_iota(jnp.int32, (rows, tq), 0)
        qpos = i * tq + lax.broadcasted_iota(jnp.int32, (rows, tq), 1)
        return kpos <= qpos

    @pl.when(g == 0)
    def _():
        gmax_ref[...] = jnp.full(gmax_ref.shape, INT_MIN, jnp.int32)

        def score_chunk(c, carry):
            for part in range(CHUNK // sub):
                r0 = pl.multiple_of(c * CHUNK + part * sub, sub)
                kit = ki_ref[pl.ds(r0, sub), :]
                sc = jnp.zeros((sub, tq), F32)
                for h in range(IDX_HEADS):
                    d = jnp.dot(kit, qit_ref[h * IDX_DIM:(h + 1) * IDX_DIM, :], preferred_element_type=F32)
                    sc = sc + wt_ref[h:h + 1, :] * jnp.maximum(d, 0.0)
                sc = jnp.where(sc == 0.0, 0.0, sc)
                sc = jnp.where(causal(c, sub, part * sub), sc, -jnp.inf)
                bits = pltpu.bitcast(sc, jnp.int32)
                key = bits ^ ((bits >> 31) & INT_MAX)
                keys_ref[pl.ds(r0, sub), :] = key
                gsl = slice(part * sub, (part + 1) * sub)
                gmax_ref[gsl, :] = jnp.maximum(gmax_ref[gsl, :], key)
            return carry

        def score_pair(u, carry):
            score_chunk(2 * u, carry)
            return score_chunk(2 * u + 1, carry)

        lax.fori_loop(0, n_chunks // 2, score_pair, 0)

        @pl.when(n_chunks % 2 == 1)
        def _():
            score_chunk(n_chunks - 1, 0)

        def count_rows(load, n_blocks, t):
            def count_block(b, acc):
                ge = jnp.where(load(b) >= t, 1, 0).astype(jnp.int32)
                return acc + jnp.sum(ge.reshape(CHUNK // SUBLANES, SUBLANES, tq), axis=0)

            part = jnp.zeros((SUBLANES, tq), jnp.int32)
            if isinstance(n_blocks, int):
                for b in range(n_blocks):
                    part = count_block(b, part)
            else:
                def count_group(u, acc):
                    for j in range(COUNT_UNROLL):
                        acc = count_block(u * COUNT_UNROLL + j, acc)
                    return acc

                n_groups = n_blocks // COUNT_UNROLL
                part = lax.fori_loop(0, n_groups, count_group, part)
                part = lax.fori_loop(n_groups * COUNT_UNROLL, n_blocks, count_block, part)
            return jnp.sum(part, axis=0, keepdims=True)

        def key_chunk(c):
            return keys_ref[pl.ds(pl.multiple_of(c * CHUNK, CHUNK), CHUNK), :]

        def count_all(t):
            return count_rows(key_chunk, n_chunks, t)

        def count_cand(t):
            return count_rows(lambda b: cand_ref[b], CAND_LEVELS, t)

        def any_set(flags):
            return jnp.max(flags) > 0.0

        def bisect_pass(count_fn, base, live, state):
            lo, hi, c_lo, c_hi, done = state
            mid = (lo >> 1) + (hi >> 1) + (lo & hi & 1)
            cnt = count_fn(mid) + base
            up = live & (cnt >= topk)
            down = live & (cnt < topk)
            lo, c_lo = jnp.where(up, mid, lo), jnp.where(up, cnt, c_lo)
            hi, c_hi = jnp.where(down, mid, hi), jnp.where(down, cnt, c_hi)
            finished = live & ((cnt == topk) | (hi <= lo + 1))
            return lo, hi, c_lo, c_hi, jnp.where(finished, 1.0, done)

        def bisect_while(count_fn, base, live_fn, keep_going, state):
            def cond(carry):
                it, state = carry
                return jnp.logical_and(it < MAX_PASSES, keep_going(state))

            def body(carry):
                it, state = carry
                return it + 1, bisect_pass(count_fn, base, live_fn(state), state)

            return lax.while_loop(cond, body, (jnp.int32(0), state))[1]

        def in_bracket(state):
            return state[2] - state[3]

        def active(state):
            return state[4] == 0.0

        gm = gmax_ref[...]
        lo = jnp.min(gm, axis=0, keepdims=True)
        top = jnp.max(gm, axis=0, keepdims=True)
        hi = jnp.where(top == INT_MAX, top, top + 1)
        c_lo = count_all(lo)
        state = (lo, hi, c_lo, jnp.zeros_like(c_lo), (c_lo == topk).astype(F32))

        def wide(state):
            return any_set(jnp.where(active(state) & (in_bracket(state) > CAND_MAX), 1.0, 0.0))

        state = bisect_while(count_all, 0, active, wide, state)

        lo, hi, c_lo, c_hi, _ = state
        cand_ref[...] = jnp.full(cand_ref.shape, INT_MIN, jnp.int32)

        def capture(c, carry):
            blk = key_chunk(c)
            x = jnp.where((blk >= lo) & (blk < hi), blk, INT_MIN)
            for level in range(CAND_LEVELS):
                held = cand_ref[level]
                cand_ref[level] = jnp.maximum(held, x)
                x = jnp.minimum(held, x)
            return carry

        lax.fori_loop(0, n_chunks, capture, 0)
        captured = count_cand(lo) == in_bracket(state)

        def live_captured(state):
            return active(state) & captured

        state = bisect_while(count_cand, c_hi, live_captured,
                             lambda st: any_set(jnp.where(live_captured(st), 1.0, 0.0)), state)

        state = bisect_while(count_all, 0, active,
                             lambda st: any_set(jnp.where(active(st), 1.0, 0.0)), state)
        thr = state[0]
        thr_ref[...] = thr

        surplus = state[2] - topk

        @pl.when(jnp.max(surplus.astype(F32)) > 0.0)
        def _():
            row = lax.broadcasted_iota(jnp.int32, (CHUNK, CHUNK), 0)
            col = lax.broadcasted_iota(jnp.int32, (CHUNK, CHUNK), 1)
            prefix_ones = jnp.where(row >= col, 1.0, 0.0).astype(BF16)

            def count_equal(c, acc):
                return acc + jnp.sum(jnp.where(key_chunk(c) == thr, 1.0, 0.0), axis=0, keepdims=True)

            n_equal = lax.fori_loop(0, n_chunks, count_equal, jnp.zeros((1, tq), F32))
            keep = n_equal - surplus.astype(F32)

            def drop_late_ties(c, seen):
                blk = key_chunk(c)
                tie = blk == thr
                tie_f = jnp.where(tie, 1.0, 0.0)
                rank = seen + jnp.dot(prefix_ones, tie_f.astype(BF16), preferred_element_type=F32)
                r0 = pl.multiple_of(c * CHUNK, CHUNK)
                keys_ref[pl.ds(r0, CHUNK), :] = jnp.where(tie & (rank > keep), blk - 1, blk)
                return seen + jnp.sum(tie_f, axis=0, keepdims=True)

            lax.fori_loop(0, n_chunks, drop_late_ties, jnp.zeros((1, tq), F32))

    _load_q_group(qt_ref, qall_ref)
    _softmax_init(m_ref, acc_ref)
    thr = thr_ref[...]

    def logits(c):
        r0 = pl.multiple_of(jnp.minimum(c, n_chunks - 1) * CHUNK, CHUNK)
        sel = (keys_ref[pl.ds(r0, CHUNK), :] >= thr) & causal(c, CHUNK)
        s = jnp.dot(k_ref[pl.ds(r0, CHUNK), :], qall_ref[...], preferred_element_type=F32)
        return _add_group_bias(s, jnp.where(sel, 0.0, NEG))

    _attend_range(n_chunks, logits, vt_ref, (sa_ref, sb_ref), (cma_ref, cmb_ref), m_ref, acc_ref,
                  fuse_pairs=True)
    _softmax_finish(o_ref, acc_ref)


def _dsa(qit, wt, ki, qt, k, vtt, topk):
    s = ki.shape[0]
    tq = CHUNK
    assert s % (tq * CHUNKS_PER_STEP) == 0 and topk <= CHUNK
    gw = GROUP * HEAD_DIM
    return pl.pallas_call(
        functools.partial(_dsa_body, topk=topk),
        grid=(s // tq, N_KV_HEADS),
        in_specs=[
            pl.BlockSpec((IDX_HEADS * IDX_DIM, tq), lambda i, g: (0, i)),
            pl.BlockSpec((IDX_HEADS, tq), lambda i, g: (0, i)),
            pl.BlockSpec((s, IDX_DIM), lambda i, g: (0, 0), pipeline_mode=pl.Buffered(1)),
            pl.BlockSpec((gw, tq), lambda i, g: (g, i)),
            pl.BlockSpec((s, HEAD_DIM), lambda i, g: (0, g)),
            pl.BlockSpec((s // tq, V_ROWS, tq), lambda i, g: (0, g, 0)),
        ],
        out_specs=pl.BlockSpec((tq, gw), lambda i, g: (i, g)),
        out_shape=jax.ShapeDtypeStruct((s, N_MAIN_HEADS * HEAD_DIM), BF16),
        scratch_shapes=[
            pltpu.VMEM((s, tq), jnp.int32),
            pltpu.VMEM((CHUNK, tq), jnp.int32),
            pltpu.VMEM((CAND_LEVELS, CHUNK, tq), jnp.int32),
            pltpu.VMEM((1, tq), jnp.int32),
            pltpu.VMEM((HEAD_DIM, GROUP * tq), BF16),
            pltpu.VMEM((CHUNKS_PER_STEP * CHUNK, GROUP * tq), F32),
            pltpu.VMEM((CHUNKS_PER_STEP * CHUNK, GROUP * tq), F32),
            pltpu.VMEM((1, GROUP * tq), F32),
            pltpu.VMEM((1, GROUP * tq), F32),
            pltpu.VMEM((1, GROUP * tq), F32),
            pltpu.VMEM((V_ROWS, GROUP * tq), F32),
        ],
        compiler_params=_params("arbitrary", "arbitrary"),
        name="dsa",
    )(qit, wt, ki, qt, k, vtt)


def _moba_body(qt_ref, k_ref, vt_ref, km_ref, o_ref, qaug_ref, sa_ref, sb_ref, cma_ref, cmb_ref,
               m_ref, acc_ref, *, n_sel):
    cur = pl.program_id(1)
    tq = CHUNK
    nb = km_ref.shape[0]
    width = GROUP * tq
    for hh in range(GROUP):
        qaug_ref[:HEAD_DIM, hh * tq:(hh + 1) * tq] = qt_ref[hh * HEAD_DIM:(hh + 1) * HEAD_DIM, :]
    q_all = qaug_ref[:HEAD_DIM, :]
    _softmax_init(m_ref, acc_ref)

    tri = (lax.broadcasted_iota(jnp.int32, (tq, tq), 0) <= lax.broadcasted_iota(jnp.int32, (tq, tq), 1))

    def own_logits(c):
        r0 = pl.multiple_of(c * CHUNK, CHUNK)
        s = jnp.dot(k_ref[pl.ds(r0, CHUNK), :HEAD_DIM], q_all, preferred_element_type=F32)
        return _add_group_bias(s, jnp.where(tri, 0.0, NEG))

    _logits_pass([cur], own_logits, sa_ref, cma_ref)
    _softmax_pass([cur], vt_ref, sa_ref, cma_ref, m_ref, acc_ref)

    blk_id = lax.broadcasted_iota(jnp.int32, (nb, width), 0)
    past = blk_id < cur
    gate = jnp.dot(km_ref[...], q_all, preferred_element_type=F32)
    gate = jnp.where(past, gate, -jnp.inf)
    chosen = jnp.zeros((nb, width), jnp.bool_)
    for _ in range(n_sel):
        best = jnp.max(gate, axis=0, keepdims=True)
        first = jnp.min(jnp.where(gate == best, blk_id, nb), axis=0, keepdims=True)
        pick = blk_id == first
        chosen = chosen | pick
        gate = jnp.where(pick, -jnp.inf, gate)
    qaug_ref[HEAD_DIM:HEAD_DIM + nb, :] = jnp.where(chosen & past, 0.0, NEG).astype(BF16)
    pad = BF16_TILE_ROWS
    qaug_ref[HEAD_DIM + nb:HEAD_DIM + nb + pad, :] = jnp.full((pad, width), NEG, BF16)
    if nb + pad < HEAD_DIM:
        qaug_ref[HEAD_DIM + nb + pad:, :] = jnp.zeros((HEAD_DIM - nb - pad, width), BF16)

    state = (vt_ref, (sa_ref, sb_ref), (cma_ref, cmb_ref), m_ref, acc_ref)

    def past_logits(c):
        r0 = pl.multiple_of(jnp.where(c < cur, c, nb) * CHUNK, CHUNK)
        return jnp.dot(k_ref[pl.ds(r0, CHUNK), :], qaug_ref[...], preferred_element_type=F32)

    _attend_range(cur, past_logits, *state, fuse_pairs=False)
    _softmax_finish(o_ref, acc_ref)


def _moba(qt, k_aug, vtt, kmeans, n_sel):
    s = k_aug.shape[0] - CHUNK
    assert MOBA_BLOCK == CHUNK and s % (CHUNK * CHUNKS_PER_STEP) == 0
    tq = CHUNK
    nb = s // tq
    assert nb + BF16_TILE_ROWS <= HEAD_DIM and vtt.shape[0] == nb + 1
    gw = GROUP * HEAD_DIM
    return pl.pallas_call(
        functools.partial(_moba_body, n_sel=n_sel),
        grid=(N_KV_HEADS, nb),
        in_specs=[
            pl.BlockSpec((gw, tq), lambda g, i: (g, i)),
            pl.BlockSpec((s + CHUNK, 2 * HEAD_DIM), lambda g, i: (0, g)),
            pl.BlockSpec((nb + 1, V_ROWS, tq), lambda g, i: (0, g, 0)),
            pl.BlockSpec((nb, HEAD_DIM), lambda g, i: (0, g)),
        ],
        out_specs=pl.BlockSpec((tq, gw), lambda g, i: (i, g)),
        out_shape=jax.ShapeDtypeStruct((s, N_MAIN_HEADS * HEAD_DIM), BF16),
        scratch_shapes=[
            pltpu.VMEM((2 * HEAD_DIM, GROUP * tq), BF16),
            pltpu.VMEM((CHUNKS_PER_STEP * CHUNK, GROUP * tq), F32),
            pltpu.VMEM((CHUNKS_PER_STEP * CHUNK, GROUP * tq), F32),
            pltpu.VMEM((1, GROUP * tq), F32),
            pltpu.VMEM((1, GROUP * tq), F32),
            pltpu.VMEM((1, GROUP * tq), F32),
            pltpu.VMEM((V_ROWS, GROUP * tq), F32),
        ],
        compiler_params=_params("parallel", "arbitrary"),
        name="moba",
    )(qt, k_aug, vtt, kmeans)


def _mem_attn_body(q_ref, kt_ref, v_ref, o_ref):
    for h in range(N_MEM_HEADS):
        sl = slice(h * HEAD_DIM, (h + 1) * HEAD_DIM)
        s = jnp.dot(q_ref[:, sl], kt_ref[sl, :], preferred_element_type=F32)
        p = jnp.exp(s - jnp.max(s, axis=-1, keepdims=True))
        o = jnp.dot(p.astype(BF16), v_ref[:, sl], preferred_element_type=F32)
        o_ref[:, sl] = (o / jnp.sum(p, axis=-1, keepdims=True)).astype(o_ref.dtype)


def _mem_attn(qm, kmt, vm, *, tq=512):
    s, w = qm.shape
    m = vm.shape[0]
    tq = min(tq, s)
    assert s % tq == 0
    return pl.pallas_call(
        _mem_attn_body,
        grid=(s // tq,),
        in_specs=[pl.BlockSpec((tq, w), lambda i: (i, 0)),
                  pl.BlockSpec((w, m), lambda i: (0, 0)),
                  pl.BlockSpec((m, w), lambda i: (0, 0))],
        out_specs=pl.BlockSpec((tq, w), lambda i: (i, 0)),
        out_shape=jax.ShapeDtypeStruct((s, w), BF16),
        compiler_params=_params("parallel"),
        name="mem_attn",
    )(qm, kmt, vm)


def _out_proj_body(x_ref, om_ref, oq_ref, w_ref, o_ref):
    n_main = om_ref.shape[1]
    o_ref[...] = (x_ref[...]
                  + jnp.dot(om_ref[...], w_ref[:n_main, :], preferred_element_type=F32)
                  + jnp.dot(oq_ref[...], w_ref[n_main:, :], preferred_element_type=F32))


def _out_proj(x, o_main, o_mem, w, *, layer=0, tm=512):
    s, d = x.shape
    tm = min(tm, s)
    assert s % tm == 0 and w.shape[-2] == o_main.shape[1] + o_mem.shape[1]
    row = lambda i: (i, 0)
    const = lambda i: (0, 0)
    return pl.pallas_call(
        _out_proj_body,
        grid=(s // tm,),
        in_specs=[pl.BlockSpec((tm, d), row),
                  pl.BlockSpec((tm, o_main.shape[1]), row),
                  pl.BlockSpec((tm, o_mem.shape[1]), row),
                  _layer_weight_spec(w, layer)],
        out_specs=pl.BlockSpec((tm, d), row),
        out_shape=jax.ShapeDtypeStruct((s, d), F32),
        compiler_params=_params("parallel"),
        name="out_proj",
    )(x, o_main, o_mem, w)


def _rope_tables(positions, dim):
    inv = 1.0 / (ROPE_THETA ** (jnp.arange(0, dim, 2, dtype=F32) / dim))
    ang = positions.astype(F32)[:, None] * inv
    c, s = jnp.cos(ang), jnp.sin(ang)
    reps = LANES // dim
    return jnp.tile(jnp.concatenate([c, c], -1), (1, reps)), jnp.tile(jnp.concatenate([-s, s], -1), (1, reps))


def kernel(x, mem, positions, ffn1_norm, ffn1_w_gate_up, ffn1_w_down, attn_norm, mem_norm, a_w_in, idx_k_norm, b_w_in, w_mem_kv, w_out, ffn2_norm, ffn2_w_gate_up, ffn2_w_down, kv_norm, w_kv_shared, final_norm):
    b, s, d = x.shape
    assert b == 1 and mem.shape[0] == 1
    depth = ffn1_norm.shape[0]
    n_a = a_w_in.shape[0]
    main_w = N_MAIN_HEADS * HEAD_DIM
    kv_w = N_KV_HEADS * HEAD_DIM
    idx_w = IDX_HEADS * IDX_DIM
    mem_w = N_MEM_HEADS * HEAD_DIM
    topk = min(IDX_TOPK_MAX, s // 4)
    nb = s // MOBA_BLOCK
    n_sel = min(MOBA_TOPK_MAX, max(nb - 1, 1))

    cos, sin = _rope_tables(positions[0], HEAD_DIM)
    cosi, sini = _rope_tables(positions[0], IDX_DIM)
    tables = (cos, sin, cosi, sini)
    mem_tables = tuple(t[:mem.shape[1]] for t in tables)
    no_gk = jnp.zeros((1, LANES), F32)

    ffn_w = [w.astype(BF16) for w in (ffn1_w_gate_up, ffn1_w_down, ffn2_w_gate_up, ffn2_w_down)]
    w_mem_all = w_mem_kv.astype(BF16)
    w_out_all = w_out.astype(BF16)

    xs = x[0]
    mem2 = mem[0]
    k_sh = vtt_sh = kmeans = None
    for i in range(depth):
        if i == n_a:
            wk = w_kv_shared[:, :kv_w].astype(BF16)
            wv = w_kv_shared[:, kv_w:].astype(BF16)
            k_sh, vtt_sh, km = _kv_shared(xs, kv_norm, cos, sin, wk, wv)
            kmeans = km[:nb].reshape(nb, kv_w).astype(BF16)

        last = i == depth - 1
        xs = _ffn(xs, ffn1_norm[i], ffn_w[0], ffn_w[1], i)

        mk, mv = _proj(mem2, mem_norm[i], mem_tables, no_gk, w_mem_all,
                       [_Piece("plain", 0, mem_w, 0, mem_w, BF16),
                        _Piece("plain", mem_w, mem_w, 0, mem_w, BF16)], layer=i)
        if i < n_a:
            tail = main_w + 2 * kv_w + idx_w
            kiwi_w = IDX_DIM + IDX_HEADS
            assert a_w_in.shape[2] == tail + kiwi_w + mem_w and kiwi_w <= LANES
            wa = a_w_in[i].astype(BF16)
            wa = jnp.concatenate([wa[:, :tail + kiwi_w], jnp.zeros((d, LANES - kiwi_w), BF16),
                                  wa[:, tail + kiwi_w:]], axis=1)
            gk = jnp.pad(idx_k_norm[i], (0, LANES - IDX_DIM)).reshape(1, LANES)
            qt, k, vtt, qit, kiwi, qm = _proj(
                xs, attn_norm[i], tables, gk, wa,
                [_Piece("rope_scaled_t", 0, main_w, 0, main_w, BF16),
                 _Piece("rope", main_w, kv_w, 0, kv_w, BF16),
                 _Piece("value_chunks", main_w + kv_w, kv_w, 0, kv_w, BF16),
                 _Piece("rope_idx_t", main_w + 2 * kv_w, idx_w, 0, idx_w, BF16),
                 _Piece("kiwi", tail, LANES, 0, LANES, F32),
                 _Piece("scaled", tail + LANES, mem_w, 0, mem_w, BF16)])
            ki = kiwi[:, :IDX_DIM].astype(BF16)
            wt = kiwi[:, IDX_DIM:IDX_DIM + IDX_HEADS].T
            o_main = _dsa(qit, wt, ki, qt, k, vtt, topk)
        else:
            qt, qm = _proj(xs, attn_norm[i], tables, no_gk, b_w_in[i - n_a].astype(BF16),
                           [_Piece("rope_scaled_t", 0, main_w, 0, main_w, BF16),
                            _Piece("scaled", main_w, mem_w, 0, mem_w, BF16)])
            o_main = _moba(qt, k_sh, vtt_sh, kmeans, n_sel)
        o_mem = _mem_attn(qm, mk.T, mv)
        xs = _out_proj(xs, o_main, o_mem, w_out_all, layer=i)

        xs = _ffn(xs, ffn2_norm[i], ffn_w[2], ffn_w[3], i, final_norm if last else None)
    return xs[None]
```

```python
import functools
from typing import Any, NamedTuple

import jax
import jax.numpy as jnp
import numpy as np
from jax import lax
from jax.experimental import pallas as pl
from jax.experimental.pallas import tpu as pltpu

HEAD_DIM = 128
N_MAIN_HEADS = 12
N_KV_HEADS = 4
GROUP = N_MAIN_HEADS // N_KV_HEADS
N_MEM_HEADS = 4
IDX_HEADS = 16
IDX_DIM = 64
IDX_TOPK_MAX = 256
MOBA_BLOCK = 256
MOBA_TOPK_MAX = 3
ROPE_THETA = 10000.0
RMS_EPS = 1e-6

LANES = 128
SUBLANES = 8
VMEM_LIMIT = 56 * 1024 * 1024
NEG = -1e30
LOG2_E = 1.4426950408889634
INT_MIN = -2 ** 31
INT_MAX = 2 ** 31 - 1
CHUNK = 256
CHUNKS_PER_STEP = 4
CAND_LEVELS = 2
CAND_MAX = 16
MAX_PASSES = 40
COUNT_UNROLL = 4
BF16_TILE_ROWS = 2 * SUBLANES
ONES_ROWS = BF16_TILE_ROWS
V_ROWS = HEAD_DIM + ONES_ROWS

F32 = jnp.float32
BF16 = jnp.bfloat16


def _params(*sem):
    return pltpu.CompilerParams(dimension_semantics=sem, vmem_limit_bytes=VMEM_LIMIT)


def _rms(x, gain):
    return x * lax.rsqrt(jnp.mean(x * x, axis=-1, keepdims=True) + RMS_EPS) * gain


def _ffn_body(x_ref, g_ref, wg_ref, wu_ref, wd_ref, pg_ref, o_ref, h_ref, *, final_norm):
    j = pl.program_id(1)

    @pl.when(j == 0)
    def _():
        x = x_ref[...]
        h_ref[...] = _rms(x, g_ref[...]).astype(BF16)
        o_ref[...] = x

    h = h_ref[...]
    gate = jnp.dot(h, wg_ref[...], preferred_element_type=F32)
    up = jnp.dot(h, wu_ref[...], preferred_element_type=F32)
    act = (gate * (0.5 / (1.0 + jnp.exp(-gate))) * up).astype(BF16)
    o_ref[...] += jnp.dot(act, wd_ref[...], preferred_element_type=F32)

    if final_norm:
        @pl.when(j == pl.num_programs(1) - 1)
        def _():
            o_ref[...] = _rms(o_ref[...], pg_ref[...])


def _ffn(x, gain, w_gate_up, w_down, layer, post_gain=None, *, tm=512, tf=512):
    s, d = x.shape
    f = w_down.shape[1]
    tm = min(tm, s)
    tf = min(tf, f)
    assert s % tm == 0 and f % tf == 0
    nf = f // tf
    final_norm = post_gain is not None
    pg = post_gain if final_norm else gain
    return pl.pallas_call(
        functools.partial(_ffn_body, final_norm=final_norm),
        grid=(s // tm, nf),
        in_specs=[
            pl.BlockSpec((tm, d), lambda i, j: (i, 0)),
            pl.BlockSpec((1, d), lambda i, j: (0, 0)),
            pl.BlockSpec((None, d, tf), lambda i, j: (layer, 0, j)),
            pl.BlockSpec((None, d, tf), lambda i, j: (layer, 0, j + nf)),
            pl.BlockSpec((None, tf, d), lambda i, j: (layer, j, 0)),
            pl.BlockSpec((1, d), lambda i, j: (0, 0)),
        ],
        out_specs=pl.BlockSpec((tm, d), lambda i, j: (i, 0)),
        out_shape=jax.ShapeDtypeStruct((s, d), F32),
        scratch_shapes=[pltpu.VMEM((tm, d), BF16)],
        compiler_params=_params("parallel", "arbitrary"),
        name="ffn",
    )(x, gain.reshape(1, d), w_gate_up, w_gate_up, w_down, pg.reshape(1, d))


def _rope_heads(y, cos, sin, o_ref, scale, transposed=False):
    for h in range(y.shape[1] // HEAD_DIM):
        t = y[:, h * HEAD_DIM:(h + 1) * HEAD_DIM]
        r = t * cos + pltpu.roll(t, HEAD_DIM // 2, 1) * sin
        if scale != 1.0:
            r = r * scale
        if transposed:
            o_ref[h * HEAD_DIM:(h + 1) * HEAD_DIM, :] = r.T.astype(o_ref.dtype)
        else:
            o_ref[:, h * HEAD_DIM:(h + 1) * HEAD_DIM] = r.astype(o_ref.dtype)


def _store_value_chunks(y, o_ref):
    for b in range(y.shape[0] // CHUNK):
        for g in range(N_KV_HEADS):
            blk = y[b * CHUNK:(b + 1) * CHUNK, g * HEAD_DIM:(g + 1) * HEAD_DIM]
            o_ref[b, g * V_ROWS:g * V_ROWS + HEAD_DIM, :] = blk.T.astype(o_ref.dtype)
            o_ref[b, g * V_ROWS + HEAD_DIM:(g + 1) * V_ROWS, :] = jnp.ones((ONES_ROWS, CHUNK), o_ref.dtype)


def _rot_idx(t):
    lane = lax.broadcasted_iota(jnp.int32, t.shape, 1)
    first_half = (lane & (IDX_DIM // 2)) == 0
    return jnp.where(first_half, pltpu.roll(t, LANES - IDX_DIM // 2, 1),
                     pltpu.roll(t, IDX_DIM // 2, 1))


class _Piece(NamedTuple):
    kind: str
    start: int
    width: int
    lo: int
    hi: int
    dtype: Any


def _proj_body(*refs, pieces, q_scale, idx_w_scale):
    x_ref, g_ref, cos_ref, sin_ref, cosi_ref, sini_ref, gk_ref, w_ref = refs[:8]
    o_refs = refs[8:]
    h = _rms(x_ref[...], g_ref[...]).astype(BF16)
    products = {}
    for piece, o_ref in zip(pieces, o_refs):
        cols = (piece.start, piece.width)
        if cols not in products:
            products[cols] = jnp.dot(h, w_ref[:, piece.start:piece.start + piece.width],
                                     preferred_element_type=F32)
        y = products[cols]
        if (piece.lo, piece.hi) != (0, piece.width):
            y = y[:, piece.lo:piece.hi]
        kind = piece.kind
        if kind == "plain":
            o_ref[...] = y.astype(o_ref.dtype)
        elif kind == "scaled":
            o_ref[...] = (y * q_scale).astype(o_ref.dtype)
        elif kind == "rope":
            _rope_heads(y, cos_ref[...], sin_ref[...], o_ref, 1.0)
        elif kind == "rope_scaled_t":
            _rope_heads(y, cos_ref[...], sin_ref[...], o_ref, q_scale * LOG2_E, transposed=True)
        elif kind == "rope_idx_t":
            cosi, sini = cosi_ref[...], sini_ref[...]
            for c in range(y.shape[1] // LANES):
                t = y[:, c * LANES:(c + 1) * LANES]
                o_ref[c * LANES:(c + 1) * LANES, :] = (t * cosi + _rot_idx(t) * sini).T.astype(o_ref.dtype)
        elif kind == "value_chunks":
            _store_value_chunks(y, o_ref)
        elif kind == "kiwi":
            lane = lax.broadcasted_iota(jnp.int32, y.shape, 1)
            is_k = lane < IDX_DIM
            kk = jnp.where(is_k, y, 0.0)
            ms = jnp.sum(kk * kk, axis=-1, keepdims=True) * (1.0 / IDX_DIM)
            kn = kk * lax.rsqrt(ms + RMS_EPS) * gk_ref[...]
            kr = kn * cosi_ref[...] + _rot_idx(kn) * sini_ref[...]
            o_ref[...] = jnp.where(is_k, kr, y * idx_w_scale)
        else:
            raise ValueError(kind)


def _layer_weight_spec(w, layer):
    if w.ndim == 2:
        return pl.BlockSpec(w.shape, lambda i: (0, 0), pipeline_mode=pl.Buffered(1))
    return pl.BlockSpec((None,) + w.shape[1:], lambda i: (layer, 0, 0), pipeline_mode=pl.Buffered(1))


def _proj(x, gain, tables, gk, w, pieces, *, layer=0, tm=256):
    s, d = x.shape
    tm = min(tm, s)
    assert s % tm == 0
    cos, sin, cosi, sini = tables
    row = lambda i: (i, 0)
    const = lambda i: (0, 0)
    in_specs = [pl.BlockSpec((tm, d), row), pl.BlockSpec((1, d), const)]
    in_specs += [pl.BlockSpec((tm, LANES), row)] * 4
    in_specs += [pl.BlockSpec((1, LANES), const)]
    in_specs += [_layer_weight_spec(w, layer)]
    out_specs, out_shape = [], []
    for p in pieces:
        assert p.start % LANES == 0 and p.width % LANES == 0 and p.start + p.width <= w.shape[-1]
        n = p.hi - p.lo
        if p.kind.endswith("_t"):
            out_specs.append(pl.BlockSpec((n, tm), lambda i: (0, i)))
            out_shape.append(jax.ShapeDtypeStruct((n, s), p.dtype))
        elif p.kind == "value_chunks":
            assert tm % CHUNK == 0 and n == N_KV_HEADS * HEAD_DIM
            out_specs.append(pl.BlockSpec((tm // CHUNK, N_KV_HEADS * V_ROWS, CHUNK), lambda i: (i, 0, 0)))
            out_shape.append(jax.ShapeDtypeStruct((s // CHUNK, N_KV_HEADS * V_ROWS, CHUNK), p.dtype))
        else:
            out_specs.append(pl.BlockSpec((tm, n), row))
            out_shape.append(jax.ShapeDtypeStruct((s, n), p.dtype))
    return pl.pallas_call(
        functools.partial(_proj_body, pieces=tuple(pieces), q_scale=HEAD_DIM ** -0.5,
                          idx_w_scale=(IDX_HEADS ** -0.5) * (IDX_DIM ** -0.5)),
        grid=(s // tm,),
        in_specs=in_specs,
        out_specs=out_specs,
        out_shape=out_shape,
        compiler_params=_params("parallel"),
        name="proj",
    )(x, gain.reshape(1, d), cos, sin, cosi, sini, gk, w)


def _kv_shared_body(x_ref, g_ref, cos_ref, sin_ref, wk_ref, wv_ref, k_ref, v_ref, km_ref):
    h = _rms(x_ref[...], g_ref[...]).astype(BF16)
    yk = jnp.dot(h, wk_ref[...], preferred_element_type=F32)
    cos, sin = cos_ref[...], sin_ref[...]
    tm = yk.shape[0]
    first_blk = pl.program_id(0) * (tm // MOBA_BLOCK)
    lane = lax.broadcasted_iota(jnp.int32, (MOBA_BLOCK, HEAD_DIM), 1)
    for hd in range(yk.shape[1] // HEAD_DIM):
        t = yk[:, hd * HEAD_DIM:(hd + 1) * HEAD_DIM]
        r = t * cos + pltpu.roll(t, HEAD_DIM // 2, 1) * sin
        k_ref[:, 2 * hd * HEAD_DIM:(2 * hd + 1) * HEAD_DIM] = r.astype(k_ref.dtype)
        for b in range(tm // MOBA_BLOCK):
            rows = slice(b * MOBA_BLOCK, (b + 1) * MOBA_BLOCK)
            k_ref[rows, (2 * hd + 1) * HEAD_DIM:(2 * hd + 2) * HEAD_DIM] = (
                jnp.where(lane == first_blk + b, 1.0, 0.0).astype(k_ref.dtype))
            km_ref[b, :, hd * HEAD_DIM:(hd + 1) * HEAD_DIM] = (
                jnp.sum(r[rows], axis=0, keepdims=True) * (1.0 / MOBA_BLOCK))
    _store_value_chunks(jnp.dot(h, wv_ref[...], preferred_element_type=F32), v_ref)


def _kv_shared(x, gain, cos, sin, wk, wv):
    s, d = x.shape
    tm = MOBA_BLOCK
    assert s % tm == 0 and CHUNK == MOBA_BLOCK
    n = s // tm
    nkv = wk.shape[1]
    src = lambda i: (jnp.minimum(i, n - 1), 0)
    row = lambda i: (i, 0)
    const = lambda i: (0, 0)
    return pl.pallas_call(
        _kv_shared_body,
        grid=(n + 1,),
        in_specs=[pl.BlockSpec((tm, d), src), pl.BlockSpec((1, d), const),
                  pl.BlockSpec((tm, LANES), src), pl.BlockSpec((tm, LANES), src),
                  pl.BlockSpec(wk.shape, const, pipeline_mode=pl.Buffered(1)),
                  pl.BlockSpec(wv.shape, const, pipeline_mode=pl.Buffered(1))],
        out_specs=[pl.BlockSpec((tm, 2 * nkv), row),
                   pl.BlockSpec((1, N_KV_HEADS * V_ROWS, CHUNK), lambda i: (i, 0, 0)),
                   pl.BlockSpec((1, 1, nkv), lambda i: (i, 0, 0))],
        out_shape=[jax.ShapeDtypeStruct((s + tm, 2 * nkv), BF16),
                   jax.ShapeDtypeStruct((n + 1, N_KV_HEADS * V_ROWS, CHUNK), BF16),
                   jax.ShapeDtypeStruct((n + 1, 1, nkv), F32)],
        compiler_params=_params("parallel"),
        name="kv_shared",
    )(x, gain.reshape(1, d), cos, sin, wk, wv)


def _load_q_group(qt_ref, qall_ref):
    tq = qt_ref.shape[1]
    for hh in range(GROUP):
        qall_ref[:, hh * tq:(hh + 1) * tq] = qt_ref[hh * HEAD_DIM:(hh + 1) * HEAD_DIM, :]


def _softmax_init(m_ref, acc_ref):
    m_ref[...] = jnp.full(m_ref.shape, NEG, F32)
    acc_ref[...] = jnp.zeros(acc_ref.shape, F32)


def _add_group_bias(s, b):
    tq = b.shape[1]
    return jnp.concatenate([s[:, hh * tq:(hh + 1) * tq] + b for hh in range(GROUP)], axis=1)


def _logits_pass(chunk_ids, logits_fn, s_ref, cmax_ref):
    cmax = None
    for j, c in enumerate(chunk_ids):
        s = logits_fn(c)
        s_ref[j * CHUNK:(j + 1) * CHUNK, :] = s
        cm = jnp.max(s, axis=0, keepdims=True)
        cmax = cm if cmax is None else jnp.maximum(cmax, cm)
    cmax_ref[...] = cmax


def _softmax_pass(chunk_ids, vt_ref, s_ref, cmax_ref, m_ref, acc_ref):
    m_old = m_ref[...]
    m_new = jnp.maximum(m_old, cmax_ref[...])
    alpha = jnp.exp2(m_old - m_new)
    pv = None
    for j, c in enumerate(chunk_ids):
        p = jnp.exp2(s_ref[j * CHUNK:(j + 1) * CHUNK, :] - m_new)
        d = jnp.dot(vt_ref[c], p.astype(BF16), preferred_element_type=F32)
        pv = d if pv is None else pv + d
    m_ref[...] = m_new
    acc_ref[...] = alpha * acc_ref[...] + pv


def _attend_range(n_chunks, logits_fn, vt_ref, s_refs, cmax_refs, m_ref, acc_ref, *, fuse_pairs):
    n_steps = pl.cdiv(n_chunks, CHUNKS_PER_STEP)
    last_chunk = vt_ref.shape[0] - 1

    def chunks(step):
        return [step * CHUNKS_PER_STEP + j for j in range(CHUNKS_PER_STEP)]

    def logits(step, slot):
        _logits_pass(chunks(step), logits_fn, s_refs[slot], cmax_refs[slot])

    def softmax(step, slot):
        ids = [jnp.minimum(c, last_chunk) for c in chunks(step)]
        _softmax_pass(ids, vt_ref, s_refs[slot], cmax_refs[slot], m_ref, acc_ref)

    @pl.when(n_steps > 0)
    def _():
        logits(0, 0)

    def pair(u, carry):
        t = 2 * u
        if fuse_pairs:
            @pl.when(t + 1 < n_steps)
            def _():
                logits(t + 1, 1)
                softmax(t, 0)
                logits(t + 2, 0)
                softmax(t + 1, 1)

            @pl.when(t + 1 >= n_steps)
            def _():
                softmax(t, 0)
        else:
            logits(t + 1, 1)
            softmax(t, 0)

            @pl.when(t + 1 < n_steps)
            def _():
                logits(t + 2, 0)
                softmax(t + 1, 1)

        return carry

    lax.fori_loop(0, pl.cdiv(n_steps, 2), pair, 0)


def _softmax_finish(o_ref, acc_ref):
    tq = o_ref.shape[0]
    out = acc_ref[:HEAD_DIM, :] / acc_ref[HEAD_DIM:HEAD_DIM + 1, :]
    for hh in range(GROUP):
        o_ref[:, hh * HEAD_DIM:(hh + 1) * HEAD_DIM] = out[:, hh * tq:(hh + 1) * tq].T.astype(o_ref.dtype)


def _dsa_body(qit_ref, wt_ref, ki_ref, qt_ref, k_ref, vt_ref, o_ref,
              keys_ref, gmax_ref, cand_ref, qall_ref, sa_ref, sb_ref, cma_ref, cmb_ref,
              m_ref, acc_ref, *, topk):
    i = pl.program_id(0)
    g = pl.program_id(1)
    tq = CHUNK
    n_chunks = i + 1
    last_chunk = keys_ref.shape[0] // CHUNK - 1
    sub = 128

    def causal(c, rows, row_off=0):
        kpos = c * CHUNK + row_off + lax.broadcasted_iota(jnp.int32, (rows, tq), 0)
        qpos = i * tq + lax.broadcasted_iota(jnp.int32, (rows, tq), 1)
        return kpos <= qpos

    @pl.when(g == 0)
    def _():
        gmax_ref[...] = jnp.full(gmax_ref.shape, INT_MIN, jnp.int32)

        def score_chunk(c, carry):
            for part in range(CHUNK // sub):
                r0 = pl.multiple_of(c * CHUNK + part * sub, sub)
                kit = ki_ref[pl.ds(r0, sub), :]
                sc = jnp.zeros((sub, tq), F32)
                for h in range(IDX_HEADS):
                    d = jnp.dot(kit, qit_ref[h * IDX_DIM:(h + 1) * IDX_DIM, :], preferred_element_type=F32)
                    sc = sc + wt_ref[h:h + 1, :] * jnp.maximum(d, 0.0)
                sc = jnp.where(sc == 0.0, 0.0, sc)
                sc = jnp.where(causal(c, sub, part * sub), sc, -jnp.inf)
                bits = pltpu.bitcast(sc, jnp.int32)
                key = bits ^ ((bits >> 31) & INT_MAX)
                keys_ref[pl.ds(r0, sub), :] = key
                gsl = slice(part * sub, (part + 1) * sub)
                gmax_ref[gsl, :] = jnp.maximum(gmax_ref[gsl, :], key)
            return carry

        def score_pair(u, carry):
            score_chunk(2 * u, carry)
            return score_chunk(2 * u + 1, carry)

        lax.fori_loop(0, n_chunks // 2, score_pair, 0)

        @pl.when(n_chunks % 2 == 1)
        def _():
            score_chunk(n_chunks - 1, 0)

        def count_rows(load, n_blocks, t):
            def count_block(b, acc):
                ge = jnp.where(load(b) >= t, 1, 0).astype(jnp.int32)
                return acc + jnp.sum(ge.reshape(CHUNK // SUBLANES, SUBLANES, tq), axis=0)

            part = jnp.zeros((SUBLANES, tq), jnp.int32)
            if isinstance(n_blocks, int):
                for b in range(n_blocks):
                    part = count_block(b, part)
            else:
                def count_group(u, acc):
                    for j in range(COUNT_UNROLL):
                        acc = count_block(u * COUNT_UNROLL + j, acc)
                    return acc

                n_groups = n_blocks // COUNT_UNROLL
                part = lax.fori_loop(0, n_groups, count_group, part)
                part = lax.fori_loop(n_groups * COUNT_UNROLL, n_blocks, count_block, part)
            return jnp.sum(part, axis=0, keepdims=True)

        def key_chunk(c):
            return keys_ref[pl.ds(pl.multiple_of(c * CHUNK, CHUNK), CHUNK), :]

        def count_all(t):
            return count_rows(key_chunk, n_chunks, t)

        def count_cand(t):
            return count_rows(lambda b: cand_ref[b], CAND_LEVELS, t)

        def any_set(flags):
            return jnp.max(flags) > 0.0

        def bisect_pass(count_fn, base, live, state):
            lo, hi, c_lo, c_hi, done = state
            mid = (lo >> 1) + (hi >> 1) + (lo & hi & 1)
            cnt = count_fn(mid) + base
            up = live & (cnt >= topk)
            down = live & (cnt < topk)
            lo, c_lo = jnp.where(up, mid, lo), jnp.where(up, cnt, c_lo)
            hi, c_hi = jnp.where(down, mid, hi), jnp.where(down, cnt, c_hi)
            finished = live & ((cnt == topk) | (hi <= lo + 1))
            return lo, hi, c_lo, c_hi, jnp.where(finished, 1.0, done)

        def bisect_while(count_fn, base, live_fn, keep_going, state):
            def cond(carry):
                it, state = carry
                return jnp.logical_and(it < MAX_PASSES, keep_going(state))

            def body(carry):
                it, state = carry
                return it + 1, bisect_pass(count_fn, base, live_fn(state), state)

            return lax.while_loop(cond, body, (jnp.int32(0), state))[1]

        def in_bracket(state):
            return state[2] - state[3]

        def active(state):
            return state[4] == 0.0

        gm = gmax_ref[...]
        lo = jnp.min(gm, axis=0, keepdims=True)
        top = jnp.max(gm, axis=0, keepdims=True)
        hi = jnp.where(top == INT_MAX, top, top + 1)
        c_lo = count_all(lo)
        state = (lo, hi, c_lo, jnp.zeros_like(c_lo), (c_lo == topk).astype(F32))

        def wide(state):
            return any_set(jnp.where(active(state) & (in_bracket(state) > CAND_MAX), 1.0, 0.0))

        state = bisect_while(count_all, 0, active, wide, state)

        lo, hi, c_lo, c_hi, _ = state
        cand_ref[...] = jnp.full(cand_ref.shape, INT_MIN, jnp.int32)

        def capture(c, carry):
            blk = key_chunk(c)
            x = jnp.where((blk >= lo) & (blk < hi), blk, INT_MIN)
            for level in range(CAND_LEVELS):
                held = cand_ref[level]
                cand_ref[level] = jnp.maximum(held, x)
                x = jnp.minimum(held, x)
            return carry

        lax.fori_loop(0, n_chunks, capture, 0)
        captured = count_cand(lo) == in_bracket(state)

        def live_captured(state):
            return active(state) & captured

        state = bisect_while(count_cand, c_hi, live_captured,
                             lambda st: any_set(jnp.where(live_captured(st), 1.0, 0.0)), state)

        state = bisect_while(count_all, 0, active,
                             lambda st: any_set(jnp.where(active(st), 1.0, 0.0)), state)
        thr = state[0]

        surplus = state[2] - topk

        @pl.when(jnp.max(surplus.astype(F32)) > 0.0)
        def _():
            row = lax.broadcasted_iota(jnp.int32, (CHUNK, CHUNK), 0)
            col = lax.broadcasted_iota(jnp.int32, (CHUNK, CHUNK), 1)
            prefix_ones = jnp.where(row >= col, 1.0, 0.0).astype(BF16)

            def count_equal(c, acc):
                return acc + jnp.sum(jnp.where(key_chunk(c) == thr, 1.0, 0.0), axis=0, keepdims=True)

            n_equal = lax.fori_loop(0, n_chunks, count_equal, jnp.zeros((1, tq), F32))
            keep = n_equal - surplus.astype(F32)

            def drop_late_ties(c, seen):
                blk = key_chunk(c)
                tie = blk == thr
                tie_f = jnp.where(tie, 1.0, 0.0)
                rank = seen + jnp.dot(prefix_ones, tie_f.astype(BF16), preferred_element_type=F32)
                r0 = pl.multiple_of(c * CHUNK, CHUNK)
                keys_ref[pl.ds(r0, CHUNK), :] = jnp.where(tie & (rank > keep), blk - 1, blk)
                return seen + jnp.sum(tie_f, axis=0, keepdims=True)

            lax.fori_loop(0, n_chunks, drop_late_ties, jnp.zeros((1, tq), F32))

        def to_bias(c, carry):
            r0 = pl.multiple_of(c * CHUNK, CHUNK)
            sel = (keys_ref[pl.ds(r0, CHUNK), :] >= thr) & causal(c, CHUNK)
            keys_ref[pl.ds(r0, CHUNK), :] = pltpu.bitcast(jnp.where(sel, 0.0, NEG), jnp.int32)
            return carry

        lax.fori_loop(0, n_chunks, to_bias, 0)

        @pl.when(n_chunks <= last_chunk)
        def _():
            r0 = pl.multiple_of(n_chunks * CHUNK, CHUNK)
            keys_ref[pl.ds(r0, CHUNK), :] = pltpu.bitcast(jnp.full((CHUNK, tq), NEG, F32), jnp.int32)

    _load_q_group(qt_ref, qall_ref)
    _softmax_init(m_ref, acc_ref)

    def logits(c):
        rb = pl.multiple_of(jnp.minimum(jnp.minimum(c, n_chunks), last_chunk) * CHUNK, CHUNK)
        rk = pl.multiple_of(jnp.minimum(c, n_chunks - 1) * CHUNK, CHUNK)
        s = jnp.dot(k_ref[pl.ds(rk, CHUNK), :], qall_ref[...], preferred_element_type=F32)
        return _add_group_bias(s, pltpu.bitcast(keys_ref[pl.ds(rb, CHUNK), :], F32))

    _attend_range(n_chunks, logits, vt_ref, (sa_ref, sb_ref), (cma_ref, cmb_ref), m_ref, acc_ref,
                  fuse_pairs=True)
    _softmax_finish(o_ref, acc_ref)


def _dsa(qit, wt, ki, qt, k, vtt, topk):
    s = ki.shape[0]
    tq = CHUNK
    assert s % (tq * CHUNKS_PER_STEP) == 0 and topk <= CHUNK
    gw = GROUP * HEAD_DIM
    return pl.pallas_call(
        functools.partial(_dsa_body, topk=topk),
        grid=(s // tq, N_KV_HEADS),
        in_specs=[
            pl.BlockSpec((IDX_HEADS * IDX_DIM, tq), lambda i, g: (0, i)),
            pl.BlockSpec((IDX_HEADS, tq), lambda i, g: (0, i)),
            pl.BlockSpec((s, IDX_DIM), lambda i, g: (0, 0), pipeline_mode=pl.Buffered(1)),
            pl.BlockSpec((gw, tq), lambda i, g: (g, i)),
            pl.BlockSpec((s, HEAD_DIM), lambda i, g: (0, g)),
            pl.BlockSpec((s // tq, V_ROWS, tq), lambda i, g: (0, g, 0)),
        ],
        out_specs=pl.BlockSpec((tq, gw), lambda i, g: (i, g)),
        out_shape=jax.ShapeDtypeStruct((s, N_MAIN_HEADS * HEAD_DIM), BF16),
        scratch_shapes=[
            pltpu.VMEM((s, tq), jnp.int32),
            pltpu.VMEM((CHUNK, tq), jnp.int32),
            pltpu.VMEM((CAND_LEVELS, CHUNK, tq), jnp.int32),
            pltpu.VMEM((HEAD_DIM, GROUP * tq), BF16),
            pltpu.VMEM((CHUNKS_PER_STEP * CHUNK, GROUP * tq), F32),
            pltpu.VMEM((CHUNKS_PER_STEP * CHUNK, GROUP * tq), F32),
            pltpu.VMEM((1, GROUP * tq), F32),
            pltpu.VMEM((1, GROUP * tq), F32),
            pltpu.VMEM((1, GROUP * tq), F32),
            pltpu.VMEM((V_ROWS, GROUP * tq), F32),
        ],
        compiler_params=_params("arbitrary", "arbitrary"),
        name="dsa",
    )(qit, wt, ki, qt, k, vtt)


def _moba_body(qt_ref, k_ref, vt_ref, km_ref, o_ref, qaug_ref, sa_ref, sb_ref, cma_ref, cmb_ref,
               m_ref, acc_ref, *, n_sel):
    cur = pl.program_id(1)
    tq = CHUNK
    nb = km_ref.shape[0]
    width = GROUP * tq
    for hh in range(GROUP):
        qaug_ref[:HEAD_DIM, hh * tq:(hh + 1) * tq] = qt_ref[hh * HEAD_DIM:(hh + 1) * HEAD_DIM, :]
    q_all = qaug_ref[:HEAD_DIM, :]
    _softmax_init(m_ref, acc_ref)

    tri = (lax.broadcasted_iota(jnp.int32, (tq, tq), 0) <= lax.broadcasted_iota(jnp.int32, (tq, tq), 1))

    def own_logits(c):
        r0 = pl.multiple_of(c * CHUNK, CHUNK)
        s = jnp.dot(k_ref[pl.ds(r0, CHUNK), :HEAD_DIM], q_all, preferred_element_type=F32)
        return _add_group_bias(s, jnp.where(tri, 0.0, NEG))

    _logits_pass([cur], own_logits, sa_ref, cma_ref)
    _softmax_pass([cur], vt_ref, sa_ref, cma_ref, m_ref, acc_ref)

    blk_id = lax.broadcasted_iota(jnp.int32, (nb, width), 0)
    past = blk_id < cur
    gate = jnp.dot(km_ref[...], q_all, preferred_element_type=F32)
    gate = jnp.where(past, gate, -jnp.inf)
    chosen = jnp.zeros((nb, width), jnp.bool_)
    for _ in range(n_sel):
        best = jnp.max(gate, axis=0, keepdims=True)
        first = jnp.min(jnp.where(gate == best, blk_id, nb), axis=0, keepdims=True)
        pick = blk_id == first
        chosen = chosen | pick
        gate = jnp.where(pick, -jnp.inf, gate)
    qaug_ref[HEAD_DIM:HEAD_DIM + nb, :] = jnp.where(chosen & past, 0.0, NEG).astype(BF16)
    pad = BF16_TILE_ROWS
    qaug_ref[HEAD_DIM + nb:HEAD_DIM + nb + pad, :] = jnp.full((pad, width), NEG, BF16)
    if nb + pad < HEAD_DIM:
        qaug_ref[HEAD_DIM + nb + pad:, :] = jnp.zeros((HEAD_DIM - nb - pad, width), BF16)

    state = (vt_ref, (sa_ref, sb_ref), (cma_ref, cmb_ref), m_ref, acc_ref)

    def past_logits(c):
        r0 = pl.multiple_of(jnp.where(c < cur, c, nb) * CHUNK, CHUNK)
        return jnp.dot(k_ref[pl.ds(r0, CHUNK), :], qaug_ref[...], preferred_element_type=F32)

    _attend_range(cur, past_logits, *state, fuse_pairs=False)
    _softmax_finish(o_ref, acc_ref)


def _moba(qt, k_aug, vtt, kmeans, n_sel):
    s = k_aug.shape[0] - CHUNK
    assert MOBA_BLOCK == CHUNK and s % (CHUNK * CHUNKS_PER_STEP) == 0
    tq = CHUNK
    nb = s // tq
    assert nb + BF16_TILE_ROWS <= HEAD_DIM and vtt.shape[0] == nb + 1
    gw = GROUP * HEAD_DIM
    return pl.pallas_call(
        functools.partial(_moba_body, n_sel=n_sel),
        grid=(N_KV_HEADS, nb),
        in_specs=[
            pl.BlockSpec((gw, tq), lambda g, i: (g, i)),
            pl.BlockSpec((s + CHUNK, 2 * HEAD_DIM), lambda g, i: (0, g)),
            pl.BlockSpec((nb + 1, V_ROWS, tq), lambda g, i: (0, g, 0)),
            pl.BlockSpec((nb, HEAD_DIM), lambda g, i: (0, g)),
        ],
        out_specs=pl.BlockSpec((tq, gw), lambda g, i: (i, g)),
        out_shape=jax.ShapeDtypeStruct((s, N_MAIN_HEADS * HEAD_DIM), BF16),
        scratch_shapes=[
            pltpu.VMEM((2 * HEAD_DIM, GROUP * tq), BF16),
            pltpu.VMEM((CHUNKS_PER_STEP * CHUNK, GROUP * tq), F32),
            pltpu.VMEM((CHUNKS_PER_STEP * CHUNK, GROUP * tq), F32),
            pltpu.VMEM((1, GROUP * tq), F32),
            pltpu.VMEM((1, GROUP * tq), F32),
            pltpu.VMEM((1, GROUP * tq), F32),
            pltpu.VMEM((V_ROWS, GROUP * tq), F32),
        ],
        compiler_params=_params("parallel", "arbitrary"),
        name="moba",
    )(qt, k_aug, vtt, kmeans)


def _mem_attn_body(q_ref, kt_ref, v_ref, o_ref):
    for h in range(N_MEM_HEADS):
        sl = slice(h * HEAD_DIM, (h + 1) * HEAD_DIM)
        s = jnp.dot(q_ref[:, sl], kt_ref[sl, :], preferred_element_type=F32)
        p = jnp.exp(s - jnp.max(s, axis=-1, keepdims=True))
        o = jnp.dot(p.astype(BF16), v_ref[:, sl], preferred_element_type=F32)
        o_ref[:, sl] = (o / jnp.sum(p, axis=-1, keepdims=True)).astype(o_ref.dtype)


def _mem_attn(qm, kmt, vm, *, tq=512):
    s, w = qm.shape
    m = vm.shape[0]
    tq = min(tq, s)
    assert s % tq == 0
    return pl.pallas_call(
        _mem_attn_body,
        grid=(s // tq,),
        in_specs=[pl.BlockSpec((tq, w), lambda i: (i, 0)),
                  pl.BlockSpec((w, m), lambda i: (0, 0)),
                  pl.BlockSpec((m, w), lambda i: (0, 0))],
        out_specs=pl.BlockSpec((tq, w), lambda i: (i, 0)),
        out_shape=jax.ShapeDtypeStruct((s, w), BF16),
        compiler_params=_params("parallel"),
        name="mem_attn",
    )(qm, kmt, vm)


def _out_proj_body(x_ref, om_ref, oq_ref, w_ref, o_ref):
    n_main = om_ref.shape[1]
    o_ref[...] = (x_ref[...]
                  + jnp.dot(om_ref[...], w_ref[:n_main, :], preferred_element_type=F32)
                  + jnp.dot(oq_ref[...], w_ref[n_main:, :], preferred_element_type=F32))


def _out_proj(x, o_main, o_mem, w, *, layer=0, tm=512):
    s, d = x.shape
    tm = min(tm, s)
    assert s % tm == 0 and w.shape[-2] == o_main.shape[1] + o_mem.shape[1]
    row = lambda i: (i, 0)
    const = lambda i: (0, 0)
    return pl.pallas_call(
        _out_proj_body,
        grid=(s // tm,),
        in_specs=[pl.BlockSpec((tm, d), row),
                  pl.BlockSpec((tm, o_main.shape[1]), row),
                  pl.BlockSpec((tm, o_mem.shape[1]), row),
                  _layer_weight_spec(w, layer)],
        out_specs=pl.BlockSpec((tm, d), row),
        out_shape=jax.ShapeDtypeStruct((s, d), F32),
        compiler_params=_params("parallel"),
        name="out_proj",
    )(x, o_main, o_mem, w)


def _rope_tables(positions, dim):
    inv = 1.0 / (ROPE_THETA ** (jnp.arange(0, dim, 2, dtype=F32) / dim))
    ang = positions.astype(F32)[:, None] * inv
    c, s = jnp.cos(ang), jnp.sin(ang)
    reps = LANES // dim
    return jnp.tile(jnp.concatenate([c, c], -1), (1, reps)), jnp.tile(jnp.concatenate([-s, s], -1), (1, reps))


def kernel(x, mem, positions, ffn1_norm, ffn1_w_gate_up, ffn1_w_down, attn_norm, mem_norm, a_w_in, idx_k_norm, b_w_in, w_mem_kv, w_out, ffn2_norm, ffn2_w_gate_up, ffn2_w_down, kv_norm, w_kv_shared, final_norm):
    b, s, d = x.shape
    assert b == 1 and mem.shape[0] == 1
    depth = ffn1_norm.shape[0]
    n_a = a_w_in.shape[0]
    main_w = N_MAIN_HEADS * HEAD_DIM
    kv_w = N_KV_HEADS * HEAD_DIM
    idx_w = IDX_HEADS * IDX_DIM
    mem_w = N_MEM_HEADS * HEAD_DIM
    topk = min(IDX_TOPK_MAX, s // 4)
    nb = s // MOBA_BLOCK
    n_sel = min(MOBA_TOPK_MAX, max(nb - 1, 1))

    cos, sin = _rope_tables(positions[0], HEAD_DIM)
    cosi, sini = _rope_tables(positions[0], IDX_DIM)
    tables = (cos, sin, cosi, sini)
    mem_tables = tuple(t[:mem.shape[1]] for t in tables)
    no_gk = jnp.zeros((1, LANES), F32)

    ffn_w = [w.astype(BF16) for w in (ffn1_w_gate_up, ffn1_w_down, ffn2_w_gate_up, ffn2_w_down)]
    w_mem_all = w_mem_kv.astype(BF16)
    w_out_all = w_out.astype(BF16)

    xs = x[0]
    mem2 = mem[0]
    k_sh = vtt_sh = kmeans = None
    for i in range(depth):
        if i == n_a:
            wk = w_kv_shared[:, :kv_w].astype(BF16)
            wv = w_kv_shared[:, kv_w:].astype(BF16)
            k_sh, vtt_sh, km = _kv_shared(xs, kv_norm, cos, sin, wk, wv)
            kmeans = km[:nb].reshape(nb, kv_w).astype(BF16)

        last = i == depth - 1
        xs = _ffn(xs, ffn1_norm[i], ffn_w[0], ffn_w[1], i)

        mk, mv = _proj(mem2, mem_norm[i], mem_tables, no_gk, w_mem_all,
                       [_Piece("plain", 0, mem_w, 0, mem_w, BF16),
                        _Piece("plain", mem_w, mem_w, 0, mem_w, BF16)], layer=i)
        if i < n_a:
            tail = main_w + 2 * kv_w + idx_w
            kiwi_w = IDX_DIM + IDX_HEADS
            assert a_w_in.shape[2] == tail + kiwi_w + mem_w and kiwi_w <= LANES
            wa = a_w_in[i].astype(BF16)
            wa = jnp.concatenate([wa[:, :tail + kiwi_w], jnp.zeros((d, LANES - kiwi_w), BF16),
                                  wa[:, tail + kiwi_w:]], axis=1)
            gk = jnp.pad(idx_k_norm[i], (0, LANES - IDX_DIM)).reshape(1, LANES)
            qt, k, vtt, qit, kiwi, qm = _proj(
                xs, attn_norm[i], tables, gk, wa,
                [_Piece("rope_scaled_t", 0, main_w, 0, main_w, BF16),
                 _Piece("rope", main_w, kv_w, 0, kv_w, BF16),
                 _Piece("value_chunks", main_w + kv_w, kv_w, 0, kv_w, BF16),
                 _Piece("rope_idx_t", main_w + 2 * kv_w, idx_w, 0, idx_w, BF16),
                 _Piece("kiwi", tail, LANES, 0, LANES, F32),
                 _Piece("scaled", tail + LANES, mem_w, 0, mem_w, BF16)])
            ki = kiwi[:, :IDX_DIM].astype(BF16)
            wt = kiwi[:, IDX_DIM:IDX_DIM + IDX_HEADS].T
            o_main = _dsa(qit, wt, ki, qt, k, vtt, topk)
        else:
            qt, qm = _proj(xs, attn_norm[i], tables, no_gk, b_w_in[i - n_a].astype(BF16),
                           [_Piece("rope_scaled_t", 0, main_w, 0, main_w, BF16),
                            _Piece("scaled", main_w, mem_w, 0, mem_w, BF16)])
            o_main = _moba(qt, k_sh, vtt_sh, kmeans, n_sel)
        o_mem = _mem_attn(qm, mk.T, mv)
        xs = _out_proj(xs, o_main, o_mem, w_out_all, layer=i)

        xs = _ffn(xs, ffn2_norm[i], ffn_w[2], ffn_w[3], i, final_norm if last else None)
    return xs[None]
```

```python
import functools
from typing import Any, NamedTuple

import jax
import jax.numpy as jnp
import numpy as np
from jax import lax
from jax.experimental import pallas as pl
from jax.experimental.pallas import tpu as pltpu

HEAD_DIM = 128
N_MAIN_HEADS = 12
N_KV_HEADS = 4
GROUP = N_MAIN_HEADS // N_KV_HEADS
N_MEM_HEADS = 4
IDX_HEADS = 16
IDX_DIM = 64
IDX_TOPK_MAX = 256
MOBA_BLOCK = 256
MOBA_TOPK_MAX = 3
ROPE_THETA = 10000.0
RMS_EPS = 1e-6

LANES = 128
SUBLANES = 8
VMEM_LIMIT = 56 * 1024 * 1024
NEG = -1e30
LOG2_E = 1.4426950408889634
INT_MIN = -2 ** 31
INT_MAX = 2 ** 31 - 1
CHUNK = 256
CHUNKS_PER_STEP = 4
CAND_LEVELS = 2
CAND_MAX = 16
MAX_PASSES = 40
COUNT_UNROLL = 4
SCORE_UNROLL = 4
BF16_TILE_ROWS = 2 * SUBLANES
ONES_ROWS = BF16_TILE_ROWS
V_ROWS = HEAD_DIM + ONES_ROWS

F32 = jnp.float32
BF16 = jnp.bfloat16


def _params(*sem):
    return pltpu.CompilerParams(dimension_semantics=sem, vmem_limit_bytes=VMEM_LIMIT)


def _rms(x, gain):
    return x * lax.rsqrt(jnp.mean(x * x, axis=-1, keepdims=True) + RMS_EPS) * gain


def _ffn_body(x_ref, g_ref, wg_ref, wu_ref, wd_ref, pg_ref, o_ref, h_ref, *, final_norm):
    j = pl.program_id(1)

    @pl.when(j == 0)
    def _():
        x = x_ref[...]
        h_ref[...] = _rms(x, g_ref[...]).astype(BF16)
        o_ref[...] = x

    h = h_ref[...]
    gate = jnp.dot(h, wg_ref[...], preferred_element_type=F32)
    up = jnp.dot(h, wu_ref[...], preferred_element_type=F32)
    act = (gate * (0.5 / (1.0 + jnp.exp(-gate))) * up).astype(BF16)
    o_ref[...] += jnp.dot(act, wd_ref[...], preferred_element_type=F32)

    if final_norm:
        @pl.when(j == pl.num_programs(1) - 1)
        def _():
            o_ref[...] = _rms(o_ref[...], pg_ref[...])


def _ffn(x, gain, w_gate_up, w_down, layer, post_gain=None, *, tm=512, tf=512):
    s, d = x.shape
    f = w_down.shape[1]
    tm = min(tm, s)
    tf = min(tf, f)
    assert s % tm == 0 and f % tf == 0
    nf = f // tf
    final_norm = post_gain is not None
    pg = post_gain if final_norm else gain
    return pl.pallas_call(
        functools.partial(_ffn_body, final_norm=final_norm),
        grid=(s // tm, nf),
        in_specs=[
            pl.BlockSpec((tm, d), lambda i, j: (i, 0)),
            pl.BlockSpec((1, d), lambda i, j: (0, 0)),
            pl.BlockSpec((None, d, tf), lambda i, j: (layer, 0, j)),
            pl.BlockSpec((None, d, tf), lambda i, j: (layer, 0, j + nf)),
            pl.BlockSpec((None, tf, d), lambda i, j: (layer, j, 0)),
            pl.BlockSpec((1, d), lambda i, j: (0, 0)),
        ],
        out_specs=pl.BlockSpec((tm, d), lambda i, j: (i, 0)),
        out_shape=jax.ShapeDtypeStruct((s, d), F32),
        scratch_shapes=[pltpu.VMEM((tm, d), BF16)],
        compiler_params=_params("parallel", "arbitrary"),
        name="ffn",
    )(x, gain.reshape(1, d), w_gate_up, w_gate_up, w_down, pg.reshape(1, d))


def _rope_heads(y, cos, sin, o_ref, scale, transposed=False):
    for h in range(y.shape[1] // HEAD_DIM):
        t = y[:, h * HEAD_DIM:(h + 1) * HEAD_DIM]
        r = t * cos + pltpu.roll(t, HEAD_DIM // 2, 1) * sin
        if scale != 1.0:
            r = r * scale
        if transposed:
            o_ref[h * HEAD_DIM:(h + 1) * HEAD_DIM, :] = r.T.astype(o_ref.dtype)
        else:
            o_ref[:, h * HEAD_DIM:(h + 1) * HEAD_DIM] = r.astype(o_ref.dtype)


def _store_value_chunks(y, o_ref):
    for b in range(y.shape[0] // CHUNK):
        for g in range(N_KV_HEADS):
            blk = y[b * CHUNK:(b + 1) * CHUNK, g * HEAD_DIM:(g + 1) * HEAD_DIM]
            o_ref[b, g * V_ROWS:g * V_ROWS + HEAD_DIM, :] = blk.T.astype(o_ref.dtype)
            o_ref[b, g * V_ROWS + HEAD_DIM:(g + 1) * V_ROWS, :] = jnp.ones((ONES_ROWS, CHUNK), o_ref.dtype)


def _rot_idx(t):
    lane = lax.broadcasted_iota(jnp.int32, t.shape, 1)
    first_half = (lane & (IDX_DIM // 2)) == 0
    return jnp.where(first_half, pltpu.roll(t, LANES - IDX_DIM // 2, 1),
                     pltpu.roll(t, IDX_DIM // 2, 1))


class _Piece(NamedTuple):
    kind: str
    start: int
    width: int
    lo: int
    hi: int
    dtype: Any


def _proj_body(*refs, pieces, q_scale, idx_w_scale):
    x_ref, g_ref, cos_ref, sin_ref, cosi_ref, sini_ref, gk_ref, w_ref = refs[:8]
    o_refs = refs[8:]
    h = _rms(x_ref[...], g_ref[...]).astype(BF16)
    products = {}
    for piece, o_ref in zip(pieces, o_refs):
        cols = (piece.start, piece.width)
        if cols not in products:
            products[cols] = jnp.dot(h, w_ref[:, piece.start:piece.start + piece.width],
                                     preferred_element_type=F32)
        y = products[cols]
        if (piece.lo, piece.hi) != (0, piece.width):
            y = y[:, piece.lo:piece.hi]
        kind = piece.kind
        if kind == "plain":
            o_ref[...] = y.astype(o_ref.dtype)
        elif kind == "scaled":
            o_ref[...] = (y * q_scale).astype(o_ref.dtype)
        elif kind == "rope":
            _rope_heads(y, cos_ref[...], sin_ref[...], o_ref, 1.0)
        elif kind == "rope_scaled_t":
            _rope_heads(y, cos_ref[...], sin_ref[...], o_ref, q_scale * LOG2_E, transposed=True)
        elif kind == "rope_idx_t":
            cosi, sini = cosi_ref[...], sini_ref[...]
            for c in range(y.shape[1] // LANES):
                t = y[:, c * LANES:(c + 1) * LANES]
                o_ref[c * LANES:(c + 1) * LANES, :] = (t * cosi + _rot_idx(t) * sini).T.astype(o_ref.dtype)
        elif kind == "value_chunks":
            _store_value_chunks(y, o_ref)
        elif kind == "kiwi":
            lane = lax.broadcasted_iota(jnp.int32, y.shape, 1)
            is_k = lane < IDX_DIM
            kk = jnp.where(is_k, y, 0.0)
            ms = jnp.sum(kk * kk, axis=-1, keepdims=True) * (1.0 / IDX_DIM)
            kn = kk * lax.rsqrt(ms + RMS_EPS) * gk_ref[...]
            kr = kn * cosi_ref[...] + _rot_idx(kn) * sini_ref[...]
            o_ref[...] = jnp.where(is_k, kr, y * idx_w_scale)
        else:
            raise ValueError(kind)


def _layer_weight_spec(w, layer):
    if w.ndim == 2:
        return pl.BlockSpec(w.shape, lambda i: (0, 0), pipeline_mode=pl.Buffered(1))
    return pl.BlockSpec((None,) + w.shape[1:], lambda i: (layer, 0, 0), pipeline_mode=pl.Buffered(1))


def _proj(x, gain, tables, gk, w, pieces, *, layer=0, tm=256):
    s, d = x.shape
    tm = min(tm, s)
    assert s % tm == 0
    cos, sin, cosi, sini = tables
    row = lambda i: (i, 0)
    const = lambda i: (0, 0)
    in_specs = [pl.BlockSpec((tm, d), row), pl.BlockSpec((1, d), const)]
    in_specs += [pl.BlockSpec((tm, LANES), row)] * 4
    in_specs += [pl.BlockSpec((1, LANES), const)]
    in_specs += [_layer_weight_spec(w, layer)]
    out_specs, out_shape = [], []
    for p in pieces:
        assert p.start % LANES == 0 and p.width % LANES == 0 and p.start + p.width <= w.shape[-1]
        n = p.hi - p.lo
        if p.kind.endswith("_t"):
            out_specs.append(pl.BlockSpec((n, tm), lambda i: (0, i)))
            out_shape.append(jax.ShapeDtypeStruct((n, s), p.dtype))
        elif p.kind == "value_chunks":
            assert tm % CHUNK == 0 and n == N_KV_HEADS * HEAD_DIM
            out_specs.append(pl.BlockSpec((tm // CHUNK, N_KV_HEADS * V_ROWS, CHUNK), lambda i: (i, 0, 0)))
            out_shape.append(jax.ShapeDtypeStruct((s // CHUNK, N_KV_HEADS * V_ROWS, CHUNK), p.dtype))
        else:
            out_specs.append(pl.BlockSpec((tm, n), row))
            out_shape.append(jax.ShapeDtypeStruct((s, n), p.dtype))
    return pl.pallas_call(
        functools.partial(_proj_body, pieces=tuple(pieces), q_scale=HEAD_DIM ** -0.5,
                          idx_w_scale=(IDX_HEADS ** -0.5) * (IDX_DIM ** -0.5)),
        grid=(s // tm,),
        in_specs=in_specs,
        out_specs=out_specs,
        out_shape=out_shape,
        compiler_params=_params("parallel"),
        name="proj",
    )(x, gain.reshape(1, d), cos, sin, cosi, sini, gk, w)


def _kv_shared_body(x_ref, g_ref, cos_ref, sin_ref, wk_ref, wv_ref, k_ref, v_ref, km_ref):
    h = _rms(x_ref[...], g_ref[...]).astype(BF16)
    yk = jnp.dot(h, wk_ref[...], preferred_element_type=F32)
    cos, sin = cos_ref[...], sin_ref[...]
    tm = yk.shape[0]
    first_blk = pl.program_id(0) * (tm // MOBA_BLOCK)
    lane = lax.broadcasted_iota(jnp.int32, (MOBA_BLOCK, HEAD_DIM), 1)
    for hd in range(yk.shape[1] // HEAD_DIM):
        t = yk[:, hd * HEAD_DIM:(hd + 1) * HEAD_DIM]
        r = t * cos + pltpu.roll(t, HEAD_DIM // 2, 1) * sin
        k_ref[:, 2 * hd * HEAD_DIM:(2 * hd + 1) * HEAD_DIM] = r.astype(k_ref.dtype)
        for b in range(tm // MOBA_BLOCK):
            rows = slice(b * MOBA_BLOCK, (b + 1) * MOBA_BLOCK)
            k_ref[rows, (2 * hd + 1) * HEAD_DIM:(2 * hd + 2) * HEAD_DIM] = (
                jnp.where(lane == first_blk + b, 1.0, 0.0).astype(k_ref.dtype))
            km_ref[b, :, hd * HEAD_DIM:(hd + 1) * HEAD_DIM] = (
                jnp.sum(r[rows], axis=0, keepdims=True) * (1.0 / MOBA_BLOCK))
    _store_value_chunks(jnp.dot(h, wv_ref[...], preferred_element_type=F32), v_ref)


def _kv_shared(x, gain, cos, sin, wk, wv):
    s, d = x.shape
    tm = MOBA_BLOCK
    assert s % tm == 0 and CHUNK == MOBA_BLOCK
    n = s // tm
    nkv = wk.shape[1]
    src = lambda i: (jnp.minimum(i, n - 1), 0)
    row = lambda i: (i, 0)
    const = lambda i: (0, 0)
    return pl.pallas_call(
        _kv_shared_body,
        grid=(n + 1,),
        in_specs=[pl.BlockSpec((tm, d), src), pl.BlockSpec((1, d), const),
                  pl.BlockSpec((tm, LANES), src), pl.BlockSpec((tm, LANES), src),
                  pl.BlockSpec(wk.shape, const, pipeline_mode=pl.Buffered(1)),
                  pl.BlockSpec(wv.shape, const, pipeline_mode=pl.Buffered(1))],
        out_specs=[pl.BlockSpec((tm, 2 * nkv), row),
                   pl.BlockSpec((1, N_KV_HEADS * V_ROWS, CHUNK), lambda i: (i, 0, 0)),
                   pl.BlockSpec((1, 1, nkv), lambda i: (i, 0, 0))],
        out_shape=[jax.ShapeDtypeStruct((s + tm, 2 * nkv), BF16),
                   jax.ShapeDtypeStruct((n + 1, N_KV_HEADS * V_ROWS, CHUNK), BF16),
                   jax.ShapeDtypeStruct((n + 1, 1, nkv), F32)],
        compiler_params=_params("parallel"),
        name="kv_shared",
    )(x, gain.reshape(1, d), cos, sin, wk, wv)


def _load_q_group(qt_ref, qall_ref):
    tq = qt_ref.shape[1]
    for hh in range(GROUP):
        qall_ref[:, hh * tq:(hh + 1) * tq] = qt_ref[hh * HEAD_DIM:(hh + 1) * HEAD_DIM, :]


def _softmax_init(m_ref, acc_ref):
    m_ref[...] = jnp.full(m_ref.shape, NEG, F32)
    acc_ref[...] = jnp.zeros(acc_ref.shape, F32)


def _add_group_bias(s, b):
    tq = b.shape[1]
    return jnp.concatenate([s[:, hh * tq:(hh + 1) * tq] + b for hh in range(GROUP)], axis=1)


def _logits_pass(chunk_ids, logits_fn, s_ref, cmax_ref):
    cmax = None
    for j, c in enumerate(chunk_ids):
        s = logits_fn(c)
        s_ref[j * CHUNK:(j + 1) * CHUNK, :] = s
        cm = jnp.max(s, axis=0, keepdims=True)
        cmax = cm if cmax is None else jnp.maximum(cmax, cm)
    cmax_ref[...] = cmax


def _softmax_pass(chunk_ids, vt_ref, s_ref, cmax_ref, m_ref, acc_ref):
    m_old = m_ref[...]
    m_new = jnp.maximum(m_old, cmax_ref[...])
    alpha = jnp.exp2(m_old - m_new)
    pv = None
    for j, c in enumerate(chunk_ids):
        p = jnp.exp2(s_ref[j * CHUNK:(j + 1) * CHUNK, :] - m_new)
        d = jnp.dot(vt_ref[c], p.astype(BF16), preferred_element_type=F32)
        pv = d if pv is None else pv + d
    m_ref[...] = m_new
    acc_ref[...] = alpha * acc_ref[...] + pv


def _attend_range(n_chunks, logits_fn, vt_ref, s_refs, cmax_refs, m_ref, acc_ref, *, fuse_pairs):
    n_steps = pl.cdiv(n_chunks, CHUNKS_PER_STEP)
    last_chunk = vt_ref.shape[0] - 1

    def chunks(step):
        return [step * CHUNKS_PER_STEP + j for j in range(CHUNKS_PER_STEP)]

    def logits(step, slot):
        _logits_pass(chunks(step), logits_fn, s_refs[slot], cmax_refs[slot])

    def softmax(step, slot):
        ids = [jnp.minimum(c, last_chunk) for c in chunks(step)]
        _softmax_pass(ids, vt_ref, s_refs[slot], cmax_refs[slot], m_ref, acc_ref)

    @pl.when(n_steps > 0)
    def _():
        logits(0, 0)

    def pair(u, carry):
        t = 2 * u
        if fuse_pairs:
            @pl.when(t + 1 < n_steps)
            def _():
                logits(t + 1, 1)
                softmax(t, 0)
                logits(t + 2, 0)
                softmax(t + 1, 1)

            @pl.when(t + 1 >= n_steps)
            def _():
                softmax(t, 0)
        else:
            logits(t + 1, 1)
            softmax(t, 0)

            @pl.when(t + 1 < n_steps)
            def _():
                logits(t + 2, 0)
                softmax(t + 1, 1)

        return carry

    lax.fori_loop(0, pl.cdiv(n_steps, 2), pair, 0)


def _softmax_finish(o_ref, acc_ref):
    tq = o_ref.shape[0]
    out = acc_ref[:HEAD_DIM, :] / acc_ref[HEAD_DIM:HEAD_DIM + 1, :]
    for hh in range(GROUP):
        o_ref[:, hh * HEAD_DIM:(hh + 1) * HEAD_DIM] = out[:, hh * tq:(hh + 1) * tq].T.astype(o_ref.dtype)


def _dsa_body(qit_ref, wt_ref, ki_ref, qt_ref, k_ref, vt_ref, o_ref,
              keys_ref, gmax_ref, cand_ref, qall_ref, sa_ref, sb_ref, cma_ref, cmb_ref,
              m_ref, acc_ref, *, topk):
    i = pl.program_id(0)
    g = pl.program_id(1)
    tq = CHUNK
    n_chunks = i + 1
    last_chunk = keys_ref.shape[0] // CHUNK - 1
    sub = 128

    def causal(c, rows, row_off=0):
        kpos = c * CHUNK + row_off + lax.broadcasted_iota(jnp.int32, (rows, tq), 0)
        qpos = i * tq + lax.broadcasted_iota(jnp.int32, (rows, tq), 1)
        return kpos <= qpos

    @pl.when(g == 0)
    def _():
        gmax_ref[...] = jnp.full(gmax_ref.shape, INT_MIN, jnp.int32)

        def score_chunk(c, carry):
            for part in range(CHUNK // sub):
                r0 = pl.multiple_of(c * CHUNK + part * sub, sub)
                kit = ki_ref[pl.ds(r0, sub), :]
                sc = jnp.zeros((sub, tq), F32)
                for h in range(IDX_HEADS):
                    d = jnp.dot(kit, qit_ref[h * IDX_DIM:(h + 1) * IDX_DIM, :], preferred_element_type=F32)
                    sc = sc + wt_ref[h:h + 1, :] * jnp.maximum(d, 0.0)
                sc = jnp.where(sc == 0.0, 0.0, sc)
                sc = jnp.where(causal(c, sub, part * sub), sc, -jnp.inf)
                bits = pltpu.bitcast(sc, jnp.int32)
                key = bits ^ ((bits >> 31) & INT_MAX)
                keys_ref[pl.ds(r0, sub), :] = key
                gsl = slice(part * sub, (part + 1) * sub)
                gmax_ref[gsl, :] = jnp.maximum(gmax_ref[gsl, :], key)
            return carry

        def chunk_loop(body, unroll):
            def group(u, carry):
                for j in range(unroll):
                    body(u * unroll + j, carry)
                return carry

            n_groups = n_chunks // unroll
            lax.fori_loop(0, n_groups, group, 0)
            lax.fori_loop(n_groups * unroll, n_chunks, body, 0)

        chunk_loop(score_chunk, SCORE_UNROLL)

        def count_rows(load, n_blocks, t):
            def count_block(b, acc):
                ge = jnp.where(load(b) >= t, 1, 0).astype(jnp.int32)
                return acc + jnp.sum(ge.reshape(CHUNK // SUBLANES, SUBLANES, tq), axis=0)

            part = jnp.zeros((SUBLANES, tq), jnp.int32)
            if isinstance(n_blocks, int):
                for b in range(n_blocks):
                    part = count_block(b, part)
            else:
                def count_group(u, acc):
                    for j in range(COUNT_UNROLL):
                        acc = count_block(u * COUNT_UNROLL + j, acc)
                    return acc

                n_groups = n_blocks // COUNT_UNROLL
                part = lax.fori_loop(0, n_groups, count_group, part)
                part = lax.fori_loop(n_groups * COUNT_UNROLL, n_blocks, count_block, part)
            return jnp.sum(part, axis=0, keepdims=True)

        def key_chunk(c):
            return keys_ref[pl.ds(pl.multiple_of(c * CHUNK, CHUNK), CHUNK), :]

        def count_all(t):
            return count_rows(key_chunk, n_chunks, t)

        def count_cand(t):
            return count_rows(lambda b: cand_ref[b], CAND_LEVELS, t)

        def any_set(flags):
            return jnp.max(flags) > 0.0

        def bisect_pass(count_fn, base, live, state):
            lo, hi, c_lo, c_hi, done = state
            mid = (lo >> 1) + (hi >> 1) + (lo & hi & 1)
            cnt = count_fn(mid) + base
            up = live & (cnt >= topk)
            down = live & (cnt < topk)
            lo, c_lo = jnp.where(up, mid, lo), jnp.where(up, cnt, c_lo)
            hi, c_hi = jnp.where(down, mid, hi), jnp.where(down, cnt, c_hi)
            finished = live & ((cnt == topk) | (hi <= lo + 1))
            return lo, hi, c_lo, c_hi, jnp.where(finished, 1.0, done)

        def bisect_while(count_fn, base, live_fn, keep_going, state):
            def cond(carry):
                it, state = carry
                return jnp.logical_and(it < MAX_PASSES, keep_going(state))

            def body(carry):
                it, state = carry
                return it + 1, bisect_pass(count_fn, base, live_fn(state), state)

            return lax.while_loop(cond, body, (jnp.int32(0), state))[1]

        def in_bracket(state):
            return state[2] - state[3]

        def active(state):
            return state[4] == 0.0

        gm = gmax_ref[...]
        lo = jnp.min(gm, axis=0, keepdims=True)
        top = jnp.max(gm, axis=0, keepdims=True)
        hi = jnp.where(top == INT_MAX, top, top + 1)
        c_lo = count_all(lo)
        state = (lo, hi, c_lo, jnp.zeros_like(c_lo), (c_lo == topk).astype(F32))

        def wide(state):
            return any_set(jnp.where(active(state) & (in_bracket(state) > CAND_MAX), 1.0, 0.0))

        state = bisect_while(count_all, 0, active, wide, state)

        lo, hi, c_lo, c_hi, _ = state
        cand_ref[...] = jnp.full(cand_ref.shape, INT_MIN, jnp.int32)

        def capture(c, carry):
            blk = key_chunk(c)
            x = jnp.where((blk >= lo) & (blk < hi), blk, INT_MIN)
            for level in range(CAND_LEVELS):
                held = cand_ref[level]
                cand_ref[level] = jnp.maximum(held, x)
                x = jnp.minimum(held, x)
            return carry

        chunk_loop(capture, 2)
        captured = count_cand(lo) == in_bracket(state)

        def live_captured(state):
            return active(state) & captured

        state = bisect_while(count_cand, c_hi, live_captured,
                             lambda st: any_set(jnp.where(live_captured(st), 1.0, 0.0)), state)

        state = bisect_while(count_all, 0, active,
                             lambda st: any_set(jnp.where(active(st), 1.0, 0.0)), state)
        thr = state[0]

        surplus = state[2] - topk

        @pl.when(jnp.max(surplus.astype(F32)) > 0.0)
        def _():
            row = lax.broadcasted_iota(jnp.int32, (CHUNK, CHUNK), 0)
            col = lax.broadcasted_iota(jnp.int32, (CHUNK, CHUNK), 1)
            prefix_ones = jnp.where(row >= col, 1.0, 0.0).astype(BF16)

            def count_equal(c, acc):
                return acc + jnp.sum(jnp.where(key_chunk(c) == thr, 1.0, 0.0), axis=0, keepdims=True)

            n_equal = lax.fori_loop(0, n_chunks, count_equal, jnp.zeros((1, tq), F32))
            keep = n_equal - surplus.astype(F32)

            def drop_late_ties(c, seen):
                blk = key_chunk(c)
                tie = blk == thr
                tie_f = jnp.where(tie, 1.0, 0.0)
                rank = seen + jnp.dot(prefix_ones, tie_f.astype(BF16), preferred_element_type=F32)
                r0 = pl.multiple_of(c * CHUNK, CHUNK)
                keys_ref[pl.ds(r0, CHUNK), :] = jnp.where(tie & (rank > keep), blk - 1, blk)
                return seen + jnp.sum(tie_f, axis=0, keepdims=True)

            lax.fori_loop(0, n_chunks, drop_late_ties, jnp.zeros((1, tq), F32))

        def to_bias(c, carry):
            r0 = pl.multiple_of(c * CHUNK, CHUNK)
            sel = (keys_ref[pl.ds(r0, CHUNK), :] >= thr) & causal(c, CHUNK)
            keys_ref[pl.ds(r0, CHUNK), :] = pltpu.bitcast(jnp.where(sel, 0.0, NEG), jnp.int32)
            return carry

        chunk_loop(to_bias, 2)

        @pl.when(n_chunks <= last_chunk)
        def _():
            r0 = pl.multiple_of(n_chunks * CHUNK, CHUNK)
            keys_ref[pl.ds(r0, CHUNK), :] = pltpu.bitcast(jnp.full((CHUNK, tq), NEG, F32), jnp.int32)

    _load_q_group(qt_ref, qall_ref)
    _softmax_init(m_ref, acc_ref)

    def logits(c):
        rb = pl.multiple_of(jnp.minimum(jnp.minimum(c, n_chunks), last_chunk) * CHUNK, CHUNK)
        rk = pl.multiple_of(jnp.minimum(c, n_chunks - 1) * CHUNK, CHUNK)
        s = jnp.dot(k_ref[pl.ds(rk, CHUNK), :], qall_ref[...], preferred_element_type=F32)
        return _add_group_bias(s, pltpu.bitcast(keys_ref[pl.ds(rb, CHUNK), :], F32))

    _attend_range(n_chunks, logits, vt_ref, (sa_ref, sb_ref), (cma_ref, cmb_ref), m_ref, acc_ref,
                  fuse_pairs=True)
    _softmax_finish(o_ref, acc_ref)


def _dsa(qit, wt, ki, qt, k, vtt, topk):
    s = ki.shape[0]
    tq = CHUNK
    assert s % (tq * CHUNKS_PER_STEP) == 0 and topk <= CHUNK
    gw = GROUP * HEAD_DIM
    return pl.pallas_call(
        functools.partial(_dsa_body, topk=topk),
        grid=(s // tq, N_KV_HEADS),
        in_specs=[
            pl.BlockSpec((IDX_HEADS * IDX_DIM, tq), lambda i, g: (0, i)),
            pl.BlockSpec((IDX_HEADS, tq), lambda i, g: (0, i)),
            pl.BlockSpec((s, IDX_DIM), lambda i, g: (0, 0), pipeline_mode=pl.Buffered(1)),
            pl.BlockSpec((gw, tq), lambda i, g: (g, i)),
            pl.BlockSpec((s, HEAD_DIM), lambda i, g: (0, g)),
            pl.BlockSpec((s // tq, V_ROWS, tq), lambda i, g: (0, g, 0)),
        ],
        out_specs=pl.BlockSpec((tq, gw), lambda i, g: (i, g)),
        out_shape=jax.ShapeDtypeStruct((s, N_MAIN_HEADS * HEAD_DIM), BF16),
        scratch_shapes=[
            pltpu.VMEM((s, tq), jnp.int32),
            pltpu.VMEM((CHUNK, tq), jnp.int32),
            pltpu.VMEM((CAND_LEVELS, CHUNK, tq), jnp.int32),
            pltpu.VMEM((HEAD_DIM, GROUP * tq), BF16),
            pltpu.VMEM((CHUNKS_PER_STEP * CHUNK, GROUP * tq), F32),
            pltpu.VMEM((CHUNKS_PER_STEP * CHUNK, GROUP * tq), F32),
            pltpu.VMEM((1, GROUP * tq), F32),
            pltpu.VMEM((1, GROUP * tq), F32),
            pltpu.VMEM((1, GROUP * tq), F32),
            pltpu.VMEM((V_ROWS, GROUP * tq), F32),
        ],
        compiler_params=_params("arbitrary", "arbitrary"),
        name="dsa",
    )(qit, wt, ki, qt, k, vtt)


def _moba_body(qt_ref, k_ref, vt_ref, km_ref, o_ref, qaug_ref, sa_ref, sb_ref, cma_ref, cmb_ref,
               m_ref, acc_ref, *, n_sel):
    cur = pl.program_id(1)
    tq = CHUNK
    nb = km_ref.shape[0]
    width = GROUP * tq
    for hh in range(GROUP):
        qaug_ref[:HEAD_DIM, hh * tq:(hh + 1) * tq] = qt_ref[hh * HEAD_DIM:(hh + 1) * HEAD_DIM, :]
    q_all = qaug_ref[:HEAD_DIM, :]
    _softmax_init(m_ref, acc_ref)

    tri = (lax.broadcasted_iota(jnp.int32, (tq, tq), 0) <= lax.broadcasted_iota(jnp.int32, (tq, tq), 1))

    def own_logits(c):
        r0 = pl.multiple_of(c * CHUNK, CHUNK)
        s = jnp.dot(k_ref[pl.ds(r0, CHUNK), :HEAD_DIM], q_all, preferred_element_type=F32)
        return _add_group_bias(s, jnp.where(tri, 0.0, NEG))

    _logits_pass([cur], own_logits, sa_ref, cma_ref)
    _softmax_pass([cur], vt_ref, sa_ref, cma_ref, m_ref, acc_ref)

    blk_id = lax.broadcasted_iota(jnp.int32, (nb, width), 0)
    past = blk_id < cur
    gate = jnp.dot(km_ref[...], q_all, preferred_element_type=F32)
    gate = jnp.where(past, gate, -jnp.inf)
    chosen = jnp.zeros((nb, width), jnp.bool_)
    for _ in range(n_sel):
        best = jnp.max(gate, axis=0, keepdims=True)
        first = jnp.min(jnp.where(gate == best, blk_id, nb), axis=0, keepdims=True)
        pick = blk_id == first
        chosen = chosen | pick
        gate = jnp.where(pick, -jnp.inf, gate)
    qaug_ref[HEAD_DIM:HEAD_DIM + nb, :] = jnp.where(chosen & past, 0.0, NEG).astype(BF16)
    pad = BF16_TILE_ROWS
    qaug_ref[HEAD_DIM + nb:HEAD_DIM + nb + pad, :] = jnp.full((pad, width), NEG, BF16)
    if nb + pad < HEAD_DIM:
        qaug_ref[HEAD_DIM + nb + pad:, :] = jnp.zeros((HEAD_DIM - nb - pad, width), BF16)

    state = (vt_ref, (sa_ref, sb_ref), (cma_ref, cmb_ref), m_ref, acc_ref)

    def past_logits(c):
        r0 = pl.multiple_of(jnp.where(c < cur, c, nb) * CHUNK, CHUNK)
        return jnp.dot(k_ref[pl.ds(r0, CHUNK), :], qaug_ref[...], preferred_element_type=F32)

    _attend_range(cur, past_logits, *state, fuse_pairs=False)
    _softmax_finish(o_ref, acc_ref)


def _moba(qt, k_aug, vtt, kmeans, n_sel):
    s = k_aug.shape[0] - CHUNK
    assert MOBA_BLOCK == CHUNK and s % (CHUNK * CHUNKS_PER_STEP) == 0
    tq = CHUNK
    nb = s // tq
    assert nb + BF16_TILE_ROWS <= HEAD_DIM and vtt.shape[0] == nb + 1
    gw = GROUP * HEAD_DIM
    return pl.pallas_call(
        functools.partial(_moba_body, n_sel=n_sel),
        grid=(N_KV_HEADS, nb),
        in_specs=[
            pl.BlockSpec((gw, tq), lambda g, i: (g, i)),
            pl.BlockSpec((s + CHUNK, 2 * HEAD_DIM), lambda g, i: (0, g)),
            pl.BlockSpec((nb + 1, V_ROWS, tq), lambda g, i: (0, g, 0)),
            pl.BlockSpec((nb, HEAD_DIM), lambda g, i: (0, g)),
        ],
        out_specs=pl.BlockSpec((tq, gw), lambda g, i: (i, g)),
        out_shape=jax.ShapeDtypeStruct((s, N_MAIN_HEADS * HEAD_DIM), BF16),
        scratch_shapes=[
            pltpu.VMEM((2 * HEAD_DIM, GROUP * tq), BF16),
            pltpu.VMEM((CHUNKS_PER_STEP * CHUNK, GROUP * tq), F32),
            pltpu.VMEM((CHUNKS_PER_STEP * CHUNK, GROUP * tq), F32),
            pltpu.VMEM((1, GROUP * tq), F32),
            pltpu.VMEM((1, GROUP * tq), F32),
            pltpu.VMEM((1, GROUP * tq), F32),
            pltpu.VMEM((V_ROWS, GROUP * tq), F32),
        ],
        compiler_params=_params("parallel", "arbitrary"),
        name="moba",
    )(qt, k_aug, vtt, kmeans)


def _mem_attn_body(q_ref, kt_ref, v_ref, o_ref):
    for h in range(N_MEM_HEADS):
        sl = slice(h * HEAD_DIM, (h + 1) * HEAD_DIM)
        s = jnp.dot(q_ref[:, sl], kt_ref[sl, :], preferred_element_type=F32)
        p = jnp.exp(s - jnp.max(s, axis=-1, keepdims=True))
        o = jnp.dot(p.astype(BF16), v_ref[:, sl], preferred_element_type=F32)
        o_ref[:, sl] = (o / jnp.sum(p, axis=-1, keepdims=True)).astype(o_ref.dtype)


def _mem_attn(qm, kmt, vm, *, tq=512):
    s, w = qm.shape
    m = vm.shape[0]
    tq = min(tq, s)
    assert s % tq == 0
    return pl.pallas_call(
        _mem_attn_body,
        grid=(s // tq,),
        in_specs=[pl.BlockSpec((tq, w), lambda i: (i, 0)),
                  pl.BlockSpec((w, m), lambda i: (0, 0)),
                  pl.BlockSpec((m, w), lambda i: (0, 0))],
        out_specs=pl.BlockSpec((tq, w), lambda i: (i, 0)),
        out_shape=jax.ShapeDtypeStruct((s, w), BF16),
        compiler_params=_params("parallel"),
        name="mem_attn",
    )(qm, kmt, vm)


def _out_proj_body(x_ref, om_ref, oq_ref, w_ref, o_ref):
    n_main = om_ref.shape[1]
    o_ref[...] = (x_ref[...]
                  + jnp.dot(om_ref[...], w_ref[:n_main, :], preferred_element_type=F32)
                  + jnp.dot(oq_ref[...], w_ref[n_main:, :], preferred_element_type=F32))


def _out_proj(x, o_main, o_mem, w, *, layer=0, tm=512):
    s, d = x.shape
    tm = min(tm, s)
    assert s % tm == 0 and w.shape[-2] == o_main.shape[1] + o_mem.shape[1]
    row = lambda i: (i, 0)
    const = lambda i: (0, 0)
    return pl.pallas_call(
        _out_proj_body,
        grid=(s // tm,),
        in_specs=[pl.BlockSpec((tm, d), row),
                  pl.BlockSpec((tm, o_main.shape[1]), row),
                  pl.BlockSpec((tm, o_mem.shape[1]), row),
                  _layer_weight_spec(w, layer)],
        out_specs=pl.BlockSpec((tm, d), row),
        out_shape=jax.ShapeDtypeStruct((s, d), F32),
        compiler_params=_params("parallel"),
        name="out_proj",
    )(x, o_main, o_mem, w)


def _rope_tables(positions, dim):
    inv = 1.0 / (ROPE_THETA ** (jnp.arange(0, dim, 2, dtype=F32) / dim))
    ang = positions.astype(F32)[:, None] * inv
    c, s = jnp.cos(ang), jnp.sin(ang)
    reps = LANES // dim
    return jnp.tile(jnp.concatenate([c, c], -1), (1, reps)), jnp.tile(jnp.concatenate([-s, s], -1), (1, reps))


def kernel(x, mem, positions, ffn1_norm, ffn1_w_gate_up, ffn1_w_down, attn_norm, mem_norm, a_w_in, idx_k_norm, b_w_in, w_mem_kv, w_out, ffn2_norm, ffn2_w_gate_up, ffn2_w_down, kv_norm, w_kv_shared, final_norm):
    b, s, d = x.shape
    assert b == 1 and mem.shape[0] == 1
    depth = ffn1_norm.shape[0]
    n_a = a_w_in.shape[0]
    main_w = N_MAIN_HEADS * HEAD_DIM
    kv_w = N_KV_HEADS * HEAD_DIM
    idx_w = IDX_HEADS * IDX_DIM
    mem_w = N_MEM_HEADS * HEAD_DIM
    topk = min(IDX_TOPK_MAX, s // 4)
    nb = s // MOBA_BLOCK
    n_sel = min(MOBA_TOPK_MAX, max(nb - 1, 1))

    cos, sin = _rope_tables(positions[0], HEAD_DIM)
    cosi, sini = _rope_tables(positions[0], IDX_DIM)
    tables = (cos, sin, cosi, sini)
    mem_tables = tuple(t[:mem.shape[1]] for t in tables)
    no_gk = jnp.zeros((1, LANES), F32)

    ffn_w = [w.astype(BF16) for w in (ffn1_w_gate_up, ffn1_w_down, ffn2_w_gate_up, ffn2_w_down)]
    w_mem_all = w_mem_kv.astype(BF16)
    w_out_all = w_out.astype(BF16)

    xs = x[0]
    mem2 = mem[0]
    k_sh = vtt_sh = kmeans = None
    for i in range(depth):
        if i == n_a:
            wk = w_kv_shared[:, :kv_w].astype(BF16)
            wv = w_kv_shared[:, kv_w:].astype(BF16)
            k_sh, vtt_sh, km = _kv_shared(xs, kv_norm, cos, sin, wk, wv)
            kmeans = km[:nb].reshape(nb, kv_w).astype(BF16)

        last = i == depth - 1
        xs = _ffn(xs, ffn1_norm[i], ffn_w[0], ffn_w[1], i)

        mk, mv = _proj(mem2, mem_norm[i], mem_tables, no_gk, w_mem_all,
                       [_Piece("plain", 0, mem_w, 0, mem_w, BF16),
                        _Piece("plain", mem_w, mem_w, 0, mem_w, BF16)], layer=i)
        if i < n_a:
            tail = main_w + 2 * kv_w + idx_w
            kiwi_w = IDX_DIM + IDX_HEADS
            assert a_w_in.shape[2] == tail + kiwi_w + mem_w and kiwi_w <= LANES
            wa = a_w_in[i].astype(BF16)
            wa = jnp.concatenate([wa[:, :tail + kiwi_w], jnp.zeros((d, LANES - kiwi_w), BF16),
                                  wa[:, tail + kiwi_w:]], axis=1)
            gk = jnp.pad(idx_k_norm[i], (0, LANES - IDX_DIM)).reshape(1, LANES)
            qt, k, vtt, qit, kiwi, qm = _proj(
                xs, attn_norm[i], tables, gk, wa,
                [_Piece("rope_scaled_t", 0, main_w, 0, main_w, BF16),
                 _Piece("rope", main_w, kv_w, 0, kv_w, BF16),
                 _Piece("value_chunks", main_w + kv_w, kv_w, 0, kv_w, BF16),
                 _Piece("rope_idx_t", main_w + 2 * kv_w, idx_w, 0, idx_w, BF16),
                 _Piece("kiwi", tail, LANES, 0, LANES, F32),
                 _Piece("scaled", tail + LANES, mem_w, 0, mem_w, BF16)])
            ki = kiwi[:, :IDX_DIM].astype(BF16)
            wt = kiwi[:, IDX_DIM:IDX_DIM + IDX_HEADS].T
            o_main = _dsa(qit, wt, ki, qt, k, vtt, topk)
        else:
            qt, qm = _proj(xs, attn_norm[i], tables, no_gk, b_w_in[i - n_a].astype(BF16),
                           [_Piece("rope_scaled_t", 0, main_w, 0, main_w, BF16),
                            _Piece("scaled", main_w, mem_w, 0, mem_w, BF16)])
            o_main = _moba(qt, k_sh, vtt_sh, kmeans, n_sel)
        o_mem = _mem_attn(qm, mk.T, mv)
        xs = _out_proj(xs, o_main, o_mem, w_out_all, layer=i)

        xs = _ffn(xs, ffn2_norm[i], ffn_w[2], ffn_w[3], i, final_norm if last else None)
    return xs[None]
```

```python
import functools
from typing import Any, NamedTuple

import jax
import jax.numpy as jnp
import numpy as np
from jax import lax
from jax.experimental import pallas as pl
from jax.experimental.pallas import tpu as pltpu

HEAD_DIM = 128
N_MAIN_HEADS = 12
N_KV_HEADS = 4
GROUP = N_MAIN_HEADS // N_KV_HEADS
N_MEM_HEADS = 4
IDX_HEADS = 16
IDX_DIM = 64
IDX_TOPK_MAX = 256
MOBA_BLOCK = 256
MOBA_TOPK_MAX = 3
ROPE_THETA = 10000.0
RMS_EPS = 1e-6

LANES = 128
SUBLANES = 8
VMEM_LIMIT = 56 * 1024 * 1024
NEG = -1e30
LOG2_E = 1.4426950408889634
INT_MIN = -2 ** 31
INT_MAX = 2 ** 31 - 1
CHUNK = 256
CHUNKS_PER_STEP = 4
CAND_LEVELS = 2
CAND_MAX = 16
MAX_PASSES = 40
COUNT_UNROLL = 4
SCORE_UNROLL = 4
BF16_TILE_ROWS = 2 * SUBLANES
ONES_ROWS = BF16_TILE_ROWS
V_ROWS = HEAD_DIM + ONES_ROWS

F32 = jnp.float32
BF16 = jnp.bfloat16


def _params(*sem):
    return pltpu.CompilerParams(dimension_semantics=sem, vmem_limit_bytes=VMEM_LIMIT)


def _rms(x, gain):
    return x * lax.rsqrt(jnp.mean(x * x, axis=-1, keepdims=True) + RMS_EPS) * gain


def _ffn_body(x_ref, g_ref, wg_ref, wu_ref, wd_ref, pg_ref, o_ref, h_ref, *, final_norm):
    j = pl.program_id(1)

    @pl.when(j == 0)
    def _():
        x = x_ref[...]
        h_ref[...] = _rms(x, g_ref[...]).astype(BF16)
        o_ref[...] = x

    h = h_ref[...]
    gate = jnp.dot(h, wg_ref[...], preferred_element_type=F32)
    up = jnp.dot(h, wu_ref[...], preferred_element_type=F32)
    act = (gate * (0.5 / (1.0 + jnp.exp(-gate))) * up).astype(BF16)
    o_ref[...] += jnp.dot(act, wd_ref[...], preferred_element_type=F32)

    if final_norm:
        @pl.when(j == pl.num_programs(1) - 1)
        def _():
            o_ref[...] = _rms(o_ref[...], pg_ref[...])


def _ffn(x, gain, w_gate_up, w_down, layer, post_gain=None, *, tm=512, tf=512):
    s, d = x.shape
    f = w_down.shape[1]
    tm = min(tm, s)
    tf = min(tf, f)
    assert s % tm == 0 and f % tf == 0
    nf = f // tf
    final_norm = post_gain is not None
    pg = post_gain if final_norm else gain
    return pl.pallas_call(
        functools.partial(_ffn_body, final_norm=final_norm),
        grid=(s // tm, nf),
        in_specs=[
            pl.BlockSpec((tm, d), lambda i, j: (i, 0)),
            pl.BlockSpec((1, d), lambda i, j: (0, 0)),
            pl.BlockSpec((None, d, tf), lambda i, j: (layer, 0, j)),
            pl.BlockSpec((None, d, tf), lambda i, j: (layer, 0, j + nf)),
            pl.BlockSpec((None, tf, d), lambda i, j: (layer, j, 0)),
            pl.BlockSpec((1, d), lambda i, j: (0, 0)),
        ],
        out_specs=pl.BlockSpec((tm, d), lambda i, j: (i, 0)),
        out_shape=jax.ShapeDtypeStruct((s, d), F32),
        scratch_shapes=[pltpu.VMEM((tm, d), BF16)],
        compiler_params=_params("parallel", "arbitrary"),
        name="ffn",
    )(x, gain.reshape(1, d), w_gate_up, w_gate_up, w_down, pg.reshape(1, d))


def _rope_heads(y, cos, sin, o_ref, scale, transposed=False):
    for h in range(y.shape[1] // HEAD_DIM):
        t = y[:, h * HEAD_DIM:(h + 1) * HEAD_DIM]
        r = t * cos + pltpu.roll(t, HEAD_DIM // 2, 1) * sin
        if scale != 1.0:
            r = r * scale
        if transposed:
            o_ref[h * HEAD_DIM:(h + 1) * HEAD_DIM, :] = r.T.astype(o_ref.dtype)
        else:
            o_ref[:, h * HEAD_DIM:(h + 1) * HEAD_DIM] = r.astype(o_ref.dtype)


def _store_value_chunks(y, o_ref):
    for b in range(y.shape[0] // CHUNK):
        for g in range(N_KV_HEADS):
            blk = y[b * CHUNK:(b + 1) * CHUNK, g * HEAD_DIM:(g + 1) * HEAD_DIM]
            o_ref[b, g * V_ROWS:g * V_ROWS + HEAD_DIM, :] = blk.T.astype(o_ref.dtype)
            o_ref[b, g * V_ROWS + HEAD_DIM:(g + 1) * V_ROWS, :] = jnp.ones((ONES_ROWS, CHUNK), o_ref.dtype)


def _rot_idx(t):
    lane = lax.broadcasted_iota(jnp.int32, t.shape, 1)
    first_half = (lane & (IDX_DIM // 2)) == 0
    return jnp.where(first_half, pltpu.roll(t, LANES - IDX_DIM // 2, 1),
                     pltpu.roll(t, IDX_DIM // 2, 1))


class _Piece(NamedTuple):
    kind: str
    start: int
    width: int
    lo: int
    hi: int
    dtype: Any


def _proj_body(*refs, pieces, q_scale, idx_w_scale):
    x_ref, g_ref, cos_ref, sin_ref, cosi_ref, sini_ref, gk_ref, w_ref = refs[:8]
    o_refs = refs[8:]
    h = _rms(x_ref[...], g_ref[...]).astype(BF16)
    products = {}
    for piece, o_ref in zip(pieces, o_refs):
        cols = (piece.start, piece.width)
        if cols not in products:
            products[cols] = jnp.dot(h, w_ref[:, piece.start:piece.start + piece.width],
                                     preferred_element_type=F32)
        y = products[cols]
        if (piece.lo, piece.hi) != (0, piece.width):
            y = y[:, piece.lo:piece.hi]
        kind = piece.kind
        if kind == "plain":
            o_ref[...] = y.astype(o_ref.dtype)
        elif kind == "scaled":
            o_ref[...] = (y * q_scale).astype(o_ref.dtype)
        elif kind == "rope":
            _rope_heads(y, cos_ref[...], sin_ref[...], o_ref, 1.0)
        elif kind == "rope_scaled_t":
            _rope_heads(y, cos_ref[...], sin_ref[...], o_ref, q_scale * LOG2_E, transposed=True)
        elif kind == "rope_idx_t":
            cosi, sini = cosi_ref[...], sini_ref[...]
            for c in range(y.shape[1] // LANES):
                t = y[:, c * LANES:(c + 1) * LANES]
                o_ref[c * LANES:(c + 1) * LANES, :] = (t * cosi + _rot_idx(t) * sini).T.astype(o_ref.dtype)
        elif kind == "value_chunks":
            _store_value_chunks(y, o_ref)
        elif kind == "kiwi":
            lane = lax.broadcasted_iota(jnp.int32, y.shape, 1)
            is_k = lane < IDX_DIM
            kk = jnp.where(is_k, y, 0.0)
            ms = jnp.sum(kk * kk, axis=-1, keepdims=True) * (1.0 / IDX_DIM)
            kn = kk * lax.rsqrt(ms + RMS_EPS) * gk_ref[...]
            kr = kn * cosi_ref[...] + _rot_idx(kn) * sini_ref[...]
            o_ref[...] = jnp.where(is_k, kr, y * idx_w_scale)
        else:
            raise ValueError(kind)


def _layer_weight_spec(w, layer):
    if w.ndim == 2:
        return pl.BlockSpec(w.shape, lambda i: (0, 0), pipeline_mode=pl.Buffered(1))
    return pl.BlockSpec((None,) + w.shape[1:], lambda i: (layer, 0, 0), pipeline_mode=pl.Buffered(1))


def _proj(x, gain, tables, gk, w, pieces, *, layer=0, tm=256):
    s, d = x.shape
    tm = min(tm, s)
    assert s % tm == 0
    cos, sin, cosi, sini = tables
    row = lambda i: (i, 0)
    const = lambda i: (0, 0)
    in_specs = [pl.BlockSpec((tm, d), row), pl.BlockSpec((1, d), const)]
    in_specs += [pl.BlockSpec((tm, LANES), row)] * 4
    in_specs += [pl.BlockSpec((1, LANES), const)]
    in_specs += [_layer_weight_spec(w, layer)]
    out_specs, out_shape = [], []
    for p in pieces:
        assert p.start % LANES == 0 and p.width % LANES == 0 and p.start + p.width <= w.shape[-1]
        n = p.hi - p.lo
        if p.kind.endswith("_t"):
            out_specs.append(pl.BlockSpec((n, tm), lambda i: (0, i)))
            out_shape.append(jax.ShapeDtypeStruct((n, s), p.dtype))
        elif p.kind == "value_chunks":
            assert tm % CHUNK == 0 and n == N_KV_HEADS * HEAD_DIM
            out_specs.append(pl.BlockSpec((tm // CHUNK, N_KV_HEADS * V_ROWS, CHUNK), lambda i: (i, 0, 0)))
            out_shape.append(jax.ShapeDtypeStruct((s // CHUNK, N_KV_HEADS * V_ROWS, CHUNK), p.dtype))
        else:
            out_specs.append(pl.BlockSpec((tm, n), row))
            out_shape.append(jax.ShapeDtypeStruct((s, n), p.dtype))
    return pl.pallas_call(
        functools.partial(_proj_body, pieces=tuple(pieces), q_scale=HEAD_DIM ** -0.5,
                          idx_w_scale=(IDX_HEADS ** -0.5) * (IDX_DIM ** -0.5)),
        grid=(s // tm,),
        in_specs=in_specs,
        out_specs=out_specs,
        out_shape=out_shape,
        compiler_params=_params("parallel"),
        name="proj",
    )(x, gain.reshape(1, d), cos, sin, cosi, sini, gk, w)


def _kv_shared_body(x_ref, g_ref, cos_ref, sin_ref, wk_ref, wv_ref, k_ref, v_ref, km_ref):
    h = _rms(x_ref[...], g_ref[...]).astype(BF16)
    yk = jnp.dot(h, wk_ref[...], preferred_element_type=F32)
    cos, sin = cos_ref[...], sin_ref[...]
    tm = yk.shape[0]
    first_blk = pl.program_id(0) * (tm // MOBA_BLOCK)
    lane = lax.broadcasted_iota(jnp.int32, (MOBA_BLOCK, HEAD_DIM), 1)
    for hd in range(yk.shape[1] // HEAD_DIM):
        t = yk[:, hd * HEAD_DIM:(hd + 1) * HEAD_DIM]
        r = t * cos + pltpu.roll(t, HEAD_DIM // 2, 1) * sin
        k_ref[:, 2 * hd * HEAD_DIM:(2 * hd + 1) * HEAD_DIM] = r.astype(k_ref.dtype)
        for b in range(tm // MOBA_BLOCK):
            rows = slice(b * MOBA_BLOCK, (b + 1) * MOBA_BLOCK)
            k_ref[rows, (2 * hd + 1) * HEAD_DIM:(2 * hd + 2) * HEAD_DIM] = (
                jnp.where(lane == first_blk + b, 1.0, 0.0).astype(k_ref.dtype))
            km_ref[b, :, hd * HEAD_DIM:(hd + 1) * HEAD_DIM] = (
                jnp.sum(r[rows], axis=0, keepdims=True) * (1.0 / MOBA_BLOCK))
    _store_value_chunks(jnp.dot(h, wv_ref[...], preferred_element_type=F32), v_ref)


def _kv_shared(x, gain, cos, sin, wk, wv):
    s, d = x.shape
    tm = MOBA_BLOCK
    assert s % tm == 0 and CHUNK == MOBA_BLOCK
    n = s // tm
    nkv = wk.shape[1]
    src = lambda i: (jnp.minimum(i, n - 1), 0)
    row = lambda i: (i, 0)
    const = lambda i: (0, 0)
    return pl.pallas_call(
        _kv_shared_body,
        grid=(n + 1,),
        in_specs=[pl.BlockSpec((tm, d), src), pl.BlockSpec((1, d), const),
                  pl.BlockSpec((tm, LANES), src), pl.BlockSpec((tm, LANES), src),
                  pl.BlockSpec(wk.shape, const, pipeline_mode=pl.Buffered(1)),
                  pl.BlockSpec(wv.shape, const, pipeline_mode=pl.Buffered(1))],
        out_specs=[pl.BlockSpec((tm, 2 * nkv), row),
                   pl.BlockSpec((1, N_KV_HEADS * V_ROWS, CHUNK), lambda i: (i, 0, 0)),
                   pl.BlockSpec((1, 1, nkv), lambda i: (i, 0, 0))],
        out_shape=[jax.ShapeDtypeStruct((s + tm, 2 * nkv), BF16),
                   jax.ShapeDtypeStruct((n + 1, N_KV_HEADS * V_ROWS, CHUNK), BF16),
                   jax.ShapeDtypeStruct((n + 1, 1, nkv), F32)],
        compiler_params=_params("parallel"),
        name="kv_shared",
    )(x, gain.reshape(1, d), cos, sin, wk, wv)


def _load_q_group(qt_ref, qall_ref):
    tq = qt_ref.shape[1]
    for hh in range(GROUP):
        qall_ref[:, hh * tq:(hh + 1) * tq] = qt_ref[hh * HEAD_DIM:(hh + 1) * HEAD_DIM, :]


def _softmax_init(m_ref, acc_ref):
    m_ref[...] = jnp.full(m_ref.shape, NEG, F32)
    acc_ref[...] = jnp.zeros(acc_ref.shape, F32)


def _add_group_bias(s, b):
    tq = b.shape[1]
    return jnp.concatenate([s[:, hh * tq:(hh + 1) * tq] + b for hh in range(GROUP)], axis=1)


def _logits_pass(chunk_ids, logits_fn, s_ref, cmax_ref):
    cmax = None
    for j, c in enumerate(chunk_ids):
        s = logits_fn(c)
        s_ref[j * CHUNK:(j + 1) * CHUNK, :] = s
        cm = jnp.max(s, axis=0, keepdims=True)
        cmax = cm if cmax is None else jnp.maximum(cmax, cm)
    cmax_ref[...] = cmax


def _softmax_pass(chunk_ids, vt_ref, s_ref, cmax_ref, m_ref, acc_ref):
    m_old = m_ref[...]
    m_new = jnp.maximum(m_old, cmax_ref[...])
    alpha = jnp.exp2(m_old - m_new)
    pv = None
    for j, c in enumerate(chunk_ids):
        p = jnp.exp2(s_ref[j * CHUNK:(j + 1) * CHUNK, :] - m_new)
        d = jnp.dot(vt_ref[c], p.astype(BF16), preferred_element_type=F32)
        pv = d if pv is None else pv + d
    m_ref[...] = m_new
    acc_ref[...] = alpha * acc_ref[...] + pv


def _attend_range(n_chunks, logits_fn, vt_ref, s_refs, cmax_refs, m_ref, acc_ref, *, fuse_pairs):
    n_steps = pl.cdiv(n_chunks, CHUNKS_PER_STEP)
    last_chunk = vt_ref.shape[0] - 1

    def chunks(step):
        return [step * CHUNKS_PER_STEP + j for j in range(CHUNKS_PER_STEP)]

    def logits(step, slot):
        _logits_pass(chunks(step), logits_fn, s_refs[slot], cmax_refs[slot])

    def softmax(step, slot):
        ids = [jnp.minimum(c, last_chunk) for c in chunks(step)]
        _softmax_pass(ids, vt_ref, s_refs[slot], cmax_refs[slot], m_ref, acc_ref)

    @pl.when(n_steps > 0)
    def _():
        logits(0, 0)

    def pair(u, carry):
        t = 2 * u
        if fuse_pairs:
            @pl.when(t + 1 < n_steps)
            def _():
                logits(t + 1, 1)
                softmax(t, 0)
                logits(t + 2, 0)
                softmax(t + 1, 1)

            @pl.when(t + 1 >= n_steps)
            def _():
                softmax(t, 0)
        else:
            logits(t + 1, 1)
            softmax(t, 0)

            @pl.when(t + 1 < n_steps)
            def _():
                logits(t + 2, 0)
                softmax(t + 1, 1)

        return carry

    lax.fori_loop(0, pl.cdiv(n_steps, 2), pair, 0)


def _softmax_finish(o_ref, acc_ref):
    tq = o_ref.shape[0]
    out = acc_ref[:HEAD_DIM, :] / acc_ref[HEAD_DIM:HEAD_DIM + 1, :]
    for hh in range(GROUP):
        o_ref[:, hh * HEAD_DIM:(hh + 1) * HEAD_DIM] = out[:, hh * tq:(hh + 1) * tq].T.astype(o_ref.dtype)


def _dsa_body(qit_ref, wt_ref, ki_ref, qt_ref, k_ref, vt_ref, o_ref,
              keys_ref, gmax_ref, cand_ref, qall_ref, sa_ref, sb_ref, cma_ref, cmb_ref,
              m_ref, acc_ref, *, topk):
    i = pl.program_id(0)
    g = pl.program_id(1)
    tq = CHUNK
    n_chunks = i + 1
    last_chunk = keys_ref.shape[0] // CHUNK - 1
    sub = 128

    def causal(c, rows, row_off=0):
        kpos = c * CHUNK + row_off + lax.broadcasted_iota(jnp.int32, (rows, tq), 0)
        qpos = i * tq + lax.broadcasted_iota(jnp.int32, (rows, tq), 1)
        return kpos <= qpos

    @pl.when(g == 0)
    def _():
        gmax_ref[...] = jnp.full(gmax_ref.shape, INT_MIN, jnp.int32)

        def score_chunk(c, carry):
            for part in range(CHUNK // sub):
                r0 = pl.multiple_of(c * CHUNK + part * sub, sub)
                kit = ki_ref[pl.ds(r0, sub), :]
                sc = jnp.zeros((sub, tq), F32)
                for h in range(IDX_HEADS):
                    d = jnp.dot(kit, qit_ref[h * IDX_DIM:(h + 1) * IDX_DIM, :], preferred_element_type=F32)
                    sc = sc + wt_ref[h:h + 1, :] * jnp.maximum(d, 0.0)
                sc = jnp.where(sc == 0.0, 0.0, sc)
                sc = jnp.where(causal(c, sub, part * sub), sc, -jnp.inf)
                bits = pltpu.bitcast(sc, jnp.int32)
                key = bits ^ ((bits >> 31) & INT_MAX)
                keys_ref[pl.ds(r0, sub), :] = key
                gsl = slice(part * sub, (part + 1) * sub)
                gmax_ref[gsl, :] = jnp.maximum(gmax_ref[gsl, :], key)
            return carry

        def chunk_loop(body, unroll):
            def group(u, carry):
                for j in range(unroll):
                    body(u * unroll + j, carry)
                return carry

            n_groups = n_chunks // unroll
            lax.fori_loop(0, n_groups, group, 0)
            lax.fori_loop(n_groups * unroll, n_chunks, body, 0)

        chunk_loop(score_chunk, SCORE_UNROLL)

        def count_rows(load, n_blocks, t):
            def count_block(b, acc):
                ge = jnp.where(load(b) >= t, 1, 0).astype(jnp.int32)
                return acc + jnp.sum(ge.reshape(CHUNK // SUBLANES, SUBLANES, tq), axis=0)

            part = jnp.zeros((SUBLANES, tq), jnp.int32)
            if isinstance(n_blocks, int):
                for b in range(n_blocks):
                    part = count_block(b, part)
            else:
                def count_group(u, acc):
                    for j in range(COUNT_UNROLL):
                        acc = count_block(u * COUNT_UNROLL + j, acc)
                    return acc

                n_groups = n_blocks // COUNT_UNROLL
                part = lax.fori_loop(0, n_groups, count_group, part)
                part = lax.fori_loop(n_groups * COUNT_UNROLL, n_blocks, count_block, part)
            return jnp.sum(part, axis=0, keepdims=True)

        def key_chunk(c):
            return keys_ref[pl.ds(pl.multiple_of(c * CHUNK, CHUNK), CHUNK), :]

        def count_all(t):
            return count_rows(key_chunk, n_chunks, t)

        def count_cand(t):
            return count_rows(lambda b: cand_ref[b], CAND_LEVELS, t)

        def any_set(flags):
            return jnp.max(flags) > 0.0

        def bisect_pass(count_fn, base, live, state):
            lo, hi, c_lo, c_hi, done = state
            mid = (lo >> 1) + (hi >> 1) + (lo & hi & 1)
            cnt = count_fn(mid) + base
            up = live & (cnt >= topk)
            down = live & (cnt < topk)
            lo, c_lo = jnp.where(up, mid, lo), jnp.where(up, cnt, c_lo)
            hi, c_hi = jnp.where(down, mid, hi), jnp.where(down, cnt, c_hi)
            finished = live & ((cnt == topk) | (hi <= lo + 1))
            return lo, hi, c_lo, c_hi, jnp.where(finished, 1.0, done)

        def bisect_while(count_fn, base, live_fn, keep_going, state):
            def cond(carry):
                it, state = carry
                return jnp.logical_and(it < MAX_PASSES, keep_going(state))

            def body(carry):
                it, state = carry
                return it + 1, bisect_pass(count_fn, base, live_fn(state), state)

            return lax.while_loop(cond, body, (jnp.int32(0), state))[1]

        def in_bracket(state):
            return state[2] - state[3]

        def active(state):
            return state[4] == 0.0

        gm = gmax_ref[...]
        lo = jnp.min(gm, axis=0, keepdims=True)
        top = jnp.max(gm, axis=0, keepdims=True)
        hi = jnp.where(top == INT_MAX, top, top + 1)
        c_lo = count_all(lo)
        state = (lo, hi, c_lo, jnp.zeros_like(c_lo), (c_lo == topk).astype(F32))

        def wide(state):
            return any_set(jnp.where(active(state) & (in_bracket(state) > CAND_MAX), 1.0, 0.0))

        state = bisect_while(count_all, 0, active, wide, state)

        lo, hi, c_lo, c_hi, _ = state
        cand_ref[...] = jnp.full(cand_ref.shape, INT_MIN, jnp.int32)

        def capture(c, carry):
            blk = key_chunk(c)
            x = jnp.where((blk >= lo) & (blk < hi), blk, INT_MIN)
            for level in range(CAND_LEVELS):
                held = cand_ref[level]
                cand_ref[level] = jnp.maximum(held, x)
                x = jnp.minimum(held, x)
            return carry

        chunk_loop(capture, 2)
        captured = count_cand(lo) == in_bracket(state)

        def live_captured(state):
            return active(state) & captured

        state = bisect_while(count_cand, c_hi, live_captured,
                             lambda st: any_set(jnp.where(live_captured(st), 1.0, 0.0)), state)

        state = bisect_while(count_all, 0, active,
                             lambda st: any_set(jnp.where(active(st), 1.0, 0.0)), state)
        thr = state[0]

        surplus = state[2] - topk

        @pl.when(jnp.max(surplus.astype(F32)) > 0.0)
        def _():
            row = lax.broadcasted_iota(jnp.int32, (CHUNK, CHUNK), 0)
            col = lax.broadcasted_iota(jnp.int32, (CHUNK, CHUNK), 1)
            prefix_ones = jnp.where(row >= col, 1.0, 0.0).astype(BF16)

            def count_equal(c, acc):
                return acc + jnp.sum(jnp.where(key_chunk(c) == thr, 1.0, 0.0), axis=0, keepdims=True)

            n_equal = lax.fori_loop(0, n_chunks, count_equal, jnp.zeros((1, tq), F32))
            keep = n_equal - surplus.astype(F32)

            def drop_late_ties(c, seen):
                blk = key_chunk(c)
                tie = blk == thr
                tie_f = jnp.where(tie, 1.0, 0.0)
                rank = seen + jnp.dot(prefix_ones, tie_f.astype(BF16), preferred_element_type=F32)
                r0 = pl.multiple_of(c * CHUNK, CHUNK)
                keys_ref[pl.ds(r0, CHUNK), :] = jnp.where(tie & (rank > keep), blk - 1, blk)
                return seen + jnp.sum(tie_f, axis=0, keepdims=True)

            lax.fori_loop(0, n_chunks, drop_late_ties, jnp.zeros((1, tq), F32))

        def to_bias(c, carry):
            r0 = pl.multiple_of(c * CHUNK, CHUNK)
            sel = (keys_ref[pl.ds(r0, CHUNK), :] >= thr) & causal(c, CHUNK)
            keys_ref[pl.ds(r0, CHUNK), :] = pltpu.bitcast(jnp.where(sel, 0.0, NEG), jnp.int32)
            return carry

        chunk_loop(to_bias, 2)

        @pl.when(n_chunks <= last_chunk)
        def _():
            r0 = pl.multiple_of(n_chunks * CHUNK, CHUNK)
            keys_ref[pl.ds(r0, CHUNK), :] = pltpu.bitcast(jnp.full((CHUNK, tq), NEG, F32), jnp.int32)

    _load_q_group(qt_ref, qall_ref)
    _softmax_init(m_ref, acc_ref)

    def logits(c):
        rb = pl.multiple_of(jnp.minimum(jnp.minimum(c, n_chunks), last_chunk) * CHUNK, CHUNK)
        rk = pl.multiple_of(jnp.minimum(c, n_chunks - 1) * CHUNK, CHUNK)
        s = jnp.dot(k_ref[pl.ds(rk, CHUNK), :], qall_ref[...], preferred_element_type=F32)
        return _add_group_bias(s, pltpu.bitcast(keys_ref[pl.ds(rb, CHUNK), :], F32))

    _attend_range(n_chunks, logits, vt_ref, (sa_ref, sb_ref), (cma_ref, cmb_ref), m_ref, acc_ref,
                  fuse_pairs=True)
    _softmax_finish(o_ref, acc_ref)


def _dsa(qit, wt, ki, qt, k, vtt, topk):
    s = ki.shape[0]
    tq = CHUNK
    assert s % (tq * CHUNKS_PER_STEP) == 0 and topk <= CHUNK
    gw = GROUP * HEAD_DIM
    return pl.pallas_call(
        functools.partial(_dsa_body, topk=topk),
        grid=(s // tq, N_KV_HEADS),
        in_specs=[
            pl.BlockSpec((IDX_HEADS * IDX_DIM, tq), lambda i, g: (0, i)),
            pl.BlockSpec((IDX_HEADS, tq), lambda i, g: (0, i)),
            pl.BlockSpec((s, IDX_DIM), lambda i, g: (0, 0), pipeline_mode=pl.Buffered(1)),
            pl.BlockSpec((gw, tq), lambda i, g: (g, i)),
            pl.BlockSpec((s, HEAD_DIM), lambda i, g: (0, g)),
            pl.BlockSpec((s // tq, V_ROWS, tq), lambda i, g: (0, g, 0)),
        ],
        out_specs=pl.BlockSpec((tq, gw), lambda i, g: (i, g)),
        out_shape=jax.ShapeDtypeStruct((s, N_MAIN_HEADS * HEAD_DIM), BF16),
        scratch_shapes=[
            pltpu.VMEM((s, tq), jnp.int32),
            pltpu.VMEM((CHUNK, tq), jnp.int32),
            pltpu.VMEM((CAND_LEVELS, CHUNK, tq), jnp.int32),
            pltpu.VMEM((HEAD_DIM, GROUP * tq), BF16),
            pltpu.VMEM((CHUNKS_PER_STEP * CHUNK, GROUP * tq), F32),
            pltpu.VMEM((CHUNKS_PER_STEP * CHUNK, GROUP * tq), F32),
            pltpu.VMEM((1, GROUP * tq), F32),
            pltpu.VMEM((1, GROUP * tq), F32),
            pltpu.VMEM((1, GROUP * tq), F32),
            pltpu.VMEM((V_ROWS, GROUP * tq), F32),
        ],
        compiler_params=_params("arbitrary", "arbitrary"),
        name="dsa",
    )(qit, wt, ki, qt, k, vtt)


def _moba_body(qt_ref, k_ref, vt_ref, km_ref, o_ref, qaug_ref, bias_ref, sa_ref, sb_ref, cma_ref, cmb_ref,
               m_ref, acc_ref, *, n_sel):
    cur = pl.program_id(1)
    tq = CHUNK
    nb = km_ref.shape[0]
    width = GROUP * tq
    for hh in range(GROUP):
        qaug_ref[:HEAD_DIM, hh * tq:(hh + 1) * tq] = qt_ref[hh * HEAD_DIM:(hh + 1) * HEAD_DIM, :]
    q_all = qaug_ref[:HEAD_DIM, :]
    _softmax_init(m_ref, acc_ref)

    tri = (lax.broadcasted_iota(jnp.int32, (tq, tq), 0) <= lax.broadcasted_iota(jnp.int32, (tq, tq), 1))

    def own_logits(c):
        r0 = pl.multiple_of(c * CHUNK, CHUNK)
        s = jnp.dot(k_ref[pl.ds(r0, CHUNK), :HEAD_DIM], q_all, preferred_element_type=F32)
        return _add_group_bias(s, jnp.where(tri, 0.0, NEG))

    _logits_pass([cur], own_logits, sa_ref, cma_ref)
    _softmax_pass([cur], vt_ref, sa_ref, cma_ref, m_ref, acc_ref)

    blk_id = lax.broadcasted_iota(jnp.int32, (nb, width), 0)
    past = blk_id < cur
    gate = jnp.dot(km_ref[...], q_all, preferred_element_type=F32)
    gate = jnp.where(past, gate, -jnp.inf)
    chosen = jnp.zeros((nb, width), jnp.bool_)
    for _ in range(n_sel):
        best = jnp.max(gate, axis=0, keepdims=True)
        first = jnp.min(jnp.where(gate == best, blk_id, nb), axis=0, keepdims=True)
        pick = blk_id == first
        chosen = chosen | pick
        gate = jnp.where(pick, -jnp.inf, gate)
    bias_ref[:nb, :] = jnp.where(chosen & past, 0.0, NEG)
    bias_ref[nb:, :] = jnp.full((SUBLANES, width), NEG, F32)

    state = (vt_ref, (sa_ref, sb_ref), (cma_ref, cmb_ref), m_ref, acc_ref)

    def past_logits(c):
        blk = jnp.where(c < cur, c, nb)
        r0 = pl.multiple_of(blk * CHUNK, CHUNK)
        s = jnp.dot(k_ref[pl.ds(r0, CHUNK), :HEAD_DIM], q_all, preferred_element_type=F32)
        return s + bias_ref[pl.ds(blk, 1), :]

    _attend_range(cur, past_logits, *state, fuse_pairs=True)
    _softmax_finish(o_ref, acc_ref)


def _moba(qt, k_aug, vtt, kmeans, n_sel):
    s = k_aug.shape[0] - CHUNK
    assert MOBA_BLOCK == CHUNK and s % (CHUNK * CHUNKS_PER_STEP) == 0
    tq = CHUNK
    nb = s // tq
    assert nb + BF16_TILE_ROWS <= HEAD_DIM and vtt.shape[0] == nb + 1
    gw = GROUP * HEAD_DIM
    return pl.pallas_call(
        functools.partial(_moba_body, n_sel=n_sel),
        grid=(N_KV_HEADS, nb),
        in_specs=[
            pl.BlockSpec((gw, tq), lambda g, i: (g, i)),
            pl.BlockSpec((s + CHUNK, 2 * HEAD_DIM), lambda g, i: (0, g)),
            pl.BlockSpec((nb + 1, V_ROWS, tq), lambda g, i: (0, g, 0)),
            pl.BlockSpec((nb, HEAD_DIM), lambda g, i: (0, g)),
        ],
        out_specs=pl.BlockSpec((tq, gw), lambda g, i: (i, g)),
        out_shape=jax.ShapeDtypeStruct((s, N_MAIN_HEADS * HEAD_DIM), BF16),
        scratch_shapes=[
            pltpu.VMEM((2 * HEAD_DIM, GROUP * tq), BF16),
            pltpu.VMEM((nb + SUBLANES, GROUP * tq), F32),
            pltpu.VMEM((CHUNKS_PER_STEP * CHUNK, GROUP * tq), F32),
            pltpu.VMEM((CHUNKS_PER_STEP * CHUNK, GROUP * tq), F32),
            pltpu.VMEM((1, GROUP * tq), F32),
            pltpu.VMEM((1, GROUP * tq), F32),
            pltpu.VMEM((1, GROUP * tq), F32),
            pltpu.VMEM((V_ROWS, GROUP * tq), F32),
        ],
        compiler_params=_params("parallel", "arbitrary"),
        name="moba",
    )(qt, k_aug, vtt, kmeans)


def _mem_attn_body(q_ref, kt_ref, v_ref, o_ref):
    for h in range(N_MEM_HEADS):
        sl = slice(h * HEAD_DIM, (h + 1) * HEAD_DIM)
        s = jnp.dot(q_ref[:, sl], kt_ref[sl, :], preferred_element_type=F32)
        p = jnp.exp(s - jnp.max(s, axis=-1, keepdims=True))
        o = jnp.dot(p.astype(BF16), v_ref[:, sl], preferred_element_type=F32)
        o_ref[:, sl] = (o / jnp.sum(p, axis=-1, keepdims=True)).astype(o_ref.dtype)


def _mem_attn(qm, kmt, vm, *, tq=512):
    s, w = qm.shape
    m = vm.shape[0]
    tq = min(tq, s)
    assert s % tq == 0
    return pl.pallas_call(
        _mem_attn_body,
        grid=(s // tq,),
        in_specs=[pl.BlockSpec((tq, w), lambda i: (i, 0)),
                  pl.BlockSpec((w, m), lambda i: (0, 0)),
                  pl.BlockSpec((m, w), lambda i: (0, 0))],
        out_specs=pl.BlockSpec((tq, w), lambda i: (i, 0)),
        out_shape=jax.ShapeDtypeStruct((s, w), BF16),
        compiler_params=_params("parallel"),
        name="mem_attn",
    )(qm, kmt, vm)


def _out_proj_body(x_ref, om_ref, oq_ref, w_ref, o_ref):
    n_main = om_ref.shape[1]
    o_ref[...] = (x_ref[...]
                  + jnp.dot(om_ref[...], w_ref[:n_main, :], preferred_element_type=F32)
                  + jnp.dot(oq_ref[...], w_ref[n_main:, :], preferred_element_type=F32))


def _out_proj(x, o_main, o_mem, w, *, layer=0, tm=512):
    s, d = x.shape
    tm = min(tm, s)
    assert s % tm == 0 and w.shape[-2] == o_main.shape[1] + o_mem.shape[1]
    row = lambda i: (i, 0)
    const = lambda i: (0, 0)
    return pl.pallas_call(
        _out_proj_body,
        grid=(s // tm,),
        in_specs=[pl.BlockSpec((tm, d), row),
                  pl.BlockSpec((tm, o_main.shape[1]), row),
                  pl.BlockSpec((tm, o_mem.shape[1]), row),
                  _layer_weight_spec(w, layer)],
        out_specs=pl.BlockSpec((tm, d), row),
        out_shape=jax.ShapeDtypeStruct((s, d), F32),
        compiler_params=_params("parallel"),
        name="out_proj",
    )(x, o_main, o_mem, w)


def _rope_tables(positions, dim):
    inv = 1.0 / (ROPE_THETA ** (jnp.arange(0, dim, 2, dtype=F32) / dim))
    ang = positions.astype(F32)[:, None] * inv
    c, s = jnp.cos(ang), jnp.sin(ang)
    reps = LANES // dim
    return jnp.tile(jnp.concatenate([c, c], -1), (1, reps)), jnp.tile(jnp.concatenate([-s, s], -1), (1, reps))


def kernel(x, mem, positions, ffn1_norm, ffn1_w_gate_up, ffn1_w_down, attn_norm, mem_norm, a_w_in, idx_k_norm, b_w_in, w_mem_kv, w_out, ffn2_norm, ffn2_w_gate_up, ffn2_w_down, kv_norm, w_kv_shared, final_norm):
    b, s, d = x.shape
    assert b == 1 and mem.shape[0] == 1
    depth = ffn1_norm.shape[0]
    n_a = a_w_in.shape[0]
    main_w = N_MAIN_HEADS * HEAD_DIM
    kv_w = N_KV_HEADS * HEAD_DIM
    idx_w = IDX_HEADS * IDX_DIM
    mem_w = N_MEM_HEADS * HEAD_DIM
    topk = min(IDX_TOPK_MAX, s // 4)
    nb = s // MOBA_BLOCK
    n_sel = min(MOBA_TOPK_MAX, max(nb - 1, 1))

    cos, sin = _rope_tables(positions[0], HEAD_DIM)
    cosi, sini = _rope_tables(positions[0], IDX_DIM)
    tables = (cos, sin, cosi, sini)
    mem_tables = tuple(t[:mem.shape[1]] for t in tables)
    no_gk = jnp.zeros((1, LANES), F32)

    ffn_w = [w.astype(BF16) for w in (ffn1_w_gate_up, ffn1_w_down, ffn2_w_gate_up, ffn2_w_down)]
    w_mem_all = w_mem_kv.astype(BF16)
    w_out_all = w_out.astype(BF16)

    xs = x[0]
    mem2 = mem[0]
    k_sh = vtt_sh = kmeans = None
    for i in range(depth):
        if i == n_a:
            wk = w_kv_shared[:, :kv_w].astype(BF16)
            wv = w_kv_shared[:, kv_w:].astype(BF16)
            k_sh, vtt_sh, km = _kv_shared(xs, kv_norm, cos, sin, wk, wv)
            kmeans = km[:nb].reshape(nb, kv_w).astype(BF16)

        last = i == depth - 1
        xs = _ffn(xs, ffn1_norm[i], ffn_w[0], ffn_w[1], i)

        mk, mv = _proj(mem2, mem_norm[i], mem_tables, no_gk, w_mem_all,
                       [_Piece("plain", 0, mem_w, 0, mem_w, BF16),
                        _Piece("plain", mem_w, mem_w, 0, mem_w, BF16)], layer=i)
        if i < n_a:
            tail = main_w + 2 * kv_w + idx_w
            kiwi_w = IDX_DIM + IDX_HEADS
            assert a_w_in.shape[2] == tail + kiwi_w + mem_w and kiwi_w <= LANES
            wa = a_w_in[i].astype(BF16)
            wa = jnp.concatenate([wa[:, :tail + kiwi_w], jnp.zeros((d, LANES - kiwi_w), BF16),
                                  wa[:, tail + kiwi_w:]], axis=1)
            gk = jnp.pad(idx_k_norm[i], (0, LANES - IDX_DIM)).reshape(1, LANES)
            qt, k, vtt, qit, kiwi, qm = _proj(
                xs, attn_norm[i], tables, gk, wa,
                [_Piece("rope_scaled_t", 0, main_w, 0, main_w, BF16),
                 _Piece("rope", main_w, kv_w, 0, kv_w, BF16),
                 _Piece("value_chunks", main_w + kv_w, kv_w, 0, kv_w, BF16),
                 _Piece("rope_idx_t", main_w + 2 * kv_w, idx_w, 0, idx_w, BF16),
                 _Piece("kiwi", tail, LANES, 0, LANES, F32),
                 _Piece("scaled", tail + LANES, mem_w, 0, mem_w, BF16)])
            ki = kiwi[:, :IDX_DIM].astype(BF16)
            wt = kiwi[:, IDX_DIM:IDX_DIM + IDX_HEADS].T
            o_main = _dsa(qit, wt, ki, qt, k, vtt, topk)
        else:
            qt, qm = _proj(xs, attn_norm[i], tables, no_gk, b_w_in[i - n_a].astype(BF16),
                           [_Piece("rope_scaled_t", 0, main_w, 0, main_w, BF16),
                            _Piece("scaled", main_w, mem_w, 0, mem_w, BF16)])
            o_main = _moba(qt, k_sh, vtt_sh, kmeans, n_sel)
        o_mem = _mem_attn(qm, mk.T, mv)
        xs = _out_proj(xs, o_main, o_mem, w_out_all, layer=i)

        xs = _ffn(xs, ffn2_norm[i], ffn_w[2], ffn_w[3], i, final_norm if last else None)
    return xs[None]
```

```python
import functools
from typing import Any, NamedTuple

import jax
import jax.numpy as jnp
from jax import lax
from jax.experimental import pallas as pl
from jax.experimental.pallas import tpu as pltpu

HEAD_DIM = 128
N_MAIN_HEADS = 12
N_KV_HEADS = 4
GROUP = N_MAIN_HEADS // N_KV_HEADS
N_MEM_HEADS = 4
IDX_HEADS = 16
IDX_DIM = 64
IDX_TOPK_MAX = 256
MOBA_BLOCK = 256
MOBA_TOPK_MAX = 3
ROPE_THETA = 10000.0
RMS_EPS = 1e-6

LANES = 128
SUBLANES = 8
VMEM_LIMIT = 56 * 1024 * 1024
NEG = -1e30
LOG2_E = 1.4426950408889634
INT_MIN = -2 ** 31
INT_MAX = 2 ** 31 - 1
CHUNK = 256
CHUNKS_PER_STEP = 4
CAND_LEVELS = 2
CAND_MAX = 16
MAX_PASSES = 40
COUNT_UNROLL = 4
SCORE_UNROLL = 4
ONES_ROWS = 2 * SUBLANES
V_ROWS = HEAD_DIM + ONES_ROWS

F32 = jnp.float32
BF16 = jnp.bfloat16


def _params(*sem):
    return pltpu.CompilerParams(dimension_semantics=sem, vmem_limit_bytes=VMEM_LIMIT)


def _rms(x, gain):
    return x * lax.rsqrt(jnp.mean(x * x, axis=-1, keepdims=True) + RMS_EPS) * gain


def _ffn_body(x_ref, g_ref, wg_ref, wu_ref, wd_ref, pg_ref, o_ref, h_ref, *, final_norm):
    j = pl.program_id(1)

    @pl.when(j == 0)
    def _():
        x = x_ref[...]
        h_ref[...] = _rms(x, g_ref[...]).astype(BF16)
        o_ref[...] = x

    h = h_ref[...]
    gate = jnp.dot(h, wg_ref[...], preferred_element_type=F32)
    up = jnp.dot(h, wu_ref[...], preferred_element_type=F32)
    act = (gate * (0.5 / (1.0 + jnp.exp(-gate))) * up).astype(BF16)
    o_ref[...] += jnp.dot(act, wd_ref[...], preferred_element_type=F32)

    if final_norm:
        @pl.when(j == pl.num_programs(1) - 1)
        def _():
            o_ref[...] = _rms(o_ref[...], pg_ref[...])


def _ffn(x, gain, w_gate_up, w_down, layer, post_gain=None, *, tm=512, tf=512):
    s, d = x.shape
    f = w_down.shape[1]
    tm = min(tm, s)
    tf = min(tf, f)
    assert s % tm == 0 and f % tf == 0
    nf = f // tf
    final_norm = post_gain is not None
    pg = post_gain if final_norm else gain
    return pl.pallas_call(
        functools.partial(_ffn_body, final_norm=final_norm),
        grid=(s // tm, nf),
        in_specs=[
            pl.BlockSpec((tm, d), lambda i, j: (i, 0)),
            pl.BlockSpec((1, d), lambda i, j: (0, 0)),
            pl.BlockSpec((None, d, tf), lambda i, j: (layer, 0, j)),
            pl.BlockSpec((None, d, tf), lambda i, j: (layer, 0, j + nf)),
            pl.BlockSpec((None, tf, d), lambda i, j: (layer, j, 0)),
            pl.BlockSpec((1, d), lambda i, j: (0, 0)),
        ],
        out_specs=pl.BlockSpec((tm, d), lambda i, j: (i, 0)),
        out_shape=jax.ShapeDtypeStruct((s, d), F32),
        scratch_shapes=[pltpu.VMEM((tm, d), BF16)],
        compiler_params=_params("parallel", "arbitrary"),
        name="ffn",
    )(x, gain.reshape(1, d), w_gate_up, w_gate_up, w_down, pg.reshape(1, d))


def _rope_heads(y, cos, sin, o_ref, scale, transposed=False):
    for h in range(y.shape[1] // HEAD_DIM):
        t = y[:, h * HEAD_DIM:(h + 1) * HEAD_DIM]
        r = t * cos + pltpu.roll(t, HEAD_DIM // 2, 1) * sin
        if scale != 1.0:
            r = r * scale
        if transposed:
            o_ref[h * HEAD_DIM:(h + 1) * HEAD_DIM, :] = r.T.astype(o_ref.dtype)
        else:
            o_ref[:, h * HEAD_DIM:(h + 1) * HEAD_DIM] = r.astype(o_ref.dtype)


def _store_value_chunks(y, o_ref):
    for b in range(y.shape[0] // CHUNK):
        for g in range(N_KV_HEADS):
            blk = y[b * CHUNK:(b + 1) * CHUNK, g * HEAD_DIM:(g + 1) * HEAD_DIM]
            o_ref[b, g * V_ROWS:g * V_ROWS + HEAD_DIM, :] = blk.T.astype(o_ref.dtype)
            o_ref[b, g * V_ROWS + HEAD_DIM:(g + 1) * V_ROWS, :] = jnp.ones((ONES_ROWS, CHUNK), o_ref.dtype)


def _rot_idx(t):
    lane = lax.broadcasted_iota(jnp.int32, t.shape, 1)
    first_half = (lane & (IDX_DIM // 2)) == 0
    return jnp.where(first_half, pltpu.roll(t, LANES - IDX_DIM // 2, 1),
                     pltpu.roll(t, IDX_DIM // 2, 1))


class _Piece(NamedTuple):
    kind: str
    start: int
    width: int
    lo: int
    hi: int
    dtype: Any


def _proj_body(*refs, pieces, q_scale, idx_w_scale):
    x_ref, g_ref, cos_ref, sin_ref, cosi_ref, sini_ref, gk_ref, w_ref = refs[:8]
    o_refs = refs[8:]
    h = _rms(x_ref[...], g_ref[...]).astype(BF16)
    products = {}
    for piece, o_ref in zip(pieces, o_refs):
        cols = (piece.start, piece.width)
        if cols not in products:
            products[cols] = jnp.dot(h, w_ref[:, piece.start:piece.start + piece.width],
                                     preferred_element_type=F32)
        y = products[cols]
        if (piece.lo, piece.hi) != (0, piece.width):
            y = y[:, piece.lo:piece.hi]
        kind = piece.kind
        if kind == "plain":
            o_ref[...] = y.astype(o_ref.dtype)
        elif kind == "scaled":
            o_ref[...] = (y * q_scale).astype(o_ref.dtype)
        elif kind == "rope":
            _rope_heads(y, cos_ref[...], sin_ref[...], o_ref, 1.0)
        elif kind == "rope_scaled_t":
            _rope_heads(y, cos_ref[...], sin_ref[...], o_ref, q_scale * LOG2_E, transposed=True)
        elif kind == "rope_idx_t":
            cosi, sini = cosi_ref[...], sini_ref[...]
            for c in range(y.shape[1] // LANES):
                t = y[:, c * LANES:(c + 1) * LANES]
                o_ref[c * LANES:(c + 1) * LANES, :] = (t * cosi + _rot_idx(t) * sini).T.astype(o_ref.dtype)
        elif kind == "value_chunks":
            _store_value_chunks(y, o_ref)
        elif kind == "kiwi":
            lane = lax.broadcasted_iota(jnp.int32, y.shape, 1)
            is_k = lane < IDX_DIM
            kk = jnp.where(is_k, y, 0.0)
            ms = jnp.sum(kk * kk, axis=-1, keepdims=True) * (1.0 / IDX_DIM)
            kn = kk * lax.rsqrt(ms + RMS_EPS) * gk_ref[...]
            kr = kn * cosi_ref[...] + _rot_idx(kn) * sini_ref[...]
            o_ref[...] = jnp.where(is_k, kr, y * idx_w_scale)
        else:
            raise ValueError(kind)


def _layer_weight_spec(w, layer):
    if w.ndim == 2:
        return pl.BlockSpec(w.shape, lambda i: (0, 0), pipeline_mode=pl.Buffered(1))
    return pl.BlockSpec((None,) + w.shape[1:], lambda i: (layer, 0, 0), pipeline_mode=pl.Buffered(1))


def _proj(x, gain, tables, gk, w, pieces, *, layer=0, tm=256):
    s, d = x.shape
    tm = min(tm, s)
    assert s % tm == 0
    cos, sin, cosi, sini = tables
    row = lambda i: (i, 0)
    const = lambda i: (0, 0)
    in_specs = [pl.BlockSpec((tm, d), row), pl.BlockSpec((1, d), const)]
    in_specs += [pl.BlockSpec((tm, LANES), row)] * 4
    in_specs += [pl.BlockSpec((1, LANES), const)]
    in_specs += [_layer_weight_spec(w, layer)]
    out_specs, out_shape = [], []
    for p in pieces:
        assert p.start % LANES == 0 and p.width % LANES == 0 and p.start + p.width <= w.shape[-1]
        n = p.hi - p.lo
        if p.kind.endswith("_t"):
            out_specs.append(pl.BlockSpec((n, tm), lambda i: (0, i)))
            out_shape.append(jax.ShapeDtypeStruct((n, s), p.dtype))
        elif p.kind == "value_chunks":
            assert tm % CHUNK == 0 and n == N_KV_HEADS * HEAD_DIM
            out_specs.append(pl.BlockSpec((tm // CHUNK, N_KV_HEADS * V_ROWS, CHUNK), lambda i: (i, 0, 0)))
            out_shape.append(jax.ShapeDtypeStruct((s // CHUNK, N_KV_HEADS * V_ROWS, CHUNK), p.dtype))
        else:
            out_specs.append(pl.BlockSpec((tm, n), row))
            out_shape.append(jax.ShapeDtypeStruct((s, n), p.dtype))
    return pl.pallas_call(
        functools.partial(_proj_body, pieces=tuple(pieces), q_scale=HEAD_DIM ** -0.5,
                          idx_w_scale=(IDX_HEADS ** -0.5) * (IDX_DIM ** -0.5)),
        grid=(s // tm,),
        in_specs=in_specs,
        out_specs=out_specs,
        out_shape=out_shape,
        compiler_params=_params("parallel"),
        name="proj",
    )(x, gain.reshape(1, d), cos, sin, cosi, sini, gk, w)


def _kv_shared_body(x_ref, g_ref, cos_ref, sin_ref, wk_ref, wv_ref, k_ref, v_ref, km_ref):
    h = _rms(x_ref[...], g_ref[...]).astype(BF16)
    yk = jnp.dot(h, wk_ref[...], preferred_element_type=F32)
    cos, sin = cos_ref[...], sin_ref[...]
    tm = yk.shape[0]
    for hd in range(yk.shape[1] // HEAD_DIM):
        t = yk[:, hd * HEAD_DIM:(hd + 1) * HEAD_DIM]
        r = t * cos + pltpu.roll(t, HEAD_DIM // 2, 1) * sin
        k_ref[:, hd * HEAD_DIM:(hd + 1) * HEAD_DIM] = r.astype(k_ref.dtype)
        for b in range(tm // MOBA_BLOCK):
            rows = slice(b * MOBA_BLOCK, (b + 1) * MOBA_BLOCK)
            km_ref[b, :, hd * HEAD_DIM:(hd + 1) * HEAD_DIM] = (
                jnp.sum(r[rows], axis=0, keepdims=True) * (1.0 / MOBA_BLOCK))
    _store_value_chunks(jnp.dot(h, wv_ref[...], preferred_element_type=F32), v_ref)


def _kv_shared(x, gain, cos, sin, wk, wv, *, tm=256):
    s, d = x.shape
    tm = min(tm, s)
    assert s % tm == 0 and tm % MOBA_BLOCK == 0 and CHUNK == MOBA_BLOCK
    nkv = wk.shape[1]
    row = lambda i: (i, 0)
    const = lambda i: (0, 0)
    return pl.pallas_call(
        _kv_shared_body,
        grid=(s // tm,),
        in_specs=[pl.BlockSpec((tm, d), row), pl.BlockSpec((1, d), const),
                  pl.BlockSpec((tm, LANES), row), pl.BlockSpec((tm, LANES), row),
                  pl.BlockSpec(wk.shape, const, pipeline_mode=pl.Buffered(1)),
                  pl.BlockSpec(wv.shape, const, pipeline_mode=pl.Buffered(1))],
        out_specs=[pl.BlockSpec((tm, nkv), row),
                   pl.BlockSpec((tm // CHUNK, N_KV_HEADS * V_ROWS, CHUNK), lambda i: (i, 0, 0)),
                   pl.BlockSpec((tm // MOBA_BLOCK, 1, nkv), lambda i: (i, 0, 0))],
        out_shape=[jax.ShapeDtypeStruct((s, nkv), BF16),
                   jax.ShapeDtypeStruct((s // CHUNK, N_KV_HEADS * V_ROWS, CHUNK), BF16),
                   jax.ShapeDtypeStruct((s // MOBA_BLOCK, 1, nkv), F32)],
        compiler_params=_params("parallel"),
        name="kv_shared",
    )(x, gain.reshape(1, d), cos, sin, wk, wv)


def _load_q_group(qt_ref, qall_ref):
    tq = qt_ref.shape[1]
    for hh in range(GROUP):
        qall_ref[:, hh * tq:(hh + 1) * tq] = qt_ref[hh * HEAD_DIM:(hh + 1) * HEAD_DIM, :]


def _softmax_init(m_ref, acc_ref):
    m_ref[...] = jnp.full(m_ref.shape, NEG, F32)
    acc_ref[...] = jnp.zeros(acc_ref.shape, F32)


def _add_group_bias(s, b):
    tq = b.shape[1]
    return jnp.concatenate([s[:, hh * tq:(hh + 1) * tq] + b for hh in range(GROUP)], axis=1)


def _logits_pass(chunk_ids, logits_fn, s_ref, cmax_ref):
    cmax = None
    for j, c in enumerate(chunk_ids):
        s = logits_fn(c)
        s_ref[j * CHUNK:(j + 1) * CHUNK, :] = s
        cm = jnp.max(s, axis=0, keepdims=True)
        cmax = cm if cmax is None else jnp.maximum(cmax, cm)
    cmax_ref[...] = cmax


def _softmax_pass(chunk_ids, vt_ref, s_ref, cmax_ref, m_ref, acc_ref):
    m_old = m_ref[...]
    m_new = jnp.maximum(m_old, cmax_ref[...])
    alpha = jnp.exp2(m_old - m_new)
    pv = None
    for j, c in enumerate(chunk_ids):
        p = jnp.exp2(s_ref[j * CHUNK:(j + 1) * CHUNK, :] - m_new)
        d = jnp.dot(vt_ref[c], p.astype(BF16), preferred_element_type=F32)
        pv = d if pv is None else pv + d
    m_ref[...] = m_new
    acc_ref[...] = alpha * acc_ref[...] + pv


def _attend_range(n_chunks, logits_fn, vt_ref, s_refs, cmax_refs, m_ref, acc_ref, *, fuse_pairs):
    n_steps = pl.cdiv(n_chunks, CHUNKS_PER_STEP)
    last_chunk = vt_ref.shape[0] - 1

    def chunks(step):
        return [step * CHUNKS_PER_STEP + j for j in range(CHUNKS_PER_STEP)]

    def logits(step, slot):
        _logits_pass(chunks(step), logits_fn, s_refs[slot], cmax_refs[slot])

    def softmax(step, slot):
        ids = [jnp.minimum(c, last_chunk) for c in chunks(step)]
        _softmax_pass(ids, vt_ref, s_refs[slot], cmax_refs[slot], m_ref, acc_ref)

    @pl.when(n_steps > 0)
    def _():
        logits(0, 0)

    def pair(u, carry):
        t = 2 * u
        if fuse_pairs:
            @pl.when(t + 1 < n_steps)
            def _():
                logits(t + 1, 1)
                softmax(t, 0)
                logits(t + 2, 0)
                softmax(t + 1, 1)

            @pl.when(t + 1 >= n_steps)
            def _():
                softmax(t, 0)
        else:
            logits(t + 1, 1)
            softmax(t, 0)

            @pl.when(t + 1 < n_steps)
            def _():
                logits(t + 2, 0)
                softmax(t + 1, 1)

        return carry

    lax.fori_loop(0, pl.cdiv(n_steps, 2), pair, 0)


def _softmax_finish(o_ref, acc_ref):
    tq = o_ref.shape[0]
    out = acc_ref[:HEAD_DIM, :] / acc_ref[HEAD_DIM:HEAD_DIM + 1, :]
    for hh in range(GROUP):
        o_ref[:, hh * HEAD_DIM:(hh + 1) * HEAD_DIM] = out[:, hh * tq:(hh + 1) * tq].T.astype(o_ref.dtype)


def _dsa_body(qit_ref, wt_ref, ki_ref, qt_ref, k_ref, vt_ref, o_ref,
              keys_ref, gmax_ref, cand_ref, qall_ref, sa_ref, sb_ref, cma_ref, cmb_ref,
              m_ref, acc_ref, *, topk):
    i = pl.program_id(0)
    g = pl.program_id(1)
    tq = CHUNK
    n_chunks = i + 1
    last_chunk = keys_ref.shape[0] // CHUNK - 1
    sub = 128

    def causal(c, rows, row_off=0):
        kpos = c * CHUNK + row_off + lax.broadcasted_iota(jnp.int32, (rows, tq), 0)
        qpos = i * tq + lax.broadcasted_iota(jnp.int32, (rows, tq), 1)
        return kpos <= qpos

    @pl.when(g == 0)
    def _():
        gmax_ref[...] = jnp.full(gmax_ref.shape, INT_MIN, jnp.int32)

        def score_chunk(c, carry):
            for part in range(CHUNK // sub):
                r0 = pl.multiple_of(c * CHUNK + part * sub, sub)
                kit = ki_ref[pl.ds(r0, sub), :]
                sc = jnp.zeros((sub, tq), F32)
                for h in range(IDX_HEADS):
                    d = jnp.dot(kit, qit_ref[h * IDX_DIM:(h + 1) * IDX_DIM, :], preferred_element_type=F32)
                    sc = sc + wt_ref[h:h + 1, :] * jnp.maximum(d, 0.0)
                sc = jnp.where(sc == 0.0, 0.0, sc)
                sc = jnp.where(causal(c, sub, part * sub), sc, -jnp.inf)
                bits = pltpu.bitcast(sc, jnp.int32)
                key = bits ^ ((bits >> 31) & INT_MAX)
                keys_ref[pl.ds(r0, sub), :] = key
                gsl = slice(part * sub, (part + 1) * sub)
                gmax_ref[gsl, :] = jnp.maximum(gmax_ref[gsl, :], key)
            return carry

        def chunk_loop(body, unroll):
            def group(u, carry):
                for j in range(unroll):
                    body(u * unroll + j, carry)
                return carry

            n_groups = n_chunks // unroll
            lax.fori_loop(0, n_groups, group, 0)
            lax.fori_loop(n_groups * unroll, n_chunks, body, 0)

        chunk_loop(score_chunk, SCORE_UNROLL)

        def count_rows(load, n_blocks, t):
            def count_block(b, acc):
                ge = jnp.where(load(b) >= t, 1, 0).astype(jnp.int32)
                return acc + jnp.sum(ge.reshape(CHUNK // SUBLANES, SUBLANES, tq), axis=0)

            part = jnp.zeros((SUBLANES, tq), jnp.int32)
            if isinstance(n_blocks, int):
                for b in range(n_blocks):
                    part = count_block(b, part)
            else:
                def count_group(u, acc):
                    for j in range(COUNT_UNROLL):
                        acc = count_block(u * COUNT_UNROLL + j, acc)
                    return acc

                n_groups = n_blocks // COUNT_UNROLL
                part = lax.fori_loop(0, n_groups, count_group, part)
                part = lax.fori_loop(n_groups * COUNT_UNROLL, n_blocks, count_block, part)
            return jnp.sum(part, axis=0, keepdims=True)

        def key_chunk(c):
            return keys_ref[pl.ds(pl.multiple_of(c * CHUNK, CHUNK), CHUNK), :]

        def count_all(t):
            return count_rows(key_chunk, n_chunks, t)

        def count_cand(t):
            return count_rows(lambda b: cand_ref[b], CAND_LEVELS, t)

        def any_set(flags):
            return jnp.max(flags) > 0.0

        def bisect_pass(count_fn, base, live, state):
            lo, hi, c_lo, c_hi, done = state
            mid = (lo >> 1) + (hi >> 1) + (lo & hi & 1)
            cnt = count_fn(mid) + base
            up = live & (cnt >= topk)
            down = live & (cnt < topk)
            lo, c_lo = jnp.where(up, mid, lo), jnp.where(up, cnt, c_lo)
            hi, c_hi = jnp.where(down, mid, hi), jnp.where(down, cnt, c_hi)
            finished = live & ((cnt == topk) | (hi <= lo + 1))
            return lo, hi, c_lo, c_hi, jnp.where(finished, 1.0, done)

        def bisect_while(count_fn, base, live_fn, keep_going, state):
            def cond(carry):
                it, state = carry
                return jnp.logical_and(it < MAX_PASSES, keep_going(state))

            def body(carry):
                it, state = carry
                return it + 1, bisect_pass(count_fn, base, live_fn(state), state)

            return lax.while_loop(cond, body, (jnp.int32(0), state))[1]

        def in_bracket(state):
            return state[2] - state[3]

        def active(state):
            return state[4] == 0.0

        gm = gmax_ref[...]
        lo = jnp.min(gm, axis=0, keepdims=True)
        top = jnp.max(gm, axis=0, keepdims=True)
        hi = jnp.where(top == INT_MAX, top, top + 1)
        c_lo = count_all(lo)
        state = (lo, hi, c_lo, jnp.zeros_like(c_lo), (c_lo == topk).astype(F32))

        def wide(state):
            return any_set(jnp.where(active(state) & (in_bracket(state) > CAND_MAX), 1.0, 0.0))

        state = bisect_while(count_all, 0, active, wide, state)

        lo, hi, c_lo, c_hi, _ = state
        cand_ref[...] = jnp.full(cand_ref.shape, INT_MIN, jnp.int32)

        def capture(c, carry):
            blk = key_chunk(c)
            x = jnp.where((blk >= lo) & (blk < hi), blk, INT_MIN)
            for level in range(CAND_LEVELS):
                held = cand_ref[level]
                cand_ref[level] = jnp.maximum(held, x)
                x = jnp.minimum(held, x)
            return carry

        chunk_loop(capture, 2)
        captured = count_cand(lo) == in_bracket(state)

        def live_captured(state):
            return active(state) & captured

        state = bisect_while(count_cand, c_hi, live_captured,
                             lambda st: any_set(jnp.where(live_captured(st), 1.0, 0.0)), state)

        state = bisect_while(count_all, 0, active,
                             lambda st: any_set(jnp.where(active(st), 1.0, 0.0)), state)
        thr = state[0]

        surplus = state[2] - topk

        @pl.when(jnp.max(surplus.astype(F32)) > 0.0)
        def _():
            row = lax.broadcasted_iota(jnp.int32, (CHUNK, CHUNK), 0)
            col = lax.broadcasted_iota(jnp.int32, (CHUNK, CHUNK), 1)
            prefix_ones = jnp.where(row >= col, 1.0, 0.0).astype(BF16)

            def count_equal(c, acc):
                return acc + jnp.sum(jnp.where(key_chunk(c) == thr, 1.0, 0.0), axis=0, keepdims=True)

            n_equal = lax.fori_loop(0, n_chunks, count_equal, jnp.zeros((1, tq), F32))
            keep = n_equal - surplus.astype(F32)

            def drop_late_ties(c, seen):
                blk = key_chunk(c)
                tie = blk == thr
                tie_f = jnp.where(tie, 1.0, 0.0)
                rank = seen + jnp.dot(prefix_ones, tie_f.astype(BF16), preferred_element_type=F32)
                r0 = pl.multiple_of(c * CHUNK, CHUNK)
                keys_ref[pl.ds(r0, CHUNK), :] = jnp.where(tie & (rank > keep), blk - 1, blk)
                return seen + jnp.sum(tie_f, axis=0, keepdims=True)

            lax.fori_loop(0, n_chunks, drop_late_ties, jnp.zeros((1, tq), F32))

        def to_bias(c, carry):
            r0 = pl.multiple_of(c * CHUNK, CHUNK)
            sel = (keys_ref[pl.ds(r0, CHUNK), :] >= thr) & causal(c, CHUNK)
            keys_ref[pl.ds(r0, CHUNK), :] = pltpu.bitcast(jnp.where(sel, 0.0, NEG), jnp.int32)
            return carry

        chunk_loop(to_bias, 2)

        @pl.when(n_chunks <= last_chunk)
        def _():
            r0 = pl.multiple_of(n_chunks * CHUNK, CHUNK)
            keys_ref[pl.ds(r0, CHUNK), :] = pltpu.bitcast(jnp.full((CHUNK, tq), NEG, F32), jnp.int32)

    _load_q_group(qt_ref, qall_ref)
    _softmax_init(m_ref, acc_ref)

    def logits(c):
        rb = pl.multiple_of(jnp.minimum(jnp.minimum(c, n_chunks), last_chunk) * CHUNK, CHUNK)
        rk = pl.multiple_of(jnp.minimum(c, n_chunks - 1) * CHUNK, CHUNK)
        s = jnp.dot(k_ref[pl.ds(rk, CHUNK), :], qall_ref[...], preferred_element_type=F32)
        return _add_group_bias(s, pltpu.bitcast(keys_ref[pl.ds(rb, CHUNK), :], F32))

    _attend_range(n_chunks, logits, vt_ref, (sa_ref, sb_ref), (cma_ref, cmb_ref), m_ref, acc_ref,
                  fuse_pairs=True)
    _softmax_finish(o_ref, acc_ref)


def _dsa(qit, wt, ki, qt, k, vtt, topk):
    s = ki.shape[0]
    tq = CHUNK
    assert s % (tq * CHUNKS_PER_STEP) == 0 and topk <= CHUNK
    gw = GROUP * HEAD_DIM
    return pl.pallas_call(
        functools.partial(_dsa_body, topk=topk),
        grid=(s // tq, N_KV_HEADS),
        in_specs=[
            pl.BlockSpec((IDX_HEADS * IDX_DIM, tq), lambda i, g: (0, i)),
            pl.BlockSpec((IDX_HEADS, tq), lambda i, g: (0, i)),
            pl.BlockSpec((s, IDX_DIM), lambda i, g: (0, 0), pipeline_mode=pl.Buffered(1)),
            pl.BlockSpec((gw, tq), lambda i, g: (g, i)),
            pl.BlockSpec((s, HEAD_DIM), lambda i, g: (0, g)),
            pl.BlockSpec((s // tq, V_ROWS, tq), lambda i, g: (0, g, 0)),
        ],
        out_specs=pl.BlockSpec((tq, gw), lambda i, g: (i, g)),
        out_shape=jax.ShapeDtypeStruct((s, N_MAIN_HEADS * HEAD_DIM), BF16),
        scratch_shapes=[
            pltpu.VMEM((s, tq), jnp.int32),
            pltpu.VMEM((CHUNK, tq), jnp.int32),
            pltpu.VMEM((CAND_LEVELS, CHUNK, tq), jnp.int32),
            pltpu.VMEM((HEAD_DIM, GROUP * tq), BF16),
            pltpu.VMEM((CHUNKS_PER_STEP * CHUNK, GROUP * tq), F32),
            pltpu.VMEM((CHUNKS_PER_STEP * CHUNK, GROUP * tq), F32),
            pltpu.VMEM((1, GROUP * tq), F32),
            pltpu.VMEM((1, GROUP * tq), F32),
            pltpu.VMEM((1, GROUP * tq), F32),
            pltpu.VMEM((V_ROWS, GROUP * tq), F32),
        ],
        compiler_params=_params("arbitrary", "arbitrary"),
        name="dsa",
    )(qit, wt, ki, qt, k, vtt)


def _moba_body(qt_ref, k_ref, vt_ref, km_ref, o_ref, qall_ref, bias_ref, sa_ref, sb_ref, cma_ref, cmb_ref,
               m_ref, acc_ref, *, n_sel):
    cur = pl.program_id(1)
    tq = CHUNK
    nb = km_ref.shape[0]
    width = GROUP * tq
    _load_q_group(qt_ref, qall_ref)
    q_all = qall_ref[...]
    _softmax_init(m_ref, acc_ref)

    tri = (lax.broadcasted_iota(jnp.int32, (tq, tq), 0) <= lax.broadcasted_iota(jnp.int32, (tq, tq), 1))

    def own_logits(c):
        r0 = pl.multiple_of(c * CHUNK, CHUNK)
        s = jnp.dot(k_ref[pl.ds(r0, CHUNK), :], q_all, preferred_element_type=F32)
        return _add_group_bias(s, jnp.where(tri, 0.0, NEG))

    _logits_pass([cur], own_logits, sa_ref, cma_ref)
    _softmax_pass([cur], vt_ref, sa_ref, cma_ref, m_ref, acc_ref)

    blk_id = lax.broadcasted_iota(jnp.int32, (nb, width), 0)
    past = blk_id < cur
    gate = jnp.dot(km_ref[...], q_all, preferred_element_type=F32)
    gate = jnp.where(past, gate, -jnp.inf)
    chosen = jnp.zeros((nb, width), jnp.bool_)
    for _ in range(n_sel):
        best = jnp.max(gate, axis=0, keepdims=True)
        first = jnp.min(jnp.where(gate == best, blk_id, nb), axis=0, keepdims=True)
        pick = blk_id == first
        chosen = chosen | pick
        gate = jnp.where(pick, -jnp.inf, gate)
    bias_ref[:nb, :] = jnp.where(chosen & past, 0.0, NEG)
    bias_ref[nb:, :] = jnp.full((SUBLANES, width), NEG, F32)

    def past_logits(c):
        r0 = pl.multiple_of(jnp.minimum(c, nb - 1) * CHUNK, CHUNK)
        s = jnp.dot(k_ref[pl.ds(r0, CHUNK), :], q_all, preferred_element_type=F32)
        return s + bias_ref[pl.ds(jnp.where(c < cur, c, nb), 1), :]

    _attend_range(cur, past_logits, vt_ref, (sa_ref, sb_ref), (cma_ref, cmb_ref), m_ref, acc_ref,
                  fuse_pairs=True)
    _softmax_finish(o_ref, acc_ref)


def _moba(qt, k, vtt, kmeans, n_sel):
    s = k.shape[0]
    assert MOBA_BLOCK == CHUNK and s % (CHUNK * CHUNKS_PER_STEP) == 0
    tq = CHUNK
    nb = s // tq
    gw = GROUP * HEAD_DIM
    return pl.pallas_call(
        functools.partial(_moba_body, n_sel=n_sel),
        grid=(N_KV_HEADS, nb),
        in_specs=[
            pl.BlockSpec((gw, tq), lambda g, i: (g, i)),
            pl.BlockSpec((s, HEAD_DIM), lambda g, i: (0, g)),
            pl.BlockSpec((nb, V_ROWS, tq), lambda g, i: (0, g, 0)),
            pl.BlockSpec((nb, HEAD_DIM), lambda g, i: (0, g)),
        ],
        out_specs=pl.BlockSpec((tq, gw), lambda g, i: (i, g)),
        out_shape=jax.ShapeDtypeStruct((s, N_MAIN_HEADS * HEAD_DIM), BF16),
        scratch_shapes=[
            pltpu.VMEM((HEAD_DIM, GROUP * tq), BF16),
            pltpu.VMEM((nb + SUBLANES, GROUP * tq), F32),
            pltpu.VMEM((CHUNKS_PER_STEP * CHUNK, GROUP * tq), F32),
            pltpu.VMEM((CHUNKS_PER_STEP * CHUNK, GROUP * tq), F32),
            pltpu.VMEM((1, GROUP * tq), F32),
            pltpu.VMEM((1, GROUP * tq), F32),
            pltpu.VMEM((1, GROUP * tq), F32),
            pltpu.VMEM((V_ROWS, GROUP * tq), F32),
        ],
        compiler_params=_params("parallel", "arbitrary"),
        name="moba",
    )(qt, k, vtt, kmeans)


def _mem_attn_body(q_ref, kt_ref, v_ref, o_ref):
    for h in range(N_MEM_HEADS):
        sl = slice(h * HEAD_DIM, (h + 1) * HEAD_DIM)
        s = jnp.dot(q_ref[:, sl], kt_ref[sl, :], preferred_element_type=F32)
        p = jnp.exp(s - jnp.max(s, axis=-1, keepdims=True))
        o = jnp.dot(p.astype(BF16), v_ref[:, sl], preferred_element_type=F32)
        o_ref[:, sl] = (o / jnp.sum(p, axis=-1, keepdims=True)).astype(o_ref.dtype)


def _mem_attn(qm, kmt, vm, *, tq=512):
    s, w = qm.shape
    m = vm.shape[0]
    tq = min(tq, s)
    assert s % tq == 0
    return pl.pallas_call(
        _mem_attn_body,
        grid=(s // tq,),
        in_specs=[pl.BlockSpec((tq, w), lambda i: (i, 0)),
                  pl.BlockSpec((w, m), lambda i: (0, 0)),
                  pl.BlockSpec((m, w), lambda i: (0, 0))],
        out_specs=pl.BlockSpec((tq, w), lambda i: (i, 0)),
        out_shape=jax.ShapeDtypeStruct((s, w), BF16),
        compiler_params=_params("parallel"),
        name="mem_attn",
    )(qm, kmt, vm)


def _out_proj_body(x_ref, om_ref, oq_ref, w_ref, o_ref):
    n_main = om_ref.shape[1]
    o_ref[...] = (x_ref[...]
                  + jnp.dot(om_ref[...], w_ref[:n_main, :], preferred_element_type=F32)
                  + jnp.dot(oq_ref[...], w_ref[n_main:, :], preferred_element_type=F32))


def _out_proj(x, o_main, o_mem, w, *, layer=0, tm=512):
    s, d = x.shape
    tm = min(tm, s)
    assert s % tm == 0 and w.shape[-2] == o_main.shape[1] + o_mem.shape[1]
    row = lambda i: (i, 0)
    const = lambda i: (0, 0)
    return pl.pallas_call(
        _out_proj_body,
        grid=(s // tm,),
        in_specs=[pl.BlockSpec((tm, d), row),
                  pl.BlockSpec((tm, o_main.shape[1]), row),
                  pl.BlockSpec((tm, o_mem.shape[1]), row),
                  _layer_weight_spec(w, layer)],
        out_specs=pl.BlockSpec((tm, d), row),
        out_shape=jax.ShapeDtypeStruct((s, d), F32),
        compiler_params=_params("parallel"),
        name="out_proj",
    )(x, o_main, o_mem, w)


def _rope_tables(positions, dim):
    inv = 1.0 / (ROPE_THETA ** (jnp.arange(0, dim, 2, dtype=F32) / dim))
    ang = positions.astype(F32)[:, None] * inv
    c, s = jnp.cos(ang), jnp.sin(ang)
    reps = LANES // dim
    return jnp.tile(jnp.concatenate([c, c], -1), (1, reps)), jnp.tile(jnp.concatenate([-s, s], -1), (1, reps))


def kernel(x, mem, positions, ffn1_norm, ffn1_w_gate_up, ffn1_w_down, attn_norm, mem_norm, a_w_in, idx_k_norm, b_w_in, w_mem_kv, w_out, ffn2_norm, ffn2_w_gate_up, ffn2_w_down, kv_norm, w_kv_shared, final_norm):
    b, s, d = x.shape
    assert b == 1 and mem.shape[0] == 1
    depth = ffn1_norm.shape[0]
    n_a = a_w_in.shape[0]
    main_w = N_MAIN_HEADS * HEAD_DIM
    kv_w = N_KV_HEADS * HEAD_DIM
    idx_w = IDX_HEADS * IDX_DIM
    mem_w = N_MEM_HEADS * HEAD_DIM
    topk = min(IDX_TOPK_MAX, s // 4)
    nb = s // MOBA_BLOCK
    n_sel = min(MOBA_TOPK_MAX, max(nb - 1, 1))

    cos, sin = _rope_tables(positions[0], HEAD_DIM)
    cosi, sini = _rope_tables(positions[0], IDX_DIM)
    tables = (cos, sin, cosi, sini)
    mem_tables = tuple(t[:mem.shape[1]] for t in tables)
    no_gk = jnp.zeros((1, LANES), F32)

    ffn_w = [w.astype(BF16) for w in (ffn1_w_gate_up, ffn1_w_down, ffn2_w_gate_up, ffn2_w_down)]
    w_mem_all = w_mem_kv.astype(BF16)
    w_out_all = w_out.astype(BF16)

    xs = x[0]
    mem2 = mem[0]
    k_sh = vtt_sh = kmeans = None
    for i in range(depth):
        if i == n_a:
            wk = w_kv_shared[:, :kv_w].astype(BF16)
            wv = w_kv_shared[:, kv_w:].astype(BF16)
            k_sh, vtt_sh, km = _kv_shared(xs, kv_norm, cos, sin, wk, wv)
            kmeans = km.reshape(nb, kv_w).astype(BF16)

        last = i == depth - 1
        xs = _ffn(xs, ffn1_norm[i], ffn_w[0], ffn_w[1], i)

        mk, mv = _proj(mem2, mem_norm[i], mem_tables, no_gk, w_mem_all,
                       [_Piece("plain", 0, mem_w, 0, mem_w, BF16),
                        _Piece("plain", mem_w, mem_w, 0, mem_w, BF16)], layer=i)
        if i < n_a:
            tail = main_w + 2 * kv_w + idx_w
            kiwi_w = IDX_DIM + IDX_HEADS
            assert a_w_in.shape[2] == tail + kiwi_w + mem_w and kiwi_w <= LANES
            wa = a_w_in[i].astype(BF16)
            wa = jnp.concatenate([wa[:, :tail + kiwi_w], jnp.zeros((d, LANES - kiwi_w), BF16),
                                  wa[:, tail + kiwi_w:]], axis=1)
            gk = jnp.pad(idx_k_norm[i], (0, LANES - IDX_DIM)).reshape(1, LANES)
            qt, k, vtt, qit, kiwi, qm = _proj(
                xs, attn_norm[i], tables, gk, wa,
                [_Piece("rope_scaled_t", 0, main_w, 0, main_w, BF16),
                 _Piece("rope", main_w, kv_w, 0, kv_w, BF16),
                 _Piece("value_chunks", main_w + kv_w, kv_w, 0, kv_w, BF16),
                 _Piece("rope_idx_t", main_w + 2 * kv_w, idx_w, 0, idx_w, BF16),
                 _Piece("kiwi", tail, LANES, 0, LANES, F32),
                 _Piece("scaled", tail + LANES, mem_w, 0, mem_w, BF16)])
            ki = kiwi[:, :IDX_DIM].astype(BF16)
            wt = kiwi[:, IDX_DIM:IDX_DIM + IDX_HEADS].T
            o_main = _dsa(qit, wt, ki, qt, k, vtt, topk)
        else:
            qt, qm = _proj(xs, attn_norm[i], tables, no_gk, b_w_in[i - n_a].astype(BF16),
                           [_Piece("rope_scaled_t", 0, main_w, 0, main_w, BF16),
                            _Piece("scaled", main_w, mem_w, 0, mem_w, BF16)])
            o_main = _moba(qt, k_sh, vtt_sh, kmeans, n_sel)
        o_mem = _mem_attn(qm, mk.T, mv)
        xs = _out_proj(xs, o_main, o_mem, w_out_all, layer=i)

        xs = _ffn(xs, ffn2_norm[i], ffn_w[2], ffn_w[3], i, final_norm if last else None)
    return xs[None]
```
